```python
import jax, jax.numpy as jnp
from jax import lax
import numpy as np

D_MODEL = 4096
BATCH = 8
SEQ = 4096
DEPTH = 1

HEAD_DIM = 128
D_MIX = D_MODEL
W_A = D_MIX // 2
W_B = D_MIX - W_A
N_GROUPS_A = W_A // HEAD_DIM
N_GROUPS_B = W_B // HEAD_DIM
CONV_A = 3
CONV_B = 31
D_IN = 4 * W_A + 3 * W_B
SPLITS = (W_A, 2 * W_A, 3 * W_A, 4 * W_A, 4 * W_A + W_B, 4 * W_A + 2 * W_B)
EPS = 1e-6

kernel_name = "hymba_style_conv_hybrid_adaln"


def _rmsnorm(x, g):
    xf = x.astype(jnp.float32)
    y = xf * lax.rsqrt(jnp.mean(xf * xf, axis=-1, keepdims=True) + EPS)
    return (y * g.astype(jnp.float32)).astype(x.dtype)


def _layernorm(x, g, b):
    xf = x.astype(jnp.float32)
    mu = jnp.mean(xf, axis=-1, keepdims=True)
    xc = xf - mu
    var = jnp.mean(xc * xc, axis=-1, keepdims=True)
    y = xc * lax.rsqrt(var + EPS) * g.astype(jnp.float32) + b.astype(jnp.float32)
    return y.astype(x.dtype)


def _causal_depthwise_conv(u, w):
    k, ch = w.shape
    return lax.conv_general_dilated(
        u, w[:, None, :].astype(u.dtype),
        window_strides=(1,), padding=((k - 1, 0),),
        dimension_numbers=("NWC", "WIO", "NWC"),
        feature_group_count=ch)


def _fwd_setup_inputs(seed: int = 0) -> dict:
    key = jax.random.key(seed)
    ks = jax.random.split(key, 14)
    f32 = jnp.float32
    x = jax.random.normal(ks[0], (BATCH, SEQ, D_MODEL), f32)
    c = jax.random.normal(ks[1], (BATCH, D_MODEL), f32)
    norm_g = 1.0 + 0.01 * jax.random.normal(ks[2], (DEPTH, D_MODEL), f32)
    w_ada = 0.5 * D_MODEL ** -0.5 * jax.random.normal(ks[3], (DEPTH, D_MODEL, 3 * D_MODEL), f32)
    b_ada = 0.01 * jax.random.normal(ks[4], (DEPTH, 3 * D_MODEL), f32)
    w_in = D_MODEL ** -0.5 * jax.random.normal(ks[5], (DEPTH, D_MODEL, D_IN), f32)
    conv_a_w = CONV_A ** -0.5 * jax.random.normal(ks[6], (DEPTH, CONV_A, W_A), f32)
    conv_b_w = CONV_B ** -0.5 * jax.random.normal(ks[7], (DEPTH, CONV_B, W_B), f32)
    conv_b_b = 0.01 * jax.random.normal(ks[8], (DEPTH, W_B), f32)
    ln_b_g = 1.0 + 0.01 * jax.random.normal(ks[9], (DEPTH, W_B), f32)
    ln_b_b = 0.01 * jax.random.normal(ks[10], (DEPTH, W_B), f32)
    w_out = D_MIX ** -0.5 * jax.random.normal(ks[11], (DEPTH, D_MIX, D_MODEL), f32)
    final_g = 1.0 + 0.01 * jax.random.normal(ks[12], (D_MODEL,), f32)
    return {"x": x, "c": c, "norm_g": norm_g, "w_ada": w_ada, "b_ada": b_ada,
            "w_in": w_in, "conv_a_w": conv_a_w, "conv_b_w": conv_b_w,
            "conv_b_b": conv_b_b, "ln_b_g": ln_b_g, "ln_b_b": ln_b_b,
            "w_out": w_out, "final_g": final_g}


def _fwd_reference(x, c, norm_g, w_ada, b_ada, w_in, conv_a_w, conv_b_w, conv_b_b,
              ln_b_g, ln_b_b, w_out, final_g):
    c_act = jax.nn.silu(c)
    for l in range(DEPTH):
        mod = c_act @ w_ada[l] + b_ada[l]
        shift, scale, gate = jnp.split(mod, 3, axis=-1)
        h = _rmsnorm(x, norm_g[l]) * (1.0 + scale[:, None, :]) + shift[:, None, :]

        proj = jnp.einsum("bsd,de->bse", h, w_in[l])
        a_b, a_c, a_x, a_z, b_v, b_g, b_z = jnp.split(proj, SPLITS, axis=-1)

        y_a = a_b * _causal_depthwise_conv(a_c * a_x, conv_a_w[l]) * jax.nn.silu(a_z)

        u = b_v * jax.nn.sigmoid(b_g)
        u = _causal_depthwise_conv(u, conv_b_w[l]) + conv_b_b[l]
        y_b = jax.nn.silu(_layernorm(u, ln_b_g[l], ln_b_b[l])) * jax.nn.silu(b_z)

        y = jnp.concatenate([y_a, y_b], axis=-1)
        x = x + gate[:, None, :] * jnp.einsum("bse,ed->bsd", y, w_out[l])
    return _rmsnorm(x, final_g)


import jax as _jax
import jax.numpy as _jnp

TWIN_FORMAT = 'train_step'
FWD_PARAMS = ['x', 'c', 'norm_g', 'w_ada', 'b_ada', 'w_in', 'conv_a_w', 'conv_b_w', 'conv_b_b', 'ln_b_g', 'ln_b_b', 'w_out', 'final_g']
TWIN_WEIGHTS = ['norm_g', 'w_ada', 'b_ada', 'w_in', 'conv_a_w', 'conv_b_w', 'conv_b_b', 'ln_b_g', 'ln_b_b', 'w_out', 'final_g']
TWIN_DIFF_INPUT = 'x'
TWIN_INPUTS = ['x', 'c', 'norm_g', 'w_ada', 'b_ada', 'w_in', 'conv_a_w', 'conv_b_w', 'conv_b_b', 'ln_b_g', 'ln_b_b', 'w_out', 'final_g', 'loss_target', 'm_norm_g', 'm_w_ada', 'm_b_ada', 'm_w_in', 'm_conv_a_w', 'm_conv_b_w', 'm_conv_b_b', 'm_ln_b_g', 'm_ln_b_b', 'm_w_out', 'm_final_g', 'v_norm_g', 'v_w_ada', 'v_b_ada', 'v_w_in', 'v_conv_a_w', 'v_conv_b_w', 'v_conv_b_b', 'v_ln_b_g', 'v_ln_b_b', 'v_w_out', 'v_final_g']
TWIN_OUTPUTS = ['loss', 'grad_x', 'grad_norm_g', 'grad_w_ada', 'grad_b_ada', 'grad_w_in', 'grad_conv_a_w', 'grad_conv_b_w', 'grad_conv_b_b', 'grad_ln_b_g', 'grad_ln_b_b', 'grad_w_out', 'grad_final_g', 'delta_norm_g', 'delta_w_ada', 'delta_b_ada', 'delta_w_in', 'delta_conv_a_w', 'delta_conv_b_w', 'delta_conv_b_b', 'delta_ln_b_g', 'delta_ln_b_b', 'delta_w_out', 'delta_final_g', 'new_m_norm_g', 'new_m_w_ada', 'new_m_b_ada', 'new_m_w_in', 'new_m_conv_a_w', 'new_m_conv_b_w', 'new_m_conv_b_b', 'new_m_ln_b_g', 'new_m_ln_b_b', 'new_m_w_out', 'new_m_final_g', 'new_v_norm_g', 'new_v_w_ada', 'new_v_b_ada', 'new_v_w_in', 'new_v_conv_a_w', 'new_v_conv_b_w', 'new_v_conv_b_b', 'new_v_ln_b_g', 'new_v_ln_b_b', 'new_v_w_out', 'new_v_final_g']
TWIN_LEAF_KINDS = {'loss': 'loss', 'grad_x': 'grad_x', 'grad_norm_g': 'grad_w', 'grad_w_ada': 'grad_w', 'grad_b_ada': 'grad_w', 'grad_w_in': 'grad_w', 'grad_conv_a_w': 'grad_w', 'grad_conv_b_w': 'grad_w', 'grad_conv_b_b': 'grad_w', 'grad_ln_b_g': 'grad_w', 'grad_ln_b_b': 'grad_w', 'grad_w_out': 'grad_w', 'grad_final_g': 'grad_w', 'delta_norm_g': 'delta_w', 'delta_w_ada': 'delta_w', 'delta_b_ada': 'delta_w', 'delta_w_in': 'delta_w', 'delta_conv_a_w': 'delta_w', 'delta_conv_b_w': 'delta_w', 'delta_conv_b_b': 'delta_w', 'delta_ln_b_g': 'delta_w', 'delta_ln_b_b': 'delta_w', 'delta_w_out': 'delta_w', 'delta_final_g': 'delta_w', 'new_m_norm_g': 'new_m', 'new_m_w_ada': 'new_m', 'new_m_b_ada': 'new_m', 'new_m_w_in': 'new_m', 'new_m_conv_a_w': 'new_m', 'new_m_conv_b_w': 'new_m', 'new_m_conv_b_b': 'new_m', 'new_m_ln_b_g': 'new_m', 'new_m_ln_b_b': 'new_m', 'new_m_w_out': 'new_m', 'new_m_final_g': 'new_m', 'new_v_norm_g': 'new_v', 'new_v_w_ada': 'new_v', 'new_v_b_ada': 'new_v', 'new_v_w_in': 'new_v', 'new_v_conv_a_w': 'new_v', 'new_v_conv_b_w': 'new_v', 'new_v_conv_b_b': 'new_v', 'new_v_ln_b_g': 'new_v', 'new_v_ln_b_b': 'new_v', 'new_v_w_out': 'new_v', 'new_v_final_g': 'new_v'}


def _forward(args):
    return _fwd_reference(*[args[k] for k in FWD_PARAMS])


def _output_shape():
    out = _jax.eval_shape(lambda: _forward(_fwd_setup_inputs(0)))
    return out.shape, out.dtype

N_MICROBATCH = 1
ADAM_LR = 0.001
ADAM_B1 = 0.9
ADAM_B2 = 0.999
ADAM_EPS = 1e-08
ADAM_WD = 0.01
ADAM_STEP = 10
PER_EXAMPLE_BATCH_AXIS = {'x': 0, 'c': 0, 'loss_target': 0}
SHARED_INPUTS = []
_WEIGHT_DTYPES = {'norm_g': _jnp.float32, 'w_ada': _jnp.float32, 'b_ada': _jnp.float32, 'w_in': _jnp.float32, 'conv_a_w': _jnp.float32, 'conv_b_w': _jnp.float32, 'conv_b_b': _jnp.float32, 'ln_b_g': _jnp.float32, 'ln_b_b': _jnp.float32, 'w_out': _jnp.float32, 'final_g': _jnp.float32}
MOMENT_SCALE = {'norm_g': 1.643274e-02, 'w_ada': 1.260091e-02, 'b_ada': 2.079272e-02, 'w_in': 9.023125e-03, 'conv_a_w': 1.133448e-02, 'conv_b_w': 5.482004e-03, 'conv_b_b': 1.008158e-02, 'ln_b_g': 6.634758e-03, 'ln_b_b': 5.468035e-03, 'w_out': 8.655635e-03, 'final_g': 7.988019e+00}


def _to_microbatches(a, axis):
    t = _jnp.moveaxis(a, axis, 0)
    t = t.reshape((N_MICROBATCH, t.shape[0] // N_MICROBATCH) + t.shape[1:])
    return _jnp.moveaxis(t, 1, axis + 1)


def setup_inputs(seed: int = 0) -> dict:
    inp = _fwd_setup_inputs(seed)
    key = _jax.random.fold_in(_jax.random.key(seed), 7919)
    shape, _ = _output_shape()
    out = dict(inp)
    out["loss_target"] = _jax.random.normal(_jax.random.fold_in(key, 0), shape, _jnp.float32)
    for i, name in enumerate(TWIN_WEIGHTS):
        w = inp[name].astype(_jnp.float32)
        if MOMENT_SCALE is None:
            s = _jnp.sqrt(_jnp.mean(_jnp.square(w)) + 1e-30)
        else:
            s = MOMENT_SCALE[name]
        km, kv = _jax.random.split(_jax.random.fold_in(key, i + 1))
        out[name] = w
        out["m_" + name] = s * _jax.random.normal(km, w.shape, _jnp.float32)
        out["v_" + name] = (s * s) * _jax.random.uniform(kv, w.shape, _jnp.float32, 0.5, 1.5)
    if N_MICROBATCH > 1:
        for name, axis in PER_EXAMPLE_BATCH_AXIS.items():
            out[name] = _to_microbatches(out[name], axis)
    return {'x': out['x'], 'c': out['c'], 'norm_g': out['norm_g'], 'w_ada': out['w_ada'], 'b_ada': out['b_ada'], 'w_in': out['w_in'], 'conv_a_w': out['conv_a_w'], 'conv_b_w': out['conv_b_w'], 'conv_b_b': out['conv_b_b'], 'ln_b_g': out['ln_b_g'], 'ln_b_b': out['ln_b_b'], 'w_out': out['w_out'], 'final_g': out['final_g'], 'loss_target': out['loss_target'], 'm_norm_g': out['m_norm_g'], 'm_w_ada': out['m_w_ada'], 'm_b_ada': out['m_b_ada'], 'm_w_in': out['m_w_in'], 'm_conv_a_w': out['m_conv_a_w'], 'm_conv_b_w': out['m_conv_b_w'], 'm_conv_b_b': out['m_conv_b_b'], 'm_ln_b_g': out['m_ln_b_g'], 'm_ln_b_b': out['m_ln_b_b'], 'm_w_out': out['m_w_out'], 'm_final_g': out['m_final_g'], 'v_norm_g': out['v_norm_g'], 'v_w_ada': out['v_w_ada'], 'v_b_ada': out['v_b_ada'], 'v_w_in': out['v_w_in'], 'v_conv_a_w': out['v_conv_a_w'], 'v_conv_b_w': out['v_conv_b_w'], 'v_conv_b_b': out['v_conv_b_b'], 'v_ln_b_g': out['v_ln_b_g'], 'v_ln_b_b': out['v_ln_b_b'], 'v_w_out': out['v_w_out'], 'v_final_g': out['v_final_g']}


def _loss(weights, diff, rest, loss_target):
    with _jax.named_scope("forward"):
        args = {**rest, TWIN_DIFF_INPUT: diff, **{k: w.astype(_WEIGHT_DTYPES[k]) for k, w in weights.items()}}
        y = _forward(args)
    with _jax.named_scope("loss_head"):
        err = _jnp.square(y.astype(_jnp.float32) - loss_target)
        return 0.5 * _jnp.sum(_jnp.mean(err, axis=-1)) if err.ndim else 0.5 * err


def _adamw(w, g, m, v):
    m = ADAM_B1 * m + (1.0 - ADAM_B1) * g
    v = ADAM_B2 * v + (1.0 - ADAM_B2) * _jnp.square(g)
    m_hat = m / (1.0 - ADAM_B1 ** ADAM_STEP)
    v_hat = v / (1.0 - ADAM_B2 ** ADAM_STEP)
    delta = -ADAM_LR * (m_hat / (_jnp.sqrt(v_hat) + ADAM_EPS) + ADAM_WD * w)
    return delta, m, v


def reference(x, c, norm_g, w_ada, b_ada, w_in, conv_a_w, conv_b_w, conv_b_b, ln_b_g, ln_b_b, w_out, final_g, loss_target, m_norm_g, m_w_ada, m_b_ada, m_w_in, m_conv_a_w, m_conv_b_w, m_conv_b_b, m_ln_b_g, m_ln_b_b, m_w_out, m_final_g, v_norm_g, v_w_ada, v_b_ada, v_w_in, v_conv_a_w, v_conv_b_w, v_conv_b_b, v_ln_b_g, v_ln_b_b, v_w_out, v_final_g):
    given = dict(x=x, c=c, norm_g=norm_g, w_ada=w_ada, b_ada=b_ada, w_in=w_in, conv_a_w=conv_a_w, conv_b_w=conv_b_w, conv_b_b=conv_b_b, ln_b_g=ln_b_g, ln_b_b=ln_b_b, w_out=w_out, final_g=final_g, loss_target=loss_target, m_norm_g=m_norm_g, m_w_ada=m_w_ada, m_b_ada=m_b_ada, m_w_in=m_w_in, m_conv_a_w=m_conv_a_w, m_conv_b_w=m_conv_b_w, m_conv_b_b=m_conv_b_b, m_ln_b_g=m_ln_b_g, m_ln_b_b=m_ln_b_b, m_w_out=m_w_out, m_final_g=m_final_g, v_norm_g=v_norm_g, v_w_ada=v_w_ada, v_b_ada=v_b_ada, v_w_in=v_w_in, v_conv_a_w=v_conv_a_w, v_conv_b_w=v_conv_b_w, v_conv_b_b=v_conv_b_b, v_ln_b_g=v_ln_b_g, v_ln_b_b=v_ln_b_b, v_w_out=v_w_out, v_final_g=v_final_g)
    weights = {n: given[n] for n in TWIN_WEIGHTS}
    shared = {n: given[n] for n in SHARED_INPUTS}
    per_example = {n: given[n] for n in ['x', 'c']}
    grad_fn = _jax.value_and_grad(_loss, argnums=(0, 1))

    def one_microbatch(ex, loss_target):
        ex = dict(ex)
        diff = ex.pop(TWIN_DIFF_INPUT)
        return grad_fn(weights, diff, {**shared, **ex}, loss_target)

    if N_MICROBATCH == 1:
        loss, (grad_w, grad_x) = one_microbatch(per_example, given["loss_target"])
    else:
        def body(carry, xs):
            loss_sum, grad_sum = carry
            l_k, (gw_k, gx_k) = one_microbatch(xs[0], xs[1])
            with _jax.named_scope("update"):
                return (loss_sum + l_k, _jax.tree.map(_jnp.add, grad_sum, gw_k)), gx_k

        init = (_jnp.zeros((), _jnp.float32), _jax.tree.map(_jnp.zeros_like, weights))
        (loss, grad_w), grad_x = _jax.lax.scan(body, init, (per_example, given["loss_target"]))
    with _jax.named_scope("update"):
        delta_w, new_m, new_v = {}, {}, {}
        for n in TWIN_WEIGHTS:
            delta_w[n], new_m[n], new_v[n] = _adamw(weights[n], grad_w[n], given["m_" + n], given["v_" + n])
    return (loss, grad_x, *[grad_w[n] for n in TWIN_WEIGHTS], *[delta_w[n] for n in TWIN_WEIGHTS],
            *[new_m[n] for n in TWIN_WEIGHTS], *[new_v[n] for n in TWIN_WEIGHTS])
```

```python
import functools

import jax
import jax.numpy as jnp
from jax import lax
from jax.experimental import pallas as pl
from jax.experimental.pallas import tpu as pltpu

F32 = jnp.float32
BF16 = jnp.bfloat16
N_DEV = 8
N_CHIP = 4
EPS = 1e-6
ADAM_LR = 0.001
ADAM_B1 = 0.9
ADAM_B2 = 0.999
ADAM_EPS = 1e-08
ADAM_WD = 0.01
ADAM_STEP = 10
CONV_A = 3
CONV_B = 31
HALO_A = 8
HALO_B = 32
LANES = 128
VMEM_LIMIT = 56 * 1024 * 1024
MESH = pl.DeviceIdType.MESH


def _cparams(n_grid_axes=0):
    if n_grid_axes:
        return pltpu.CompilerParams(dimension_semantics=("arbitrary",) * n_grid_axes, vmem_limit_bytes=VMEM_LIMIT)
    return pltpu.CompilerParams(vmem_limit_bytes=VMEM_LIMIT)


def _flip(v, bit):
    return 1 - v if bit else v


def _position():
    return lax.axis_index("x"), lax.axis_index("y"), lax.axis_index("c")


def _sigmoid(v):
    return jax.nn.sigmoid(v)


def _all_gather_vmem(arrs, name):
    n = len(arrs)

    def body(*refs):
        ins, outs = refs[:n], refs[n : 2 * n]
        send_sems, recv_sems = refs[2 * n :]
        x, y, c = _position()
        me = 4 * x + 2 * y + c
        for a in range(n):
            outs[a][me] = ins[a][...]
        sends = []
        for k in range(1, N_DEV):
            peer = (_flip(x, k & 4), _flip(y, k & 2), _flip(c, k & 1))
            for a in range(n):
                cp = pltpu.make_async_remote_copy(
                    src_ref=ins[a], dst_ref=outs[a].at[me], send_sem=send_sems.at[a, k - 1], recv_sem=recv_sems.at[a, k - 1],
                    device_id=peer, device_id_type=MESH)
                cp.start()
                sends.append(cp)
        for k in range(1, N_DEV):
            peer = (_flip(x, k & 4), _flip(y, k & 2), _flip(c, k & 1))
            src = 4 * peer[0] + 2 * peer[1] + peer[2]
            for a in range(n):
                pltpu.make_async_remote_copy(
                    src_ref=ins[a], dst_ref=outs[a].at[src], send_sem=send_sems.at[a, k - 1], recv_sem=recv_sems.at[a, k - 1],
                    device_id=peer, device_id_type=MESH).wait_recv()
        for cp in sends:
            cp.wait_send()

    vm = pl.BlockSpec(memory_space=pltpu.VMEM)
    return pl.pallas_call(
        body, name=name,
        out_shape=[jax.ShapeDtypeStruct((N_DEV,) + a.shape, a.dtype) for a in arrs],
        in_specs=[vm] * n, out_specs=[vm] * n,
        scratch_shapes=[pltpu.SemaphoreType.DMA((n, N_DEV - 1)), pltpu.SemaphoreType.DMA((n, N_DEV - 1))],
        compiler_params=_cparams(),
    )(*arrs)


def _all_gather_hbm(arrs, name):
    n = len(arrs)

    def body(*refs):
        ins, outs = refs[:n], refs[n : 2 * n]
        send_sems, recv_sems, local_sems = refs[2 * n :]
        x, y, c = _position()
        me = (x, y, c)
        sibling = (x, y, 1 - c)
        chips = [(1 - x, y), (x, 1 - y), (1 - x, 1 - y)]

        def slot(p):
            return 4 * p[0] + 2 * p[1] + p[2]

        def copy(a, k, block, to, src=None):
            dst = outs[a].at[slot(block)]
            return pltpu.make_async_remote_copy(
                src_ref=dst if src is None else src, dst_ref=dst, send_sem=send_sems.at[a, k], recv_sem=recv_sems.at[a, k],
                device_id=to, device_id_type=MESH)

        mine = [pltpu.make_async_copy(ins[a], outs[a].at[slot(me)], local_sems.at[a]) for a in range(n)]
        for cp in mine:
            cp.start()
        first = []
        for a in range(n):
            first.append(copy(a, 0, me, sibling, src=ins[a]))
            first += [copy(a, 1 + j, me, (*chip, c), src=ins[a]) for j, chip in enumerate(chips)]
        for cp in first:
            cp.start()
        passed = []
        for j, chip in enumerate(chips):
            for a in range(n):
                copy(a, 1 + j, (*chip, c), me).wait_recv()
                cp = copy(a, 4 + j, (*chip, c), sibling)
                cp.start()
                passed.append(cp)
        for a in range(n):
            copy(a, 0, sibling, me).wait_recv()
            for j, chip in enumerate(chips):
                copy(a, 4 + j, (*chip, 1 - c), me).wait_recv()
        for cp in first + passed:
            cp.wait_send()
        for cp in mine:
            cp.wait()

    hbm = pl.BlockSpec(memory_space=pl.ANY)
    return pl.pallas_call(
        body, name=name,
        out_shape=[jax.ShapeDtypeStruct((N_DEV,) + a.shape, a.dtype) for a in arrs],
        in_specs=[hbm] * n, out_specs=[hbm] * n,
        scratch_shapes=[pltpu.SemaphoreType.DMA((n, 7)), pltpu.SemaphoreType.DMA((n, 7)), pltpu.SemaphoreType.DMA((n,))],
        compiler_params=_cparams(),
    )(*arrs)


def _swap_with_sibling(parts, name):
    n = len(parts)

    def body(*refs):
        ins, outs = refs[:n], refs[n : 2 * n]
        send_sems, recv_sems = refs[2 * n :]
        x, y, c = _position()
        sibling = (x, y, 1 - c)
        sends = []
        for a in range(n):
            for q in range(N_CHIP):
                cp = pltpu.make_async_remote_copy(
                    src_ref=ins[a].at[2 * q + (1 - c)], dst_ref=outs[a].at[q], send_sem=send_sems.at[a, q],
                    recv_sem=recv_sems.at[a, q], device_id=sibling, device_id_type=MESH)
                cp.start()
                sends.append(cp)
        for cp in sends:
            cp.wait_recv()
        for cp in sends:
            cp.wait_send()

    hbm = pl.BlockSpec(memory_space=pl.ANY)
    return pl.pallas_call(
        body, name=name,
        out_shape=[jax.ShapeDtypeStruct((N_CHIP,) + p.shape[1:], p.dtype) for p in parts],
        in_specs=[hbm] * n, out_specs=[hbm] * n,
        scratch_shapes=[pltpu.SemaphoreType.DMA((n, N_CHIP)), pltpu.SemaphoreType.DMA((n, N_CHIP))],
        compiler_params=_cparams(),
    )(*parts)


def _send_to_chips(combs, name):
    n = len(combs)

    def body(*refs):
        ins, outs = refs[:n], refs[n : 2 * n]
        send_sems, recv_sems = refs[2 * n :]
        x, y, c = _position()
        chips = [(1 - x, y), (x, 1 - y), (1 - x, 1 - y)]
        sends = []
        for a in range(n):
            for j, chip in enumerate(chips):
                cp = pltpu.make_async_remote_copy(
                    src_ref=ins[a].at[2 * chip[0] + chip[1]], dst_ref=outs[a].at[j], send_sem=send_sems.at[a, j],
                    recv_sem=recv_sems.at[a, j], device_id=(*chip, c), device_id_type=MESH)
                cp.start()
                sends.append(cp)
        for cp in sends:
            cp.wait_recv()
        for cp in sends:
            cp.wait_send()

    hbm = pl.BlockSpec(memory_space=pl.ANY)
    return pl.pallas_call(
        body, name=name,
        out_shape=[jax.ShapeDtypeStruct((3,) + p.shape[1:], p.dtype) for p in combs],
        in_specs=[hbm] * n, out_specs=[hbm] * n,
        scratch_shapes=[pltpu.SemaphoreType.DMA((n, 3)), pltpu.SemaphoreType.DMA((n, 3))],
        compiler_params=_cparams(),
    )(*combs)


NN = (((1,), (0,)), ((), ()))
NT = (((1,), (1,)), ((), ()))
TN = (((0,), (0,)), ((), ()))


def _n_chunk(n):
    return 256 if n % 256 == 0 else LANES


def _mm_body(dims, n_out, accumulate):
    nc = _n_chunk(n_out)

    def body(a_ref, b_ref, o_ref):
        for n0 in range(0, n_out, nc):
            b = b_ref[n0 : n0 + nc, :] if dims is NT else b_ref[:, n0 : n0 + nc]
            r = lax.dot_general(a_ref[...], b, dims, preferred_element_type=F32)
            if accumulate:
                k = pl.program_id(2)

                @pl.when(k == 0)
                def _():
                    o_ref[:, n0 : n0 + nc] = r

                @pl.when(k > 0)
                def _():
                    o_ref[:, n0 : n0 + nc] += r
            else:
                o_ref[:, n0 : n0 + nc] = r.astype(o_ref.dtype)

    return body


def _mm_proj(h, w_full, tm):
    s, d = h.shape
    _, _, ne = w_full.shape
    return pl.pallas_call(
        _mm_body(NN, ne, False), name="mm_proj", grid=(N_DEV, s // tm),
        in_specs=[pl.BlockSpec((tm, d), lambda j, i: (i, 0)), pl.BlockSpec((None, d, ne), lambda j, i: (j, 0, 0))],
        out_specs=pl.BlockSpec((tm, ne), lambda j, i: (i, j)),
        out_shape=jax.ShapeDtypeStruct((s, N_DEV * ne), BF16), compiler_params=_cparams(2),
    )(h, w_full)


def _mm_out(y, w_out, tm, tn):
    s, m = y.shape
    _, d = w_out.shape
    return pl.pallas_call(
        _mm_body(NN, tn, False), name="mm_out", grid=(d // tn, s // tm),
        in_specs=[pl.BlockSpec((tm, m), lambda j, i: (i, 0)), pl.BlockSpec((m, tn), lambda j, i: (0, j))],
        out_specs=pl.BlockSpec((tm, tn), lambda j, i: (i, j)),
        out_shape=jax.ShapeDtypeStruct((s, d), F32), compiler_params=_cparams(2),
    )(y, w_out)


def _mm_dy(dout, w_out, tm, tn):
    s, d = dout.shape
    m, _ = w_out.shape
    return pl.pallas_call(
        _mm_body(NT, tn, False), name="mm_dy", grid=(m // tn, s // tm),
        in_specs=[pl.BlockSpec((tm, d), lambda j, i: (i, 0)), pl.BlockSpec((tn, d), lambda j, i: (j, 0))],
        out_specs=pl.BlockSpec((tm, tn), lambda j, i: (i, j)),
        out_shape=jax.ShapeDtypeStruct((s, m), BF16), compiler_params=_cparams(2),
    )(dout, w_out)


def _mm_dw_out(y, dout, tn):
    s, m = y.shape
    _, d = dout.shape
    mr = m // N_DEV
    return pl.pallas_call(
        _mm_body(TN, tn, False), name="mm_dw_out", grid=(d // tn, N_DEV),
        in_specs=[pl.BlockSpec((s, mr), lambda j, i: (0, i)), pl.BlockSpec((s, tn), lambda j, i: (0, j))],
        out_specs=pl.BlockSpec((None, mr, tn), lambda j, i: (i, 0, j)),
        out_shape=jax.ShapeDtypeStruct((N_DEV, mr, d), BF16), compiler_params=_cparams(2),
    )(y, dout)


def _mm_dw_in(h, dproj, tm):
    s, d = h.shape
    ne = dproj.shape[1] // N_DEV
    return pl.pallas_call(
        _mm_body(TN, ne, False), name="mm_dw_in", grid=(N_DEV, d // tm),
        in_specs=[pl.BlockSpec((s, tm), lambda j, i: (0, i)), pl.BlockSpec((s, ne), lambda j, i: (0, j))],
        out_specs=pl.BlockSpec((None, tm, ne), lambda j, i: (j, i, 0)),
        out_shape=jax.ShapeDtypeStruct((N_DEV, d, ne), BF16), compiler_params=_cparams(2),
    )(h, dproj)


def _mm_dh(dproj, w_full, tm, tn):
    s = dproj.shape[0]
    _, d, ne = w_full.shape
    return pl.pallas_call(
        _mm_body(NT, tn, True), name="mm_dh", grid=(s // tm, d // tn, N_DEV),
        in_specs=[pl.BlockSpec((tm, ne), lambda i, n, k: (i, k)), pl.BlockSpec((None, tn, ne), lambda i, n, k: (k, n, 0))],
        out_specs=pl.BlockSpec((tm, tn), lambda i, n, k: (i, n)),
        out_shape=jax.ShapeDtypeStruct((s, d), F32), compiler_params=_cparams(3),
    )(dproj, w_full)


def _mod_matmul(c_all, w_ada, b_loc, tk):
    nb, d = c_all.shape
    na = w_ada.shape[1]

    def body(c_ref, w_ref, b_ref, o_ref):
        k = pl.program_id(0)
        r = jnp.dot(c_ref[...], w_ref[...], preferred_element_type=F32, precision=lax.Precision.HIGHEST)

        @pl.when(k == 0)
        def _():
            o_ref[...] = r + b_ref[...]

        @pl.when(k > 0)
        def _():
            o_ref[...] += r

    return pl.pallas_call(
        body, name="mod_matmul", grid=(d // tk,),
        in_specs=[pl.BlockSpec((nb, tk), lambda k: (0, k)), pl.BlockSpec((tk, na), lambda k: (k, 0)), pl.BlockSpec((1, na), lambda k: (0, 0))],
        out_specs=pl.BlockSpec((nb, na), lambda k: (0, 0)),
        out_shape=jax.ShapeDtypeStruct((nb, na), F32), compiler_params=_cparams(1),
    )(c_all, w_ada, b_loc)


def _col_chunks(width):
    return [slice(c0, c0 + LANES) for c0 in range(0, width, LANES)]


def _row_sum(acc):
    return jnp.sum(acc, axis=1, keepdims=True)


def _prep(x, g1, scale, shift, ts):
    s, d = x.shape
    chunks = _col_chunks(d)

    def body(x_ref, g_ref, sc_ref, sh_ref, h_ref):
        acc = jnp.zeros((ts, LANES), F32)
        for cs in chunks:
            v = x_ref[:, cs]
            acc = acc + v * v
        r = lax.rsqrt(_row_sum(acc) / d + EPS)
        for cs in chunks:
            n1 = (x_ref[:, cs] * r) * g_ref[:, cs]
            h_ref[:, cs] = (n1 * (1.0 + sc_ref[:, cs]) + sh_ref[:, cs]).astype(BF16)

    row = pl.BlockSpec((1, d), lambda i: (0, 0))
    return pl.pallas_call(
        body, name="prep", grid=(s // ts,),
        in_specs=[pl.BlockSpec((ts, d), lambda i: (i, 0)), row, row, row],
        out_specs=pl.BlockSpec((ts, d), lambda i: (i, 0)),
        out_shape=jax.ShapeDtypeStruct((s, d), BF16), compiler_params=_cparams(1),
    )(x, g1, scale, shift)


def _mix_a_fwd(proj, wa, w, ts, cb):
    s = proj.shape[0]
    nb = w // cb
    chunks = _col_chunks(cb)

    def body(ab, ac, ax, az, wa_ref, y_ref, cv_ref, ext):
        @pl.when(pl.program_id(1) == 0)
        def _():
            ext[0:HALO_A, :] = jnp.zeros((HALO_A, cb), F32)

        for cs in chunks:
            ext[HALO_A : HALO_A + ts, cs] = ac[:, cs].astype(F32) * ax[:, cs].astype(F32)
        for cs in chunks:
            cv = wa_ref[0:1, cs] * ext[HALO_A - 2 : HALO_A - 2 + ts, cs]
            cv = cv + wa_ref[1:2, cs] * ext[HALO_A - 1 : HALO_A - 1 + ts, cs]
            cv = cv + wa_ref[2:3, cs] * ext[HALO_A : HALO_A + ts, cs]
            z = az[:, cs].astype(F32)
            y_ref[:, cs] = (ab[:, cs].astype(F32) * cv * (z * _sigmoid(z))).astype(BF16)
            cv_ref[:, cs] = cv.astype(BF16)
        ext[0:HALO_A, :] = ext[ts : ts + HALO_A, :]

    def seg(q):
        return pl.BlockSpec((ts, cb), lambda c, i: (i, q * nb + c))

    return pl.pallas_call(
        body, name="mix_a_fwd", grid=(nb, s // ts),
        in_specs=[seg(0), seg(1), seg(2), seg(3), pl.BlockSpec((CONV_A, cb), lambda c, i: (0, c))],
        out_specs=[pl.BlockSpec((ts, cb), lambda c, i: (i, c)), pl.BlockSpec((ts, cb), lambda c, i: (i, c))],
        out_shape=[jax.ShapeDtypeStruct((s, 2 * w), BF16), jax.ShapeDtypeStruct((s, w), BF16)],
        scratch_shapes=[pltpu.VMEM((HALO_A + ts, cb), F32)], compiler_params=_cparams(2),
    )(proj, proj, proj, proj, wa)


def _mix_b_conv_fwd(proj, wb, bb, w, ts, cb):
    s = proj.shape[0]
    nb = w // cb
    chunks = _col_chunks(cb)

    def body(bv, bg, wb_ref, bb_ref, u2_ref, ext):
        @pl.when(pl.program_id(1) == 0)
        def _():
            ext[0:HALO_B, :] = jnp.zeros((HALO_B, cb), F32)

        for cs in chunks:
            ext[HALO_B : HALO_B + ts, cs] = bv[:, cs].astype(F32) * _sigmoid(bg[:, cs].astype(F32))
        for cs in chunks:
            acc = jnp.zeros((ts, LANES), F32) + bb_ref[:, cs]
            for k in range(CONV_B):
                o = HALO_B - (CONV_B - 1) + k
                acc = acc + wb_ref[k : k + 1, cs] * ext[o : o + ts, cs]
            u2_ref[:, cs] = acc
        ext[0:HALO_B, :] = ext[ts : ts + HALO_B, :]

    return pl.pallas_call(
        body, name="mix_b_conv_fwd", grid=(nb, s // ts),
        in_specs=[pl.BlockSpec((ts, cb), lambda c, i: (i, 4 * nb + c)), pl.BlockSpec((ts, cb), lambda c, i: (i, 5 * nb + c)),
                  pl.BlockSpec((CONV_B, cb), lambda c, i: (0, c)), pl.BlockSpec((1, cb), lambda c, i: (0, c))],
        out_specs=pl.BlockSpec((ts, cb), lambda c, i: (i, c)),
        out_shape=jax.ShapeDtypeStruct((s, w), F32),
        scratch_shapes=[pltpu.VMEM((HALO_B + ts, cb), F32)], compiler_params=_cparams(2),
    )(proj, proj, wb, bb)


def _ln_stats(u2_ref, ts, w, chunks):
    acc = jnp.zeros((ts, LANES), F32)
    for cs in chunks:
        acc = acc + u2_ref[:, cs]
    mu = _row_sum(acc) / w
    acc = jnp.zeros((ts, LANES), F32)
    for cs in chunks:
        xc = u2_ref[:, cs] - mu
        acc = acc + xc * xc
    return mu, lax.rsqrt(_row_sum(acc) / w + EPS)


def _mix_b_gate_fwd(u2, proj, lg, lb, y, ts):
    s, w = u2.shape
    chunks = _col_chunks(w)

    def body(u2_ref, bz, lg_ref, lb_ref, y_in, y_ref):
        del y_in
        mu, rstd = _ln_stats(u2_ref, ts, w, chunks)
        for cs in chunks:
            ln = (u2_ref[:, cs] - mu) * rstd * lg_ref[:, cs] + lb_ref[:, cs]
            z = bz[:, cs].astype(F32)
            y_ref[:, cs] = ((ln * _sigmoid(ln)) * (z * _sigmoid(z))).astype(BF16)

    row = pl.BlockSpec((1, w), lambda i: (0, 0))
    return pl.pallas_call(
        body, name="mix_b_gate_fwd", grid=(s // ts,),
        in_specs=[pl.BlockSpec((ts, w), lambda i: (i, 0)), pl.BlockSpec((ts, w), lambda i: (i, 6)), row, row,
                  pl.BlockSpec(memory_space=pl.ANY)],
        out_specs=pl.BlockSpec((ts, w), lambda i: (i, 1)),
        out_shape=jax.ShapeDtypeStruct(y.shape, y.dtype), input_output_aliases={4: 0}, compiler_params=_cparams(1),
    )(u2, proj, lg, lb, y)


def _post(x, out, tgt, gate, fg, ts):
    s, d = x.shape
    chunks = _col_chunks(d)

    def body(x_ref, o_ref, t_ref, gt_ref, fg_ref, dx2_ref, dout_ref, loss_ref, dgate_ref, dfg_ref):
        @pl.when(pl.program_id(0) == 0)
        def _():
            loss_ref[...] = jnp.zeros_like(loss_ref)
            dgate_ref[...] = jnp.zeros_like(dgate_ref)
            dfg_ref[...] = jnp.zeros_like(dfg_ref)

        acc = jnp.zeros((ts, LANES), F32)
        for cs in chunks:
            x2 = x_ref[:, cs] + gt_ref[:, cs] * o_ref[:, cs]
            acc = acc + x2 * x2
        r2 = lax.rsqrt(_row_sum(acc) / d + EPS)
        acc = jnp.zeros((ts, LANES), F32)
        for cs in chunks:
            n2 = (x_ref[:, cs] + gt_ref[:, cs] * o_ref[:, cs]) * r2
            diff = n2 * fg_ref[:, cs] - t_ref[:, cs]
            loss_ref[:, cs] += jnp.sum(diff * diff, axis=0, keepdims=True)
            dyf = diff / d
            dfg_ref[:, cs] += jnp.sum(dyf * n2, axis=0, keepdims=True)
            acc = acc + (dyf * fg_ref[:, cs]) * n2
        mdot = _row_sum(acc) / d
        for cs in chunks:
            o = o_ref[:, cs]
            n2 = (x_ref[:, cs] + gt_ref[:, cs] * o) * r2
            dn = ((n2 * fg_ref[:, cs] - t_ref[:, cs]) / d) * fg_ref[:, cs]
            dx2 = r2 * (dn - n2 * mdot)
            dx2_ref[:, cs] = dx2
            dgate_ref[:, cs] += jnp.sum(dx2 * o, axis=0, keepdims=True)
            dout_ref[:, cs] = (dx2 * gt_ref[:, cs]).astype(BF16)

    tile = pl.BlockSpec((ts, d), lambda i: (i, 0))
    row = pl.BlockSpec((1, d), lambda i: (0, 0))
    return pl.pallas_call(
        body, name="post", grid=(s // ts,),
        in_specs=[tile, tile, tile, row, row], out_specs=[tile, tile, row, row, row],
        out_shape=[jax.ShapeDtypeStruct((s, d), F32), jax.ShapeDtypeStruct((s, d), BF16)] + [jax.ShapeDtypeStruct((1, d), F32)] * 3,
        compiler_params=_cparams(1),
    )(x, out, tgt, gate, fg)


def _mix_b_gate_bwd(u2, proj, dy, lg, lb, n_cols, ts):
    s, w = u2.shape
    chunks = _col_chunks(w)

    def body(u2_ref, bz, dyb, lg_ref, lb_ref, du2_ref, dbz_ref, dlg_ref, dlb_ref, dbb_ref):
        @pl.when(pl.program_id(0) == 0)
        def _():
            dlg_ref[...] = jnp.zeros_like(dlg_ref)
            dlb_ref[...] = jnp.zeros_like(dlb_ref)
            dbb_ref[...] = jnp.zeros_like(dbb_ref)

        mu, rstd = _ln_stats(u2_ref, ts, w, chunks)
        acc1 = jnp.zeros((ts, LANES), F32)
        acc2 = jnp.zeros((ts, LANES), F32)
        for cs in chunks:
            xh = (u2_ref[:, cs] - mu) * rstd
            ln = xh * lg_ref[:, cs] + lb_ref[:, cs]
            sl = _sigmoid(ln)
            z = bz[:, cs].astype(F32)
            sz = _sigmoid(z)
            g = dyb[:, cs].astype(F32)
            dbz_ref[:, cs] = (g * (ln * sl) * (sz * (1.0 + z * (1.0 - sz)))).astype(BF16)
            dln = g * (z * sz) * (sl * (1.0 + ln * (1.0 - sl)))
            dlg_ref[:, cs] += jnp.sum(dln * xh, axis=0, keepdims=True)
            dlb_ref[:, cs] += jnp.sum(dln, axis=0, keepdims=True)
            dxh = dln * lg_ref[:, cs]
            du2_ref[:, cs] = dxh
            acc1 = acc1 + dxh
            acc2 = acc2 + dxh * xh
        m1 = _row_sum(acc1) / w
        m2 = _row_sum(acc2) / w
        for cs in chunks:
            xh = (u2_ref[:, cs] - mu) * rstd
            du2 = rstd * (du2_ref[:, cs] - m1 - xh * m2)
            du2_ref[:, cs] = du2
            dbb_ref[:, cs] += jnp.sum(du2, axis=0, keepdims=True)

    tile = pl.BlockSpec((ts, w), lambda i: (i, 0))
    row = pl.BlockSpec((1, w), lambda i: (0, 0))
    return pl.pallas_call(
        body, name="mix_b_gate_bwd", grid=(s // ts,),
        in_specs=[tile, pl.BlockSpec((ts, w), lambda i: (i, 6)), pl.BlockSpec((ts, w), lambda i: (i, 1)), row, row],
        out_specs=[tile, pl.BlockSpec((ts, w), lambda i: (i, 6)), row, row, row],
        out_shape=[jax.ShapeDtypeStruct((s, w), F32), jax.ShapeDtypeStruct((s, n_cols), BF16)] + [jax.ShapeDtypeStruct((1, w), F32)] * 3,
        compiler_params=_cparams(1),
    )(u2, proj, dy, lg, lb)


def _mix_b_conv_bwd(du2, proj, wb, dproj, w, ts, cb):
    s = proj.shape[0]
    nb = w // cb
    ns = s // ts
    chunks = _col_chunks(cb)

    def body(du2_ref, bv, bg, wb_ref, dp_in, dp_ref, dwb_ref, ext, stash):
        del dp_in
        q = pl.program_id(2)

        @pl.when(jnp.logical_and(pl.program_id(1) == 0, q == 0))
        def _():
            ext[ts : ts + HALO_B, :] = jnp.zeros((HALO_B, cb), F32)
            dwb_ref[...] = jnp.zeros_like(dwb_ref)

        @pl.when(q == 0)
        def _():
            ext[0:ts, :] = du2_ref[...]
            for cs in chunks:
                v = bv[:, cs].astype(F32)
                sg = _sigmoid(bg[:, cs].astype(F32))
                u = v * sg
                du = jnp.zeros((ts, LANES), F32)
                for k in range(CONV_B):
                    o = CONV_B - 1 - k
                    win = ext[o : o + ts, cs]
                    du = du + wb_ref[k : k + 1, cs] * win
                    dwb_ref[k : k + 1, cs] += jnp.sum(u * win, axis=0, keepdims=True)
                dp_ref[:, cs] = (du * sg).astype(BF16)
                stash[:, cs] = (du * v * (sg * (1.0 - sg))).astype(BF16)
            ext[ts : ts + HALO_B, :] = ext[0:HALO_B, :]

        @pl.when(q == 1)
        def _():
            dp_ref[...] = stash[...]

    def rev(col):
        return lambda c, i, q: (ns - 1 - i, col(c))

    return pl.pallas_call(
        body, name="mix_b_conv_bwd", grid=(nb, ns, 2),
        in_specs=[pl.BlockSpec((ts, cb), rev(lambda c: c)), pl.BlockSpec((ts, cb), rev(lambda c: 4 * nb + c)),
                  pl.BlockSpec((ts, cb), rev(lambda c: 5 * nb + c)), pl.BlockSpec((CONV_B, cb), lambda c, i, q: (0, c)),
                  pl.BlockSpec(memory_space=pl.ANY)],
        out_specs=[pl.BlockSpec((ts, cb), lambda c, i, q: (ns - 1 - i, (4 + q) * nb + c)),
                   pl.BlockSpec((CONV_B, cb), lambda c, i, q: (0, c))],
        out_shape=[jax.ShapeDtypeStruct(dproj.shape, dproj.dtype), jax.ShapeDtypeStruct((CONV_B, w), F32)],
        input_output_aliases={4: 0},
        scratch_shapes=[pltpu.VMEM((ts + HALO_B, cb), F32), pltpu.VMEM((ts, cb), BF16)], compiler_params=_cparams(3),
    )(du2, proj, proj, wb, dproj)


def _mix_a_bwd(proj, cv, dy, wa, dproj, w, ts, cb):
    s = proj.shape[0]
    nb = w // cb
    ns = s // ts
    chunks = _col_chunks(cb)

    def body(ab, ac, ax, az, cv_ref, dya, wa_ref, dp_in, dp_ref, dwa_ref, ext, stash):
        del dp_in
        q = pl.program_id(2)

        @pl.when(jnp.logical_and(pl.program_id(1) == 0, q == 0))
        def _():
            ext[ts : ts + HALO_A, :] = jnp.zeros((HALO_A, cb), F32)
            dwa_ref[...] = jnp.zeros_like(dwa_ref)

        @pl.when(q == 0)
        def _():
            for cs in chunks:
                b = ab[:, cs].astype(F32)
                z = az[:, cs].astype(F32)
                sz = _sigmoid(z)
                g = dya[:, cs].astype(F32)
                conv = cv_ref[:, cs].astype(F32)
                dp_ref[:, cs] = (g * conv * (z * sz)).astype(BF16)
                stash[2, :, cs] = (g * b * conv * (sz * (1.0 + z * (1.0 - sz)))).astype(BF16)
                ext[0:ts, cs] = g * b * (z * sz)
            for cs in chunks:
                a_c = ac[:, cs].astype(F32)
                a_x = ax[:, cs].astype(F32)
                ca = a_c * a_x
                dca = jnp.zeros((ts, LANES), F32)
                for k in range(CONV_A):
                    o = CONV_A - 1 - k
                    win = ext[o : o + ts, cs]
                    dca = dca + wa_ref[k : k + 1, cs] * win
                    dwa_ref[k : k + 1, cs] += jnp.sum(ca * win, axis=0, keepdims=True)
                stash[0, :, cs] = (dca * a_x).astype(BF16)
                stash[1, :, cs] = (dca * a_c).astype(BF16)
            ext[ts : ts + HALO_A, :] = ext[0:HALO_A, :]

        for seg_q in range(1, 4):

            @pl.when(q == seg_q)
            def _(seg_q=seg_q):
                dp_ref[...] = stash[seg_q - 1]

    def rev(col):
        return lambda c, i, q: (ns - 1 - i, col(c))

    def seg(sq):
        return pl.BlockSpec((ts, cb), rev(lambda c: sq * nb + c))

    return pl.pallas_call(
        body, name="mix_a_bwd", grid=(nb, ns, 4),
        in_specs=[seg(0), seg(1), seg(2), seg(3), pl.BlockSpec((ts, cb), rev(lambda c: c)), pl.BlockSpec((ts, cb), rev(lambda c: c)),
                  pl.BlockSpec((CONV_A, cb), lambda c, i, q: (0, c)), pl.BlockSpec(memory_space=pl.ANY)],
        out_specs=[pl.BlockSpec((ts, cb), lambda c, i, q: (ns - 1 - i, q * nb + c)), pl.BlockSpec((CONV_A, cb), lambda c, i, q: (0, c))],
        out_shape=[jax.ShapeDtypeStruct(dproj.shape, dproj.dtype), jax.ShapeDtypeStruct((CONV_A, w), F32)],
        input_output_aliases={7: 0},
        scratch_shapes=[pltpu.VMEM((ts + HALO_A, cb), F32), pltpu.VMEM((3, ts, cb), BF16)], compiler_params=_cparams(3),
    )(proj, proj, proj, proj, cv, dy, wa, dproj)


def _pre_bwd(x, dh, dx2, g1, scale, ts):
    s, d = x.shape
    chunks = _col_chunks(d)

    def body(x_ref, dh_ref, dx2_ref, g_ref, sc_ref, gx_ref, dsh_ref, dsc_ref, dg_ref):
        @pl.when(pl.program_id(0) == 0)
        def _():
            dsh_ref[...] = jnp.zeros_like(dsh_ref)
            dsc_ref[...] = jnp.zeros_like(dsc_ref)
            dg_ref[...] = jnp.zeros_like(dg_ref)

        acc = jnp.zeros((ts, LANES), F32)
        for cs in chunks:
            v = x_ref[:, cs]
            acc = acc + v * v
        r1 = lax.rsqrt(_row_sum(acc) / d + EPS)
        acc = jnp.zeros((ts, LANES), F32)
        for cs in chunks:
            xn = x_ref[:, cs] * r1
            g = dh_ref[:, cs]
            dsh_ref[:, cs] += jnp.sum(g, axis=0, keepdims=True)
            dsc_ref[:, cs] += jnp.sum(g * (xn * g_ref[:, cs]), axis=0, keepdims=True)
            dn1 = g * (1.0 + sc_ref[:, cs])
            dg_ref[:, cs] += jnp.sum(dn1 * xn, axis=0, keepdims=True)
            acc = acc + (dn1 * g_ref[:, cs]) * xn
        mdot = _row_sum(acc) / d
        for cs in chunks:
            xn = x_ref[:, cs] * r1
            dxn = (dh_ref[:, cs] * (1.0 + sc_ref[:, cs])) * g_ref[:, cs]
            gx_ref[:, cs] = dx2_ref[:, cs] + r1 * (dxn - xn * mdot)

    tile = pl.BlockSpec((ts, d), lambda i: (i, 0))
    row = pl.BlockSpec((1, d), lambda i: (0, 0))
    return pl.pallas_call(
        body, name="pre_bwd", grid=(s // ts,),
        in_specs=[tile, tile, tile, row, row], out_specs=[tile, row, row, row],
        out_shape=[jax.ShapeDtypeStruct((s, d), F32)] + [jax.ShapeDtypeStruct((1, d), F32)] * 3,
        compiler_params=_cparams(1),
    )(x, dh, dx2, g1, scale)


def _adamw(w, g, m, v):
    m = ADAM_B1 * m + (1.0 - ADAM_B1) * g
    v = ADAM_B2 * v + (1.0 - ADAM_B2) * (g * g)
    m_hat = m / (1.0 - ADAM_B1**ADAM_STEP)
    v_hat = v / (1.0 - ADAM_B2**ADAM_STEP)
    delta = -ADAM_LR * (m_hat / (jnp.sqrt(v_hat) + ADAM_EPS) + ADAM_WD * w)
    return delta, m, v


def _combine_halves(part, swapped, core, tr):
    _, r, n = part.shape

    def body(core_ref, p_ref, s_ref, o_ref):
        del core_ref
        o_ref[...] = (p_ref[...].astype(F32) + s_ref[...].astype(F32)).astype(BF16)

    return pl.pallas_call(
        body, name="combine_halves",
        grid_spec=pltpu.PrefetchScalarGridSpec(
            num_scalar_prefetch=1, grid=(N_CHIP, r // tr),
            in_specs=[pl.BlockSpec((None, tr, n), lambda q, i, cr: (2 * q + cr[0], i, 0)), pl.BlockSpec((None, tr, n), lambda q, i, cr: (q, i, 0))],
            out_specs=pl.BlockSpec((None, tr, n), lambda q, i, cr: (q, i, 0))),
        out_shape=jax.ShapeDtypeStruct((N_CHIP, r, n), BF16), compiler_params=_cparams(2),
    )(core, part, swapped)


def _adam_sharded(w, m, v, comb, recv, chip, tr, name):
    r, n = w.shape

    def body(chip_ref, w_ref, m_ref, v_ref, c_ref, r_ref, g_ref, d_ref, nm_ref, nv_ref):
        del chip_ref
        g = c_ref[...].astype(F32) + r_ref[0].astype(F32) + r_ref[1].astype(F32) + r_ref[2].astype(F32)
        delta, nm, nv = _adamw(w_ref[...], g, m_ref[...], v_ref[...])
        g_ref[...] = g
        d_ref[...] = delta
        nm_ref[...] = nm
        nv_ref[...] = nv

    tile = pl.BlockSpec((tr, n), lambda i, ch: (i, 0))
    return pl.pallas_call(
        body, name=name,
        grid_spec=pltpu.PrefetchScalarGridSpec(
            num_scalar_prefetch=1, grid=(r // tr,),
            in_specs=[tile, tile, tile, pl.BlockSpec((None, tr, n), lambda i, ch: (ch[0], i, 0)), pl.BlockSpec((3, tr, n), lambda i, ch: (0, i, 0))],
            out_specs=[tile] * 4),
        out_shape=[jax.ShapeDtypeStruct((r, n), F32)] * 4, compiler_params=_cparams(1),
    )(chip, w, m, v, comb, recv)


def _adam_w_ada(c_t, dm_loc, w, m, v, tr):
    d, nb = c_t.shape
    na = w.shape[1]

    def body(c_ref, dm_ref, w_ref, m_ref, v_ref, g_ref, d_ref, nm_ref, nv_ref):
        g = jnp.dot(c_ref[...], dm_ref[...], preferred_element_type=F32, precision=lax.Precision.HIGHEST)
        delta, nm, nv = _adamw(w_ref[...], g, m_ref[...], v_ref[...])
        g_ref[...] = g
        d_ref[...] = delta
        nm_ref[...] = nm
        nv_ref[...] = nv

    tile = pl.BlockSpec((tr, na), lambda i: (i, 0))
    return pl.pallas_call(
        body, name="adam_w_ada", grid=(d // tr,),
        in_specs=[pl.BlockSpec((tr, nb), lambda i: (i, 0)), pl.BlockSpec((nb, na), lambda i: (0, 0)), tile, tile, tile],
        out_specs=[tile] * 4, out_shape=[jax.ShapeDtypeStruct((d, na), F32)] * 4, compiler_params=_cparams(1),
    )(c_t, dm_loc, w, m, v)


def _small_final(me, gathered, params, d, w, cl):
    flat_params = [t for p in params for t in p]

    def total(ref):
        acc = ref[0]
        for b in range(1, N_DEV):
            acc = acc + ref[b]
        return acc

    def body(me_ref, loss_g, dfg_g, dg1_g, dsh_g, dsc_g, dgt_g, dbb_g, dlg_g, dlb_g, dwa_g, dwb_g, *rest):
        del me_ref
        prm = rest[:24]
        loss_ref = rest[24]
        outs = rest[25:]

        loss_ref[...] = jnp.sum(total(loss_g), axis=1, keepdims=True) * (0.5 / d)

        def update(idx, g, cols=None):
            w_ref, m_ref, v_ref = prm[3 * idx : 3 * idx + 3]
            g_ref, d_ref, nm_ref, nv_ref = outs[4 * idx : 4 * idx + 4]
            sl = (slice(None), slice(None)) if cols is None else (slice(None), cols)
            delta, nm, nv = _adamw(w_ref[sl], g, m_ref[sl], v_ref[sl])
            g_ref[sl] = g
            d_ref[sl] = delta
            nm_ref[sl] = nm
            nv_ref[sl] = nv

        update(0, total(dg1_g))
        update(1, total(dsh_g), slice(0, d))
        update(1, total(dsc_g), slice(d, 2 * d))
        update(1, total(dgt_g), slice(2 * d, 3 * d))
        update(2, total(dwa_g))
        update(3, total(dwb_g))
        update(4, total(dbb_g))
        update(5, total(dlg_g))
        update(6, total(dlb_g))
        update(7, total(dfg_g))

    def full(shape):
        nd = len(shape)
        return pl.BlockSpec(shape, lambda i, mr: (0,) * nd)

    in_specs = [full(g.shape) for g in gathered[:9]]
    in_specs += [pl.BlockSpec((N_DEV, CONV_A, cl), lambda i, mr: (0, 0, mr[0])), pl.BlockSpec((N_DEV, CONV_B, cl), lambda i, mr: (0, 0, mr[0]))]
    in_specs += [full(t.shape) for t in flat_params]
    out_shapes = [jax.ShapeDtypeStruct((1, 1), F32)]
    for p in params:
        out_shapes += [jax.ShapeDtypeStruct(p[0].shape, F32)] * 4
    return pl.pallas_call(
        body, name="small_final",
        grid_spec=pltpu.PrefetchScalarGridSpec(
            num_scalar_prefetch=1, grid=(1,), in_specs=in_specs, out_specs=[full(o.shape) for o in out_shapes]),
        out_shape=out_shapes, compiler_params=_cparams(1),
    )(me, *gathered, *flat_params)


def _silu_rows(c):
    def body(c_ref, o_ref):
        v = c_ref[...]
        o_ref[...] = v * _sigmoid(v)

    return pl.pallas_call(body, name="silu_c", out_shape=jax.ShapeDtypeStruct(c.shape, F32), compiler_params=_cparams())(c)


def kernel(x, c, norm_g, w_ada, b_ada, w_in, conv_a_w, conv_b_w, conv_b_b, ln_b_g, ln_b_b, w_out, final_g, loss_target, m_norm_g, m_w_ada, m_b_ada, m_w_in, m_conv_a_w, m_conv_b_w, m_conv_b_b, m_ln_b_g, m_ln_b_b, m_w_out, m_final_g, v_norm_g, v_w_ada, v_b_ada, v_w_in, v_conv_a_w, v_conv_b_w, v_conv_b_b, v_ln_b_g, v_ln_b_b, v_w_out, v_final_g):
    _, s, d = x.shape
    w = conv_b_b.shape[-1]
    cl = conv_a_w.shape[-1]
    na = w_ada.shape[-1]
    assert ln_b_g.shape[-1] == w and w_out.shape[1] * N_DEV == 2 * w and w_in.shape[-1] * N_DEV == 7 * w and cl * N_DEV == w
    ts = min(256, s)
    ts_row = min(128, s)
    cb = min(512, w)
    tm = min(512, s)
    tn = min(1024, d)

    px, py, pc = _position()
    me = 4 * px + 2 * py + pc
    me_arr = jnp.reshape(me, (1,)).astype(jnp.int32)
    core_arr = jnp.reshape(pc, (1,)).astype(jnp.int32)
    chip_arr = jnp.reshape(2 * px + py, (1,)).astype(jnp.int32)

    x2d, tgt = x[0], loss_target[0]
    w_ada2, m_w_ada2, v_w_ada2 = w_ada[0], m_w_ada[0], v_w_ada[0]
    w_in2, m_w_in2, v_w_in2 = w_in[0], m_w_in[0], v_w_in[0]
    w_out2, m_w_out2, v_w_out2 = w_out[0], m_w_out[0], v_w_out[0]
    fg = final_g.reshape(1, d)

    c_act = _silu_rows(c)
    c_all, wa_all, wb_all = _all_gather_vmem([c_act, conv_a_w[0], conv_b_w[0]], "gather_small_in")
    c_all = c_all.reshape(N_DEV, d)
    wa = jnp.transpose(wa_all, (1, 0, 2)).reshape(CONV_A, w)
    wb = jnp.transpose(wb_all, (1, 0, 2)).reshape(CONV_B, w)
    w_in_full, w_out_full = _all_gather_hbm([w_in2.astype(BF16), w_out2.astype(BF16)], "gather_weights")
    w_out_full = w_out_full.reshape(2 * w, d)

    b_loc = lax.dynamic_slice_in_dim(b_ada, me * na, na, axis=1)
    mod_loc = _mod_matmul(c_all, w_ada2, b_loc, min(512, d))
    (mod_all,) = _all_gather_vmem([mod_loc], "gather_mod")
    mod_mine = lax.dynamic_index_in_dim(mod_all, me, axis=1, keepdims=False).reshape(1, 3 * d)
    shift, scale, gate = mod_mine[:, :d], mod_mine[:, d : 2 * d], mod_mine[:, 2 * d :]

    h = _prep(x2d, norm_g, scale, shift, ts_row)
    proj = _mm_proj(h, w_in_full, tm)
    y, cv = _mix_a_fwd(proj, wa, w, ts, cb)
    u2 = _mix_b_conv_fwd(proj, wb, conv_b_b, w, ts, cb)
    y = _mix_b_gate_fwd(u2, proj, ln_b_g, ln_b_b, y, ts)
    out = _mm_out(y, w_out_full, tm, tn)
    dx2, dout, loss_row, dgate_row, dfg_row = _post(x2d, out, tgt, gate, fg, ts_row)

    dy = _mm_dy(dout, w_out_full, tm, tn)
    part_out = _mm_dw_out(y, dout, tn)
    du2, dproj, dlg_row, dlb_row, dbb_row = _mix_b_gate_bwd(u2, proj, dy, ln_b_g, ln_b_b, 7 * w, ts)
    dproj, dwb_part = _mix_b_conv_bwd(du2, proj, wb, dproj, w, ts, cb)
    dproj, dwa_part = _mix_a_bwd(proj, cv, dy, wa, dproj, w, ts, cb)
    part_in = _mm_dw_in(h, dproj, tm)
    dh = _mm_dh(dproj, w_in_full, tm, min(2048, d))
    grad_x, dshift_row, dscale_row, dg1_row = _pre_bwd(x2d, dh, dx2, norm_g, scale, ts_row)

    swapped_in, swapped_out = _swap_with_sibling([part_in, part_out], "swap_halves")
    comb_in = _combine_halves(part_in, swapped_in, core_arr, min(256, d))
    comb_out = _combine_halves(part_out, swapped_out, core_arr, min(256, part_out.shape[1]))
    recv_in, recv_out = _send_to_chips([comb_in, comb_out], "send_to_chips")
    g_w_in, d_w_in, nm_w_in, nv_w_in = _adam_sharded(w_in2, m_w_in2, v_w_in2, comb_in, recv_in, chip_arr, min(128, d), "adam_w_in")
    g_w_out, d_w_out, nm_w_out, nv_w_out = _adam_sharded(w_out2, m_w_out2, v_w_out2, comb_out, recv_out, chip_arr, min(128, w_out2.shape[0]), "adam_w_out")

    gathered = _all_gather_vmem(
        [loss_row, dfg_row, dg1_row, dshift_row, dscale_row, dgate_row, dbb_row, dlg_row, dlb_row, dwa_part, dwb_part], "gather_small_grads")
    dmod_all = jnp.concatenate([gathered[3].reshape(N_DEV, d), gathered[4].reshape(N_DEV, d), gathered[5].reshape(N_DEV, d)], axis=1)
    dm_loc = lax.dynamic_slice_in_dim(dmod_all, me * na, na, axis=1)
    g_w_ada, d_w_ada, nm_w_ada, nv_w_ada = _adam_w_ada(jnp.transpose(c_all), dm_loc, w_ada2, m_w_ada2, v_w_ada2, min(256, d))

    params = [
        (norm_g, m_norm_g, v_norm_g), (b_ada, m_b_ada, v_b_ada), (conv_a_w[0], m_conv_a_w[0], v_conv_a_w[0]),
        (conv_b_w[0], m_conv_b_w[0], v_conv_b_w[0]), (conv_b_b, m_conv_b_b, v_conv_b_b), (ln_b_g, m_ln_b_g, v_ln_b_g),
        (ln_b_b, m_ln_b_b, v_ln_b_b), (fg, m_final_g.reshape(1, d), v_final_g.reshape(1, d)),
    ]
    small = _small_final(me_arr, gathered, params, d, w, cl)
    loss = small[0].reshape(())
    sm = [small[1 + 4 * i : 5 + 4 * i] for i in range(8)]

    def pick(k):
        return [
            sm[0][k], (g_w_ada, d_w_ada, nm_w_ada, nv_w_ada)[k][None], sm[1][k], (g_w_in, d_w_in, nm_w_in, nv_w_in)[k][None],
            sm[2][k][None], sm[3][k][None], sm[4][k], sm[5][k], sm[6][k], (g_w_out, d_w_out, nm_w_out, nv_w_out)[k][None],
            sm[7][k].reshape(d),
        ]

    return (loss, grad_x[None], *pick(0), *pick(1), *pick(2), *pick(3))
```

```python
import functools

import jax
import jax.numpy as jnp
from jax import lax
from jax.experimental import pallas as pl
from jax.experimental.pallas import tpu as pltpu

F32 = jnp.float32
BF16 = jnp.bfloat16
N_DEV = 8
N_CHIP = 4
EPS = 1e-6
ADAM_LR = 0.001
ADAM_B1 = 0.9
ADAM_B2 = 0.999
ADAM_EPS = 1e-08
ADAM_WD = 0.01
ADAM_STEP = 10
CONV_A = 3
CONV_B = 31
HALO_A = 8
HALO_B = 32
LANES = 128
VMEM_LIMIT = 56 * 1024 * 1024
MESH = pl.DeviceIdType.MESH


def _cparams(n_grid_axes=0):
    if n_grid_axes:
        return pltpu.CompilerParams(dimension_semantics=("arbitrary",) * n_grid_axes, vmem_limit_bytes=VMEM_LIMIT)
    return pltpu.CompilerParams(vmem_limit_bytes=VMEM_LIMIT)


def _flip(v, bit):
    return 1 - v if bit else v


def _position():
    return lax.axis_index("x"), lax.axis_index("y"), lax.axis_index("c")


def _sigmoid(v):
    return jax.nn.sigmoid(v)


def _all_gather_vmem(arrs, name):
    n = len(arrs)

    def body(*refs):
        ins, outs = refs[:n], refs[n : 2 * n]
        send_sems, recv_sems = refs[2 * n :]
        x, y, c = _position()
        me = 4 * x + 2 * y + c
        for a in range(n):
            outs[a][me] = ins[a][...]
        sends = []
        for k in range(1, N_DEV):
            peer = (_flip(x, k & 4), _flip(y, k & 2), _flip(c, k & 1))
            for a in range(n):
                cp = pltpu.make_async_remote_copy(
                    src_ref=ins[a], dst_ref=outs[a].at[me], send_sem=send_sems.at[a, k - 1], recv_sem=recv_sems.at[a, k - 1],
                    device_id=peer, device_id_type=MESH)
                cp.start()
                sends.append(cp)
        for k in range(1, N_DEV):
            peer = (_flip(x, k & 4), _flip(y, k & 2), _flip(c, k & 1))
            src = 4 * peer[0] + 2 * peer[1] + peer[2]
            for a in range(n):
                pltpu.make_async_remote_copy(
                    src_ref=ins[a], dst_ref=outs[a].at[src], send_sem=send_sems.at[a, k - 1], recv_sem=recv_sems.at[a, k - 1],
                    device_id=peer, device_id_type=MESH).wait_recv()
        for cp in sends:
            cp.wait_send()

    vm = pl.BlockSpec(memory_space=pltpu.VMEM)
    return pl.pallas_call(
        body, name=name,
        out_shape=[jax.ShapeDtypeStruct((N_DEV,) + a.shape, a.dtype) for a in arrs],
        in_specs=[vm] * n, out_specs=[vm] * n,
        scratch_shapes=[pltpu.SemaphoreType.DMA((n, N_DEV - 1)), pltpu.SemaphoreType.DMA((n, N_DEV - 1))],
        compiler_params=_cparams(),
    )(*arrs)


def _all_gather_hbm(arrs, name):
    n = len(arrs)

    def body(*refs):
        ins, outs = refs[:n], refs[n : 2 * n]
        send_sems, recv_sems, local_sems = refs[2 * n :]
        x, y, c = _position()
        me = (x, y, c)
        sibling = (x, y, 1 - c)
        chips = [(1 - x, y), (x, 1 - y), (1 - x, 1 - y)]

        def slot(p):
            return 4 * p[0] + 2 * p[1] + p[2]

        def copy(a, k, block, to, src=None):
            dst = outs[a].at[slot(block)]
            return pltpu.make_async_remote_copy(
                src_ref=dst if src is None else src, dst_ref=dst, send_sem=send_sems.at[a, k], recv_sem=recv_sems.at[a, k],
                device_id=to, device_id_type=MESH)

        mine = [pltpu.make_async_copy(ins[a], outs[a].at[slot(me)], local_sems.at[a]) for a in range(n)]
        for cp in mine:
            cp.start()
        first = []
        for a in range(n):
            first.append(copy(a, 0, me, sibling, src=ins[a]))
            first += [copy(a, 1 + j, me, (*chip, c), src=ins[a]) for j, chip in enumerate(chips)]
        for cp in first:
            cp.start()
        passed = []
        for j, chip in enumerate(chips):
            for a in range(n):
                copy(a, 1 + j, (*chip, c), me).wait_recv()
                cp = copy(a, 4 + j, (*chip, c), sibling)
                cp.start()
                passed.append(cp)
        for a in range(n):
            copy(a, 0, sibling, me).wait_recv()
            for j, chip in enumerate(chips):
                copy(a, 4 + j, (*chip, 1 - c), me).wait_recv()
        for cp in first + passed:
            cp.wait_send()
        for cp in mine:
            cp.wait()

    hbm = pl.BlockSpec(memory_space=pl.ANY)
    return pl.pallas_call(
        body, name=name,
        out_shape=[jax.ShapeDtypeStruct((N_DEV,) + a.shape, a.dtype) for a in arrs],
        in_specs=[hbm] * n, out_specs=[hbm] * n,
        scratch_shapes=[pltpu.SemaphoreType.DMA((n, 7)), pltpu.SemaphoreType.DMA((n, 7)), pltpu.SemaphoreType.DMA((n,))],
        compiler_params=_cparams(),
    )(*arrs)


def _swap_with_sibling(parts, name):
    n = len(parts)

    def body(*refs):
        ins, outs = refs[:n], refs[n : 2 * n]
        send_sems, recv_sems = refs[2 * n :]
        x, y, c = _position()
        sibling = (x, y, 1 - c)
        sends = []
        for a in range(n):
            for q in range(N_CHIP):
                cp = pltpu.make_async_remote_copy(
                    src_ref=ins[a].at[2 * q + (1 - c)], dst_ref=outs[a].at[q], send_sem=send_sems.at[a, q],
                    recv_sem=recv_sems.at[a, q], device_id=sibling, device_id_type=MESH)
                cp.start()
                sends.append(cp)
        for cp in sends:
            cp.wait_recv()
        for cp in sends:
            cp.wait_send()

    hbm = pl.BlockSpec(memory_space=pl.ANY)
    return pl.pallas_call(
        body, name=name,
        out_shape=[jax.ShapeDtypeStruct((N_CHIP,) + p.shape[1:], p.dtype) for p in parts],
        in_specs=[hbm] * n, out_specs=[hbm] * n,
        scratch_shapes=[pltpu.SemaphoreType.DMA((n, N_CHIP)), pltpu.SemaphoreType.DMA((n, N_CHIP))],
        compiler_params=_cparams(),
    )(*parts)


def _send_to_chips(combs, name):
    n = len(combs)

    def body(*refs):
        ins, outs = refs[:n], refs[n : 2 * n]
        send_sems, recv_sems = refs[2 * n :]
        x, y, c = _position()
        chips = [(1 - x, y), (x, 1 - y), (1 - x, 1 - y)]
        sends = []
        for a in range(n):
            for j, chip in enumerate(chips):
                cp = pltpu.make_async_remote_copy(
                    src_ref=ins[a].at[2 * chip[0] + chip[1]], dst_ref=outs[a].at[j], send_sem=send_sems.at[a, j],
                    recv_sem=recv_sems.at[a, j], device_id=(*chip, c), device_id_type=MESH)
                cp.start()
                sends.append(cp)
        for cp in sends:
            cp.wait_recv()
        for cp in sends:
            cp.wait_send()

    hbm = pl.BlockSpec(memory_space=pl.ANY)
    return pl.pallas_call(
        body, name=name,
        out_shape=[jax.ShapeDtypeStruct((3,) + p.shape[1:], p.dtype) for p in combs],
        in_specs=[hbm] * n, out_specs=[hbm] * n,
        scratch_shapes=[pltpu.SemaphoreType.DMA((n, 3)), pltpu.SemaphoreType.DMA((n, 3))],
        compiler_params=_cparams(),
    )(*combs)


NN = (((1,), (0,)), ((), ()))
NT = (((1,), (1,)), ((), ()))
TN = (((0,), (0,)), ((), ()))


def _n_chunk(n):
    return 256 if n % 256 == 0 else LANES


def _mm_body(dims, n_out, accumulate):
    nc = _n_chunk(n_out)

    def body(a_ref, b_ref, o_ref):
        for n0 in range(0, n_out, nc):
            b = b_ref[n0 : n0 + nc, :] if dims is NT else b_ref[:, n0 : n0 + nc]
            r = lax.dot_general(a_ref[...], b, dims, preferred_element_type=F32)
            if accumulate:
                k = pl.program_id(2)

                @pl.when(k == 0)
                def _():
                    o_ref[:, n0 : n0 + nc] = r

                @pl.when(k > 0)
                def _():
                    o_ref[:, n0 : n0 + nc] += r
            else:
                o_ref[:, n0 : n0 + nc] = r.astype(o_ref.dtype)

    return body


def _arrival_order():
    x, y, c = _position()
    blocks = [(x, y, c), (x, y, 1 - c)]
    for chip in [(1 - x, y), (x, 1 - y), (1 - x, 1 - y)]:
        blocks += [(*chip, c), (*chip, 1 - c)]
    return jnp.stack([4 * b[0] + 2 * b[1] + b[2] for b in blocks]).astype(jnp.int32)


def _proj_gather(h, w_in_loc, w_out_loc, order, tm):
    s, d = h.shape
    ne = w_in_loc.shape[1]
    ni = s // tm
    nc = _n_chunk(ne)

    def body(order_ref, h_ref, win_ref, wout_ref, proj_ref, fin_ref, fout_ref, bbuf, send_sems, recv_sems, local_sems, load_sems):
        ins, fulls = [win_ref, wout_ref], [fin_ref, fout_ref]
        jj, i = pl.program_id(0), pl.program_id(1)
        x, y, c = _position()
        me = (x, y, c)
        sibling = (x, y, 1 - c)
        chips = [(1 - x, y), (x, 1 - y), (1 - x, 1 - y)]

        def slot(p):
            return 4 * p[0] + 2 * p[1] + p[2]

        def copy(a, k, block, to, src=None):
            dst = fulls[a].at[slot(block)]
            return pltpu.make_async_remote_copy(
                src_ref=dst if src is None else src, dst_ref=dst, send_sem=send_sems.at[a, k], recv_sem=recv_sems.at[a, k],
                device_id=to, device_id_type=MESH)

        def own(a):
            return pltpu.make_async_copy(ins[a], fulls[a].at[slot(me)], local_sems.at[a])

        def load(src, buf):
            return pltpu.make_async_copy(src, bbuf.at[buf], load_sems.at[buf])

        @pl.when((jj == 0) & (i == 0))
        def _():
            load(win_ref, 0).start()
            for a in range(2):
                own(a).start()
            for a in range(2):
                copy(a, 0, me, sibling, src=ins[a]).start()
                for j, chip in enumerate(chips):
                    copy(a, 1 + j, me, (*chip, c), src=ins[a]).start()
            load(win_ref, 0).wait()

        for nxt in range(1, N_DEV):

            @pl.when((jj == nxt - 1) & (i == ni - 1))
            def _(nxt=nxt):
                if nxt == 1:
                    copy(0, 0, sibling, me).wait_recv()
                elif nxt % 2 == 0:
                    j = nxt // 2 - 1
                    copy(0, 1 + j, (*chips[j], c), me).wait_recv()
                    copy(0, 4 + j, (*chips[j], c), sibling).start()
                else:
                    j = nxt // 2 - 1
                    copy(0, 4 + j, (*chips[j], 1 - c), me).wait_recv()
                load(fin_ref.at[order_ref[nxt]], nxt % 2).start()

        @pl.when((jj > 0) & (i == 0))
        def _():
            load(fin_ref.at[0], jj % 2).wait()

        for n0 in range(0, ne, nc):
            proj_ref[:, n0 : n0 + nc] = lax.dot_general(
                h_ref[...], bbuf[jj % 2, :, n0 : n0 + nc], NN, preferred_element_type=F32).astype(BF16)

        @pl.when((jj == N_DEV - 1) & (i == ni - 1))
        def _():
            for j, chip in enumerate(chips):
                copy(1, 1 + j, (*chip, c), me).wait_recv()
                copy(1, 4 + j, (*chip, c), sibling).start()
            copy(1, 0, sibling, me).wait_recv()
            for j, chip in enumerate(chips):
                copy(1, 4 + j, (*chip, 1 - c), me).wait_recv()
            for a in range(2):
                copy(a, 0, me, sibling, src=ins[a]).wait_send()
                for j, chip in enumerate(chips):
                    copy(a, 1 + j, me, (*chip, c), src=ins[a]).wait_send()
                    copy(a, 4 + j, (*chip, c), sibling).wait_send()
                own(a).wait()

    hbm = pl.BlockSpec(memory_space=pl.ANY)
    return pl.pallas_call(
        body, name="proj_gather",
        grid_spec=pltpu.PrefetchScalarGridSpec(
            num_scalar_prefetch=1, grid=(N_DEV, ni),
            in_specs=[pl.BlockSpec((tm, d), lambda jj, i, od: (i, 0)), hbm, hbm],
            out_specs=[pl.BlockSpec((tm, ne), lambda jj, i, od: (i, od[jj])), hbm, hbm],
            scratch_shapes=[pltpu.VMEM((2, d, ne), BF16), pltpu.SemaphoreType.DMA((2, 7)), pltpu.SemaphoreType.DMA((2, 7)),
                            pltpu.SemaphoreType.DMA((2,)), pltpu.SemaphoreType.DMA((2,))]),
        out_shape=[jax.ShapeDtypeStruct((s, N_DEV * ne), BF16), jax.ShapeDtypeStruct((N_DEV,) + w_in_loc.shape, BF16),
                   jax.ShapeDtypeStruct((N_DEV,) + w_out_loc.shape, BF16)],
        compiler_params=_cparams(2),
    )(order, h, w_in_loc, w_out_loc)


def _mm_out(y, w_out, tm, tn):
    s, m = y.shape
    _, d = w_out.shape
    return pl.pallas_call(
        _mm_body(NN, tn, False), name="mm_out", grid=(d // tn, s // tm),
        in_specs=[pl.BlockSpec((tm, m), lambda j, i: (i, 0)), pl.BlockSpec((m, tn), lambda j, i: (0, j))],
        out_specs=pl.BlockSpec((tm, tn), lambda j, i: (i, j)),
        out_shape=jax.ShapeDtypeStruct((s, d), F32), compiler_params=_cparams(2),
    )(y, w_out)


def _mm_dy(dout, w_out, tm, tn):
    s, d = dout.shape
    m, _ = w_out.shape
    return pl.pallas_call(
        _mm_body(NT, tn, False), name="mm_dy", grid=(m // tn, s // tm),
        in_specs=[pl.BlockSpec((tm, d), lambda j, i: (i, 0)), pl.BlockSpec((tn, d), lambda j, i: (j, 0))],
        out_specs=pl.BlockSpec((tm, tn), lambda j, i: (i, j)),
        out_shape=jax.ShapeDtypeStruct((s, m), BF16), compiler_params=_cparams(2),
    )(dout, w_out)


def _mm_dw_out(y, dout, tn):
    s, m = y.shape
    _, d = dout.shape
    mr = m // N_DEV
    return pl.pallas_call(
        _mm_body(TN, tn, False), name="mm_dw_out", grid=(d // tn, N_DEV),
        in_specs=[pl.BlockSpec((s, mr), lambda j, i: (0, i)), pl.BlockSpec((s, tn), lambda j, i: (0, j))],
        out_specs=pl.BlockSpec((None, mr, tn), lambda j, i: (i, 0, j)),
        out_shape=jax.ShapeDtypeStruct((N_DEV, mr, d), BF16), compiler_params=_cparams(2),
    )(y, dout)


def _mm_dw_in(h, dproj, tm):
    s, d = h.shape
    ne = dproj.shape[1] // N_DEV
    return pl.pallas_call(
        _mm_body(TN, ne, False), name="mm_dw_in", grid=(N_DEV, d // tm),
        in_specs=[pl.BlockSpec((s, tm), lambda j, i: (0, i)), pl.BlockSpec((s, ne), lambda j, i: (0, j))],
        out_specs=pl.BlockSpec((None, tm, ne), lambda j, i: (j, i, 0)),
        out_shape=jax.ShapeDtypeStruct((N_DEV, d, ne), BF16), compiler_params=_cparams(2),
    )(h, dproj)


def _mm_dh(dproj, w_full, combs, tm, tn):
    s = dproj.shape[0]
    _, d, ne = w_full.shape
    kb = 2
    nk = N_DEV // kb
    nc = _n_chunk(tn)
    mh = min(512, tm)
    nq = len(combs)
    grid = (s // tm, d // tn, nk)

    def body(*refs):
        a_ref, b_ref = refs[:2]
        c_refs = refs[2 : 2 + nq]
        o_ref = refs[2 + nq]
        r_refs = refs[3 + nq : 3 + 2 * nq]
        send_sems, recv_sems = refs[3 + 2 * nq :]
        i, n, k = pl.program_id(0), pl.program_id(1), pl.program_id(2)
        x, y, c = _position()
        chips = [(1 - x, y), (x, 1 - y), (1 - x, 1 - y)]

        def copies():
            return [
                pltpu.make_async_remote_copy(
                    src_ref=c_refs[a].at[2 * chip[0] + chip[1]], dst_ref=r_refs[a].at[j], send_sem=send_sems.at[a, j],
                    recv_sem=recv_sems.at[a, j], device_id=(*chip, c), device_id_type=MESH)
                for a in range(nq) for j, chip in enumerate(chips)]

        @pl.when((i == 0) & (n == 0) & (k == 0))
        def _():
            for cp in copies():
                cp.start()

        @pl.when(k == 0)
        def _():
            o_ref[...] = jnp.zeros_like(o_ref)

        for m0 in range(0, tm, mh):
            for n0 in range(0, tn, nc):
                r = None
                for jj in range(kb):
                    t = lax.dot_general(a_ref[m0 : m0 + mh, jj * ne : (jj + 1) * ne], b_ref[jj, n0 : n0 + nc, :], NT, preferred_element_type=F32)
                    r = t if r is None else r + t
                o_ref[m0 : m0 + mh, n0 : n0 + nc] += r

        @pl.when((i == grid[0] - 1) & (n == grid[1] - 1) & (k == nk - 1))
        def _():
            for cp in copies():
                cp.wait_recv()
            for cp in copies():
                cp.wait_send()

    hbm = pl.BlockSpec(memory_space=pl.ANY)
    outs = pl.pallas_call(
        body, name="mm_dh", grid=grid,
        in_specs=[pl.BlockSpec((tm, kb * ne), lambda i, n, k: (i, k)), pl.BlockSpec((kb, tn, ne), lambda i, n, k: (k, n, 0))] + [hbm] * nq,
        out_specs=[pl.BlockSpec((tm, tn), lambda i, n, k: (i, n))] + [hbm] * nq,
        out_shape=[jax.ShapeDtypeStruct((s, d), F32)] + [jax.ShapeDtypeStruct((3,) + cb.shape[1:], cb.dtype) for cb in combs],
        scratch_shapes=[pltpu.SemaphoreType.DMA((nq, 3)), pltpu.SemaphoreType.DMA((nq, 3))],
        compiler_params=_cparams(3),
    )(dproj, w_full, *combs)
    return outs[0], outs[1:]


def _mod_matmul(c_all, w_ada, b_loc, tk):
    nb, d = c_all.shape
    na = w_ada.shape[1]

    def body(c_ref, w_ref, b_ref, o_ref):
        k = pl.program_id(0)
        r = jnp.dot(c_ref[...], w_ref[...], preferred_element_type=F32, precision=lax.Precision.HIGHEST)

        @pl.when(k == 0)
        def _():
            o_ref[...] = r + b_ref[...]

        @pl.when(k > 0)
        def _():
            o_ref[...] += r

    return pl.pallas_call(
        body, name="mod_matmul", grid=(d // tk,),
        in_specs=[pl.BlockSpec((nb, tk), lambda k: (0, k)), pl.BlockSpec((tk, na), lambda k: (k, 0)), pl.BlockSpec((1, na), lambda k: (0, 0))],
        out_specs=pl.BlockSpec((nb, na), lambda k: (0, 0)),
        out_shape=jax.ShapeDtypeStruct((nb, na), F32), compiler_params=_cparams(1),
    )(c_all, w_ada, b_loc)


def _col_chunks(width):
    return [slice(c0, c0 + LANES) for c0 in range(0, width, LANES)]


def _row_sum(acc):
    return jnp.sum(acc, axis=1, keepdims=True)


def _prep(x, g1, scale, shift, ts):
    s, d = x.shape
    chunks = _col_chunks(d)

    def body(x_ref, g_ref, sc_ref, sh_ref, h_ref):
        acc = jnp.zeros((ts, LANES), F32)
        for cs in chunks:
            v = x_ref[:, cs]
            acc = acc + v * v
        r = lax.rsqrt(_row_sum(acc) / d + EPS)
        for cs in chunks:
            n1 = (x_ref[:, cs] * r) * g_ref[:, cs]
            h_ref[:, cs] = (n1 * (1.0 + sc_ref[:, cs]) + sh_ref[:, cs]).astype(BF16)

    row = pl.BlockSpec((1, d), lambda i: (0, 0))
    return pl.pallas_call(
        body, name="prep", grid=(s // ts,),
        in_specs=[pl.BlockSpec((ts, d), lambda i: (i, 0)), row, row, row],
        out_specs=pl.BlockSpec((ts, d), lambda i: (i, 0)),
        out_shape=jax.ShapeDtypeStruct((s, d), BF16), compiler_params=_cparams(1),
    )(x, g1, scale, shift)


def _mix_a_fwd(proj, wa, w, ts, cb):
    s = proj.shape[0]
    nb = w // cb
    chunks = _col_chunks(cb)

    def body(ab, ac, ax, az, wa_ref, y_ref, cv_ref, ext):
        @pl.when(pl.program_id(1) == 0)
        def _():
            ext[0:HALO_A, :] = jnp.zeros((HALO_A, cb), F32)

        for cs in chunks:
            ext[HALO_A : HALO_A + ts, cs] = ac[:, cs].astype(F32) * ax[:, cs].astype(F32)
        for cs in chunks:
            cv = wa_ref[0:1, cs] * ext[HALO_A - 2 : HALO_A - 2 + ts, cs]
            cv = cv + wa_ref[1:2, cs] * ext[HALO_A - 1 : HALO_A - 1 + ts, cs]
            cv = cv + wa_ref[2:3, cs] * ext[HALO_A : HALO_A + ts, cs]
            z = az[:, cs].astype(F32)
            y_ref[:, cs] = (ab[:, cs].astype(F32) * cv * (z * _sigmoid(z))).astype(BF16)
            cv_ref[:, cs] = cv.astype(BF16)
        ext[0:HALO_A, :] = ext[ts : ts + HALO_A, :]

    def seg(q):
        return pl.BlockSpec((ts, cb), lambda c, i: (i, q * nb + c))

    return pl.pallas_call(
        body, name="mix_a_fwd", grid=(nb, s // ts),
        in_specs=[seg(0), seg(1), seg(2), seg(3), pl.BlockSpec((CONV_A, cb), lambda c, i: (0, c))],
        out_specs=[pl.BlockSpec((ts, cb), lambda c, i: (i, c)), pl.BlockSpec((ts, cb), lambda c, i: (i, c))],
        out_shape=[jax.ShapeDtypeStruct((s, 2 * w), BF16), jax.ShapeDtypeStruct((s, w), BF16)],
        scratch_shapes=[pltpu.VMEM((HALO_A + ts, cb), F32)], compiler_params=_cparams(2),
    )(proj, proj, proj, proj, wa)


def _mix_b_conv_fwd(proj, wb, bb, w, ts, cb):
    s = proj.shape[0]
    nb = w // cb
    chunks = _col_chunks(cb)

    def body(bv, bg, wb_ref, bb_ref, u2_ref, ext):
        @pl.when(pl.program_id(1) == 0)
        def _():
            ext[0:HALO_B, :] = jnp.zeros((HALO_B, cb), F32)

        for cs in chunks:
            ext[HALO_B : HALO_B + ts, cs] = bv[:, cs].astype(F32) * _sigmoid(bg[:, cs].astype(F32))
        for cs in chunks:
            acc = jnp.zeros((ts, LANES), F32) + bb_ref[:, cs]
            for k in range(CONV_B):
                o = HALO_B - (CONV_B - 1) + k
                acc = acc + wb_ref[k : k + 1, cs] * ext[o : o + ts, cs]
            u2_ref[:, cs] = acc
        ext[0:HALO_B, :] = ext[ts : ts + HALO_B, :]

    return pl.pallas_call(
        body, name="mix_b_conv_fwd", grid=(nb, s // ts),
        in_specs=[pl.BlockSpec((ts, cb), lambda c, i: (i, 4 * nb + c)), pl.BlockSpec((ts, cb), lambda c, i: (i, 5 * nb + c)),
                  pl.BlockSpec((CONV_B, cb), lambda c, i: (0, c)), pl.BlockSpec((1, cb), lambda c, i: (0, c))],
        out_specs=pl.BlockSpec((ts, cb), lambda c, i: (i, c)),
        out_shape=jax.ShapeDtypeStruct((s, w), F32),
        scratch_shapes=[pltpu.VMEM((HALO_B + ts, cb), F32)], compiler_params=_cparams(2),
    )(proj, proj, wb, bb)


def _ln_stats(u2_ref, ts, w, chunks):
    acc = jnp.zeros((ts, LANES), F32)
    for cs in chunks:
        acc = acc + u2_ref[:, cs]
    mu = _row_sum(acc) / w
    acc = jnp.zeros((ts, LANES), F32)
    for cs in chunks:
        xc = u2_ref[:, cs] - mu
        acc = acc + xc * xc
    return mu, lax.rsqrt(_row_sum(acc) / w + EPS)


def _mix_b_gate_fwd(u2, proj, lg, lb, y, ts):
    s, w = u2.shape
    chunks = _col_chunks(w)

    def body(u2_ref, bz, lg_ref, lb_ref, y_in, y_ref):
        del y_in
        mu, rstd = _ln_stats(u2_ref, ts, w, chunks)
        for cs in chunks:
            ln = (u2_ref[:, cs] - mu) * rstd * lg_ref[:, cs] + lb_ref[:, cs]
            z = bz[:, cs].astype(F32)
            y_ref[:, cs] = ((ln * _sigmoid(ln)) * (z * _sigmoid(z))).astype(BF16)

    row = pl.BlockSpec((1, w), lambda i: (0, 0))
    return pl.pallas_call(
        body, name="mix_b_gate_fwd", grid=(s // ts,),
        in_specs=[pl.BlockSpec((ts, w), lambda i: (i, 0)), pl.BlockSpec((ts, w), lambda i: (i, 6)), row, row,
                  pl.BlockSpec(memory_space=pl.ANY)],
        out_specs=pl.BlockSpec((ts, w), lambda i: (i, 1)),
        out_shape=jax.ShapeDtypeStruct(y.shape, y.dtype), input_output_aliases={4: 0}, compiler_params=_cparams(1),
    )(u2, proj, lg, lb, y)


def _post(x, out, tgt, gate, fg, ts):
    s, d = x.shape
    chunks = _col_chunks(d)

    def body(x_ref, o_ref, t_ref, gt_ref, fg_ref, dx2_ref, dout_ref, loss_ref, dgate_ref, dfg_ref):
        @pl.when(pl.program_id(0) == 0)
        def _():
            loss_ref[...] = jnp.zeros_like(loss_ref)
            dgate_ref[...] = jnp.zeros_like(dgate_ref)
            dfg_ref[...] = jnp.zeros_like(dfg_ref)

        acc = jnp.zeros((ts, LANES), F32)
        for cs in chunks:
            x2 = x_ref[:, cs] + gt_ref[:, cs] * o_ref[:, cs]
            acc = acc + x2 * x2
        r2 = lax.rsqrt(_row_sum(acc) / d + EPS)
        acc = jnp.zeros((ts, LANES), F32)
        for cs in chunks:
            n2 = (x_ref[:, cs] + gt_ref[:, cs] * o_ref[:, cs]) * r2
            diff = n2 * fg_ref[:, cs] - t_ref[:, cs]
            loss_ref[:, cs] += jnp.sum(diff * diff, axis=0, keepdims=True)
            dyf = diff / d
            dfg_ref[:, cs] += jnp.sum(dyf * n2, axis=0, keepdims=True)
            acc = acc + (dyf * fg_ref[:, cs]) * n2
        mdot = _row_sum(acc) / d
        for cs in chunks:
            o = o_ref[:, cs]
            n2 = (x_ref[:, cs] + gt_ref[:, cs] * o) * r2
            dn = ((n2 * fg_ref[:, cs] - t_ref[:, cs]) / d) * fg_ref[:, cs]
            dx2 = r2 * (dn - n2 * mdot)
            dx2_ref[:, cs] = dx2
            dgate_ref[:, cs] += jnp.sum(dx2 * o, axis=0, keepdims=True)
            dout_ref[:, cs] = (dx2 * gt_ref[:, cs]).astype(BF16)

    tile = pl.BlockSpec((ts, d), lambda i: (i, 0))
    row = pl.BlockSpec((1, d), lambda i: (0, 0))
    return pl.pallas_call(
        body, name="post", grid=(s // ts,),
        in_specs=[tile, tile, tile, row, row], out_specs=[tile, tile, row, row, row],
        out_shape=[jax.ShapeDtypeStruct((s, d), F32), jax.ShapeDtypeStruct((s, d), BF16)] + [jax.ShapeDtypeStruct((1, d), F32)] * 3,
        compiler_params=_cparams(1),
    )(x, out, tgt, gate, fg)


def _mix_b_gate_bwd(u2, proj, dy, lg, lb, n_cols, ts):
    s, w = u2.shape
    chunks = _col_chunks(w)

    def body(u2_ref, bz, dyb, lg_ref, lb_ref, du2_ref, dbz_ref, dlg_ref, dlb_ref, dbb_ref):
        @pl.when(pl.program_id(0) == 0)
        def _():
            dlg_ref[...] = jnp.zeros_like(dlg_ref)
            dlb_ref[...] = jnp.zeros_like(dlb_ref)
            dbb_ref[...] = jnp.zeros_like(dbb_ref)

        mu, rstd = _ln_stats(u2_ref, ts, w, chunks)
        acc1 = jnp.zeros((ts, LANES), F32)
        acc2 = jnp.zeros((ts, LANES), F32)
        for cs in chunks:
            xh = (u2_ref[:, cs] - mu) * rstd
            ln = xh * lg_ref[:, cs] + lb_ref[:, cs]
            sl = _sigmoid(ln)
            z = bz[:, cs].astype(F32)
            sz = _sigmoid(z)
            g = dyb[:, cs].astype(F32)
            dbz_ref[:, cs] = (g * (ln * sl) * (sz * (1.0 + z * (1.0 - sz)))).astype(BF16)
            dln = g * (z * sz) * (sl * (1.0 + ln * (1.0 - sl)))
            dlg_ref[:, cs] += jnp.sum(dln * xh, axis=0, keepdims=True)
            dlb_ref[:, cs] += jnp.sum(dln, axis=0, keepdims=True)
            dxh = dln * lg_ref[:, cs]
            du2_ref[:, cs] = dxh
            acc1 = acc1 + dxh
            acc2 = acc2 + dxh * xh
        m1 = _row_sum(acc1) / w
        m2 = _row_sum(acc2) / w
        for cs in chunks:
            xh = (u2_ref[:, cs] - mu) * rstd
            du2 = rstd * (du2_ref[:, cs] - m1 - xh * m2)
            du2_ref[:, cs] = du2
            dbb_ref[:, cs] += jnp.sum(du2, axis=0, keepdims=True)

    tile = pl.BlockSpec((ts, w), lambda i: (i, 0))
    row = pl.BlockSpec((1, w), lambda i: (0, 0))
    return pl.pallas_call(
        body, name="mix_b_gate_bwd", grid=(s // ts,),
        in_specs=[tile, pl.BlockSpec((ts, w), lambda i: (i, 6)), pl.BlockSpec((ts, w), lambda i: (i, 1)), row, row],
        out_specs=[tile, pl.BlockSpec((ts, w), lambda i: (i, 6)), row, row, row],
        out_shape=[jax.ShapeDtypeStruct((s, w), F32), jax.ShapeDtypeStruct((s, n_cols), BF16)] + [jax.ShapeDtypeStruct((1, w), F32)] * 3,
        compiler_params=_cparams(1),
    )(u2, proj, dy, lg, lb)


def _mix_b_conv_bwd(du2, proj, wb, dproj, w, ts, cb):
    s = proj.shape[0]
    nb = w // cb
    ns = s // ts
    chunks = _col_chunks(cb)

    def body(du2_ref, bv, bg, wb_ref, dp_in, dp_ref, dwb_ref, ext, stash):
        del dp_in
        q = pl.program_id(2)

        @pl.when(jnp.logical_and(pl.program_id(1) == 0, q == 0))
        def _():
            ext[ts : ts + HALO_B, :] = jnp.zeros((HALO_B, cb), F32)
            dwb_ref[...] = jnp.zeros_like(dwb_ref)

        @pl.when(q == 0)
        def _():
            ext[0:ts, :] = du2_ref[...]
            for cs in chunks:
                v = bv[:, cs].astype(F32)
                sg = _sigmoid(bg[:, cs].astype(F32))
                u = v * sg
                du = jnp.zeros((ts, LANES), F32)
                for k in range(CONV_B):
                    o = CONV_B - 1 - k
                    win = ext[o : o + ts, cs]
                    du = du + wb_ref[k : k + 1, cs] * win
                    dwb_ref[k : k + 1, cs] += jnp.sum(u * win, axis=0, keepdims=True)
                dp_ref[:, cs] = (du * sg).astype(BF16)
                stash[:, cs] = (du * v * (sg * (1.0 - sg))).astype(BF16)
            ext[ts : ts + HALO_B, :] = ext[0:HALO_B, :]

        @pl.when(q == 1)
        def _():
            dp_ref[...] = stash[...]

    def rev(col):
        return lambda c, i, q: (ns - 1 - i, col(c))

    return pl.pallas_call(
        body, name="mix_b_conv_bwd", grid=(nb, ns, 2),
        in_specs=[pl.BlockSpec((ts, cb), rev(lambda c: c)), pl.BlockSpec((ts, cb), rev(lambda c: 4 * nb + c)),
                  pl.BlockSpec((ts, cb), rev(lambda c: 5 * nb + c)), pl.BlockSpec((CONV_B, cb), lambda c, i, q: (0, c)),
                  pl.BlockSpec(memory_space=pl.ANY)],
        out_specs=[pl.BlockSpec((ts, cb), lambda c, i, q: (ns - 1 - i, (4 + q) * nb + c)),
                   pl.BlockSpec((CONV_B, cb), lambda c, i, q: (0, c))],
        out_shape=[jax.ShapeDtypeStruct(dproj.shape, dproj.dtype), jax.ShapeDtypeStruct((CONV_B, w), F32)],
        input_output_aliases={4: 0},
        scratch_shapes=[pltpu.VMEM((ts + HALO_B, cb), F32), pltpu.VMEM((ts, cb), BF16)], compiler_params=_cparams(3),
    )(du2, proj, proj, wb, dproj)


def _mix_a_bwd(proj, cv, dy, wa, dproj, w, ts, cb):
    s = proj.shape[0]
    nb = w // cb
    ns = s // ts
    chunks = _col_chunks(cb)

    def body(ab, ac, ax, az, cv_ref, dya, wa_ref, dp_in, dp_ref, dwa_ref, ext, stash):
        del dp_in
        q = pl.program_id(2)

        @pl.when(jnp.logical_and(pl.program_id(1) == 0, q == 0))
        def _():
            ext[ts : ts + HALO_A, :] = jnp.zeros((HALO_A, cb), F32)
            dwa_ref[...] = jnp.zeros_like(dwa_ref)

        @pl.when(q == 0)
        def _():
            for cs in chunks:
                b = ab[:, cs].astype(F32)
                z = az[:, cs].astype(F32)
                sz = _sigmoid(z)
                g = dya[:, cs].astype(F32)
                conv = cv_ref[:, cs].astype(F32)
                dp_ref[:, cs] = (g * conv * (z * sz)).astype(BF16)
                stash[2, :, cs] = (g * b * conv * (sz * (1.0 + z * (1.0 - sz)))).astype(BF16)
                ext[0:ts, cs] = g * b * (z * sz)
            for cs in chunks:
                a_c = ac[:, cs].astype(F32)
                a_x = ax[:, cs].astype(F32)
                ca = a_c * a_x
                dca = jnp.zeros((ts, LANES), F32)
                for k in range(CONV_A):
                    o = CONV_A - 1 - k
                    win = ext[o : o + ts, cs]
                    dca = dca + wa_ref[k : k + 1, cs] * win
                    dwa_ref[k : k + 1, cs] += jnp.sum(ca * win, axis=0, keepdims=True)
                stash[0, :, cs] = (dca * a_x).astype(BF16)
                stash[1, :, cs] = (dca * a_c).astype(BF16)
            ext[ts : ts + HALO_A, :] = ext[0:HALO_A, :]

        for seg_q in range(1, 4):

            @pl.when(q == seg_q)
            def _(seg_q=seg_q):
                dp_ref[...] = stash[seg_q - 1]

    def rev(col):
        return lambda c, i, q: (ns - 1 - i, col(c))

    def seg(sq):
        return pl.BlockSpec((ts, cb), rev(lambda c: sq * nb + c))

    return pl.pallas_call(
        body, name="mix_a_bwd", grid=(nb, ns, 4),
        in_specs=[seg(0), seg(1), seg(2), seg(3), pl.BlockSpec((ts, cb), rev(lambda c: c)), pl.BlockSpec((ts, cb), rev(lambda c: c)),
                  pl.BlockSpec((CONV_A, cb), lambda c, i, q: (0, c)), pl.BlockSpec(memory_space=pl.ANY)],
        out_specs=[pl.BlockSpec((ts, cb), lambda c, i, q: (ns - 1 - i, q * nb + c)), pl.BlockSpec((CONV_A, cb), lambda c, i, q: (0, c))],
        out_shape=[jax.ShapeDtypeStruct(dproj.shape, dproj.dtype), jax.ShapeDtypeStruct((CONV_A, w), F32)],
        input_output_aliases={7: 0},
        scratch_shapes=[pltpu.VMEM((ts + HALO_A, cb), F32), pltpu.VMEM((3, ts, cb), BF16)], compiler_params=_cparams(3),
    )(proj, proj, proj, proj, cv, dy, wa, dproj)


def _pre_bwd(x, dh, dx2, g1, scale, ts):
    s, d = x.shape
    chunks = _col_chunks(d)

    def body(x_ref, dh_ref, dx2_ref, g_ref, sc_ref, gx_ref, dsh_ref, dsc_ref, dg_ref):
        @pl.when(pl.program_id(0) == 0)
        def _():
            dsh_ref[...] = jnp.zeros_like(dsh_ref)
            dsc_ref[...] = jnp.zeros_like(dsc_ref)
            dg_ref[...] = jnp.zeros_like(dg_ref)

        acc = jnp.zeros((ts, LANES), F32)
        for cs in chunks:
            v = x_ref[:, cs]
            acc = acc + v * v
        r1 = lax.rsqrt(_row_sum(acc) / d + EPS)
        acc = jnp.zeros((ts, LANES), F32)
        for cs in chunks:
            xn = x_ref[:, cs] * r1
            g = dh_ref[:, cs]
            dsh_ref[:, cs] += jnp.sum(g, axis=0, keepdims=True)
            dsc_ref[:, cs] += jnp.sum(g * (xn * g_ref[:, cs]), axis=0, keepdims=True)
            dn1 = g * (1.0 + sc_ref[:, cs])
            dg_ref[:, cs] += jnp.sum(dn1 * xn, axis=0, keepdims=True)
            acc = acc + (dn1 * g_ref[:, cs]) * xn
        mdot = _row_sum(acc) / d
        for cs in chunks:
            xn = x_ref[:, cs] * r1
            dxn = (dh_ref[:, cs] * (1.0 + sc_ref[:, cs])) * g_ref[:, cs]
            gx_ref[:, cs] = dx2_ref[:, cs] + r1 * (dxn - xn * mdot)

    tile = pl.BlockSpec((ts, d), lambda i: (i, 0))
    row = pl.BlockSpec((1, d), lambda i: (0, 0))
    return pl.pallas_call(
        body, name="pre_bwd", grid=(s // ts,),
        in_specs=[tile, tile, tile, row, row], out_specs=[tile, row, row, row],
        out_shape=[jax.ShapeDtypeStruct((s, d), F32)] + [jax.ShapeDtypeStruct((1, d), F32)] * 3,
        compiler_params=_cparams(1),
    )(x, dh, dx2, g1, scale)


def _adamw(w, g, m, v):
    m = ADAM_B1 * m + (1.0 - ADAM_B1) * g
    v = ADAM_B2 * v + (1.0 - ADAM_B2) * (g * g)
    m_hat = m / (1.0 - ADAM_B1**ADAM_STEP)
    v_hat = v / (1.0 - ADAM_B2**ADAM_STEP)
    delta = -ADAM_LR * (m_hat / (jnp.sqrt(v_hat) + ADAM_EPS) + ADAM_WD * w)
    return delta, m, v


def _combine_halves(part, swapped, core, tr):
    _, r, n = part.shape

    def body(core_ref, p_ref, s_ref, o_ref):
        del core_ref
        o_ref[...] = (p_ref[...].astype(F32) + s_ref[...].astype(F32)).astype(BF16)

    return pl.pallas_call(
        body, name="combine_halves",
        grid_spec=pltpu.PrefetchScalarGridSpec(
            num_scalar_prefetch=1, grid=(N_CHIP, r // tr),
            in_specs=[pl.BlockSpec((None, tr, n), lambda q, i, cr: (2 * q + cr[0], i, 0)), pl.BlockSpec((None, tr, n), lambda q, i, cr: (q, i, 0))],
            out_specs=pl.BlockSpec((None, tr, n), lambda q, i, cr: (q, i, 0))),
        out_shape=jax.ShapeDtypeStruct((N_CHIP, r, n), BF16), compiler_params=_cparams(2),
    )(core, part, swapped)


def _adam_sharded(w, m, v, comb, recv, chip, tr, name):
    r, n = w.shape

    def body(chip_ref, w_ref, m_ref, v_ref, c_ref, r_ref, g_ref, d_ref, nm_ref, nv_ref):
        del chip_ref
        g = c_ref[...].astype(F32) + r_ref[0].astype(F32) + r_ref[1].astype(F32) + r_ref[2].astype(F32)
        delta, nm, nv = _adamw(w_ref[...], g, m_ref[...], v_ref[...])
        g_ref[...] = g
        d_ref[...] = delta
        nm_ref[...] = nm
        nv_ref[...] = nv

    tile = pl.BlockSpec((tr, n), lambda i, ch: (i, 0))
    return pl.pallas_call(
        body, name=name,
        grid_spec=pltpu.PrefetchScalarGridSpec(
            num_scalar_prefetch=1, grid=(r // tr,),
            in_specs=[tile, tile, tile, pl.BlockSpec((None, tr, n), lambda i, ch: (ch[0], i, 0)), pl.BlockSpec((3, tr, n), lambda i, ch: (0, i, 0))],
            out_specs=[tile] * 4),
        out_shape=[jax.ShapeDtypeStruct((r, n), F32)] * 4, compiler_params=_cparams(1),
    )(chip, w, m, v, comb, recv)


def _adam_w_ada(c_t, dm_loc, w, m, v, tr):
    d, nb = c_t.shape
    na = w.shape[1]

    def body(c_ref, dm_ref, w_ref, m_ref, v_ref, g_ref, d_ref, nm_ref, nv_ref):
        g = jnp.dot(c_ref[...], dm_ref[...], preferred_element_type=F32, precision=lax.Precision.HIGHEST)
        delta, nm, nv = _adamw(w_ref[...], g, m_ref[...], v_ref[...])
        g_ref[...] = g
        d_ref[...] = delta
        nm_ref[...] = nm
        nv_ref[...] = nv

    tile = pl.BlockSpec((tr, na), lambda i: (i, 0))
    return pl.pallas_call(
        body, name="adam_w_ada", grid=(d // tr,),
        in_specs=[pl.BlockSpec((tr, nb), lambda i: (i, 0)), pl.BlockSpec((nb, na), lambda i: (0, 0)), tile, tile, tile],
        out_specs=[tile] * 4, out_shape=[jax.ShapeDtypeStruct((d, na), F32)] * 4, compiler_params=_cparams(1),
    )(c_t, dm_loc, w, m, v)


def _small_final(me, gathered, params, d, w, cl):
    flat_params = [t for p in params for t in p]

    def total(ref):
        acc = ref[0]
        for b in range(1, N_DEV):
            acc = acc + ref[b]
        return acc

    def body(me_ref, loss_g, dfg_g, dg1_g, dsh_g, dsc_g, dgt_g, dbb_g, dlg_g, dlb_g, dwa_g, dwb_g, *rest):
        del me_ref
        prm = rest[:24]
        loss_ref = rest[24]
        outs = rest[25:]

        loss_ref[...] = jnp.sum(total(loss_g), axis=1, keepdims=True) * (0.5 / d)

        def update(idx, g, cols=None):
            w_ref, m_ref, v_ref = prm[3 * idx : 3 * idx + 3]
            g_ref, d_ref, nm_ref, nv_ref = outs[4 * idx : 4 * idx + 4]
            sl = (slice(None), slice(None)) if cols is None else (slice(None), cols)
            delta, nm, nv = _adamw(w_ref[sl], g, m_ref[sl], v_ref[sl])
            g_ref[sl] = g
            d_ref[sl] = delta
            nm_ref[sl] = nm
            nv_ref[sl] = nv

        update(0, total(dg1_g))
        update(1, total(dsh_g), slice(0, d))
        update(1, total(dsc_g), slice(d, 2 * d))
        update(1, total(dgt_g), slice(2 * d, 3 * d))
        update(2, total(dwa_g))
        update(3, total(dwb_g))
        update(4, total(dbb_g))
        update(5, total(dlg_g))
        update(6, total(dlb_g))
        update(7, total(dfg_g))

    def full(shape):
        nd = len(shape)
        return pl.BlockSpec(shape, lambda i, mr: (0,) * nd)

    in_specs = [full(g.shape) for g in gathered[:9]]
    in_specs += [pl.BlockSpec((N_DEV, CONV_A, cl), lambda i, mr: (0, 0, mr[0])), pl.BlockSpec((N_DEV, CONV_B, cl), lambda i, mr: (0, 0, mr[0]))]
    in_specs += [full(t.shape) for t in flat_params]
    out_shapes = [jax.ShapeDtypeStruct((1, 1), F32)]
    for p in params:
        out_shapes += [jax.ShapeDtypeStruct(p[0].shape, F32)] * 4
    return pl.pallas_call(
        body, name="small_final",
        grid_spec=pltpu.PrefetchScalarGridSpec(
            num_scalar_prefetch=1, grid=(1,), in_specs=in_specs, out_specs=[full(o.shape) for o in out_shapes]),
        out_shape=out_shapes, compiler_params=_cparams(1),
    )(me, *gathered, *flat_params)


def _silu_rows(c):
    def body(c_ref, o_ref):
        v = c_ref[...]
        o_ref[...] = v * _sigmoid(v)

    return pl.pallas_call(body, name="silu_c", out_shape=jax.ShapeDtypeStruct(c.shape, F32), compiler_params=_cparams())(c)


def kernel(x, c, norm_g, w_ada, b_ada, w_in, conv_a_w, conv_b_w, conv_b_b, ln_b_g, ln_b_b, w_out, final_g, loss_target, m_norm_g, m_w_ada, m_b_ada, m_w_in, m_conv_a_w, m_conv_b_w, m_conv_b_b, m_ln_b_g, m_ln_b_b, m_w_out, m_final_g, v_norm_g, v_w_ada, v_b_ada, v_w_in, v_conv_a_w, v_conv_b_w, v_conv_b_b, v_ln_b_g, v_ln_b_b, v_w_out, v_final_g):
    _, s, d = x.shape
    w = conv_b_b.shape[-1]
    cl = conv_a_w.shape[-1]
    na = w_ada.shape[-1]
    assert ln_b_g.shape[-1] == w and w_out.shape[1] * N_DEV == 2 * w and w_in.shape[-1] * N_DEV == 7 * w and cl * N_DEV == w
    ts = min(256, s)
    ts_row = min(128, s)
    cb = min(512, w)
    tm = min(512, s)
    tn = min(1024, d)

    px, py, pc = _position()
    me = 4 * px + 2 * py + pc
    me_arr = jnp.reshape(me, (1,)).astype(jnp.int32)
    core_arr = jnp.reshape(pc, (1,)).astype(jnp.int32)
    chip_arr = jnp.reshape(2 * px + py, (1,)).astype(jnp.int32)

    x2d, tgt = x[0], loss_target[0]
    w_ada2, m_w_ada2, v_w_ada2 = w_ada[0], m_w_ada[0], v_w_ada[0]
    w_in2, m_w_in2, v_w_in2 = w_in[0], m_w_in[0], v_w_in[0]
    w_out2, m_w_out2, v_w_out2 = w_out[0], m_w_out[0], v_w_out[0]
    fg = final_g.reshape(1, d)

    c_act = _silu_rows(c)
    c_all, wa_all, wb_all = _all_gather_vmem([c_act, conv_a_w[0], conv_b_w[0]], "gather_small_in")
    c_all = c_all.reshape(N_DEV, d)
    wa = jnp.transpose(wa_all, (1, 0, 2)).reshape(CONV_A, w)
    wb = jnp.transpose(wb_all, (1, 0, 2)).reshape(CONV_B, w)
    b_loc = lax.dynamic_slice_in_dim(b_ada, me * na, na, axis=1)
    mod_loc = _mod_matmul(c_all, w_ada2, b_loc, min(512, d))
    (mod_all,) = _all_gather_vmem([mod_loc], "gather_mod")
    mod_mine = lax.dynamic_index_in_dim(mod_all, me, axis=1, keepdims=False).reshape(1, 3 * d)
    shift, scale, gate = mod_mine[:, :d], mod_mine[:, d : 2 * d], mod_mine[:, 2 * d :]

    h = _prep(x2d, norm_g, scale, shift, ts_row)
    proj, w_in_full, w_out_full = _proj_gather(h, w_in2.astype(BF16), w_out2.astype(BF16), _arrival_order(), tm)
    w_out_full = w_out_full.reshape(2 * w, d)
    y, cv = _mix_a_fwd(proj, wa, w, ts, cb)
    u2 = _mix_b_conv_fwd(proj, wb, conv_b_b, w, ts, cb)
    y = _mix_b_gate_fwd(u2, proj, ln_b_g, ln_b_b, y, ts)
    out = _mm_out(y, w_out_full, tm, tn)
    dx2, dout, loss_row, dgate_row, dfg_row = _post(x2d, out, tgt, gate, fg, ts_row)

    dy = _mm_dy(dout, w_out_full, tm, tn)
    part_out = _mm_dw_out(y, dout, tn)
    du2, dproj, dlg_row, dlb_row, dbb_row = _mix_b_gate_bwd(u2, proj, dy, ln_b_g, ln_b_b, 7 * w, ts)
    dproj, dwb_part = _mix_b_conv_bwd(du2, proj, wb, dproj, w, ts, cb)
    dproj, dwa_part = _mix_a_bwd(proj, cv, dy, wa, dproj, w, ts, cb)
    part_in = _mm_dw_in(h, dproj, tm)

    swapped_in, swapped_out = _swap_with_sibling([part_in, part_out], "swap_halves")
    comb_in = _combine_halves(part_in, swapped_in, core_arr, min(256, d))
    comb_out = _combine_halves(part_out, swapped_out, core_arr, min(256, part_out.shape[1]))
    dh, (recv_in, recv_out) = _mm_dh(dproj, w_in_full, [comb_in, comb_out], min(1024, s), tn)
    grad_x, dshift_row, dscale_row, dg1_row = _pre_bwd(x2d, dh, dx2, norm_g, scale, ts_row)
    g_w_in, d_w_in, nm_w_in, nv_w_in = _adam_sharded(w_in2, m_w_in2, v_w_in2, comb_in, recv_in, chip_arr, min(128, d), "adam_w_in")
    g_w_out, d_w_out, nm_w_out, nv_w_out = _adam_sharded(w_out2, m_w_out2, v_w_out2, comb_out, recv_out, chip_arr, min(128, w_out2.shape[0]), "adam_w_out")

    gathered = _all_gather_vmem(
        [loss_row, dfg_row, dg1_row, dshift_row, dscale_row, dgate_row, dbb_row, dlg_row, dlb_row, dwa_part, dwb_part], "gather_small_grads")
    dmod_all = jnp.concatenate([gathered[3].reshape(N_DEV, d), gathered[4].reshape(N_DEV, d), gathered[5].reshape(N_DEV, d)], axis=1)
    dm_loc = lax.dynamic_slice_in_dim(dmod_all, me * na, na, axis=1)
    g_w_ada, d_w_ada, nm_w_ada, nv_w_ada = _adam_w_ada(jnp.transpose(c_all), dm_loc, w_ada2, m_w_ada2, v_w_ada2, min(256, d))

    params = [
        (norm_g, m_norm_g, v_norm_g), (b_ada, m_b_ada, v_b_ada), (conv_a_w[0], m_conv_a_w[0], v_conv_a_w[0]),
        (conv_b_w[0], m_conv_b_w[0], v_conv_b_w[0]), (conv_b_b, m_conv_b_b, v_conv_b_b), (ln_b_g, m_ln_b_g, v_ln_b_g),
        (ln_b_b, m_ln_b_b, v_ln_b_b), (fg, m_final_g.reshape(1, d), v_final_g.reshape(1, d)),
    ]
    small = _small_final(me_arr, gathered, params, d, w, cl)
    loss = small[0].reshape(())
    sm = [small[1 + 4 * i : 5 + 4 * i] for i in range(8)]

    def pick(k):
        return [
            sm[0][k], (g_w_ada, d_w_ada, nm_w_ada, nv_w_ada)[k][None], sm[1][k], (g_w_in, d_w_in, nm_w_in, nv_w_in)[k][None],
            sm[2][k][None], sm[3][k][None], sm[4][k], sm[5][k], sm[6][k], (g_w_out, d_w_out, nm_w_out, nv_w_out)[k][None],
            sm[7][k].reshape(d),
        ]

    return (loss, grad_x[None], *pick(0), *pick(1), *pick(2), *pick(3))
```

```python
import functools

import jax
import jax.numpy as jnp
from jax import lax
from jax.experimental import pallas as pl
from jax.experimental.pallas import tpu as pltpu

F32 = jnp.float32
BF16 = jnp.bfloat16
N_DEV = 8
N_CHIP = 4
EPS = 1e-6
ADAM_LR = 0.001
ADAM_B1 = 0.9
ADAM_B2 = 0.999
ADAM_EPS = 1e-08
ADAM_WD = 0.01
ADAM_STEP = 10
CONV_A = 3
CONV_B = 31
HALO_A = 8
HALO_B = 32
LANES = 128
VMEM_LIMIT = 56 * 1024 * 1024
MESH = pl.DeviceIdType.MESH


def _cparams(n_grid_axes=0):
    if n_grid_axes:
        return pltpu.CompilerParams(dimension_semantics=("arbitrary",) * n_grid_axes, vmem_limit_bytes=VMEM_LIMIT)
    return pltpu.CompilerParams(vmem_limit_bytes=VMEM_LIMIT)


def _flip(v, bit):
    return 1 - v if bit else v


def _position():
    return lax.axis_index("x"), lax.axis_index("y"), lax.axis_index("c")


def _sigmoid(v):
    return jax.nn.sigmoid(v)


def _all_gather_vmem(arrs, name):
    n = len(arrs)

    def body(*refs):
        ins, outs = refs[:n], refs[n : 2 * n]
        send_sems, recv_sems = refs[2 * n :]
        x, y, c = _position()
        me = 4 * x + 2 * y + c
        for a in range(n):
            outs[a][me] = ins[a][...]
        sends = []
        for k in range(1, N_DEV):
            peer = (_flip(x, k & 4), _flip(y, k & 2), _flip(c, k & 1))
            for a in range(n):
                cp = pltpu.make_async_remote_copy(
                    src_ref=ins[a], dst_ref=outs[a].at[me], send_sem=send_sems.at[a, k - 1], recv_sem=recv_sems.at[a, k - 1],
                    device_id=peer, device_id_type=MESH)
                cp.start()
                sends.append(cp)
        for k in range(1, N_DEV):
            peer = (_flip(x, k & 4), _flip(y, k & 2), _flip(c, k & 1))
            src = 4 * peer[0] + 2 * peer[1] + peer[2]
            for a in range(n):
                pltpu.make_async_remote_copy(
                    src_ref=ins[a], dst_ref=outs[a].at[src], send_sem=send_sems.at[a, k - 1], recv_sem=recv_sems.at[a, k - 1],
                    device_id=peer, device_id_type=MESH).wait_recv()
        for cp in sends:
            cp.wait_send()

    vm = pl.BlockSpec(memory_space=pltpu.VMEM)
    return pl.pallas_call(
        body, name=name,
        out_shape=[jax.ShapeDtypeStruct((N_DEV,) + a.shape, a.dtype) for a in arrs],
        in_specs=[vm] * n, out_specs=[vm] * n,
        scratch_shapes=[pltpu.SemaphoreType.DMA((n, N_DEV - 1)), pltpu.SemaphoreType.DMA((n, N_DEV - 1))],
        compiler_params=_cparams(),
    )(*arrs)


def _swap_with_sibling(parts, name):
    n = len(parts)

    def body(*refs):
        ins, outs = refs[:n], refs[n : 2 * n]
        send_sems, recv_sems = refs[2 * n :]
        x, y, c = _position()
        sibling = (x, y, 1 - c)
        sends = []
        for a in range(n):
            for q in range(N_CHIP):
                cp = pltpu.make_async_remote_copy(
                    src_ref=ins[a].at[2 * q + (1 - c)], dst_ref=outs[a].at[q], send_sem=send_sems.at[a, q],
                    recv_sem=recv_sems.at[a, q], device_id=sibling, device_id_type=MESH)
                cp.start()
                sends.append(cp)
        for cp in sends:
            cp.wait_recv()
        for cp in sends:
            cp.wait_send()

    hbm = pl.BlockSpec(memory_space=pl.ANY)
    return pl.pallas_call(
        body, name=name,
        out_shape=[jax.ShapeDtypeStruct((N_CHIP,) + p.shape[1:], p.dtype) for p in parts],
        in_specs=[hbm] * n, out_specs=[hbm] * n,
        scratch_shapes=[pltpu.SemaphoreType.DMA((n, N_CHIP)), pltpu.SemaphoreType.DMA((n, N_CHIP))],
        compiler_params=_cparams(),
    )(*parts)


NN = (((1,), (0,)), ((), ()))
NT = (((1,), (1,)), ((), ()))
TN = (((0,), (0,)), ((), ()))


def _n_chunk(n):
    return 256 if n % 256 == 0 else LANES


def _mm_body(dims, n_out, accumulate):
    nc = _n_chunk(n_out)

    def body(a_ref, b_ref, o_ref):
        for n0 in range(0, n_out, nc):
            b = b_ref[n0 : n0 + nc, :] if dims is NT else b_ref[:, n0 : n0 + nc]
            r = lax.dot_general(a_ref[...], b, dims, preferred_element_type=F32)
            if accumulate:
                k = pl.program_id(2)

                @pl.when(k == 0)
                def _():
                    o_ref[:, n0 : n0 + nc] = r

                @pl.when(k > 0)
                def _():
                    o_ref[:, n0 : n0 + nc] += r
            else:
                o_ref[:, n0 : n0 + nc] = r.astype(o_ref.dtype)

    return body


def _ring_chips():
    x, y, c = _position()
    first = (x + (1 - c) - 2 * x * (1 - c), y + c - 2 * y * c)
    second = (x + c - 2 * x * c, y + (1 - c) - 2 * y * (1 - c))
    return first, second, (1 - x, 1 - y)


def _arrival_order():
    x, y, c = _position()
    first, second, diag = _ring_chips()
    blocks = [(x, y, c), (x, y, 1 - c), (*first, c), (*second, 1 - c), (*second, c), (*first, 1 - c), (*diag, c), (*diag, 1 - c)]
    return jnp.stack([4 * b[0] + 2 * b[1] + b[2] for b in blocks]).astype(jnp.int32)


def _proj_gather(h, w_in_loc, w_out_loc, order, tm):
    s, d = h.shape
    ne = w_in_loc.shape[1]
    ni = s // tm
    nc = _n_chunk(ne)
    n_streams = 7

    def body(order_ref, h_ref, win_ref, wout_ref, proj_ref, fin_ref, fout_ref, bbuf, send_sems, recv_sems, local_sems, load_sems):
        ins, fulls = [win_ref, wout_ref], [fin_ref, fout_ref]
        jj, i = pl.program_id(0), pl.program_id(1)
        x, y, c = _position()
        me = (x, y, c)
        sibling = (x, y, 1 - c)
        first, second, diag = _ring_chips()
        sent = [(me, sibling), (me, (*first, c)), (me, (*second, c)), ((*first, c), (*second, c)),
                ((*first, c), sibling), ((*second, c), sibling), ((*diag, c), sibling)]
        received = [sibling, (*first, c), (*second, c), (*diag, c), (*second, 1 - c), (*first, 1 - c), (*diag, 1 - c)]

        def slot(p):
            return 4 * p[0] + 2 * p[1] + p[2]

        def copy(a, k, block, to):
            dst = fulls[a].at[slot(block)]
            return pltpu.make_async_remote_copy(
                src_ref=ins[a] if k < 3 else dst, dst_ref=dst, send_sem=send_sems.at[a, k], recv_sem=recv_sems.at[a, k],
                device_id=to, device_id_type=MESH)

        def send(a, k):
            return copy(a, k, *sent[k])

        def recv(a, k):
            return copy(a, k, received[k], me)

        def own(a):
            return pltpu.make_async_copy(ins[a], fulls[a].at[slot(me)], local_sems.at[a])

        def load(src, buf):
            return pltpu.make_async_copy(src, bbuf.at[buf], load_sems.at[buf])

        @pl.when((jj == 0) & (i == 0))
        def _():
            load(win_ref, 0).start()
            for a in range(2):
                own(a).start()
            send(0, 0).start()
            send(0, 1).start()
            send(1, 0).start()
            load(win_ref, 0).wait()

        stream_of = [None, 0, 1, 4, 2, 5, 3, 6]
        passes = {1: [(0, 4), (0, 2), (0, 3), (1, 1), (1, 2)], 2: [(0, 5)], 3: [(0, 6)]}
        for nxt in range(1, N_DEV):

            @pl.when((jj == nxt - 1) & (i == ni - 1))
            def _(nxt=nxt):
                k = stream_of[nxt]
                recv(0, k).wait_recv()
                for a, k2 in passes.get(k, []):
                    send(a, k2).start()
                if nxt == N_DEV - 1:
                    recv(1, 1).wait_recv()
                    send(1, 4).start()
                    send(1, 3).start()
                    recv(1, 2).wait_recv()
                    send(1, 5).start()
                load(fin_ref.at[order_ref[nxt]], nxt % 2).start()

        @pl.when((jj > 0) & (i == 0))
        def _():
            load(fin_ref.at[0], jj % 2).wait()

        for n0 in range(0, ne, nc):
            proj_ref[:, n0 : n0 + nc] = lax.dot_general(
                h_ref[...], bbuf[jj % 2, :, n0 : n0 + nc], NN, preferred_element_type=F32).astype(BF16)

        @pl.when((jj == N_DEV - 1) & (i == ni - 1))
        def _():
            recv(1, 3).wait_recv()
            send(1, 6).start()
            for k in (0, 4, 5, 6):
                recv(1, k).wait_recv()
            for a in range(2):
                for k in range(n_streams):
                    send(a, k).wait_send()
                own(a).wait()

    hbm = pl.BlockSpec(memory_space=pl.ANY)
    return pl.pallas_call(
        body, name="proj_gather",
        grid_spec=pltpu.PrefetchScalarGridSpec(
            num_scalar_prefetch=1, grid=(N_DEV, ni),
            in_specs=[pl.BlockSpec((tm, d), lambda jj, i, od: (i, 0)), hbm, hbm],
            out_specs=[pl.BlockSpec((tm, ne), lambda jj, i, od: (i, od[jj])), hbm, hbm],
            scratch_shapes=[pltpu.VMEM((2, d, ne), BF16), pltpu.SemaphoreType.DMA((2, 7)), pltpu.SemaphoreType.DMA((2, 7)),
                            pltpu.SemaphoreType.DMA((2,)), pltpu.SemaphoreType.DMA((2,))]),
        out_shape=[jax.ShapeDtypeStruct((s, N_DEV * ne), BF16), jax.ShapeDtypeStruct((N_DEV,) + w_in_loc.shape, BF16),
                   jax.ShapeDtypeStruct((N_DEV,) + w_out_loc.shape, BF16)],
        compiler_params=_cparams(2),
    )(order, h, w_in_loc, w_out_loc)


def _mm_out(y, w_out, tm, tn):
    s, m = y.shape
    _, d = w_out.shape
    return pl.pallas_call(
        _mm_body(NN, tn, False), name="mm_out", grid=(d // tn, s // tm),
        in_specs=[pl.BlockSpec((tm, m), lambda j, i: (i, 0)), pl.BlockSpec((m, tn), lambda j, i: (0, j))],
        out_specs=pl.BlockSpec((tm, tn), lambda j, i: (i, j)),
        out_shape=jax.ShapeDtypeStruct((s, d), F32), compiler_params=_cparams(2),
    )(y, w_out)


def _mm_dy(dout, w_out, tm, tn):
    s, d = dout.shape
    m, _ = w_out.shape
    return pl.pallas_call(
        _mm_body(NT, tn, False), name="mm_dy", grid=(m // tn, s // tm),
        in_specs=[pl.BlockSpec((tm, d), lambda j, i: (i, 0)), pl.BlockSpec((tn, d), lambda j, i: (j, 0))],
        out_specs=pl.BlockSpec((tm, tn), lambda j, i: (i, j)),
        out_shape=jax.ShapeDtypeStruct((s, m), BF16), compiler_params=_cparams(2),
    )(dout, w_out)


def _mm_dw_out(y, dout, tn):
    s, m = y.shape
    _, d = dout.shape
    mr = m // N_DEV
    return pl.pallas_call(
        _mm_body(TN, tn, False), name="mm_dw_out", grid=(d // tn, N_DEV),
        in_specs=[pl.BlockSpec((s, mr), lambda j, i: (0, i)), pl.BlockSpec((s, tn), lambda j, i: (0, j))],
        out_specs=pl.BlockSpec((None, mr, tn), lambda j, i: (i, 0, j)),
        out_shape=jax.ShapeDtypeStruct((N_DEV, mr, d), BF16), compiler_params=_cparams(2),
    )(y, dout)


def _mm_dw_in(h, dproj, tm):
    s, d = h.shape
    ne = dproj.shape[1] // N_DEV
    return pl.pallas_call(
        _mm_body(TN, ne, False), name="mm_dw_in", grid=(N_DEV, d // tm),
        in_specs=[pl.BlockSpec((s, tm), lambda j, i: (0, i)), pl.BlockSpec((s, ne), lambda j, i: (0, j))],
        out_specs=pl.BlockSpec((None, tm, ne), lambda j, i: (j, i, 0)),
        out_shape=jax.ShapeDtypeStruct((N_DEV, d, ne), BF16), compiler_params=_cparams(2),
    )(h, dproj)


def _chip_copies(c_refs, r_refs, send_sems, recv_sems):
    x, y, c = _position()
    chips = [(1 - x, y), (x, 1 - y), (1 - x, 1 - y)]
    return [
        pltpu.make_async_remote_copy(
            src_ref=c_refs[a].at[2 * chip[0] + chip[1]], dst_ref=r_refs[a].at[j], send_sem=send_sems.at[a, j],
            recv_sem=recv_sems.at[a, j], device_id=(*chip, c), device_id_type=MESH)
        for a in range(len(c_refs)) for j, chip in enumerate(chips)]


def _mm_dh(dproj, w_full, combs, tm, tn):
    s = dproj.shape[0]
    _, d, ne = w_full.shape
    kb = 2
    nk = N_DEV // kb
    nc = _n_chunk(tn)
    mh = min(512, tm)
    nq = len(combs)
    grid = (s // tm, d // tn, nk)

    def body(*refs):
        a_ref, b_ref = refs[:2]
        c_refs = refs[2 : 2 + nq]
        o_ref = refs[2 + nq]
        r_refs = refs[3 + nq : 3 + 2 * nq]
        send_sems, recv_sems = refs[3 + 2 * nq :]
        i, n, k = pl.program_id(0), pl.program_id(1), pl.program_id(2)

        def copies():
            return _chip_copies(c_refs, r_refs, send_sems, recv_sems)

        @pl.when((i == 0) & (n == 0) & (k == 0))
        def _():
            for cp in copies():
                cp.start()

        @pl.when(k == 0)
        def _():
            o_ref[...] = jnp.zeros_like(o_ref)

        for m0 in range(0, tm, mh):
            for n0 in range(0, tn, nc):
                r = None
                for jj in range(kb):
                    t = lax.dot_general(a_ref[m0 : m0 + mh, jj * ne : (jj + 1) * ne], b_ref[jj, n0 : n0 + nc, :], NT, preferred_element_type=F32)
                    r = t if r is None else r + t
                o_ref[m0 : m0 + mh, n0 : n0 + nc] += r

        @pl.when((i == grid[0] - 1) & (n == grid[1] - 1) & (k == nk - 1))
        def _():
            for cp in copies():
                cp.wait_recv()
            for cp in copies():
                cp.wait_send()

    hbm = pl.BlockSpec(memory_space=pl.ANY)
    outs = pl.pallas_call(
        body, name="mm_dh", grid=grid,
        in_specs=[pl.BlockSpec((tm, kb * ne), lambda i, n, k: (i, k)), pl.BlockSpec((kb, tn, ne), lambda i, n, k: (k, n, 0))] + [hbm] * nq,
        out_specs=[pl.BlockSpec((tm, tn), lambda i, n, k: (i, n))] + [hbm] * nq,
        out_shape=[jax.ShapeDtypeStruct((s, d), F32)] + [jax.ShapeDtypeStruct((3,) + cb.shape[1:], cb.dtype) for cb in combs],
        scratch_shapes=[pltpu.SemaphoreType.DMA((nq, 3)), pltpu.SemaphoreType.DMA((nq, 3))],
        compiler_params=_cparams(3),
    )(dproj, w_full, *combs)
    return outs[0], outs[1:]


def _mod_matmul(c_all, w_ada, b_loc, tk):
    nb, d = c_all.shape
    na = w_ada.shape[1]

    def body(c_ref, w_ref, b_ref, o_ref):
        k = pl.program_id(0)
        r = jnp.dot(c_ref[...], w_ref[...], preferred_element_type=F32, precision=lax.Precision.HIGHEST)

        @pl.when(k == 0)
        def _():
            o_ref[...] = r + b_ref[...]

        @pl.when(k > 0)
        def _():
            o_ref[...] += r

    return pl.pallas_call(
        body, name="mod_matmul", grid=(d // tk,),
        in_specs=[pl.BlockSpec((nb, tk), lambda k: (0, k)), pl.BlockSpec((tk, na), lambda k: (k, 0)), pl.BlockSpec((1, na), lambda k: (0, 0))],
        out_specs=pl.BlockSpec((nb, na), lambda k: (0, 0)),
        out_shape=jax.ShapeDtypeStruct((nb, na), F32), compiler_params=_cparams(1),
    )(c_all, w_ada, b_loc)


def _col_chunks(width):
    return [slice(c0, c0 + LANES) for c0 in range(0, width, LANES)]


def _row_sum(acc):
    return jnp.sum(acc, axis=1, keepdims=True)


SUBLANES = 8


ROW_BLOCK = 64
DW_BLOCK = 96


def _tap_sum(ext, cs, ts, taps, emit):
    for t0 in range(0, ts, ROW_BLOCK):
        rb = min(ROW_BLOCK, ts - t0)
        acc = None
        for r in range(SUBLANES):
            group = [(o, wv) for o, wv in taps if o % SUBLANES == r]
            if not group:
                continue
            n = rb if r == 0 else rb + SUBLANES
            v = None
            for o, wv in group:
                term = wv * ext[t0 + o - r : t0 + o - r + n, cs]
                v = term if v is None else v + term
            part = v if r == 0 else v[r : r + rb]
            acc = part if acc is None else acc + part
        emit(slice(t0, t0 + rb), acc)


def _prep(x, g1, scale, shift, ts):
    s, d = x.shape
    chunks = _col_chunks(d)

    def body(x_ref, g_ref, sc_ref, sh_ref, h_ref):
        acc = jnp.zeros((ts, LANES), F32)
        for cs in chunks:
            v = x_ref[:, cs]
            acc = acc + v * v
        r = lax.rsqrt(_row_sum(acc) / d + EPS)
        for cs in chunks:
            n1 = (x_ref[:, cs] * r) * g_ref[:, cs]
            h_ref[:, cs] = (n1 * (1.0 + sc_ref[:, cs]) + sh_ref[:, cs]).astype(BF16)

    row = pl.BlockSpec((1, d), lambda i: (0, 0))
    return pl.pallas_call(
        body, name="prep", grid=(s // ts,),
        in_specs=[pl.BlockSpec((ts, d), lambda i: (i, 0)), row, row, row],
        out_specs=pl.BlockSpec((ts, d), lambda i: (i, 0)),
        out_shape=jax.ShapeDtypeStruct((s, d), BF16), compiler_params=_cparams(1),
    )(x, g1, scale, shift)


def _mix_a_fwd(proj, wa, w, ts, cb):
    s = proj.shape[0]
    nb = w // cb
    chunks = _col_chunks(cb)

    def body(ab, ac, ax, az, wa_ref, y_ref, cv_ref, ext):
        @pl.when(pl.program_id(1) == 0)
        def _():
            ext[0:HALO_A, :] = jnp.zeros((HALO_A, cb), F32)

        for cs in chunks:
            ext[HALO_A : HALO_A + ts, cs] = ac[:, cs].astype(F32) * ax[:, cs].astype(F32)
        for cs in chunks:

            def emit(rows, cv, cs=cs):
                z = az[rows, cs].astype(F32)
                y_ref[rows, cs] = (ab[rows, cs].astype(F32) * cv * (z * _sigmoid(z))).astype(BF16)
                cv_ref[rows, cs] = cv.astype(BF16)

            _tap_sum(ext, cs, ts, [(HALO_A - (CONV_A - 1) + k, wa_ref[k : k + 1, cs]) for k in range(CONV_A)], emit)
        ext[0:HALO_A, :] = ext[ts : ts + HALO_A, :]

    def seg(q):
        return pl.BlockSpec((ts, cb), lambda c, i: (i, q * nb + c))

    return pl.pallas_call(
        body, name="mix_a_fwd", grid=(nb, s // ts),
        in_specs=[seg(0), seg(1), seg(2), seg(3), pl.BlockSpec((CONV_A, cb), lambda c, i: (0, c))],
        out_specs=[pl.BlockSpec((ts, cb), lambda c, i: (i, c)), pl.BlockSpec((ts, cb), lambda c, i: (i, c))],
        out_shape=[jax.ShapeDtypeStruct((s, 2 * w), BF16), jax.ShapeDtypeStruct((s, w), BF16)],
        scratch_shapes=[pltpu.VMEM((HALO_A + ts, cb), F32)], compiler_params=_cparams(2),
    )(proj, proj, proj, proj, wa)


def _mix_b_conv_fwd(proj, wb, bb, w, ts, cb):
    s = proj.shape[0]
    nb = w // cb
    chunks = _col_chunks(cb)

    def body(bv, bg, wb_ref, bb_ref, u2_ref, ext):
        @pl.when(pl.program_id(1) == 0)
        def _():
            ext[0:HALO_B, :] = jnp.zeros((HALO_B, cb), F32)

        for cs in chunks:
            ext[HALO_B : HALO_B + ts, cs] = bv[:, cs].astype(F32) * _sigmoid(bg[:, cs].astype(F32))
        for cs in chunks:

            def emit(rows, acc, cs=cs):
                u2_ref[rows, cs] = acc + bb_ref[:, cs]

            _tap_sum(ext, cs, ts, [(HALO_B - (CONV_B - 1) + k, wb_ref[k : k + 1, cs]) for k in range(CONV_B)], emit)
        ext[0:HALO_B, :] = ext[ts : ts + HALO_B, :]

    return pl.pallas_call(
        body, name="mix_b_conv_fwd", grid=(nb, s // ts),
        in_specs=[pl.BlockSpec((ts, cb), lambda c, i: (i, 4 * nb + c)), pl.BlockSpec((ts, cb), lambda c, i: (i, 5 * nb + c)),
                  pl.BlockSpec((CONV_B, cb), lambda c, i: (0, c)), pl.BlockSpec((1, cb), lambda c, i: (0, c))],
        out_specs=pl.BlockSpec((ts, cb), lambda c, i: (i, c)),
        out_shape=jax.ShapeDtypeStruct((s, w), F32),
        scratch_shapes=[pltpu.VMEM((HALO_B + ts, cb), F32)], compiler_params=_cparams(2),
    )(proj, proj, wb, bb)


def _ln_stats(u2_ref, ts, w, chunks):
    acc = jnp.zeros((ts, LANES), F32)
    for cs in chunks:
        acc = acc + u2_ref[:, cs]
    mu = _row_sum(acc) / w
    acc = jnp.zeros((ts, LANES), F32)
    for cs in chunks:
        xc = u2_ref[:, cs] - mu
        acc = acc + xc * xc
    return mu, lax.rsqrt(_row_sum(acc) / w + EPS)


def _mix_b_gate_fwd(u2, proj, lg, lb, y, ts):
    s, w = u2.shape
    chunks = _col_chunks(w)

    def body(u2_ref, bz, lg_ref, lb_ref, y_in, y_ref):
        del y_in
        mu, rstd = _ln_stats(u2_ref, ts, w, chunks)
        for cs in chunks:
            ln = (u2_ref[:, cs] - mu) * rstd * lg_ref[:, cs] + lb_ref[:, cs]
            z = bz[:, cs].astype(F32)
            y_ref[:, cs] = ((ln * _sigmoid(ln)) * (z * _sigmoid(z))).astype(BF16)

    row = pl.BlockSpec((1, w), lambda i: (0, 0))
    return pl.pallas_call(
        body, name="mix_b_gate_fwd", grid=(s // ts,),
        in_specs=[pl.BlockSpec((ts, w), lambda i: (i, 0)), pl.BlockSpec((ts, w), lambda i: (i, 6)), row, row,
                  pl.BlockSpec(memory_space=pl.ANY)],
        out_specs=pl.BlockSpec((ts, w), lambda i: (i, 1)),
        out_shape=jax.ShapeDtypeStruct(y.shape, y.dtype), input_output_aliases={4: 0}, compiler_params=_cparams(1),
    )(u2, proj, lg, lb, y)


def _post(x, out, tgt, gate, fg, ts):
    s, d = x.shape
    chunks = _col_chunks(d)

    def body(x_ref, o_ref, t_ref, gt_ref, fg_ref, dx2_ref, dout_ref, loss_ref, dgate_ref, dfg_ref):
        @pl.when(pl.program_id(0) == 0)
        def _():
            loss_ref[...] = jnp.zeros_like(loss_ref)
            dgate_ref[...] = jnp.zeros_like(dgate_ref)
            dfg_ref[...] = jnp.zeros_like(dfg_ref)

        acc = jnp.zeros((ts, LANES), F32)
        for cs in chunks:
            x2 = x_ref[:, cs] + gt_ref[:, cs] * o_ref[:, cs]
            acc = acc + x2 * x2
        r2 = lax.rsqrt(_row_sum(acc) / d + EPS)
        acc = jnp.zeros((ts, LANES), F32)
        for cs in chunks:
            n2 = (x_ref[:, cs] + gt_ref[:, cs] * o_ref[:, cs]) * r2
            diff = n2 * fg_ref[:, cs] - t_ref[:, cs]
            loss_ref[:, cs] += jnp.sum(diff * diff, axis=0, keepdims=True)
            dyf = diff / d
            dfg_ref[:, cs] += jnp.sum(dyf * n2, axis=0, keepdims=True)
            acc = acc + (dyf * fg_ref[:, cs]) * n2
        mdot = _row_sum(acc) / d
        for cs in chunks:
            o = o_ref[:, cs]
            n2 = (x_ref[:, cs] + gt_ref[:, cs] * o) * r2
            dn = ((n2 * fg_ref[:, cs] - t_ref[:, cs]) / d) * fg_ref[:, cs]
            dx2 = r2 * (dn - n2 * mdot)
            dx2_ref[:, cs] = dx2
            dgate_ref[:, cs] += jnp.sum(dx2 * o, axis=0, keepdims=True)
            dout_ref[:, cs] = (dx2 * gt_ref[:, cs]).astype(BF16)

    tile = pl.BlockSpec((ts, d), lambda i: (i, 0))
    row = pl.BlockSpec((1, d), lambda i: (0, 0))
    return pl.pallas_call(
        body, name="post", grid=(s // ts,),
        in_specs=[tile, tile, tile, row, row], out_specs=[tile, tile, row, row, row],
        out_shape=[jax.ShapeDtypeStruct((s, d), F32), jax.ShapeDtypeStruct((s, d), BF16)] + [jax.ShapeDtypeStruct((1, d), F32)] * 3,
        compiler_params=_cparams(1),
    )(x, out, tgt, gate, fg)


def _mix_b_gate_bwd(u2, proj, dy, lg, lb, n_cols, ts):
    s, w = u2.shape
    chunks = _col_chunks(w)

    def body(u2_ref, bz, dyb, lg_ref, lb_ref, du2_ref, dbz_ref, dlg_ref, dlb_ref, dbb_ref):
        @pl.when(pl.program_id(0) == 0)
        def _():
            dlg_ref[...] = jnp.zeros_like(dlg_ref)
            dlb_ref[...] = jnp.zeros_like(dlb_ref)
            dbb_ref[...] = jnp.zeros_like(dbb_ref)

        mu, rstd = _ln_stats(u2_ref, ts, w, chunks)
        acc1 = jnp.zeros((ts, LANES), F32)
        acc2 = jnp.zeros((ts, LANES), F32)
        for cs in chunks:
            xh = (u2_ref[:, cs] - mu) * rstd
            ln = xh * lg_ref[:, cs] + lb_ref[:, cs]
            sl = _sigmoid(ln)
            z = bz[:, cs].astype(F32)
            sz = _sigmoid(z)
            g = dyb[:, cs].astype(F32)
            dbz_ref[:, cs] = (g * (ln * sl) * (sz * (1.0 + z * (1.0 - sz)))).astype(BF16)
            dln = g * (z * sz) * (sl * (1.0 + ln * (1.0 - sl)))
            dlg_ref[:, cs] += jnp.sum(dln * xh, axis=0, keepdims=True)
            dlb_ref[:, cs] += jnp.sum(dln, axis=0, keepdims=True)
            dxh = dln * lg_ref[:, cs]
            du2_ref[:, cs] = dxh
            acc1 = acc1 + dxh
            acc2 = acc2 + dxh * xh
        m1 = _row_sum(acc1) / w
        m2 = _row_sum(acc2) / w
        for cs in chunks:
            xh = (u2_ref[:, cs] - mu) * rstd
            du2 = rstd * (du2_ref[:, cs] - m1 - xh * m2)
            du2_ref[:, cs] = du2
            dbb_ref[:, cs] += jnp.sum(du2, axis=0, keepdims=True)

    tile = pl.BlockSpec((ts, w), lambda i: (i, 0))
    row = pl.BlockSpec((1, w), lambda i: (0, 0))
    return pl.pallas_call(
        body, name="mix_b_gate_bwd", grid=(s // ts,),
        in_specs=[tile, pl.BlockSpec((ts, w), lambda i: (i, 6)), pl.BlockSpec((ts, w), lambda i: (i, 1)), row, row],
        out_specs=[tile, pl.BlockSpec((ts, w), lambda i: (i, 6)), row, row, row],
        out_shape=[jax.ShapeDtypeStruct((s, w), F32), jax.ShapeDtypeStruct((s, n_cols), BF16)] + [jax.ShapeDtypeStruct((1, w), F32)] * 3,
        compiler_params=_cparams(1),
    )(u2, proj, dy, lg, lb)


def _mix_b_conv_bwd(du2, proj, wb, dproj, comb, w, ts, cb):
    s = proj.shape[0]
    nb = w // cb
    ns = s // ts
    chunks = _col_chunks(cb)

    u_rows = ts + 2 * HALO_B + SUBLANES
    win = ts + HALO_B
    assert win % DW_BLOCK == 0

    def body(du2_ref, bv, bg, wb_ref, dp_in, comb_ref, dp_ref, dwb_ref, recv_ref, ext, stash, uext, send_sems, recv_sems):
        del dp_in
        q = pl.program_id(2)

        def copies():
            return _chip_copies([comb_ref], [recv_ref], send_sems, recv_sems)

        @pl.when((pl.program_id(0) == 0) & (pl.program_id(1) == 0) & (q == 0))
        def _():
            for cp in copies():
                cp.start()

        @pl.when((pl.program_id(0) == nb - 1) & (pl.program_id(1) == ns - 1) & (q == 1))
        def _():
            for cp in copies():
                cp.wait_recv()
            for cp in copies():
                cp.wait_send()

        @pl.when(jnp.logical_and(pl.program_id(1) == 0, q == 0))
        def _():
            ext[ts : ts + HALO_B, :] = jnp.zeros((HALO_B, cb), F32)
            uext[0:HALO_B, :] = jnp.zeros((HALO_B, cb), F32)
            uext[HALO_B + ts : u_rows, :] = jnp.zeros((u_rows - HALO_B - ts, cb), F32)
            dwb_ref[...] = jnp.zeros_like(dwb_ref)

        @pl.when(q == 0)
        def _():
            ext[0:ts, :] = du2_ref[...]
            for cs in chunks:
                uext[HALO_B : HALO_B + ts, cs] = bv[:, cs].astype(F32) * _sigmoid(bg[:, cs].astype(F32))

                def emit(rows, du, cs=cs):
                    v = bv[rows, cs].astype(F32)
                    sg = _sigmoid(bg[rows, cs].astype(F32))
                    dp_ref[rows, cs] = (du * sg).astype(BF16)
                    stash[rows, cs] = (du * v * (sg * (1.0 - sg))).astype(BF16)

                _tap_sum(ext, cs, ts, [(CONV_B - 1 - k, wb_ref[k : k + 1, cs]) for k in range(CONV_B)], emit)
                for r in range(SUBLANES):
                    ks = [k for k in range(CONV_B) if (CONV_B - 1 - k) % SUBLANES == r]
                    accs = [jnp.zeros((SUBLANES, LANES), F32) for _ in ks]
                    for t0 in range(0, win, DW_BLOCK):
                        e = ext[t0 : t0 + DW_BLOCK, cs]
                        ush = uext[SUBLANES - r + t0 : SUBLANES - r + t0 + DW_BLOCK + HALO_B - SUBLANES, cs]
                        for n, k in enumerate(ks):
                            b0 = HALO_B - SUBLANES - (CONV_B - 1 - k - r)
                            prod = e * ush[b0 : b0 + DW_BLOCK]
                            for v0 in range(0, DW_BLOCK, SUBLANES):
                                accs[n] = accs[n] + prod[v0 : v0 + SUBLANES]
                    for n, k in enumerate(ks):
                        dwb_ref[k : k + 1, cs] += jnp.sum(accs[n], axis=0, keepdims=True)
            ext[ts : ts + HALO_B, :] = ext[0:HALO_B, :]

        @pl.when(q == 1)
        def _():
            dp_ref[...] = stash[...]

    def rev(col):
        return lambda c, i, q: (ns - 1 - i, col(c))

    hbm = pl.BlockSpec(memory_space=pl.ANY)
    return pl.pallas_call(
        body, name="mix_b_conv_bwd", grid=(nb, ns, 2),
        in_specs=[pl.BlockSpec((ts, cb), rev(lambda c: c)), pl.BlockSpec((ts, cb), rev(lambda c: 4 * nb + c)),
                  pl.BlockSpec((ts, cb), rev(lambda c: 5 * nb + c)), pl.BlockSpec((CONV_B, cb), lambda c, i, q: (0, c)), hbm, hbm],
        out_specs=[pl.BlockSpec((ts, cb), lambda c, i, q: (ns - 1 - i, (4 + q) * nb + c)),
                   pl.BlockSpec((CONV_B, cb), lambda c, i, q: (0, c)), hbm],
        out_shape=[jax.ShapeDtypeStruct(dproj.shape, dproj.dtype), jax.ShapeDtypeStruct((CONV_B, w), F32),
                   jax.ShapeDtypeStruct((3,) + comb.shape[1:], comb.dtype)],
        input_output_aliases={4: 0},
        scratch_shapes=[pltpu.VMEM((ts + HALO_B, cb), F32), pltpu.VMEM((ts, cb), BF16), pltpu.VMEM((u_rows, cb), F32),
                        pltpu.SemaphoreType.DMA((1, 3)), pltpu.SemaphoreType.DMA((1, 3))],
        compiler_params=_cparams(3),
    )(du2, proj, proj, wb, dproj, comb)


def _mix_a_bwd(proj, cv, dy, wa, dproj, w, ts, cb):
    s = proj.shape[0]
    nb = w // cb
    ns = s // ts
    chunks = _col_chunks(cb)

    def body(ab, ac, ax, az, cv_ref, dya, wa_ref, dp_in, dp_ref, dwa_ref, ext, stash):
        del dp_in
        q = pl.program_id(2)

        @pl.when(jnp.logical_and(pl.program_id(1) == 0, q == 0))
        def _():
            ext[ts : ts + HALO_A, :] = jnp.zeros((HALO_A, cb), F32)
            dwa_ref[...] = jnp.zeros_like(dwa_ref)

        @pl.when(q == 0)
        def _():
            for cs in chunks:
                b = ab[:, cs].astype(F32)
                z = az[:, cs].astype(F32)
                sz = _sigmoid(z)
                g = dya[:, cs].astype(F32)
                conv = cv_ref[:, cs].astype(F32)
                dp_ref[:, cs] = (g * conv * (z * sz)).astype(BF16)
                stash[2, :, cs] = (g * b * conv * (sz * (1.0 + z * (1.0 - sz)))).astype(BF16)
                ext[0:ts, cs] = g * b * (z * sz)
            for cs in chunks:

                def emit(rows, dca, cs=cs):
                    stash[0, rows, cs] = (dca * ax[rows, cs].astype(F32)).astype(BF16)
                    stash[1, rows, cs] = (dca * ac[rows, cs].astype(F32)).astype(BF16)

                _tap_sum(ext, cs, ts, [(CONV_A - 1 - k, wa_ref[k : k + 1, cs]) for k in range(CONV_A)], emit)
                ca = ac[:, cs].astype(F32) * ax[:, cs].astype(F32)
                for k in range(CONV_A):
                    o = CONV_A - 1 - k
                    dwa_ref[k : k + 1, cs] += jnp.sum(ca * ext[o : o + ts, cs], axis=0, keepdims=True)
            ext[ts : ts + HALO_A, :] = ext[0:HALO_A, :]

        for seg_q in range(1, 4):

            @pl.when(q == seg_q)
            def _(seg_q=seg_q):
                dp_ref[...] = stash[seg_q - 1]

    def rev(col):
        return lambda c, i, q: (ns - 1 - i, col(c))

    def seg(sq):
        return pl.BlockSpec((ts, cb), rev(lambda c: sq * nb + c))

    return pl.pallas_call(
        body, name="mix_a_bwd", grid=(nb, ns, 4),
        in_specs=[seg(0), seg(1), seg(2), seg(3), pl.BlockSpec((ts, cb), rev(lambda c: c)), pl.BlockSpec((ts, cb), rev(lambda c: c)),
                  pl.BlockSpec((CONV_A, cb), lambda c, i, q: (0, c)), pl.BlockSpec(memory_space=pl.ANY)],
        out_specs=[pl.BlockSpec((ts, cb), lambda c, i, q: (ns - 1 - i, q * nb + c)), pl.BlockSpec((CONV_A, cb), lambda c, i, q: (0, c))],
        out_shape=[jax.ShapeDtypeStruct(dproj.shape, dproj.dtype), jax.ShapeDtypeStruct((CONV_A, w), F32)],
        input_output_aliases={7: 0},
        scratch_shapes=[pltpu.VMEM((ts + HALO_A, cb), F32), pltpu.VMEM((3, ts, cb), BF16)], compiler_params=_cparams(3),
    )(proj, proj, proj, proj, cv, dy, wa, dproj)


def _pre_bwd(x, dh, dx2, g1, scale, ts):
    s, d = x.shape
    chunks = _col_chunks(d)

    def body(x_ref, dh_ref, dx2_ref, g_ref, sc_ref, gx_ref, dsh_ref, dsc_ref, dg_ref):
        @pl.when(pl.program_id(0) == 0)
        def _():
            dsh_ref[...] = jnp.zeros_like(dsh_ref)
            dsc_ref[...] = jnp.zeros_like(dsc_ref)
            dg_ref[...] = jnp.zeros_like(dg_ref)

        acc = jnp.zeros((ts, LANES), F32)
        for cs in chunks:
            v = x_ref[:, cs]
            acc = acc + v * v
        r1 = lax.rsqrt(_row_sum(acc) / d + EPS)
        acc = jnp.zeros((ts, LANES), F32)
        for cs in chunks:
            xn = x_ref[:, cs] * r1
            g = dh_ref[:, cs]
            dsh_ref[:, cs] += jnp.sum(g, axis=0, keepdims=True)
            dsc_ref[:, cs] += jnp.sum(g * (xn * g_ref[:, cs]), axis=0, keepdims=True)
            dn1 = g * (1.0 + sc_ref[:, cs])
            dg_ref[:, cs] += jnp.sum(dn1 * xn, axis=0, keepdims=True)
            acc = acc + (dn1 * g_ref[:, cs]) * xn
        mdot = _row_sum(acc) / d
        for cs in chunks:
            xn = x_ref[:, cs] * r1
            dxn = (dh_ref[:, cs] * (1.0 + sc_ref[:, cs])) * g_ref[:, cs]
            gx_ref[:, cs] = dx2_ref[:, cs] + r1 * (dxn - xn * mdot)

    tile = pl.BlockSpec((ts, d), lambda i: (i, 0))
    row = pl.BlockSpec((1, d), lambda i: (0, 0))
    return pl.pallas_call(
        body, name="pre_bwd", grid=(s // ts,),
        in_specs=[tile, tile, tile, row, row], out_specs=[tile, row, row, row],
        out_shape=[jax.ShapeDtypeStruct((s, d), F32)] + [jax.ShapeDtypeStruct((1, d), F32)] * 3,
        compiler_params=_cparams(1),
    )(x, dh, dx2, g1, scale)


def _adamw(w, g, m, v):
    m = ADAM_B1 * m + (1.0 - ADAM_B1) * g
    v = ADAM_B2 * v + (1.0 - ADAM_B2) * (g * g)
    m_hat = m / (1.0 - ADAM_B1**ADAM_STEP)
    v_hat = v / (1.0 - ADAM_B2**ADAM_STEP)
    delta = -ADAM_LR * (m_hat / (jnp.sqrt(v_hat) + ADAM_EPS) + ADAM_WD * w)
    return delta, m, v


def _combine_halves(part, swapped, core, tr):
    _, r, n = part.shape

    def body(core_ref, p_ref, s_ref, o_ref):
        del core_ref
        o_ref[...] = (p_ref[...].astype(F32) + s_ref[...].astype(F32)).astype(BF16)

    return pl.pallas_call(
        body, name="combine_halves",
        grid_spec=pltpu.PrefetchScalarGridSpec(
            num_scalar_prefetch=1, grid=(N_CHIP, r // tr),
            in_specs=[pl.BlockSpec((None, tr, n), lambda q, i, cr: (2 * q + cr[0], i, 0)), pl.BlockSpec((None, tr, n), lambda q, i, cr: (q, i, 0))],
            out_specs=pl.BlockSpec((None, tr, n), lambda q, i, cr: (q, i, 0))),
        out_shape=jax.ShapeDtypeStruct((N_CHIP, r, n), BF16), compiler_params=_cparams(2),
    )(core, part, swapped)


def _adam_sharded(w, m, v, comb, recv, chip, tr, name):
    r, n = w.shape

    def body(chip_ref, w_ref, m_ref, v_ref, c_ref, r_ref, g_ref, d_ref, nm_ref, nv_ref):
        del chip_ref
        g = c_ref[...].astype(F32) + r_ref[0].astype(F32) + r_ref[1].astype(F32) + r_ref[2].astype(F32)
        delta, nm, nv = _adamw(w_ref[...], g, m_ref[...], v_ref[...])
        g_ref[...] = g
        d_ref[...] = delta
        nm_ref[...] = nm
        nv_ref[...] = nv

    tile = pl.BlockSpec((tr, n), lambda i, ch: (i, 0))
    return pl.pallas_call(
        body, name=name,
        grid_spec=pltpu.PrefetchScalarGridSpec(
            num_scalar_prefetch=1, grid=(r // tr,),
            in_specs=[tile, tile, tile, pl.BlockSpec((None, tr, n), lambda i, ch: (ch[0], i, 0)), pl.BlockSpec((3, tr, n), lambda i, ch: (0, i, 0))],
            out_specs=[tile] * 4),
        out_shape=[jax.ShapeDtypeStruct((r, n), F32)] * 4, compiler_params=_cparams(1),
    )(chip, w, m, v, comb, recv)


def _adam_w_ada(c_t, dm_loc, w, m, v, tr):
    d, nb = c_t.shape
    na = w.shape[1]

    def body(c_ref, dm_ref, w_ref, m_ref, v_ref, g_ref, d_ref, nm_ref, nv_ref):
        g = jnp.dot(c_ref[...], dm_ref[...], preferred_element_type=F32, precision=lax.Precision.HIGHEST)
        delta, nm, nv = _adamw(w_ref[...], g, m_ref[...], v_ref[...])
        g_ref[...] = g
        d_ref[...] = delta
        nm_ref[...] = nm
        nv_ref[...] = nv

    tile = pl.BlockSpec((tr, na), lambda i: (i, 0))
    return pl.pallas_call(
        body, name="adam_w_ada", grid=(d // tr,),
        in_specs=[pl.BlockSpec((tr, nb), lambda i: (i, 0)), pl.BlockSpec((nb, na), lambda i: (0, 0)), tile, tile, tile],
        out_specs=[tile] * 4, out_shape=[jax.ShapeDtypeStruct((d, na), F32)] * 4, compiler_params=_cparams(1),
    )(c_t, dm_loc, w, m, v)


def _small_final(me, gathered, params, d, w, cl):
    flat_params = [t for p in params for t in p]

    def total(ref):
        acc = ref[0]
        for b in range(1, N_DEV):
            acc = acc + ref[b]
        return acc

    def body(me_ref, loss_g, dfg_g, dg1_g, dsh_g, dsc_g, dgt_g, dbb_g, dlg_g, dlb_g, dwa_g, dwb_g, *rest):
        del me_ref
        prm = rest[:24]
        loss_ref = rest[24]
        outs = rest[25:]

        loss_ref[...] = jnp.sum(total(loss_g), axis=1, keepdims=True) * (0.5 / d)

        def update(idx, g, cols=None):
            w_ref, m_ref, v_ref = prm[3 * idx : 3 * idx + 3]
            g_ref, d_ref, nm_ref, nv_ref = outs[4 * idx : 4 * idx + 4]
            sl = (slice(None), slice(None)) if cols is None else (slice(None), cols)
            delta, nm, nv = _adamw(w_ref[sl], g, m_ref[sl], v_ref[sl])
            g_ref[sl] = g
            d_ref[sl] = delta
            nm_ref[sl] = nm
            nv_ref[sl] = nv

        update(0, total(dg1_g))
        update(1, total(dsh_g), slice(0, d))
        update(1, total(dsc_g), slice(d, 2 * d))
        update(1, total(dgt_g), slice(2 * d, 3 * d))
        update(2, total(dwa_g))
        update(3, total(dwb_g))
        update(4, total(dbb_g))
        update(5, total(dlg_g))
        update(6, total(dlb_g))
        update(7, total(dfg_g))

    def full(shape):
        nd = len(shape)
        return pl.BlockSpec(shape, lambda i, mr: (0,) * nd)

    in_specs = [full(g.shape) for g in gathered[:9]]
    in_specs += [pl.BlockSpec((N_DEV, CONV_A, cl), lambda i, mr: (0, 0, mr[0])), pl.BlockSpec((N_DEV, CONV_B, cl), lambda i, mr: (0, 0, mr[0]))]
    in_specs += [full(t.shape) for t in flat_params]
    out_shapes = [jax.ShapeDtypeStruct((1, 1), F32)]
    for p in params:
        out_shapes += [jax.ShapeDtypeStruct(p[0].shape, F32)] * 4
    return pl.pallas_call(
        body, name="small_final",
        grid_spec=pltpu.PrefetchScalarGridSpec(
            num_scalar_prefetch=1, grid=(1,), in_specs=in_specs, out_specs=[full(o.shape) for o in out_shapes]),
        out_shape=out_shapes, compiler_params=_cparams(1),
    )(me, *gathered, *flat_params)


def _silu_rows(c):
    def body(c_ref, o_ref):
        v = c_ref[...]
        o_ref[...] = v * _sigmoid(v)

    return pl.pallas_call(body, name="silu_c", out_shape=jax.ShapeDtypeStruct(c.shape, F32), compiler_params=_cparams())(c)


def kernel(x, c, norm_g, w_ada, b_ada, w_in, conv_a_w, conv_b_w, conv_b_b, ln_b_g, ln_b_b, w_out, final_g, loss_target, m_norm_g, m_w_ada, m_b_ada, m_w_in, m_conv_a_w, m_conv_b_w, m_conv_b_b, m_ln_b_g, m_ln_b_b, m_w_out, m_final_g, v_norm_g, v_w_ada, v_b_ada, v_w_in, v_conv_a_w, v_conv_b_w, v_conv_b_b, v_ln_b_g, v_ln_b_b, v_w_out, v_final_g):
    _, s, d = x.shape
    w = conv_b_b.shape[-1]
    cl = conv_a_w.shape[-1]
    na = w_ada.shape[-1]
    assert ln_b_g.shape[-1] == w and w_out.shape[1] * N_DEV == 2 * w and w_in.shape[-1] * N_DEV == 7 * w and cl * N_DEV == w
    ts = min(256, s)
    ts_row = min(128, s)
    cb = min(512, w)
    tm = min(512, s)
    tn = min(1024, d)

    px, py, pc = _position()
    me = 4 * px + 2 * py + pc
    me_arr = jnp.reshape(me, (1,)).astype(jnp.int32)
    core_arr = jnp.reshape(pc, (1,)).astype(jnp.int32)
    chip_arr = jnp.reshape(2 * px + py, (1,)).astype(jnp.int32)

    x2d, tgt = x[0], loss_target[0]
    w_ada2, m_w_ada2, v_w_ada2 = w_ada[0], m_w_ada[0], v_w_ada[0]
    w_in2, m_w_in2, v_w_in2 = w_in[0], m_w_in[0], v_w_in[0]
    w_out2, m_w_out2, v_w_out2 = w_out[0], m_w_out[0], v_w_out[0]
    fg = final_g.reshape(1, d)

    c_act = _silu_rows(c)
    c_all, wa_all, wb_all = _all_gather_vmem([c_act, conv_a_w[0], conv_b_w[0]], "gather_small_in")
    c_all = c_all.reshape(N_DEV, d)
    wa = jnp.transpose(wa_all, (1, 0, 2)).reshape(CONV_A, w)
    wb = jnp.transpose(wb_all, (1, 0, 2)).reshape(CONV_B, w)
    b_loc = lax.dynamic_slice_in_dim(b_ada, me * na, na, axis=1)
    mod_loc = _mod_matmul(c_all, w_ada2, b_loc, min(512, d))
    (mod_all,) = _all_gather_vmem([mod_loc], "gather_mod")
    mod_mine = lax.dynamic_index_in_dim(mod_all, me, axis=1, keepdims=False).reshape(1, 3 * d)
    shift, scale, gate = mod_mine[:, :d], mod_mine[:, d : 2 * d], mod_mine[:, 2 * d :]

    h = _prep(x2d, norm_g, scale, shift, ts_row)
    proj, w_in_full, w_out_full = _proj_gather(h, w_in2.astype(BF16), w_out2.astype(BF16), _arrival_order(), tm)
    w_out_full = w_out_full.reshape(2 * w, d)
    y, cv = _mix_a_fwd(proj, wa, w, ts, cb)
    u2 = _mix_b_conv_fwd(proj, wb, conv_b_b, w, ts, cb)
    y = _mix_b_gate_fwd(u2, proj, ln_b_g, ln_b_b, y, ts)
    out = _mm_out(y, w_out_full, tm, tn)
    dx2, dout, loss_row, dgate_row, dfg_row = _post(x2d, out, tgt, gate, fg, ts_row)

    dy = _mm_dy(dout, w_out_full, tm, tn)
    part_out = _mm_dw_out(y, dout, tn)
    (swapped_out,) = _swap_with_sibling([part_out], "swap_halves_out")
    comb_out = _combine_halves(part_out, swapped_out, core_arr, min(256, part_out.shape[1]))
    du2, dproj, dlg_row, dlb_row, dbb_row = _mix_b_gate_bwd(u2, proj, dy, ln_b_g, ln_b_b, 7 * w, ts)
    dproj, dwb_part, recv_out = _mix_b_conv_bwd(du2, proj, wb, dproj, comb_out, w, ts, cb)
    dproj, dwa_part = _mix_a_bwd(proj, cv, dy, wa, dproj, w, ts, cb)
    part_in = _mm_dw_in(h, dproj, tm)

    (swapped_in,) = _swap_with_sibling([part_in], "swap_halves_in")
    comb_in = _combine_halves(part_in, swapped_in, core_arr, min(256, d))
    dh, (recv_in,) = _mm_dh(dproj, w_in_full, [comb_in], min(1024, s), tn)
    grad_x, dshift_row, dscale_row, dg1_row = _pre_bwd(x2d, dh, dx2, norm_g, scale, ts_row)
    g_w_in, d_w_in, nm_w_in, nv_w_in = _adam_sharded(w_in2, m_w_in2, v_w_in2, comb_in, recv_in, chip_arr, min(128, d), "adam_w_in")
    g_w_out, d_w_out, nm_w_out, nv_w_out = _adam_sharded(w_out2, m_w_out2, v_w_out2, comb_out, recv_out, chip_arr, min(128, w_out2.shape[0]), "adam_w_out")

    gathered = _all_gather_vmem(
        [loss_row, dfg_row, dg1_row, dshift_row, dscale_row, dgate_row, dbb_row, dlg_row, dlb_row, dwa_part, dwb_part], "gather_small_grads")
    dmod_all = jnp.concatenate([gathered[3].reshape(N_DEV, d), gathered[4].reshape(N_DEV, d), gathered[5].reshape(N_DEV, d)], axis=1)
    dm_loc = lax.dynamic_slice_in_dim(dmod_all, me * na, na, axis=1)
    g_w_ada, d_w_ada, nm_w_ada, nv_w_ada = _adam_w_ada(jnp.transpose(c_all), dm_loc, w_ada2, m_w_ada2, v_w_ada2, min(256, d))

    params = [
        (norm_g, m_norm_g, v_norm_g), (b_ada, m_b_ada, v_b_ada), (conv_a_w[0], m_conv_a_w[0], v_conv_a_w[0]),
        (conv_b_w[0], m_conv_b_w[0], v_conv_b_w[0]), (conv_b_b, m_conv_b_b, v_conv_b_b), (ln_b_g, m_ln_b_g, v_ln_b_g),
        (ln_b_b, m_ln_b_b, v_ln_b_b), (fg, m_final_g.reshape(1, d), v_final_g.reshape(1, d)),
    ]
    small = _small_final(me_arr, gathered, params, d, w, cl)
    loss = small[0].reshape(())
    sm = [small[1 + 4 * i : 5 + 4 * i] for i in range(8)]

    def pick(k):
        return [
            sm[0][k], (g_w_ada, d_w_ada, nm_w_ada, nv_w_ada)[k][None], sm[1][k], (g_w_in, d_w_in, nm_w_in, nv_w_in)[k][None],
            sm[2][k][None], sm[3][k][None], sm[4][k], sm[5][k], sm[6][k], (g_w_out, d_w_out, nm_w_out, nv_w_out)[k][None],
            sm[7][k].reshape(d),
        ]

    return (loss, grad_x[None], *pick(0), *pick(1), *pick(2), *pick(3))
```

```python
import functools

import jax
import jax.numpy as jnp
from jax import lax
from jax.experimental import pallas as pl
from jax.experimental.pallas import tpu as pltpu

F32 = jnp.float32
BF16 = jnp.bfloat16
N_DEV = 8
N_CHIP = 4
EPS = 1e-6
ADAM_LR = 0.001
ADAM_B1 = 0.9
ADAM_B2 = 0.999
ADAM_EPS = 1e-08
ADAM_WD = 0.01
ADAM_STEP = 10
CONV_A = 3
CONV_B = 31
HALO_A = 8
HALO_B = 32
LANES = 128
VMEM_LIMIT = 56 * 1024 * 1024
MESH = pl.DeviceIdType.MESH


def _cparams(n_grid_axes=0):
    if n_grid_axes:
        return pltpu.CompilerParams(dimension_semantics=("arbitrary",) * n_grid_axes, vmem_limit_bytes=VMEM_LIMIT)
    return pltpu.CompilerParams(vmem_limit_bytes=VMEM_LIMIT)


def _flip(v, bit):
    return 1 - v if bit else v


def _position():
    return lax.axis_index("x"), lax.axis_index("y"), lax.axis_index("c")


def _sigmoid(v):
    return jax.nn.sigmoid(v)


def _all_gather_vmem(arrs, name):
    n = len(arrs)

    def body(*refs):
        ins, outs = refs[:n], refs[n : 2 * n]
        send_sems, recv_sems = refs[2 * n :]
        x, y, c = _position()
        me = 4 * x + 2 * y + c
        for a in range(n):
            outs[a][me] = ins[a][...]
        sends = []
        for k in range(1, N_DEV):
            peer = (_flip(x, k & 4), _flip(y, k & 2), _flip(c, k & 1))
            for a in range(n):
                cp = pltpu.make_async_remote_copy(
                    src_ref=ins[a], dst_ref=outs[a].at[me], send_sem=send_sems.at[a, k - 1], recv_sem=recv_sems.at[a, k - 1],
                    device_id=peer, device_id_type=MESH)
                cp.start()
                sends.append(cp)
        for k in range(1, N_DEV):
            peer = (_flip(x, k & 4), _flip(y, k & 2), _flip(c, k & 1))
            src = 4 * peer[0] + 2 * peer[1] + peer[2]
            for a in range(n):
                pltpu.make_async_remote_copy(
                    src_ref=ins[a], dst_ref=outs[a].at[src], send_sem=send_sems.at[a, k - 1], recv_sem=recv_sems.at[a, k - 1],
                    device_id=peer, device_id_type=MESH).wait_recv()
        for cp in sends:
            cp.wait_send()

    vm = pl.BlockSpec(memory_space=pltpu.VMEM)
    return pl.pallas_call(
        body, name=name,
        out_shape=[jax.ShapeDtypeStruct((N_DEV,) + a.shape, a.dtype) for a in arrs],
        in_specs=[vm] * n, out_specs=[vm] * n,
        scratch_shapes=[pltpu.SemaphoreType.DMA((n, N_DEV - 1)), pltpu.SemaphoreType.DMA((n, N_DEV - 1))],
        compiler_params=_cparams(),
    )(*arrs)


def _swap_with_sibling(parts, name):
    n = len(parts)

    def body(*refs):
        ins, outs = refs[:n], refs[n : 2 * n]
        send_sems, recv_sems = refs[2 * n :]
        x, y, c = _position()
        sibling = (x, y, 1 - c)
        sends = []
        for a in range(n):
            for q in range(N_CHIP):
                cp = pltpu.make_async_remote_copy(
                    src_ref=ins[a].at[2 * q + (1 - c)], dst_ref=outs[a].at[q], send_sem=send_sems.at[a, q],
                    recv_sem=recv_sems.at[a, q], device_id=sibling, device_id_type=MESH)
                cp.start()
                sends.append(cp)
        for cp in sends:
            cp.wait_recv()
        for cp in sends:
            cp.wait_send()

    hbm = pl.BlockSpec(memory_space=pl.ANY)
    return pl.pallas_call(
        body, name=name,
        out_shape=[jax.ShapeDtypeStruct((N_CHIP,) + p.shape[1:], p.dtype) for p in parts],
        in_specs=[hbm] * n, out_specs=[hbm] * n,
        scratch_shapes=[pltpu.SemaphoreType.DMA((n, N_CHIP)), pltpu.SemaphoreType.DMA((n, N_CHIP))],
        compiler_params=_cparams(),
    )(*parts)


NN = (((1,), (0,)), ((), ()))
NT = (((1,), (1,)), ((), ()))
TN = (((0,), (0,)), ((), ()))


def _n_chunk(n):
    return 256 if n % 256 == 0 else LANES


def _mm_body(dims, n_out, accumulate):
    nc = _n_chunk(n_out)

    def body(a_ref, b_ref, o_ref):
        for n0 in range(0, n_out, nc):
            b = b_ref[n0 : n0 + nc, :] if dims is NT else b_ref[:, n0 : n0 + nc]
            r = lax.dot_general(a_ref[...], b, dims, preferred_element_type=F32)
            if accumulate:
                k = pl.program_id(2)

                @pl.when(k == 0)
                def _():
                    o_ref[:, n0 : n0 + nc] = r

                @pl.when(k > 0)
                def _():
                    o_ref[:, n0 : n0 + nc] += r
            else:
                o_ref[:, n0 : n0 + nc] = r.astype(o_ref.dtype)

    return body


def _ring_chips():
    x, y, c = _position()
    first = (x + (1 - c) - 2 * x * (1 - c), y + c - 2 * y * c)
    second = (x + c - 2 * x * c, y + (1 - c) - 2 * y * (1 - c))
    return first, second, (1 - x, 1 - y)


def _arrival_order():
    x, y, c = _position()
    first, second, diag = _ring_chips()
    blocks = [(x, y, c), (x, y, 1 - c), (*first, c), (*second, 1 - c), (*second, c), (*first, 1 - c), (*diag, c), (*diag, 1 - c)]
    return jnp.stack([4 * b[0] + 2 * b[1] + b[2] for b in blocks]).astype(jnp.int32)


def _proj_gather(h, w_in_loc, w_out_loc, order, tm):
    s, d = h.shape
    ne = w_in_loc.shape[1]
    ni = s // tm
    nc = _n_chunk(ne)
    n_streams = 7

    def body(order_ref, h_ref, win_ref, wout_ref, proj_ref, fin_ref, fout_ref, bbuf, send_sems, recv_sems, local_sems, load_sems):
        ins, fulls = [win_ref, wout_ref], [fin_ref, fout_ref]
        jj, i = pl.program_id(0), pl.program_id(1)
        x, y, c = _position()
        me = (x, y, c)
        sibling = (x, y, 1 - c)
        first, second, diag = _ring_chips()
        sent = [(me, sibling), (me, (*first, c)), (me, (*second, c)), ((*first, c), (*second, c)),
                ((*first, c), sibling), ((*second, c), sibling), ((*diag, c), sibling)]
        received = [sibling, (*first, c), (*second, c), (*diag, c), (*second, 1 - c), (*first, 1 - c), (*diag, 1 - c)]

        def slot(p):
            return 4 * p[0] + 2 * p[1] + p[2]

        def copy(a, k, block, to):
            dst = fulls[a].at[slot(block)]
            return pltpu.make_async_remote_copy(
                src_ref=ins[a] if k < 3 else dst, dst_ref=dst, send_sem=send_sems.at[a, k], recv_sem=recv_sems.at[a, k],
                device_id=to, device_id_type=MESH)

        def send(a, k):
            return copy(a, k, *sent[k])

        def recv(a, k):
            return copy(a, k, received[k], me)

        def own(a):
            return pltpu.make_async_copy(ins[a], fulls[a].at[slot(me)], local_sems.at[a])

        def load(src, buf):
            return pltpu.make_async_copy(src, bbuf.at[buf], load_sems.at[buf])

        @pl.when((jj == 0) & (i == 0))
        def _():
            load(win_ref, 0).start()
            for a in range(2):
                own(a).start()
            send(0, 0).start()
            send(0, 1).start()
            send(1, 0).start()
            load(win_ref, 0).wait()

        stream_of = [None, 0, 1, 4, 2, 5, 3, 6]
        passes = {1: [(0, 4), (0, 2), (0, 3), (1, 1), (1, 2)], 2: [(0, 5)], 3: [(0, 6)]}
        for nxt in range(1, N_DEV):

            @pl.when((jj == nxt - 1) & (i == ni - 1))
            def _(nxt=nxt):
                k = stream_of[nxt]
                recv(0, k).wait_recv()
                for a, k2 in passes.get(k, []):
                    send(a, k2).start()
                if nxt == N_DEV - 1:
                    recv(1, 1).wait_recv()
                    send(1, 4).start()
                    send(1, 3).start()
                    recv(1, 2).wait_recv()
                    send(1, 5).start()
                load(fin_ref.at[order_ref[nxt]], nxt % 2).start()

        @pl.when((jj > 0) & (i == 0))
        def _():
            load(fin_ref.at[0], jj % 2).wait()

        for n0 in range(0, ne, nc):
            proj_ref[:, n0 : n0 + nc] = lax.dot_general(
                h_ref[...], bbuf[jj % 2, :, n0 : n0 + nc], NN, preferred_element_type=F32).astype(BF16)

        @pl.when((jj == N_DEV - 1) & (i == ni - 1))
        def _():
            recv(1, 3).wait_recv()
            send(1, 6).start()
            for k in (0, 4, 5, 6):
                recv(1, k).wait_recv()
            for a in range(2):
                for k in range(n_streams):
                    send(a, k).wait_send()
                own(a).wait()

    hbm = pl.BlockSpec(memory_space=pl.ANY)
    return pl.pallas_call(
        body, name="proj_gather",
        grid_spec=pltpu.PrefetchScalarGridSpec(
            num_scalar_prefetch=1, grid=(N_DEV, ni),
            in_specs=[pl.BlockSpec((tm, d), lambda jj, i, od: (i, 0)), hbm, hbm],
            out_specs=[pl.BlockSpec((tm, ne), lambda jj, i, od: (i, od[jj])), hbm, hbm],
            scratch_shapes=[pltpu.VMEM((2, d, ne), BF16), pltpu.SemaphoreType.DMA((2, 7)), pltpu.SemaphoreType.DMA((2, 7)),
                            pltpu.SemaphoreType.DMA((2,)), pltpu.SemaphoreType.DMA((2,))]),
        out_shape=[jax.ShapeDtypeStruct((s, N_DEV * ne), BF16), jax.ShapeDtypeStruct((N_DEV,) + w_in_loc.shape, BF16),
                   jax.ShapeDtypeStruct((N_DEV,) + w_out_loc.shape, BF16)],
        compiler_params=_cparams(2),
    )(order, h, w_in_loc, w_out_loc)


def _mm_out(y, w_out, tm, tn):
    s, m = y.shape
    _, d = w_out.shape
    return pl.pallas_call(
        _mm_body(NN, tn, False), name="mm_out", grid=(d // tn, s // tm),
        in_specs=[pl.BlockSpec((tm, m), lambda j, i: (i, 0)), pl.BlockSpec((m, tn), lambda j, i: (0, j))],
        out_specs=pl.BlockSpec((tm, tn), lambda j, i: (i, j)),
        out_shape=jax.ShapeDtypeStruct((s, d), F32), compiler_params=_cparams(2),
    )(y, w_out)


def _mm_dy(dout, w_out, tm, tn):
    s, d = dout.shape
    m, _ = w_out.shape
    return pl.pallas_call(
        _mm_body(NT, tn, False), name="mm_dy", grid=(m // tn, s // tm),
        in_specs=[pl.BlockSpec((tm, d), lambda j, i: (i, 0)), pl.BlockSpec((tn, d), lambda j, i: (j, 0))],
        out_specs=pl.BlockSpec((tm, tn), lambda j, i: (i, j)),
        out_shape=jax.ShapeDtypeStruct((s, m), BF16), compiler_params=_cparams(2),
    )(dout, w_out)


def _mm_dw_out(y, dout, tn):
    s, m = y.shape
    _, d = dout.shape
    mr = m // N_DEV
    return pl.pallas_call(
        _mm_body(TN, tn, False), name="mm_dw_out", grid=(d // tn, N_DEV),
        in_specs=[pl.BlockSpec((s, mr), lambda j, i: (0, i)), pl.BlockSpec((s, tn), lambda j, i: (0, j))],
        out_specs=pl.BlockSpec((None, mr, tn), lambda j, i: (i, 0, j)),
        out_shape=jax.ShapeDtypeStruct((N_DEV, mr, d), BF16), compiler_params=_cparams(2),
    )(y, dout)


def _mm_dw_in(h, dproj, tm):
    s, d = h.shape
    ne = dproj.shape[1] // N_DEV
    return pl.pallas_call(
        _mm_body(TN, ne, False), name="mm_dw_in", grid=(N_DEV, d // tm),
        in_specs=[pl.BlockSpec((s, tm), lambda j, i: (0, i)), pl.BlockSpec((s, ne), lambda j, i: (0, j))],
        out_specs=pl.BlockSpec((None, tm, ne), lambda j, i: (j, i, 0)),
        out_shape=jax.ShapeDtypeStruct((N_DEV, d, ne), BF16), compiler_params=_cparams(2),
    )(h, dproj)


def _chip_copies(c_refs, r_refs, send_sems, recv_sems):
    x, y, c = _position()
    chips = [(1 - x, y), (x, 1 - y), (1 - x, 1 - y)]
    return [
        pltpu.make_async_remote_copy(
            src_ref=c_refs[a].at[2 * chip[0] + chip[1]], dst_ref=r_refs[a].at[j], send_sem=send_sems.at[a, j],
            recv_sem=recv_sems.at[a, j], device_id=(*chip, c), device_id_type=MESH)
        for a in range(len(c_refs)) for j, chip in enumerate(chips)]


def _mm_dh(dproj, w_full, combs, tm, tn):
    s = dproj.shape[0]
    _, d, ne = w_full.shape
    kb = 2
    nk = N_DEV // kb
    nc = _n_chunk(tn)
    mh = min(512, tm)
    nq = len(combs)
    grid = (s // tm, d // tn, nk)

    def body(*refs):
        a_ref, b_ref = refs[:2]
        c_refs = refs[2 : 2 + nq]
        o_ref = refs[2 + nq]
        r_refs = refs[3 + nq : 3 + 2 * nq]
        send_sems, recv_sems = refs[3 + 2 * nq :]
        i, n, k = pl.program_id(0), pl.program_id(1), pl.program_id(2)

        def copies():
            return _chip_copies(c_refs, r_refs, send_sems, recv_sems)

        @pl.when((i == 0) & (n == 0) & (k == 0))
        def _():
            for cp in copies():
                cp.start()

        @pl.when(k == 0)
        def _():
            o_ref[...] = jnp.zeros_like(o_ref)

        for m0 in range(0, tm, mh):
            for n0 in range(0, tn, nc):
                r = None
                for jj in range(kb):
                    t = lax.dot_general(a_ref[m0 : m0 + mh, jj * ne : (jj + 1) * ne], b_ref[jj, n0 : n0 + nc, :], NT, preferred_element_type=F32)
                    r = t if r is None else r + t
                o_ref[m0 : m0 + mh, n0 : n0 + nc] += r

        @pl.when((i == grid[0] - 1) & (n == grid[1] - 1) & (k == nk - 1))
        def _():
            for cp in copies():
                cp.wait_recv()
            for cp in copies():
                cp.wait_send()

    hbm = pl.BlockSpec(memory_space=pl.ANY)
    outs = pl.pallas_call(
        body, name="mm_dh", grid=grid,
        in_specs=[pl.BlockSpec((tm, kb * ne), lambda i, n, k: (i, k)), pl.BlockSpec((kb, tn, ne), lambda i, n, k: (k, n, 0))] + [hbm] * nq,
        out_specs=[pl.BlockSpec((tm, tn), lambda i, n, k: (i, n))] + [hbm] * nq,
        out_shape=[jax.ShapeDtypeStruct((s, d), F32)] + [jax.ShapeDtypeStruct((3,) + cb.shape[1:], cb.dtype) for cb in combs],
        scratch_shapes=[pltpu.SemaphoreType.DMA((nq, 3)), pltpu.SemaphoreType.DMA((nq, 3))],
        compiler_params=_cparams(3),
    )(dproj, w_full, *combs)
    return outs[0], outs[1:]


def _mod_matmul(c_all, w_ada, b_loc, tk):
    nb, d = c_all.shape
    na = w_ada.shape[1]

    def body(c_ref, w_ref, b_ref, o_ref):
        k = pl.program_id(0)
        r = jnp.dot(c_ref[...], w_ref[...], preferred_element_type=F32, precision=lax.Precision.HIGHEST)

        @pl.when(k == 0)
        def _():
            o_ref[...] = r + b_ref[...]

        @pl.when(k > 0)
        def _():
            o_ref[...] += r

    return pl.pallas_call(
        body, name="mod_matmul", grid=(d // tk,),
        in_specs=[pl.BlockSpec((nb, tk), lambda k: (0, k)), pl.BlockSpec((tk, na), lambda k: (k, 0)), pl.BlockSpec((1, na), lambda k: (0, 0))],
        out_specs=pl.BlockSpec((nb, na), lambda k: (0, 0)),
        out_shape=jax.ShapeDtypeStruct((nb, na), F32), compiler_params=_cparams(1),
    )(c_all, w_ada, b_loc)


def _col_chunks(width):
    return [slice(c0, c0 + LANES) for c0 in range(0, width, LANES)]


def _row_sum(acc):
    return jnp.sum(acc, axis=1, keepdims=True)


SUBLANES = 8


ROW_BLOCK = 64
DW_BLOCK = 96


def _tap_sum(ext, cs, ts, taps, emit):
    for t0 in range(0, ts, ROW_BLOCK):
        rb = min(ROW_BLOCK, ts - t0)
        acc = None
        for r in range(SUBLANES):
            group = [(o, wv) for o, wv in taps if o % SUBLANES == r]
            if not group:
                continue
            n = rb if r == 0 else rb + SUBLANES
            v = None
            for o, wv in group:
                term = wv * ext[t0 + o - r : t0 + o - r + n, cs]
                v = term if v is None else v + term
            part = v if r == 0 else v[r : r + rb]
            acc = part if acc is None else acc + part
        emit(slice(t0, t0 + rb), acc)


class _SegmentStores:
    def __init__(self, stash, dst, sems, n_seg, step, n_steps, row0, cols):
        self.stash, self.dst, self.sems, self.n_seg = stash, dst, sems, n_seg
        self.step, self.n_steps, self.row0, self.cols = step, n_steps, row0, cols
        self.slot = step % 2

    def _copy(self, slot, g, row0, col0):
        ts, cb = self.stash.shape[2:]
        return pltpu.make_async_copy(self.stash.at[slot, g], self.dst.at[pl.ds(row0, ts), pl.ds(col0, cb)], self.sems.at[slot, g])

    def _wait(self, slot):
        for g in range(self.n_seg):
            self._copy(slot, g, 0, 0).wait()

    def begin(self):
        @pl.when(self.step >= 2)
        def _():
            self._wait(self.slot)

        return self.slot

    def finish(self):
        ts = self.stash.shape[2]
        for g in range(self.n_seg):
            self._copy(self.slot, g, pl.multiple_of(self.row0, ts), pl.multiple_of(self.cols[g], LANES)).start()

        @pl.when(self.step == self.n_steps - 1)
        def _():
            self._wait(self.slot)
            if self.n_steps >= 2:
                self._wait(1 - self.slot)


def _prep(x, g1, scale, shift, ts):
    s, d = x.shape
    chunks = _col_chunks(d)

    def body(x_ref, g_ref, sc_ref, sh_ref, h_ref):
        acc = jnp.zeros((ts, LANES), F32)
        for cs in chunks:
            v = x_ref[:, cs]
            acc = acc + v * v
        r = lax.rsqrt(_row_sum(acc) / d + EPS)
        for cs in chunks:
            n1 = (x_ref[:, cs] * r) * g_ref[:, cs]
            h_ref[:, cs] = (n1 * (1.0 + sc_ref[:, cs]) + sh_ref[:, cs]).astype(BF16)

    row = pl.BlockSpec((1, d), lambda i: (0, 0))
    return pl.pallas_call(
        body, name="prep", grid=(s // ts,),
        in_specs=[pl.BlockSpec((ts, d), lambda i: (i, 0)), row, row, row],
        out_specs=pl.BlockSpec((ts, d), lambda i: (i, 0)),
        out_shape=jax.ShapeDtypeStruct((s, d), BF16), compiler_params=_cparams(1),
    )(x, g1, scale, shift)


def _mix_a_fwd(proj, wa, w, ts, cb):
    s = proj.shape[0]
    nb = w // cb
    chunks = _col_chunks(cb)

    def body(ab, ac, ax, az, wa_ref, y_ref, cv_ref, ext):
        @pl.when(pl.program_id(1) == 0)
        def _():
            ext[0:HALO_A, :] = jnp.zeros((HALO_A, cb), F32)

        for cs in chunks:
            ext[HALO_A : HALO_A + ts, cs] = ac[:, cs].astype(F32) * ax[:, cs].astype(F32)
        for cs in chunks:

            def emit(rows, cv, cs=cs):
                z = az[rows, cs].astype(F32)
                y_ref[rows, cs] = (ab[rows, cs].astype(F32) * cv * (z * _sigmoid(z))).astype(BF16)
                cv_ref[rows, cs] = cv.astype(BF16)

            _tap_sum(ext, cs, ts, [(HALO_A - (CONV_A - 1) + k, wa_ref[k : k + 1, cs]) for k in range(CONV_A)], emit)
        ext[0:HALO_A, :] = ext[ts : ts + HALO_A, :]

    def seg(q):
        return pl.BlockSpec((ts, cb), lambda c, i: (i, q * nb + c))

    return pl.pallas_call(
        body, name="mix_a_fwd", grid=(nb, s // ts),
        in_specs=[seg(0), seg(1), seg(2), seg(3), pl.BlockSpec((CONV_A, cb), lambda c, i: (0, c))],
        out_specs=[pl.BlockSpec((ts, cb), lambda c, i: (i, c)), pl.BlockSpec((ts, cb), lambda c, i: (i, c))],
        out_shape=[jax.ShapeDtypeStruct((s, 2 * w), BF16), jax.ShapeDtypeStruct((s, w), BF16)],
        scratch_shapes=[pltpu.VMEM((HALO_A + ts, cb), F32)], compiler_params=_cparams(2),
    )(proj, proj, proj, proj, wa)


def _mix_b_conv_fwd(proj, wb, bb, w, ts, cb):
    s = proj.shape[0]
    nb = w // cb
    chunks = _col_chunks(cb)

    def body(bv, bg, wb_ref, bb_ref, u2_ref, ext):
        @pl.when(pl.program_id(1) == 0)
        def _():
            ext[0:HALO_B, :] = jnp.zeros((HALO_B, cb), F32)

        for cs in chunks:
            ext[HALO_B : HALO_B + ts, cs] = bv[:, cs].astype(F32) * _sigmoid(bg[:, cs].astype(F32))
        for cs in chunks:

            def emit(rows, acc, cs=cs):
                u2_ref[rows, cs] = acc + bb_ref[:, cs]

            _tap_sum(ext, cs, ts, [(HALO_B - (CONV_B - 1) + k, wb_ref[k : k + 1, cs]) for k in range(CONV_B)], emit)
        ext[0:HALO_B, :] = ext[ts : ts + HALO_B, :]

    return pl.pallas_call(
        body, name="mix_b_conv_fwd", grid=(nb, s // ts),
        in_specs=[pl.BlockSpec((ts, cb), lambda c, i: (i, 4 * nb + c)), pl.BlockSpec((ts, cb), lambda c, i: (i, 5 * nb + c)),
                  pl.BlockSpec((CONV_B, cb), lambda c, i: (0, c)), pl.BlockSpec((1, cb), lambda c, i: (0, c))],
        out_specs=pl.BlockSpec((ts, cb), lambda c, i: (i, c)),
        out_shape=jax.ShapeDtypeStruct((s, w), F32),
        scratch_shapes=[pltpu.VMEM((HALO_B + ts, cb), F32)], compiler_params=_cparams(2),
    )(proj, proj, wb, bb)


def _ln_stats(u2_ref, ts, w, chunks):
    acc = jnp.zeros((ts, LANES), F32)
    for cs in chunks:
        acc = acc + u2_ref[:, cs]
    mu = _row_sum(acc) / w
    acc = jnp.zeros((ts, LANES), F32)
    for cs in chunks:
        xc = u2_ref[:, cs] - mu
        acc = acc + xc * xc
    return mu, lax.rsqrt(_row_sum(acc) / w + EPS)


def _mix_b_gate_fwd(u2, proj, lg, lb, y, ts):
    s, w = u2.shape
    chunks = _col_chunks(w)

    def body(u2_ref, bz, lg_ref, lb_ref, y_in, y_ref):
        del y_in
        mu, rstd = _ln_stats(u2_ref, ts, w, chunks)
        for cs in chunks:
            ln = (u2_ref[:, cs] - mu) * rstd * lg_ref[:, cs] + lb_ref[:, cs]
            z = bz[:, cs].astype(F32)
            y_ref[:, cs] = ((ln * _sigmoid(ln)) * (z * _sigmoid(z))).astype(BF16)

    row = pl.BlockSpec((1, w), lambda i: (0, 0))
    return pl.pallas_call(
        body, name="mix_b_gate_fwd", grid=(s // ts,),
        in_specs=[pl.BlockSpec((ts, w), lambda i: (i, 0)), pl.BlockSpec((ts, w), lambda i: (i, 6)), row, row,
                  pl.BlockSpec(memory_space=pl.ANY)],
        out_specs=pl.BlockSpec((ts, w), lambda i: (i, 1)),
        out_shape=jax.ShapeDtypeStruct(y.shape, y.dtype), input_output_aliases={4: 0}, compiler_params=_cparams(1),
    )(u2, proj, lg, lb, y)


def _post(x, out, tgt, gate, fg, ts):
    s, d = x.shape
    chunks = _col_chunks(d)

    def body(x_ref, o_ref, t_ref, gt_ref, fg_ref, dx2_ref, dout_ref, loss_ref, dgate_ref, dfg_ref):
        @pl.when(pl.program_id(0) == 0)
        def _():
            loss_ref[...] = jnp.zeros_like(loss_ref)
            dgate_ref[...] = jnp.zeros_like(dgate_ref)
            dfg_ref[...] = jnp.zeros_like(dfg_ref)

        acc = jnp.zeros((ts, LANES), F32)
        for cs in chunks:
            x2 = x_ref[:, cs] + gt_ref[:, cs] * o_ref[:, cs]
            acc = acc + x2 * x2
        r2 = lax.rsqrt(_row_sum(acc) / d + EPS)
        acc = jnp.zeros((ts, LANES), F32)
        for cs in chunks:
            n2 = (x_ref[:, cs] + gt_ref[:, cs] * o_ref[:, cs]) * r2
            diff = n2 * fg_ref[:, cs] - t_ref[:, cs]
            loss_ref[:, cs] += jnp.sum(diff * diff, axis=0, keepdims=True)
            dyf = diff / d
            dfg_ref[:, cs] += jnp.sum(dyf * n2, axis=0, keepdims=True)
            acc = acc + (dyf * fg_ref[:, cs]) * n2
        mdot = _row_sum(acc) / d
        for cs in chunks:
            o = o_ref[:, cs]
            n2 = (x_ref[:, cs] + gt_ref[:, cs] * o) * r2
            dn = ((n2 * fg_ref[:, cs] - t_ref[:, cs]) / d) * fg_ref[:, cs]
            dx2 = r2 * (dn - n2 * mdot)
            dx2_ref[:, cs] = dx2
            dgate_ref[:, cs] += jnp.sum(dx2 * o, axis=0, keepdims=True)
            dout_ref[:, cs] = (dx2 * gt_ref[:, cs]).astype(BF16)

    tile = pl.BlockSpec((ts, d), lambda i: (i, 0))
    row = pl.BlockSpec((1, d), lambda i: (0, 0))
    return pl.pallas_call(
        body, name="post", grid=(s // ts,),
        in_specs=[tile, tile, tile, row, row], out_specs=[tile, tile, row, row, row],
        out_shape=[jax.ShapeDtypeStruct((s, d), F32), jax.ShapeDtypeStruct((s, d), BF16)] + [jax.ShapeDtypeStruct((1, d), F32)] * 3,
        compiler_params=_cparams(1),
    )(x, out, tgt, gate, fg)


def _mix_b_gate_bwd(u2, proj, dy, lg, lb, n_cols, ts):
    s, w = u2.shape
    chunks = _col_chunks(w)

    def body(u2_ref, bz, dyb, lg_ref, lb_ref, du2_ref, dbz_ref, dlg_ref, dlb_ref, dbb_ref):
        @pl.when(pl.program_id(0) == 0)
        def _():
            dlg_ref[...] = jnp.zeros_like(dlg_ref)
            dlb_ref[...] = jnp.zeros_like(dlb_ref)
            dbb_ref[...] = jnp.zeros_like(dbb_ref)

        mu, rstd = _ln_stats(u2_ref, ts, w, chunks)
        acc1 = jnp.zeros((ts, LANES), F32)
        acc2 = jnp.zeros((ts, LANES), F32)
        for cs in chunks:
            xh = (u2_ref[:, cs] - mu) * rstd
            ln = xh * lg_ref[:, cs] + lb_ref[:, cs]
            sl = _sigmoid(ln)
            z = bz[:, cs].astype(F32)
            sz = _sigmoid(z)
            g = dyb[:, cs].astype(F32)
            dbz_ref[:, cs] = (g * (ln * sl) * (sz * (1.0 + z * (1.0 - sz)))).astype(BF16)
            dln = g * (z * sz) * (sl * (1.0 + ln * (1.0 - sl)))
            dlg_ref[:, cs] += jnp.sum(dln * xh, axis=0, keepdims=True)
            dlb_ref[:, cs] += jnp.sum(dln, axis=0, keepdims=True)
            dxh = dln * lg_ref[:, cs]
            du2_ref[:, cs] = dxh
            acc1 = acc1 + dxh
            acc2 = acc2 + dxh * xh
        m1 = _row_sum(acc1) / w
        m2 = _row_sum(acc2) / w
        for cs in chunks:
            xh = (u2_ref[:, cs] - mu) * rstd
            du2 = rstd * (du2_ref[:, cs] - m1 - xh * m2)
            du2_ref[:, cs] = du2
            dbb_ref[:, cs] += jnp.sum(du2, axis=0, keepdims=True)

    tile = pl.BlockSpec((ts, w), lambda i: (i, 0))
    row = pl.BlockSpec((1, w), lambda i: (0, 0))
    return pl.pallas_call(
        body, name="mix_b_gate_bwd", grid=(s // ts,),
        in_specs=[tile, pl.BlockSpec((ts, w), lambda i: (i, 6)), pl.BlockSpec((ts, w), lambda i: (i, 1)), row, row],
        out_specs=[tile, pl.BlockSpec((ts, w), lambda i: (i, 6)), row, row, row],
        out_shape=[jax.ShapeDtypeStruct((s, w), F32), jax.ShapeDtypeStruct((s, n_cols), BF16)] + [jax.ShapeDtypeStruct((1, w), F32)] * 3,
        compiler_params=_cparams(1),
    )(u2, proj, dy, lg, lb)


def _mix_b_conv_bwd(du2, proj, wb, dproj, comb, w, ts, cb):
    s = proj.shape[0]
    nb = w // cb
    ns = s // ts
    chunks = _col_chunks(cb)

    u_rows = ts + 2 * HALO_B + SUBLANES
    win = ts + HALO_B
    assert win % DW_BLOCK == 0

    n_seg = 2

    def body(du2_ref, bv, bg, wb_ref, dp_in, comb_ref, dp_ref, dwb_ref, recv_ref, ext, stash, uext, sems, send_sems, recv_sems):
        del dp_in
        c, i = pl.program_id(0), pl.program_id(1)
        stores = _SegmentStores(stash, dp_ref, sems, n_seg, c * ns + i, nb * ns, (ns - 1 - i) * ts, [((4 + g) * nb + c) * cb for g in range(n_seg)])
        slot = stores.begin()

        def copies():
            return _chip_copies([comb_ref], [recv_ref], send_sems, recv_sems)

        @pl.when((c == 0) & (i == 0))
        def _():
            for cp in copies():
                cp.start()

        @pl.when(i == 0)
        def _():
            ext[ts : ts + HALO_B, :] = jnp.zeros((HALO_B, cb), F32)
            uext[0:HALO_B, :] = jnp.zeros((HALO_B, cb), F32)
            uext[HALO_B + ts : u_rows, :] = jnp.zeros((u_rows - HALO_B - ts, cb), F32)
            dwb_ref[...] = jnp.zeros_like(dwb_ref)

        ext[0:ts, :] = du2_ref[...]
        for cs in chunks:
            uext[HALO_B : HALO_B + ts, cs] = bv[:, cs].astype(F32) * _sigmoid(bg[:, cs].astype(F32))

            def emit(rows, du, cs=cs):
                v = bv[rows, cs].astype(F32)
                sg = _sigmoid(bg[rows, cs].astype(F32))
                stash[slot, 0, rows, cs] = (du * sg).astype(BF16)
                stash[slot, 1, rows, cs] = (du * v * (sg * (1.0 - sg))).astype(BF16)

            _tap_sum(ext, cs, ts, [(CONV_B - 1 - k, wb_ref[k : k + 1, cs]) for k in range(CONV_B)], emit)
            for r in range(SUBLANES):
                ks = [k for k in range(CONV_B) if (CONV_B - 1 - k) % SUBLANES == r]
                accs = [jnp.zeros((SUBLANES, LANES), F32) for _ in ks]
                for t0 in range(0, win, DW_BLOCK):
                    e = ext[t0 : t0 + DW_BLOCK, cs]
                    ush = uext[SUBLANES - r + t0 : SUBLANES - r + t0 + DW_BLOCK + HALO_B - SUBLANES, cs]
                    for n, k in enumerate(ks):
                        b0 = HALO_B - SUBLANES - (CONV_B - 1 - k - r)
                        prod = e * ush[b0 : b0 + DW_BLOCK]
                        for v0 in range(0, DW_BLOCK, SUBLANES):
                            accs[n] = accs[n] + prod[v0 : v0 + SUBLANES]
                for n, k in enumerate(ks):
                    dwb_ref[k : k + 1, cs] += jnp.sum(accs[n], axis=0, keepdims=True)
        ext[ts : ts + HALO_B, :] = ext[0:HALO_B, :]
        stores.finish()

        @pl.when((c == nb - 1) & (i == ns - 1))
        def _():
            for cp in copies():
                cp.wait_recv()
            for cp in copies():
                cp.wait_send()

    def rev(col):
        return lambda c, i: (ns - 1 - i, col(c))

    hbm = pl.BlockSpec(memory_space=pl.ANY)
    return pl.pallas_call(
        body, name="mix_b_conv_bwd", grid=(nb, ns),
        in_specs=[pl.BlockSpec((ts, cb), rev(lambda c: c)), pl.BlockSpec((ts, cb), rev(lambda c: 4 * nb + c)),
                  pl.BlockSpec((ts, cb), rev(lambda c: 5 * nb + c)), pl.BlockSpec((CONV_B, cb), lambda c, i: (0, c)), hbm, hbm],
        out_specs=[hbm, pl.BlockSpec((CONV_B, cb), lambda c, i: (0, c)), hbm],
        out_shape=[jax.ShapeDtypeStruct(dproj.shape, dproj.dtype), jax.ShapeDtypeStruct((CONV_B, w), F32),
                   jax.ShapeDtypeStruct((3,) + comb.shape[1:], comb.dtype)],
        input_output_aliases={4: 0},
        scratch_shapes=[pltpu.VMEM((ts + HALO_B, cb), F32), pltpu.VMEM((2, n_seg, ts, cb), BF16), pltpu.VMEM((u_rows, cb), F32),
                        pltpu.SemaphoreType.DMA((2, n_seg)), pltpu.SemaphoreType.DMA((1, 3)), pltpu.SemaphoreType.DMA((1, 3))],
        compiler_params=_cparams(2),
    )(du2, proj, proj, wb, dproj, comb)


def _mix_a_bwd(proj, cv, dy, wa, dproj, w, ts, cb):
    s = proj.shape[0]
    nb = w // cb
    ns = s // ts
    chunks = _col_chunks(cb)

    n_seg = 4

    def body(ab, ac, ax, az, cv_ref, dya, wa_ref, dp_in, dp_ref, dwa_ref, ext, stash, sems):
        del dp_in
        c, i = pl.program_id(0), pl.program_id(1)
        stores = _SegmentStores(stash, dp_ref, sems, n_seg, c * ns + i, nb * ns, (ns - 1 - i) * ts, [(g * nb + c) * cb for g in range(n_seg)])
        slot = stores.begin()

        @pl.when(i == 0)
        def _():
            ext[ts : ts + HALO_A, :] = jnp.zeros((HALO_A, cb), F32)
            dwa_ref[...] = jnp.zeros_like(dwa_ref)

        for cs in chunks:
            b = ab[:, cs].astype(F32)
            z = az[:, cs].astype(F32)
            sz = _sigmoid(z)
            g = dya[:, cs].astype(F32)
            conv = cv_ref[:, cs].astype(F32)
            stash[slot, 0, :, cs] = (g * conv * (z * sz)).astype(BF16)
            stash[slot, 3, :, cs] = (g * b * conv * (sz * (1.0 + z * (1.0 - sz)))).astype(BF16)
            ext[0:ts, cs] = g * b * (z * sz)
        for cs in chunks:

            def emit(rows, dca, cs=cs):
                stash[slot, 1, rows, cs] = (dca * ax[rows, cs].astype(F32)).astype(BF16)
                stash[slot, 2, rows, cs] = (dca * ac[rows, cs].astype(F32)).astype(BF16)

            _tap_sum(ext, cs, ts, [(CONV_A - 1 - k, wa_ref[k : k + 1, cs]) for k in range(CONV_A)], emit)
            ca = ac[:, cs].astype(F32) * ax[:, cs].astype(F32)
            for k in range(CONV_A):
                o = CONV_A - 1 - k
                dwa_ref[k : k + 1, cs] += jnp.sum(ca * ext[o : o + ts, cs], axis=0, keepdims=True)
        ext[ts : ts + HALO_A, :] = ext[0:HALO_A, :]
        stores.finish()

    def rev(col):
        return lambda c, i: (ns - 1 - i, col(c))

    def seg(sq):
        return pl.BlockSpec((ts, cb), rev(lambda c: sq * nb + c))

    hbm = pl.BlockSpec(memory_space=pl.ANY)
    return pl.pallas_call(
        body, name="mix_a_bwd", grid=(nb, ns),
        in_specs=[seg(0), seg(1), seg(2), seg(3), pl.BlockSpec((ts, cb), rev(lambda c: c)), pl.BlockSpec((ts, cb), rev(lambda c: c)),
                  pl.BlockSpec((CONV_A, cb), lambda c, i: (0, c)), hbm],
        out_specs=[hbm, pl.BlockSpec((CONV_A, cb), lambda c, i: (0, c))],
        out_shape=[jax.ShapeDtypeStruct(dproj.shape, dproj.dtype), jax.ShapeDtypeStruct((CONV_A, w), F32)],
        input_output_aliases={7: 0},
        scratch_shapes=[pltpu.VMEM((ts + HALO_A, cb), F32), pltpu.VMEM((2, n_seg, ts, cb), BF16), pltpu.SemaphoreType.DMA((2, n_seg))],
        compiler_params=_cparams(2),
    )(proj, proj, proj, proj, cv, dy, wa, dproj)


def _pre_bwd(x, dh, dx2, g1, scale, ts):
    s, d = x.shape
    chunks = _col_chunks(d)

    def body(x_ref, dh_ref, dx2_ref, g_ref, sc_ref, gx_ref, dsh_ref, dsc_ref, dg_ref):
        @pl.when(pl.program_id(0) == 0)
        def _():
            dsh_ref[...] = jnp.zeros_like(dsh_ref)
            dsc_ref[...] = jnp.zeros_like(dsc_ref)
            dg_ref[...] = jnp.zeros_like(dg_ref)

        acc = jnp.zeros((ts, LANES), F32)
        for cs in chunks:
            v = x_ref[:, cs]
            acc = acc + v * v
        r1 = lax.rsqrt(_row_sum(acc) / d + EPS)
        acc = jnp.zeros((ts, LANES), F32)
        for cs in chunks:
            xn = x_ref[:, cs] * r1
            g = dh_ref[:, cs]
            dsh_ref[:, cs] += jnp.sum(g, axis=0, keepdims=True)
            dsc_ref[:, cs] += jnp.sum(g * (xn * g_ref[:, cs]), axis=0, keepdims=True)
            dn1 = g * (1.0 + sc_ref[:, cs])
            dg_ref[:, cs] += jnp.sum(dn1 * xn, axis=0, keepdims=True)
            acc = acc + (dn1 * g_ref[:, cs]) * xn
        mdot = _row_sum(acc) / d
        for cs in chunks:
            xn = x_ref[:, cs] * r1
            dxn = (dh_ref[:, cs] * (1.0 + sc_ref[:, cs])) * g_ref[:, cs]
            gx_ref[:, cs] = dx2_ref[:, cs] + r1 * (dxn - xn * mdot)

    tile = pl.BlockSpec((ts, d), lambda i: (i, 0))
    row = pl.BlockSpec((1, d), lambda i: (0, 0))
    return pl.pallas_call(
        body, name="pre_bwd", grid=(s // ts,),
        in_specs=[tile, tile, tile, row, row], out_specs=[tile, row, row, row],
        out_shape=[jax.ShapeDtypeStruct((s, d), F32)] + [jax.ShapeDtypeStruct((1, d), F32)] * 3,
        compiler_params=_cparams(1),
    )(x, dh, dx2, g1, scale)


def _adamw(w, g, m, v):
    m = ADAM_B1 * m + (1.0 - ADAM_B1) * g
    v = ADAM_B2 * v + (1.0 - ADAM_B2) * (g * g)
    m_hat = m / (1.0 - ADAM_B1**ADAM_STEP)
    v_hat = v / (1.0 - ADAM_B2**ADAM_STEP)
    delta = -ADAM_LR * (m_hat / (jnp.sqrt(v_hat) + ADAM_EPS) + ADAM_WD * w)
    return delta, m, v


def _combine_halves(part, swapped, core, tr):
    _, r, n = part.shape

    def body(core_ref, p_ref, s_ref, o_ref):
        del core_ref
        o_ref[...] = (p_ref[...].astype(F32) + s_ref[...].astype(F32)).astype(BF16)

    return pl.pallas_call(
        body, name="combine_halves",
        grid_spec=pltpu.PrefetchScalarGridSpec(
            num_scalar_prefetch=1, grid=(N_CHIP, r // tr),
            in_specs=[pl.BlockSpec((None, tr, n), lambda q, i, cr: (2 * q + cr[0], i, 0)), pl.BlockSpec((None, tr, n), lambda q, i, cr: (q, i, 0))],
            out_specs=pl.BlockSpec((None, tr, n), lambda q, i, cr: (q, i, 0))),
        out_shape=jax.ShapeDtypeStruct((N_CHIP, r, n), BF16), compiler_params=_cparams(2),
    )(core, part, swapped)


def _adam_sharded(w, m, v, comb, recv, chip, tr, name):
    r, n = w.shape

    def body(chip_ref, w_ref, m_ref, v_ref, c_ref, r_ref, g_ref, d_ref, nm_ref, nv_ref):
        del chip_ref
        g = c_ref[...].astype(F32) + r_ref[0].astype(F32) + r_ref[1].astype(F32) + r_ref[2].astype(F32)
        delta, nm, nv = _adamw(w_ref[...], g, m_ref[...], v_ref[...])
        g_ref[...] = g
        d_ref[...] = delta
        nm_ref[...] = nm
        nv_ref[...] = nv

    tile = pl.BlockSpec((tr, n), lambda i, ch: (i, 0))
    return pl.pallas_call(
        body, name=name,
        grid_spec=pltpu.PrefetchScalarGridSpec(
            num_scalar_prefetch=1, grid=(r // tr,),
            in_specs=[tile, tile, tile, pl.BlockSpec((None, tr, n), lambda i, ch: (ch[0], i, 0)), pl.BlockSpec((3, tr, n), lambda i, ch: (0, i, 0))],
            out_specs=[tile] * 4),
        out_shape=[jax.ShapeDtypeStruct((r, n), F32)] * 4, compiler_params=_cparams(1),
    )(chip, w, m, v, comb, recv)


def _adam_w_ada(c_t, dm_loc, w, m, v, tr):
    d, nb = c_t.shape
    na = w.shape[1]

    def body(c_ref, dm_ref, w_ref, m_ref, v_ref, g_ref, d_ref, nm_ref, nv_ref):
        g = jnp.dot(c_ref[...], dm_ref[...], preferred_element_type=F32, precision=lax.Precision.HIGHEST)
        delta, nm, nv = _adamw(w_ref[...], g, m_ref[...], v_ref[...])
        g_ref[...] = g
        d_ref[...] = delta
        nm_ref[...] = nm
        nv_ref[...] = nv

    tile = pl.BlockSpec((tr, na), lambda i: (i, 0))
    return pl.pallas_call(
        body, name="adam_w_ada", grid=(d // tr,),
        in_specs=[pl.BlockSpec((tr, nb), lambda i: (i, 0)), pl.BlockSpec((nb, na), lambda i: (0, 0)), tile, tile, tile],
        out_specs=[tile] * 4, out_shape=[jax.ShapeDtypeStruct((d, na), F32)] * 4, compiler_params=_cparams(1),
    )(c_t, dm_loc, w, m, v)


def _small_final(me, gathered, params, d, w, cl):
    flat_params = [t for p in params for t in p]

    def total(ref):
        acc = ref[0]
        for b in range(1, N_DEV):
            acc = acc + ref[b]
        return acc

    def body(me_ref, loss_g, dfg_g, dg1_g, dsh_g, dsc_g, dgt_g, dbb_g, dlg_g, dlb_g, dwa_g, dwb_g, *rest):
        del me_ref
        prm = rest[:24]
        loss_ref = rest[24]
        outs = rest[25:]

        loss_ref[...] = jnp.sum(total(loss_g), axis=1, keepdims=True) * (0.5 / d)

        def update(idx, g, cols=None):
            w_ref, m_ref, v_ref = prm[3 * idx : 3 * idx + 3]
            g_ref, d_ref, nm_ref, nv_ref = outs[4 * idx : 4 * idx + 4]
            sl = (slice(None), slice(None)) if cols is None else (slice(None), cols)
            delta, nm, nv = _adamw(w_ref[sl], g, m_ref[sl], v_ref[sl])
            g_ref[sl] = g
            d_ref[sl] = delta
            nm_ref[sl] = nm
            nv_ref[sl] = nv

        update(0, total(dg1_g))
        update(1, total(dsh_g), slice(0, d))
        update(1, total(dsc_g), slice(d, 2 * d))
        update(1, total(dgt_g), slice(2 * d, 3 * d))
        update(2, total(dwa_g))
        update(3, total(dwb_g))
        update(4, total(dbb_g))
        update(5, total(dlg_g))
        update(6, total(dlb_g))
        update(7, total(dfg_g))

    def full(shape):
        nd = len(shape)
        return pl.BlockSpec(shape, lambda i, mr: (0,) * nd)

    in_specs = [full(g.shape) for g in gathered[:9]]
    in_specs += [pl.BlockSpec((N_DEV, CONV_A, cl), lambda i, mr: (0, 0, mr[0])), pl.BlockSpec((N_DEV, CONV_B, cl), lambda i, mr: (0, 0, mr[0]))]
    in_specs += [full(t.shape) for t in flat_params]
    out_shapes = [jax.ShapeDtypeStruct((1, 1), F32)]
    for p in params:
        out_shapes += [jax.ShapeDtypeStruct(p[0].shape, F32)] * 4
    return pl.pallas_call(
        body, name="small_final",
        grid_spec=pltpu.PrefetchScalarGridSpec(
            num_scalar_prefetch=1, grid=(1,), in_specs=in_specs, out_specs=[full(o.shape) for o in out_shapes]),
        out_shape=out_shapes, compiler_params=_cparams(1),
    )(me, *gathered, *flat_params)


def _silu_rows(c):
    def body(c_ref, o_ref):
        v = c_ref[...]
        o_ref[...] = v * _sigmoid(v)

    return pl.pallas_call(body, name="silu_c", out_shape=jax.ShapeDtypeStruct(c.shape, F32), compiler_params=_cparams())(c)


def kernel(x, c, norm_g, w_ada, b_ada, w_in, conv_a_w, conv_b_w, conv_b_b, ln_b_g, ln_b_b, w_out, final_g, loss_target, m_norm_g, m_w_ada, m_b_ada, m_w_in, m_conv_a_w, m_conv_b_w, m_conv_b_b, m_ln_b_g, m_ln_b_b, m_w_out, m_final_g, v_norm_g, v_w_ada, v_b_ada, v_w_in, v_conv_a_w, v_conv_b_w, v_conv_b_b, v_ln_b_g, v_ln_b_b, v_w_out, v_final_g):
    _, s, d = x.shape
    w = conv_b_b.shape[-1]
    cl = conv_a_w.shape[-1]
    na = w_ada.shape[-1]
    assert ln_b_g.shape[-1] == w and w_out.shape[1] * N_DEV == 2 * w and w_in.shape[-1] * N_DEV == 7 * w and cl * N_DEV == w
    ts = min(256, s)
    ts_row = min(128, s)
    cb = min(512, w)
    tm = min(512, s)
    tn = min(1024, d)

    px, py, pc = _position()
    me = 4 * px + 2 * py + pc
    me_arr = jnp.reshape(me, (1,)).astype(jnp.int32)
    core_arr = jnp.reshape(pc, (1,)).astype(jnp.int32)
    chip_arr = jnp.reshape(2 * px + py, (1,)).astype(jnp.int32)

    x2d, tgt = x[0], loss_target[0]
    w_ada2, m_w_ada2, v_w_ada2 = w_ada[0], m_w_ada[0], v_w_ada[0]
    w_in2, m_w_in2, v_w_in2 = w_in[0], m_w_in[0], v_w_in[0]
    w_out2, m_w_out2, v_w_out2 = w_out[0], m_w_out[0], v_w_out[0]
    fg = final_g.reshape(1, d)

    c_act = _silu_rows(c)
    c_all, wa_all, wb_all = _all_gather_vmem([c_act, conv_a_w[0], conv_b_w[0]], "gather_small_in")
    c_all = c_all.reshape(N_DEV, d)
    wa = jnp.transpose(wa_all, (1, 0, 2)).reshape(CONV_A, w)
    wb = jnp.transpose(wb_all, (1, 0, 2)).reshape(CONV_B, w)
    b_loc = lax.dynamic_slice_in_dim(b_ada, me * na, na, axis=1)
    mod_loc = _mod_matmul(c_all, w_ada2, b_loc, min(512, d))
    (mod_all,) = _all_gather_vmem([mod_loc], "gather_mod")
    mod_mine = lax.dynamic_index_in_dim(mod_all, me, axis=1, keepdims=False).reshape(1, 3 * d)
    shift, scale, gate = mod_mine[:, :d], mod_mine[:, d : 2 * d], mod_mine[:, 2 * d :]

    h = _prep(x2d, norm_g, scale, shift, ts_row)
    proj, w_in_full, w_out_full = _proj_gather(h, w_in2.astype(BF16), w_out2.astype(BF16), _arrival_order(), tm)
    w_out_full = w_out_full.reshape(2 * w, d)
    y, cv = _mix_a_fwd(proj, wa, w, ts, cb)
    u2 = _mix_b_conv_fwd(proj, wb, conv_b_b, w, ts, cb)
    y = _mix_b_gate_fwd(u2, proj, ln_b_g, ln_b_b, y, ts)
    out = _mm_out(y, w_out_full, tm, tn)
    dx2, dout, loss_row, dgate_row, dfg_row = _post(x2d, out, tgt, gate, fg, ts_row)

    dy = _mm_dy(dout, w_out_full, tm, tn)
    part_out = _mm_dw_out(y, dout, tn)
    (swapped_out,) = _swap_with_sibling([part_out], "swap_halves_out")
    comb_out = _combine_halves(part_out, swapped_out, core_arr, min(256, part_out.shape[1]))
    du2, dproj, dlg_row, dlb_row, dbb_row = _mix_b_gate_bwd(u2, proj, dy, ln_b_g, ln_b_b, 7 * w, ts)
    dproj, dwb_part, recv_out = _mix_b_conv_bwd(du2, proj, wb, dproj, comb_out, w, ts, cb)
    dproj, dwa_part = _mix_a_bwd(proj, cv, dy, wa, dproj, w, ts, cb)
    part_in = _mm_dw_in(h, dproj, tm)

    (swapped_in,) = _swap_with_sibling([part_in], "swap_halves_in")
    comb_in = _combine_halves(part_in, swapped_in, core_arr, min(256, d))
    dh, (recv_in,) = _mm_dh(dproj, w_in_full, [comb_in], min(1024, s), tn)
    grad_x, dshift_row, dscale_row, dg1_row = _pre_bwd(x2d, dh, dx2, norm_g, scale, ts_row)
    g_w_in, d_w_in, nm_w_in, nv_w_in = _adam_sharded(w_in2, m_w_in2, v_w_in2, comb_in, recv_in, chip_arr, min(128, d), "adam_w_in")
    g_w_out, d_w_out, nm_w_out, nv_w_out = _adam_sharded(w_out2, m_w_out2, v_w_out2, comb_out, recv_out, chip_arr, min(128, w_out2.shape[0]), "adam_w_out")

    gathered = _all_gather_vmem(
        [loss_row, dfg_row, dg1_row, dshift_row, dscale_row, dgate_row, dbb_row, dlg_row, dlb_row, dwa_part, dwb_part], "gather_small_grads")
    dmod_all = jnp.concatenate([gathered[3].reshape(N_DEV, d), gathered[4].reshape(N_DEV, d), gathered[5].reshape(N_DEV, d)], axis=1)
    dm_loc = lax.dynamic_slice_in_dim(dmod_all, me * na, na, axis=1)
    g_w_ada, d_w_ada, nm_w_ada, nv_w_ada = _adam_w_ada(jnp.transpose(c_all), dm_loc, w_ada2, m_w_ada2, v_w_ada2, min(256, d))

    params = [
        (norm_g, m_norm_g, v_norm_g), (b_ada, m_b_ada, v_b_ada), (conv_a_w[0], m_conv_a_w[0], v_conv_a_w[0]),
        (conv_b_w[0], m_conv_b_w[0], v_conv_b_w[0]), (conv_b_b, m_conv_b_b, v_conv_b_b), (ln_b_g, m_ln_b_g, v_ln_b_g),
        (ln_b_b, m_ln_b_b, v_ln_b_b), (fg, m_final_g.reshape(1, d), v_final_g.reshape(1, d)),
    ]
    small = _small_final(me_arr, gathered, params, d, w, cl)
    loss = small[0].reshape(())
    sm = [small[1 + 4 * i : 5 + 4 * i] for i in range(8)]

    def pick(k):
        return [
            sm[0][k], (g_w_ada, d_w_ada, nm_w_ada, nv_w_ada)[k][None], sm[1][k], (g_w_in, d_w_in, nm_w_in, nv_w_in)[k][None],
            sm[2][k][None], sm[3][k][None], sm[4][k], sm[5][k], sm[6][k], (g_w_out, d_w_out, nm_w_out, nv_w_out)[k][None],
            sm[7][k].reshape(d),
        ]

    return (loss, grad_x[None], *pick(0), *pick(1), *pick(2), *pick(3))
```

```python
import functools

import jax
import jax.numpy as jnp
from jax import lax
from jax.experimental import pallas as pl
from jax.experimental.pallas import tpu as pltpu

F32 = jnp.float32
BF16 = jnp.bfloat16
N_DEV = 8
N_CHIP = 4
EPS = 1e-6
ADAM_LR = 0.001
ADAM_B1 = 0.9
ADAM_B2 = 0.999
ADAM_EPS = 1e-08
ADAM_WD = 0.01
ADAM_STEP = 10
CONV_A = 3
CONV_B = 31
HALO_A = 8
HALO_B = 32
LANES = 128
VMEM_LIMIT = 56 * 1024 * 1024
MESH = pl.DeviceIdType.MESH


def _cparams(n_grid_axes=0):
    if n_grid_axes:
        return pltpu.CompilerParams(dimension_semantics=("arbitrary",) * n_grid_axes, vmem_limit_bytes=VMEM_LIMIT)
    return pltpu.CompilerParams(vmem_limit_bytes=VMEM_LIMIT)


def _flip(v, bit):
    return 1 - v if bit else v


def _position():
    return lax.axis_index("x"), lax.axis_index("y"), lax.axis_index("c")


def _sigmoid(v):
    return jax.nn.sigmoid(v)


def _all_gather_vmem(arrs, name):
    n = len(arrs)

    def body(*refs):
        ins, outs = refs[:n], refs[n : 2 * n]
        send_sems, recv_sems = refs[2 * n :]
        x, y, c = _position()
        me = 4 * x + 2 * y + c
        for a in range(n):
            outs[a][me] = ins[a][...]
        sends = []
        for k in range(1, N_DEV):
            peer = (_flip(x, k & 4), _flip(y, k & 2), _flip(c, k & 1))
            for a in range(n):
                cp = pltpu.make_async_remote_copy(
                    src_ref=ins[a], dst_ref=outs[a].at[me], send_sem=send_sems.at[a, k - 1], recv_sem=recv_sems.at[a, k - 1],
                    device_id=peer, device_id_type=MESH)
                cp.start()
                sends.append(cp)
        for k in range(1, N_DEV):
            peer = (_flip(x, k & 4), _flip(y, k & 2), _flip(c, k & 1))
            src = 4 * peer[0] + 2 * peer[1] + peer[2]
            for a in range(n):
                pltpu.make_async_remote_copy(
                    src_ref=ins[a], dst_ref=outs[a].at[src], send_sem=send_sems.at[a, k - 1], recv_sem=recv_sems.at[a, k - 1],
                    device_id=peer, device_id_type=MESH).wait_recv()
        for cp in sends:
            cp.wait_send()

    vm = pl.BlockSpec(memory_space=pltpu.VMEM)
    return pl.pallas_call(
        body, name=name,
        out_shape=[jax.ShapeDtypeStruct((N_DEV,) + a.shape, a.dtype) for a in arrs],
        in_specs=[vm] * n, out_specs=[vm] * n,
        scratch_shapes=[pltpu.SemaphoreType.DMA((n, N_DEV - 1)), pltpu.SemaphoreType.DMA((n, N_DEV - 1))],
        compiler_params=_cparams(),
    )(*arrs)


def _swap_with_sibling(parts, name):
    n = len(parts)

    def body(*refs):
        ins, outs = refs[:n], refs[n : 2 * n]
        send_sems, recv_sems = refs[2 * n :]
        x, y, c = _position()
        sibling = (x, y, 1 - c)
        sends = []
        for a in range(n):
            for q in range(N_CHIP):
                cp = pltpu.make_async_remote_copy(
                    src_ref=ins[a].at[2 * q + (1 - c)], dst_ref=outs[a].at[q], send_sem=send_sems.at[a, q],
                    recv_sem=recv_sems.at[a, q], device_id=sibling, device_id_type=MESH)
                cp.start()
                sends.append(cp)
        for cp in sends:
            cp.wait_recv()
        for cp in sends:
            cp.wait_send()

    hbm = pl.BlockSpec(memory_space=pl.ANY)
    return pl.pallas_call(
        body, name=name,
        out_shape=[jax.ShapeDtypeStruct((N_CHIP,) + p.shape[1:], p.dtype) for p in parts],
        in_specs=[hbm] * n, out_specs=[hbm] * n,
        scratch_shapes=[pltpu.SemaphoreType.DMA((n, N_CHIP)), pltpu.SemaphoreType.DMA((n, N_CHIP))],
        compiler_params=_cparams(),
    )(*parts)


NN = (((1,), (0,)), ((), ()))
NT = (((1,), (1,)), ((), ()))
TN = (((0,), (0,)), ((), ()))


def _n_chunk(n):
    return 256 if n % 256 == 0 else LANES


def _mm_body(dims, n_out, accumulate):
    nc = _n_chunk(n_out)

    def body(a_ref, b_ref, o_ref):
        for n0 in range(0, n_out, nc):
            b = b_ref[n0 : n0 + nc, :] if dims is NT else b_ref[:, n0 : n0 + nc]
            r = lax.dot_general(a_ref[...], b, dims, preferred_element_type=F32)
            if accumulate:
                k = pl.program_id(2)

                @pl.when(k == 0)
                def _():
                    o_ref[:, n0 : n0 + nc] = r

                @pl.when(k > 0)
                def _():
                    o_ref[:, n0 : n0 + nc] += r
            else:
                o_ref[:, n0 : n0 + nc] = r.astype(o_ref.dtype)

    return body


def _ring_chips():
    x, y, c = _position()
    first = (x + (1 - c) - 2 * x * (1 - c), y + c - 2 * y * c)
    second = (x + c - 2 * x * c, y + (1 - c) - 2 * y * (1 - c))
    return first, second, (1 - x, 1 - y)


def _arrival_order():
    x, y, c = _position()
    first, second, diag = _ring_chips()
    blocks = [(x, y, c), (x, y, 1 - c), (*first, c), (*second, 1 - c), (*second, c), (*first, 1 - c), (*diag, c), (*diag, 1 - c)]
    return jnp.stack([4 * b[0] + 2 * b[1] + b[2] for b in blocks]).astype(jnp.int32)


def _proj_gather(h, w_in_loc, w_out_loc, order, tm):
    s, d = h.shape
    ne = w_in_loc.shape[1]
    ni = s // tm
    nc = _n_chunk(ne)
    n_streams = 7

    def body(order_ref, h_ref, win_ref, wout_ref, proj_ref, fin_ref, fout_ref, bbuf, send_sems, recv_sems, local_sems, load_sems):
        ins, fulls = [win_ref, wout_ref], [fin_ref, fout_ref]
        jj, i = pl.program_id(0), pl.program_id(1)
        x, y, c = _position()
        me = (x, y, c)
        sibling = (x, y, 1 - c)
        first, second, diag = _ring_chips()
        sent = [(me, sibling), (me, (*first, c)), (me, (*second, c)), ((*first, c), (*second, c)),
                ((*first, c), sibling), ((*second, c), sibling), ((*diag, c), sibling)]
        received = [sibling, (*first, c), (*second, c), (*diag, c), (*second, 1 - c), (*first, 1 - c), (*diag, 1 - c)]

        def slot(p):
            return 4 * p[0] + 2 * p[1] + p[2]

        def copy(a, k, block, to):
            dst = fulls[a].at[slot(block)]
            return pltpu.make_async_remote_copy(
                src_ref=ins[a] if k < 3 else dst, dst_ref=dst, send_sem=send_sems.at[a, k], recv_sem=recv_sems.at[a, k],
                device_id=to, device_id_type=MESH)

        def send(a, k):
            return copy(a, k, *sent[k])

        def recv(a, k):
            return copy(a, k, received[k], me)

        def own(a):
            return pltpu.make_async_copy(ins[a], fulls[a].at[slot(me)], local_sems.at[a])

        def load(src, buf):
            return pltpu.make_async_copy(src, bbuf.at[buf], load_sems.at[buf])

        @pl.when((jj == 0) & (i == 0))
        def _():
            load(win_ref, 0).start()
            for a in range(2):
                own(a).start()
            send(0, 0).start()
            send(0, 1).start()
            send(1, 0).start()
            load(win_ref, 0).wait()

        stream_of = [None, 0, 1, 4, 2, 5, 3, 6]
        passes = {1: [(0, 4), (0, 2), (0, 3), (1, 1), (1, 2)], 2: [(0, 5)], 3: [(0, 6)]}
        for nxt in range(1, N_DEV):

            @pl.when((jj == nxt - 1) & (i == ni - 1))
            def _(nxt=nxt):
                k = stream_of[nxt]
                recv(0, k).wait_recv()
                for a, k2 in passes.get(k, []):
                    send(a, k2).start()
                if nxt == N_DEV - 2:
                    recv(1, 1).wait_recv()
                    send(1, 4).start()
                    send(1, 3).start()
                    recv(1, 2).wait_recv()
                    send(1, 5).start()
                load(fin_ref.at[order_ref[nxt]], nxt % 2).start()

        @pl.when((jj > 0) & (i == 0))
        def _():
            load(fin_ref.at[0], jj % 2).wait()

        for n0 in range(0, ne, nc):
            proj_ref[:, n0 : n0 + nc] = lax.dot_general(
                h_ref[...], bbuf[jj % 2, :, n0 : n0 + nc], NN, preferred_element_type=F32).astype(BF16)

        @pl.when((jj == N_DEV - 1) & (i == ni - 1))
        def _():
            recv(1, 3).wait_recv()
            send(1, 6).start()
            for k in (0, 4, 5, 6):
                recv(1, k).wait_recv()
            for a in range(2):
                for k in range(n_streams):
                    send(a, k).wait_send()
                own(a).wait()

    hbm = pl.BlockSpec(memory_space=pl.ANY)
    return pl.pallas_call(
        body, name="proj_gather",
        grid_spec=pltpu.PrefetchScalarGridSpec(
            num_scalar_prefetch=1, grid=(N_DEV, ni),
            in_specs=[pl.BlockSpec((tm, d), lambda jj, i, od: (i, 0)), hbm, hbm],
            out_specs=[pl.BlockSpec((tm, ne), lambda jj, i, od: (i, od[jj])), hbm, hbm],
            scratch_shapes=[pltpu.VMEM((2, d, ne), BF16), pltpu.SemaphoreType.DMA((2, 7)), pltpu.SemaphoreType.DMA((2, 7)),
                            pltpu.SemaphoreType.DMA((2,)), pltpu.SemaphoreType.DMA((2,))]),
        out_shape=[jax.ShapeDtypeStruct((s, N_DEV * ne), BF16), jax.ShapeDtypeStruct((N_DEV,) + w_in_loc.shape, BF16),
                   jax.ShapeDtypeStruct((N_DEV,) + w_out_loc.shape, BF16)],
        compiler_params=_cparams(2),
    )(order, h, w_in_loc, w_out_loc)


def _mm_out(y, w_out, tm, tn):
    s, m = y.shape
    _, d = w_out.shape
    return pl.pallas_call(
        _mm_body(NN, tn, False), name="mm_out", grid=(d // tn, s // tm),
        in_specs=[pl.BlockSpec((tm, m), lambda j, i: (i, 0)), pl.BlockSpec((m, tn), lambda j, i: (0, j))],
        out_specs=pl.BlockSpec((tm, tn), lambda j, i: (i, j)),
        out_shape=jax.ShapeDtypeStruct((s, d), F32), compiler_params=_cparams(2),
    )(y, w_out)


def _mm_dy(dout, w_out, tm, tn):
    s, d = dout.shape
    m, _ = w_out.shape
    return pl.pallas_call(
        _mm_body(NT, tn, False), name="mm_dy", grid=(m // tn, s // tm),
        in_specs=[pl.BlockSpec((tm, d), lambda j, i: (i, 0)), pl.BlockSpec((tn, d), lambda j, i: (j, 0))],
        out_specs=pl.BlockSpec((tm, tn), lambda j, i: (i, j)),
        out_shape=jax.ShapeDtypeStruct((s, m), BF16), compiler_params=_cparams(2),
    )(dout, w_out)


def _mm_dw_out(y, dout, tn):
    s, m = y.shape
    _, d = dout.shape
    mr = m // N_DEV
    return pl.pallas_call(
        _mm_body(TN, tn, False), name="mm_dw_out", grid=(d // tn, N_DEV),
        in_specs=[pl.BlockSpec((s, mr), lambda j, i: (0, i)), pl.BlockSpec((s, tn), lambda j, i: (0, j))],
        out_specs=pl.BlockSpec((None, mr, tn), lambda j, i: (i, 0, j)),
        out_shape=jax.ShapeDtypeStruct((N_DEV, mr, d), BF16), compiler_params=_cparams(2),
    )(y, dout)


def _dw_in_targets():
    _, _, c = _position()
    return jnp.stack([2 * q + (1 - c) for q in range(N_CHIP)] + [2 * q + c for q in range(N_CHIP)]).astype(jnp.int32)


def _mm_dw_in(h, dproj, targets, tm):
    s, d = h.shape
    ne = dproj.shape[1] // N_DEV
    ni = d // tm
    nc = _n_chunk(ne)

    def body(tg_ref, h_ref, dp_ref, comb_ref, psib_ref, rsib_ref, obuf, rbuf, osems, rsems, send_sems, recv_sems):
        del tg_ref
        jj, i = pl.program_id(0), pl.program_id(1)
        x, y, c = _position()
        rows = pl.ds(pl.multiple_of(i * tm, tm), tm)
        slot = i % 2

        def put(sl, q):
            return pltpu.make_async_copy(obuf.at[sl], psib_ref.at[q, rows], osems.at[sl])

        def get(sl, q, r):
            return pltpu.make_async_copy(rsib_ref.at[q, r], rbuf.at[sl], rsems.at[sl])

        def swap(q):
            return pltpu.make_async_remote_copy(
                src_ref=psib_ref.at[q], dst_ref=rsib_ref.at[q], send_sem=send_sems.at[q], recv_sem=recv_sems.at[q],
                device_id=(x, y, 1 - c), device_id_type=MESH)

        def product(n0):
            return lax.dot_general(h_ref[...], dp_ref[:, n0 : n0 + nc], TN, preferred_element_type=F32)

        @pl.when(jj < N_CHIP)
        def _():
            @pl.when(i >= 2)
            def _():
                put(slot, 0).wait()

            for n0 in range(0, ne, nc):
                obuf[slot, :, n0 : n0 + nc] = product(n0).astype(BF16)
            put(slot, jj).start()

            @pl.when(i == ni - 1)
            def _():
                put(slot, 0).wait()
                if ni >= 2:
                    put(1 - slot, 0).wait()
                swap(jj).start()

        @pl.when(jj >= N_CHIP)
        def _():
            q = jj - N_CHIP

            @pl.when((jj == N_CHIP) & (i == 0))
            def _():
                for qq in range(N_CHIP):
                    swap(qq).wait_recv()

            @pl.when(i == 0)
            def _():
                get(0, q, rows).start()

            get(slot, 0, rows).wait()

            @pl.when(i + 1 < ni)
            def _():
                get(1 - slot, q, pl.ds(pl.multiple_of((i + 1) * tm, tm), tm)).start()

            for n0 in range(0, ne, nc):
                comb_ref[:, n0 : n0 + nc] = (product(n0) + rbuf[slot, :, n0 : n0 + nc].astype(F32)).astype(BF16)

            @pl.when((jj == N_DEV - 1) & (i == ni - 1))
            def _():
                for qq in range(N_CHIP):
                    swap(qq).wait_send()

    hbm = pl.BlockSpec(memory_space=pl.ANY)
    comb, _, _ = pl.pallas_call(
        body, name="mm_dw_in",
        grid_spec=pltpu.PrefetchScalarGridSpec(
            num_scalar_prefetch=1, grid=(N_DEV, ni),
            in_specs=[pl.BlockSpec((s, tm), lambda jj, i, tg: (0, i)), pl.BlockSpec((s, ne), lambda jj, i, tg: (0, tg[jj]))],
            out_specs=[pl.BlockSpec((None, tm, ne), lambda jj, i, tg: (jnp.maximum(jj - N_CHIP, 0), jnp.where(jj < N_CHIP, 0, i), 0)), hbm, hbm],
            scratch_shapes=[pltpu.VMEM((2, tm, ne), BF16), pltpu.VMEM((2, tm, ne), BF16), pltpu.SemaphoreType.DMA((2,)),
                            pltpu.SemaphoreType.DMA((2,)), pltpu.SemaphoreType.DMA((N_CHIP,)), pltpu.SemaphoreType.DMA((N_CHIP,))]),
        out_shape=[jax.ShapeDtypeStruct((N_CHIP, d, ne), BF16)] * 3,
        compiler_params=_cparams(2),
    )(targets, h, dproj)
    return comb


def _chip_copies(c_refs, r_refs, send_sems, recv_sems):
    x, y, c = _position()
    chips = [(1 - x, y), (x, 1 - y), (1 - x, 1 - y)]
    return [
        pltpu.make_async_remote_copy(
            src_ref=c_refs[a].at[2 * chip[0] + chip[1]], dst_ref=r_refs[a].at[j], send_sem=send_sems.at[a, j],
            recv_sem=recv_sems.at[a, j], device_id=(*chip, c), device_id_type=MESH)
        for a in range(len(c_refs)) for j, chip in enumerate(chips)]


def _mm_dh(dproj, w_full, comb, tm, tn):
    s = dproj.shape[0]
    _, d, ne = w_full.shape
    _, r_rows, r_cols = comb.shape
    kb = 2
    nk = N_DEV // kb
    nc = _n_chunk(tn)
    mh = min(512, tm)
    grid = (s // tm, d // tn, nk)
    n_steps = grid[0] * grid[1] * grid[2]
    n_relay = (3 * n_steps) // 8
    assert 0 < n_relay < n_steps - 1
    rc = min(512, r_rows)

    def body(a_ref, b_ref, c_ref, o_ref, recv_ref, relay_ref, sum_ref, va, vb, add_sems, send_sems, recv_sems):
        i, n, k = pl.program_id(0), pl.program_id(1), pl.program_id(2)
        step = (i * grid[1] + n) * nk + k
        _, _, c = _position()
        first, second, diag = _ring_chips()

        def chip_slot(p):
            return 2 * p[0] + p[1]

        def stream(j):
            src, dst, to = [(c_ref.at[chip_slot(diag)], relay_ref, first), (c_ref.at[chip_slot(first)], recv_ref.at[0], first),
                            (sum_ref, recv_ref.at[1], second)][j]
            return pltpu.make_async_remote_copy(
                src_ref=src, dst_ref=dst, send_sem=send_sems.at[j], recv_sem=recv_sems.at[j], device_id=(*to, c), device_id_type=MESH)

        @pl.when(step == 0)
        def _():
            stream(0).start()
            stream(1).start()

        @pl.when(step == n_relay)
        def _():
            stream(0).wait_recv()

            def piece(t, carry):
                rows = pl.ds(pl.multiple_of(t * rc, rc), rc)
                mine = pltpu.make_async_copy(c_ref.at[chip_slot(second), rows], va, add_sems.at[0])
                theirs = pltpu.make_async_copy(relay_ref.at[rows], vb, add_sems.at[1])
                mine.start()
                theirs.start()
                mine.wait()
                theirs.wait()
                va[...] = (va[...].astype(F32) + vb[...].astype(F32)).astype(BF16)
                out = pltpu.make_async_copy(va, sum_ref.at[rows], add_sems.at[0])
                out.start()
                out.wait()
                return carry

            lax.fori_loop(0, r_rows // rc, piece, 0)
            stream(2).start()

        @pl.when(k == 0)
        def _():
            o_ref[...] = jnp.zeros_like(o_ref)

        for m0 in range(0, tm, mh):
            for n0 in range(0, tn, nc):
                r = None
                for jj in range(kb):
                    t = lax.dot_general(a_ref[m0 : m0 + mh, jj * ne : (jj + 1) * ne], b_ref[jj, n0 : n0 + nc, :], NT, preferred_element_type=F32)
                    r = t if r is None else r + t
                o_ref[m0 : m0 + mh, n0 : n0 + nc] += r

        @pl.when(step == n_steps - 1)
        def _():
            stream(1).wait_recv()
            stream(2).wait_recv()
            for j in range(3):
                stream(j).wait_send()

    hbm = pl.BlockSpec(memory_space=pl.ANY)
    block = jax.ShapeDtypeStruct((r_rows, r_cols), comb.dtype)
    dh, recv, _, _ = pl.pallas_call(
        body, name="mm_dh", grid=grid,
        in_specs=[pl.BlockSpec((tm, kb * ne), lambda i, n, k: (i, k)), pl.BlockSpec((kb, tn, ne), lambda i, n, k: (k, n, 0)), hbm],
        out_specs=[pl.BlockSpec((tm, tn), lambda i, n, k: (i, n)), hbm, hbm, hbm],
        out_shape=[jax.ShapeDtypeStruct((s, d), F32), jax.ShapeDtypeStruct((2, r_rows, r_cols), comb.dtype), block, block],
        scratch_shapes=[pltpu.VMEM((rc, r_cols), comb.dtype), pltpu.VMEM((rc, r_cols), comb.dtype), pltpu.SemaphoreType.DMA((2,)),
                        pltpu.SemaphoreType.DMA((3,)), pltpu.SemaphoreType.DMA((3,))],
        compiler_params=_cparams(3),
    )(dproj, w_full, comb)
    return dh, recv


def _mod_matmul(c_all, w_ada, b_loc, tk):
    nb, d = c_all.shape
    na = w_ada.shape[1]

    def body(c_ref, w_ref, b_ref, o_ref):
        k = pl.program_id(0)
        r = jnp.dot(c_ref[...], w_ref[...], preferred_element_type=F32, precision=lax.Precision.HIGHEST)

        @pl.when(k == 0)
        def _():
            o_ref[...] = r + b_ref[...]

        @pl.when(k > 0)
        def _():
            o_ref[...] += r

    return pl.pallas_call(
        body, name="mod_matmul", grid=(d // tk,),
        in_specs=[pl.BlockSpec((nb, tk), lambda k: (0, k)), pl.BlockSpec((tk, na), lambda k: (k, 0)), pl.BlockSpec((1, na), lambda k: (0, 0))],
        out_specs=pl.BlockSpec((nb, na), lambda k: (0, 0)),
        out_shape=jax.ShapeDtypeStruct((nb, na), F32), compiler_params=_cparams(1),
    )(c_all, w_ada, b_loc)


def _col_chunks(width):
    return [slice(c0, c0 + LANES) for c0 in range(0, width, LANES)]


def _row_sum(acc):
    return jnp.sum(acc, axis=1, keepdims=True)


SUBLANES = 8


ROW_BLOCK = 64
DW_BLOCK = 96


def _tap_sum(ext, cs, ts, taps, emit):
    for t0 in range(0, ts, ROW_BLOCK):
        rb = min(ROW_BLOCK, ts - t0)
        acc = None
        for r in range(SUBLANES):
            group = [(o, wv) for o, wv in taps if o % SUBLANES == r]
            if not group:
                continue
            n = rb if r == 0 else rb + SUBLANES
            v = None
            for o, wv in group:
                term = wv * ext[t0 + o - r : t0 + o - r + n, cs]
                v = term if v is None else v + term
            part = v if r == 0 else v[r : r + rb]
            acc = part if acc is None else acc + part
        emit(slice(t0, t0 + rb), acc)


class _SegmentStores:
    def __init__(self, stash, dst, sems, n_seg, step, n_steps, row0, cols):
        self.stash, self.dst, self.sems, self.n_seg = stash, dst, sems, n_seg
        self.step, self.n_steps, self.row0, self.cols = step, n_steps, row0, cols
        self.slot = step % 2

    def _copy(self, slot, g, row0, col0):
        ts, cb = self.stash.shape[2:]
        return pltpu.make_async_copy(self.stash.at[slot, g], self.dst.at[pl.ds(row0, ts), pl.ds(col0, cb)], self.sems.at[slot, g])

    def _wait(self, slot):
        for g in range(self.n_seg):
            self._copy(slot, g, 0, 0).wait()

    def begin(self):
        @pl.when(self.step >= 2)
        def _():
            self._wait(self.slot)

        return self.slot

    def finish(self):
        ts = self.stash.shape[2]
        for g in range(self.n_seg):
            self._copy(self.slot, g, pl.multiple_of(self.row0, ts), pl.multiple_of(self.cols[g], LANES)).start()

        @pl.when(self.step == self.n_steps - 1)
        def _():
            self._wait(self.slot)
            if self.n_steps >= 2:
                self._wait(1 - self.slot)


def _prep(x, g1, scale, shift, ts):
    s, d = x.shape
    chunks = _col_chunks(d)

    def body(x_ref, g_ref, sc_ref, sh_ref, h_ref):
        acc = jnp.zeros((ts, LANES), F32)
        for cs in chunks:
            v = x_ref[:, cs]
            acc = acc + v * v
        r = lax.rsqrt(_row_sum(acc) / d + EPS)
        for cs in chunks:
            n1 = (x_ref[:, cs] * r) * g_ref[:, cs]
            h_ref[:, cs] = (n1 * (1.0 + sc_ref[:, cs]) + sh_ref[:, cs]).astype(BF16)

    row = pl.BlockSpec((1, d), lambda i: (0, 0))
    return pl.pallas_call(
        body, name="prep", grid=(s // ts,),
        in_specs=[pl.BlockSpec((ts, d), lambda i: (i, 0)), row, row, row],
        out_specs=pl.BlockSpec((ts, d), lambda i: (i, 0)),
        out_shape=jax.ShapeDtypeStruct((s, d), BF16), compiler_params=_cparams(1),
    )(x, g1, scale, shift)


def _mix_a_fwd(proj, wa, w, ts, cb):
    s = proj.shape[0]
    nb = w // cb
    chunks = _col_chunks(cb)

    def body(ab, ac, ax, az, wa_ref, y_ref, cv_ref, ext):
        @pl.when(pl.program_id(1) == 0)
        def _():
            ext[0:HALO_A, :] = jnp.zeros((HALO_A, cb), F32)

        for cs in chunks:
            ext[HALO_A : HALO_A + ts, cs] = ac[:, cs].astype(F32) * ax[:, cs].astype(F32)
        for cs in chunks:

            def emit(rows, cv, cs=cs):
                z = az[rows, cs].astype(F32)
                y_ref[rows, cs] = (ab[rows, cs].astype(F32) * cv * (z * _sigmoid(z))).astype(BF16)
                cv_ref[rows, cs] = cv.astype(BF16)

            _tap_sum(ext, cs, ts, [(HALO_A - (CONV_A - 1) + k, wa_ref[k : k + 1, cs]) for k in range(CONV_A)], emit)
        ext[0:HALO_A, :] = ext[ts : ts + HALO_A, :]

    def seg(q):
        return pl.BlockSpec((ts, cb), lambda c, i: (i, q * nb + c))

    return pl.pallas_call(
        body, name="mix_a_fwd", grid=(nb, s // ts),
        in_specs=[seg(0), seg(1), seg(2), seg(3), pl.BlockSpec((CONV_A, cb), lambda c, i: (0, c))],
        out_specs=[pl.BlockSpec((ts, cb), lambda c, i: (i, c)), pl.BlockSpec((ts, cb), lambda c, i: (i, c))],
        out_shape=[jax.ShapeDtypeStruct((s, 2 * w), BF16), jax.ShapeDtypeStruct((s, w), BF16)],
        scratch_shapes=[pltpu.VMEM((HALO_A + ts, cb), F32)], compiler_params=_cparams(2),
    )(proj, proj, proj, proj, wa)


def _mix_b_conv_fwd(proj, wb, bb, w, ts, cb):
    s = proj.shape[0]
    nb = w // cb
    chunks = _col_chunks(cb)

    def body(bv, bg, wb_ref, bb_ref, u2_ref, ext):
        @pl.when(pl.program_id(1) == 0)
        def _():
            ext[0:HALO_B, :] = jnp.zeros((HALO_B, cb), F32)

        for cs in chunks:
            ext[HALO_B : HALO_B + ts, cs] = bv[:, cs].astype(F32) * _sigmoid(bg[:, cs].astype(F32))
        for cs in chunks:

            def emit(rows, acc, cs=cs):
                u2_ref[rows, cs] = acc + bb_ref[:, cs]

            _tap_sum(ext, cs, ts, [(HALO_B - (CONV_B - 1) + k, wb_ref[k : k + 1, cs]) for k in range(CONV_B)], emit)
        ext[0:HALO_B, :] = ext[ts : ts + HALO_B, :]

    return pl.pallas_call(
        body, name="mix_b_conv_fwd", grid=(nb, s // ts),
        in_specs=[pl.BlockSpec((ts, cb), lambda c, i: (i, 4 * nb + c)), pl.BlockSpec((ts, cb), lambda c, i: (i, 5 * nb + c)),
                  pl.BlockSpec((CONV_B, cb), lambda c, i: (0, c)), pl.BlockSpec((1, cb), lambda c, i: (0, c))],
        out_specs=pl.BlockSpec((ts, cb), lambda c, i: (i, c)),
        out_shape=jax.ShapeDtypeStruct((s, w), F32),
        scratch_shapes=[pltpu.VMEM((HALO_B + ts, cb), F32)], compiler_params=_cparams(2),
    )(proj, proj, wb, bb)


def _ln_stats(u2_ref, ts, w, chunks):
    acc = jnp.zeros((ts, LANES), F32)
    for cs in chunks:
        acc = acc + u2_ref[:, cs]
    mu = _row_sum(acc) / w
    acc = jnp.zeros((ts, LANES), F32)
    for cs in chunks:
        xc = u2_ref[:, cs] - mu
        acc = acc + xc * xc
    return mu, lax.rsqrt(_row_sum(acc) / w + EPS)


def _mix_b_gate_fwd(u2, proj, lg, lb, y, ts):
    s, w = u2.shape
    chunks = _col_chunks(w)

    def body(u2_ref, bz, lg_ref, lb_ref, y_in, y_ref):
        del y_in
        mu, rstd = _ln_stats(u2_ref, ts, w, chunks)
        for cs in chunks:
            ln = (u2_ref[:, cs] - mu) * rstd * lg_ref[:, cs] + lb_ref[:, cs]
            z = bz[:, cs].astype(F32)
            y_ref[:, cs] = ((ln * _sigmoid(ln)) * (z * _sigmoid(z))).astype(BF16)

    row = pl.BlockSpec((1, w), lambda i: (0, 0))
    return pl.pallas_call(
        body, name="mix_b_gate_fwd", grid=(s // ts,),
        in_specs=[pl.BlockSpec((ts, w), lambda i: (i, 0)), pl.BlockSpec((ts, w), lambda i: (i, 6)), row, row,
                  pl.BlockSpec(memory_space=pl.ANY)],
        out_specs=pl.BlockSpec((ts, w), lambda i: (i, 1)),
        out_shape=jax.ShapeDtypeStruct(y.shape, y.dtype), input_output_aliases={4: 0}, compiler_params=_cparams(1),
    )(u2, proj, lg, lb, y)


def _post(x, out, tgt, gate, fg, ts):
    s, d = x.shape
    chunks = _col_chunks(d)

    def body(x_ref, o_ref, t_ref, gt_ref, fg_ref, dx2_ref, dout_ref, loss_ref, dgate_ref, dfg_ref):
        @pl.when(pl.program_id(0) == 0)
        def _():
            loss_ref[...] = jnp.zeros_like(loss_ref)
            dgate_ref[...] = jnp.zeros_like(dgate_ref)
            dfg_ref[...] = jnp.zeros_like(dfg_ref)

        acc = jnp.zeros((ts, LANES), F32)
        for cs in chunks:
            x2 = x_ref[:, cs] + gt_ref[:, cs] * o_ref[:, cs]
            acc = acc + x2 * x2
        r2 = lax.rsqrt(_row_sum(acc) / d + EPS)
        acc = jnp.zeros((ts, LANES), F32)
        for cs in chunks:
            n2 = (x_ref[:, cs] + gt_ref[:, cs] * o_ref[:, cs]) * r2
            diff = n2 * fg_ref[:, cs] - t_ref[:, cs]
            loss_ref[:, cs] += jnp.sum(diff * diff, axis=0, keepdims=True)
            dyf = diff / d
            dfg_ref[:, cs] += jnp.sum(dyf * n2, axis=0, keepdims=True)
            acc = acc + (dyf * fg_ref[:, cs]) * n2
        mdot = _row_sum(acc) / d
        for cs in chunks:
            o = o_ref[:, cs]
            n2 = (x_ref[:, cs] + gt_ref[:, cs] * o) * r2
            dn = ((n2 * fg_ref[:, cs] - t_ref[:, cs]) / d) * fg_ref[:, cs]
            dx2 = r2 * (dn - n2 * mdot)
            dx2_ref[:, cs] = dx2
            dgate_ref[:, cs] += jnp.sum(dx2 * o, axis=0, keepdims=True)
            dout_ref[:, cs] = (dx2 * gt_ref[:, cs]).astype(BF16)

    tile = pl.BlockSpec((ts, d), lambda i: (i, 0))
    row = pl.BlockSpec((1, d), lambda i: (0, 0))
    return pl.pallas_call(
        body, name="post", grid=(s // ts,),
        in_specs=[tile, tile, tile, row, row], out_specs=[tile, tile, row, row, row],
        out_shape=[jax.ShapeDtypeStruct((s, d), F32), jax.ShapeDtypeStruct((s, d), BF16)] + [jax.ShapeDtypeStruct((1, d), F32)] * 3,
        compiler_params=_cparams(1),
    )(x, out, tgt, gate, fg)


def _mix_b_gate_bwd(u2, proj, dy, lg, lb, n_cols, ts):
    s, w = u2.shape
    chunks = _col_chunks(w)

    def body(u2_ref, bz, dyb, lg_ref, lb_ref, du2_ref, dbz_ref, dlg_ref, dlb_ref, dbb_ref):
        @pl.when(pl.program_id(0) == 0)
        def _():
            dlg_ref[...] = jnp.zeros_like(dlg_ref)
            dlb_ref[...] = jnp.zeros_like(dlb_ref)
            dbb_ref[...] = jnp.zeros_like(dbb_ref)

        mu, rstd = _ln_stats(u2_ref, ts, w, chunks)
        acc1 = jnp.zeros((ts, LANES), F32)
        acc2 = jnp.zeros((ts, LANES), F32)
        for cs in chunks:
            xh = (u2_ref[:, cs] - mu) * rstd
            ln = xh * lg_ref[:, cs] + lb_ref[:, cs]
            sl = _sigmoid(ln)
            z = bz[:, cs].astype(F32)
            sz = _sigmoid(z)
            g = dyb[:, cs].astype(F32)
            dbz_ref[:, cs] = (g * (ln * sl) * (sz * (1.0 + z * (1.0 - sz)))).astype(BF16)
            dln = g * (z * sz) * (sl * (1.0 + ln * (1.0 - sl)))
            dlg_ref[:, cs] += jnp.sum(dln * xh, axis=0, keepdims=True)
            dlb_ref[:, cs] += jnp.sum(dln, axis=0, keepdims=True)
            dxh = dln * lg_ref[:, cs]
            du2_ref[:, cs] = dxh
            acc1 = acc1 + dxh
            acc2 = acc2 + dxh * xh
        m1 = _row_sum(acc1) / w
        m2 = _row_sum(acc2) / w
        for cs in chunks:
            xh = (u2_ref[:, cs] - mu) * rstd
            du2 = rstd * (du2_ref[:, cs] - m1 - xh * m2)
            du2_ref[:, cs] = du2
            dbb_ref[:, cs] += jnp.sum(du2, axis=0, keepdims=True)

    tile = pl.BlockSpec((ts, w), lambda i: (i, 0))
    row = pl.BlockSpec((1, w), lambda i: (0, 0))
    return pl.pallas_call(
        body, name="mix_b_gate_bwd", grid=(s // ts,),
        in_specs=[tile, pl.BlockSpec((ts, w), lambda i: (i, 6)), pl.BlockSpec((ts, w), lambda i: (i, 1)), row, row],
        out_specs=[tile, pl.BlockSpec((ts, w), lambda i: (i, 6)), row, row, row],
        out_shape=[jax.ShapeDtypeStruct((s, w), F32), jax.ShapeDtypeStruct((s, n_cols), BF16)] + [jax.ShapeDtypeStruct((1, w), F32)] * 3,
        compiler_params=_cparams(1),
    )(u2, proj, dy, lg, lb)


def _mix_b_conv_bwd(du2, proj, wb, dproj, comb, w, ts, cb):
    s = proj.shape[0]
    nb = w // cb
    ns = s // ts
    chunks = _col_chunks(cb)

    u_rows = ts + 2 * HALO_B + SUBLANES
    win = ts + HALO_B
    assert win % DW_BLOCK == 0

    n_seg = 2

    def body(du2_ref, bv, bg, wb_ref, dp_in, comb_ref, dp_ref, dwb_ref, recv_ref, ext, stash, uext, sems, send_sems, recv_sems):
        del dp_in
        c, i = pl.program_id(0), pl.program_id(1)
        stores = _SegmentStores(stash, dp_ref, sems, n_seg, c * ns + i, nb * ns, (ns - 1 - i) * ts, [((4 + g) * nb + c) * cb for g in range(n_seg)])
        slot = stores.begin()

        def copies():
            return _chip_copies([comb_ref], [recv_ref], send_sems, recv_sems)

        @pl.when((c == 0) & (i == 0))
        def _():
            for cp in copies():
                cp.start()

        @pl.when(i == 0)
        def _():
            ext[ts : ts + HALO_B, :] = jnp.zeros((HALO_B, cb), F32)
            uext[0:HALO_B, :] = jnp.zeros((HALO_B, cb), F32)
            uext[HALO_B + ts : u_rows, :] = jnp.zeros((u_rows - HALO_B - ts, cb), F32)
            dwb_ref[...] = jnp.zeros_like(dwb_ref)

        ext[0:ts, :] = du2_ref[...]
        for cs in chunks:
            uext[HALO_B : HALO_B + ts, cs] = bv[:, cs].astype(F32) * _sigmoid(bg[:, cs].astype(F32))

            def emit(rows, du, cs=cs):
                v = bv[rows, cs].astype(F32)
                sg = _sigmoid(bg[rows, cs].astype(F32))
                stash[slot, 0, rows, cs] = (du * sg).astype(BF16)
                stash[slot, 1, rows, cs] = (du * v * (sg * (1.0 - sg))).astype(BF16)

            _tap_sum(ext, cs, ts, [(CONV_B - 1 - k, wb_ref[k : k + 1, cs]) for k in range(CONV_B)], emit)
            for r in range(SUBLANES):
                ks = [k for k in range(CONV_B) if (CONV_B - 1 - k) % SUBLANES == r]
                accs = [jnp.zeros((SUBLANES, LANES), F32) for _ in ks]
                for t0 in range(0, win, DW_BLOCK):
                    e = ext[t0 : t0 + DW_BLOCK, cs]
                    ush = uext[SUBLANES - r + t0 : SUBLANES - r + t0 + DW_BLOCK + HALO_B - SUBLANES, cs]
                    for n, k in enumerate(ks):
                        b0 = HALO_B - SUBLANES - (CONV_B - 1 - k - r)
                        prod = e * ush[b0 : b0 + DW_BLOCK]
                        for v0 in range(0, DW_BLOCK, SUBLANES):
                            accs[n] = accs[n] + prod[v0 : v0 + SUBLANES]
                for n, k in enumerate(ks):
                    dwb_ref[k : k + 1, cs] += jnp.sum(accs[n], axis=0, keepdims=True)
        ext[ts : ts + HALO_B, :] = ext[0:HALO_B, :]
        stores.finish()

        @pl.when((c == nb - 1) & (i == ns - 1))
        def _():
            for cp in copies():
                cp.wait_recv()
            for cp in copies():
                cp.wait_send()

    def rev(col):
        return lambda c, i: (ns - 1 - i, col(c))

    hbm = pl.BlockSpec(memory_space=pl.ANY)
    return pl.pallas_call(
        body, name="mix_b_conv_bwd", grid=(nb, ns),
        in_specs=[pl.BlockSpec((ts, cb), rev(lambda c: c)), pl.BlockSpec((ts, cb), rev(lambda c: 4 * nb + c)),
                  pl.BlockSpec((ts, cb), rev(lambda c: 5 * nb + c)), pl.BlockSpec((CONV_B, cb), lambda c, i: (0, c)), hbm, hbm],
        out_specs=[hbm, pl.BlockSpec((CONV_B, cb), lambda c, i: (0, c)), hbm],
        out_shape=[jax.ShapeDtypeStruct(dproj.shape, dproj.dtype), jax.ShapeDtypeStruct((CONV_B, w), F32),
                   jax.ShapeDtypeStruct((3,) + comb.shape[1:], comb.dtype)],
        input_output_aliases={4: 0},
        scratch_shapes=[pltpu.VMEM((ts + HALO_B, cb), F32), pltpu.VMEM((2, n_seg, ts, cb), BF16), pltpu.VMEM((u_rows, cb), F32),
                        pltpu.SemaphoreType.DMA((2, n_seg)), pltpu.SemaphoreType.DMA((1, 3)), pltpu.SemaphoreType.DMA((1, 3))],
        compiler_params=_cparams(2),
    )(du2, proj, proj, wb, dproj, comb)


def _mix_a_bwd(proj, cv, dy, wa, dproj, w, ts, cb):
    s = proj.shape[0]
    nb = w // cb
    ns = s // ts
    chunks = _col_chunks(cb)

    n_seg = 4

    def body(ab, ac, ax, az, cv_ref, dya, wa_ref, dp_in, dp_ref, dwa_ref, ext, stash, sems):
        del dp_in
        c, i = pl.program_id(0), pl.program_id(1)
        stores = _SegmentStores(stash, dp_ref, sems, n_seg, c * ns + i, nb * ns, (ns - 1 - i) * ts, [(g * nb + c) * cb for g in range(n_seg)])
        slot = stores.begin()

        @pl.when(i == 0)
        def _():
            ext[ts : ts + HALO_A, :] = jnp.zeros((HALO_A, cb), F32)
            dwa_ref[...] = jnp.zeros_like(dwa_ref)

        for cs in chunks:
            b = ab[:, cs].astype(F32)
            z = az[:, cs].astype(F32)
            sz = _sigmoid(z)
            g = dya[:, cs].astype(F32)
            conv = cv_ref[:, cs].astype(F32)
            stash[slot, 0, :, cs] = (g * conv * (z * sz)).astype(BF16)
            stash[slot, 3, :, cs] = (g * b * conv * (sz * (1.0 + z * (1.0 - sz)))).astype(BF16)
            ext[0:ts, cs] = g * b * (z * sz)
        for cs in chunks:

            def emit(rows, dca, cs=cs):
                stash[slot, 1, rows, cs] = (dca * ax[rows, cs].astype(F32)).astype(BF16)
                stash[slot, 2, rows, cs] = (dca * ac[rows, cs].astype(F32)).astype(BF16)

            _tap_sum(ext, cs, ts, [(CONV_A - 1 - k, wa_ref[k : k + 1, cs]) for k in range(CONV_A)], emit)
            ca = ac[:, cs].astype(F32) * ax[:, cs].astype(F32)
            for k in range(CONV_A):
                o = CONV_A - 1 - k
                dwa_ref[k : k + 1, cs] += jnp.sum(ca * ext[o : o + ts, cs], axis=0, keepdims=True)
        ext[ts : ts + HALO_A, :] = ext[0:HALO_A, :]
        stores.finish()

    def rev(col):
        return lambda c, i: (ns - 1 - i, col(c))

    def seg(sq):
        return pl.BlockSpec((ts, cb), rev(lambda c: sq * nb + c))

    hbm = pl.BlockSpec(memory_space=pl.ANY)
    return pl.pallas_call(
        body, name="mix_a_bwd", grid=(nb, ns),
        in_specs=[seg(0), seg(1), seg(2), seg(3), pl.BlockSpec((ts, cb), rev(lambda c: c)), pl.BlockSpec((ts, cb), rev(lambda c: c)),
                  pl.BlockSpec((CONV_A, cb), lambda c, i: (0, c)), hbm],
        out_specs=[hbm, pl.BlockSpec((CONV_A, cb), lambda c, i: (0, c))],
        out_shape=[jax.ShapeDtypeStruct(dproj.shape, dproj.dtype), jax.ShapeDtypeStruct((CONV_A, w), F32)],
        input_output_aliases={7: 0},
        scratch_shapes=[pltpu.VMEM((ts + HALO_A, cb), F32), pltpu.VMEM((2, n_seg, ts, cb), BF16), pltpu.SemaphoreType.DMA((2, n_seg))],
        compiler_params=_cparams(2),
    )(proj, proj, proj, proj, cv, dy, wa, dproj)


def _pre_bwd(x, dh, dx2, g1, scale, ts):
    s, d = x.shape
    chunks = _col_chunks(d)

    def body(x_ref, dh_ref, dx2_ref, g_ref, sc_ref, gx_ref, dsh_ref, dsc_ref, dg_ref):
        @pl.when(pl.program_id(0) == 0)
        def _():
            dsh_ref[...] = jnp.zeros_like(dsh_ref)
            dsc_ref[...] = jnp.zeros_like(dsc_ref)
            dg_ref[...] = jnp.zeros_like(dg_ref)

        acc = jnp.zeros((ts, LANES), F32)
        for cs in chunks:
            v = x_ref[:, cs]
            acc = acc + v * v
        r1 = lax.rsqrt(_row_sum(acc) / d + EPS)
        acc = jnp.zeros((ts, LANES), F32)
        for cs in chunks:
            xn = x_ref[:, cs] * r1
            g = dh_ref[:, cs]
            dsh_ref[:, cs] += jnp.sum(g, axis=0, keepdims=True)
            dsc_ref[:, cs] += jnp.sum(g * (xn * g_ref[:, cs]), axis=0, keepdims=True)
            dn1 = g * (1.0 + sc_ref[:, cs])
            dg_ref[:, cs] += jnp.sum(dn1 * xn, axis=0, keepdims=True)
            acc = acc + (dn1 * g_ref[:, cs]) * xn
        mdot = _row_sum(acc) / d
        for cs in chunks:
            xn = x_ref[:, cs] * r1
            dxn = (dh_ref[:, cs] * (1.0 + sc_ref[:, cs])) * g_ref[:, cs]
            gx_ref[:, cs] = dx2_ref[:, cs] + r1 * (dxn - xn * mdot)

    tile = pl.BlockSpec((ts, d), lambda i: (i, 0))
    row = pl.BlockSpec((1, d), lambda i: (0, 0))
    return pl.pallas_call(
        body, name="pre_bwd", grid=(s // ts,),
        in_specs=[tile, tile, tile, row, row], out_specs=[tile, row, row, row],
        out_shape=[jax.ShapeDtypeStruct((s, d), F32)] + [jax.ShapeDtypeStruct((1, d), F32)] * 3,
        compiler_params=_cparams(1),
    )(x, dh, dx2, g1, scale)


def _adamw(w, g, m, v):
    m = ADAM_B1 * m + (1.0 - ADAM_B1) * g
    v = ADAM_B2 * v + (1.0 - ADAM_B2) * (g * g)
    m_hat = m / (1.0 - ADAM_B1**ADAM_STEP)
    v_hat = v / (1.0 - ADAM_B2**ADAM_STEP)
    delta = -ADAM_LR * (m_hat / (jnp.sqrt(v_hat) + ADAM_EPS) + ADAM_WD * w)
    return delta, m, v


def _combine_halves(part, swapped, core, tr):
    _, r, n = part.shape

    def body(core_ref, p_ref, s_ref, o_ref):
        del core_ref
        o_ref[...] = (p_ref[...].astype(F32) + s_ref[...].astype(F32)).astype(BF16)

    return pl.pallas_call(
        body, name="combine_halves",
        grid_spec=pltpu.PrefetchScalarGridSpec(
            num_scalar_prefetch=1, grid=(N_CHIP, r // tr),
            in_specs=[pl.BlockSpec((None, tr, n), lambda q, i, cr: (2 * q + cr[0], i, 0)), pl.BlockSpec((None, tr, n), lambda q, i, cr: (q, i, 0))],
            out_specs=pl.BlockSpec((None, tr, n), lambda q, i, cr: (q, i, 0))),
        out_shape=jax.ShapeDtypeStruct((N_CHIP, r, n), BF16), compiler_params=_cparams(2),
    )(core, part, swapped)


def _adam_sharded(w, m, v, comb, recv, chip, tr, name):
    r, n = w.shape
    nr = recv.shape[0]

    def body(chip_ref, w_ref, m_ref, v_ref, c_ref, r_ref, g_ref, d_ref, nm_ref, nv_ref):
        del chip_ref
        g = c_ref[...].astype(F32)
        for j in range(nr):
            g = g + r_ref[j].astype(F32)
        delta, nm, nv = _adamw(w_ref[...], g, m_ref[...], v_ref[...])
        g_ref[...] = g
        d_ref[...] = delta
        nm_ref[...] = nm
        nv_ref[...] = nv

    tile = pl.BlockSpec((tr, n), lambda i, ch: (i, 0))
    return pl.pallas_call(
        body, name=name,
        grid_spec=pltpu.PrefetchScalarGridSpec(
            num_scalar_prefetch=1, grid=(r // tr,),
            in_specs=[tile, tile, tile, pl.BlockSpec((None, tr, n), lambda i, ch: (ch[0], i, 0)), pl.BlockSpec((nr, tr, n), lambda i, ch: (0, i, 0))],
            out_specs=[tile] * 4),
        out_shape=[jax.ShapeDtypeStruct((r, n), F32)] * 4, compiler_params=_cparams(1),
    )(chip, w, m, v, comb, recv)


def _adam_w_ada(c_t, dm_loc, w, m, v, tr):
    d, nb = c_t.shape
    na = w.shape[1]

    def body(c_ref, dm_ref, w_ref, m_ref, v_ref, g_ref, d_ref, nm_ref, nv_ref):
        g = jnp.dot(c_ref[...], dm_ref[...], preferred_element_type=F32, precision=lax.Precision.HIGHEST)
        delta, nm, nv = _adamw(w_ref[...], g, m_ref[...], v_ref[...])
        g_ref[...] = g
        d_ref[...] = delta
        nm_ref[...] = nm
        nv_ref[...] = nv

    tile = pl.BlockSpec((tr, na), lambda i: (i, 0))
    return pl.pallas_call(
        body, name="adam_w_ada", grid=(d // tr,),
        in_specs=[pl.BlockSpec((tr, nb), lambda i: (i, 0)), pl.BlockSpec((nb, na), lambda i: (0, 0)), tile, tile, tile],
        out_specs=[tile] * 4, out_shape=[jax.ShapeDtypeStruct((d, na), F32)] * 4, compiler_params=_cparams(1),
    )(c_t, dm_loc, w, m, v)


def _small_final(me, gathered, params, d, w, cl):
    flat_params = [t for p in params for t in p]

    def total(ref):
        acc = ref[0]
        for b in range(1, N_DEV):
            acc = acc + ref[b]
        return acc

    def body(me_ref, loss_g, dfg_g, dg1_g, dsh_g, dsc_g, dgt_g, dbb_g, dlg_g, dlb_g, dwa_g, dwb_g, *rest):
        del me_ref
        prm = rest[:24]
        loss_ref = rest[24]
        outs = rest[25:]

        loss_ref[...] = jnp.sum(total(loss_g), axis=1, keepdims=True) * (0.5 / d)

        def update(idx, g, cols=None):
            w_ref, m_ref, v_ref = prm[3 * idx : 3 * idx + 3]
            g_ref, d_ref, nm_ref, nv_ref = outs[4 * idx : 4 * idx + 4]
            sl = (slice(None), slice(None)) if cols is None else (slice(None), cols)
            delta, nm, nv = _adamw(w_ref[sl], g, m_ref[sl], v_ref[sl])
            g_ref[sl] = g
            d_ref[sl] = delta
            nm_ref[sl] = nm
            nv_ref[sl] = nv

        update(0, total(dg1_g))
        update(1, total(dsh_g), slice(0, d))
        update(1, total(dsc_g), slice(d, 2 * d))
        update(1, total(dgt_g), slice(2 * d, 3 * d))
        update(2, total(dwa_g))
        update(3, total(dwb_g))
        update(4, total(dbb_g))
        update(5, total(dlg_g))
        update(6, total(dlb_g))
        update(7, total(dfg_g))

    def full(shape):
        nd = len(shape)
        return pl.BlockSpec(shape, lambda i, mr: (0,) * nd)

    in_specs = [full(g.shape) for g in gathered[:9]]
    in_specs += [pl.BlockSpec((N_DEV, CONV_A, cl), lambda i, mr: (0, 0, mr[0])), pl.BlockSpec((N_DEV, CONV_B, cl), lambda i, mr: (0, 0, mr[0]))]
    in_specs += [full(t.shape) for t in flat_params]
    out_shapes = [jax.ShapeDtypeStruct((1, 1), F32)]
    for p in params:
        out_shapes += [jax.ShapeDtypeStruct(p[0].shape, F32)] * 4
    return pl.pallas_call(
        body, name="small_final",
        grid_spec=pltpu.PrefetchScalarGridSpec(
            num_scalar_prefetch=1, grid=(1,), in_specs=in_specs, out_specs=[full(o.shape) for o in out_shapes]),
        out_shape=out_shapes, compiler_params=_cparams(1),
    )(me, *gathered, *flat_params)


def _silu_rows(c):
    def body(c_ref, o_ref):
        v = c_ref[...]
        o_ref[...] = v * _sigmoid(v)

    return pl.pallas_call(body, name="silu_c", out_shape=jax.ShapeDtypeStruct(c.shape, F32), compiler_params=_cparams())(c)


def kernel(x, c, norm_g, w_ada, b_ada, w_in, conv_a_w, conv_b_w, conv_b_b, ln_b_g, ln_b_b, w_out, final_g, loss_target, m_norm_g, m_w_ada, m_b_ada, m_w_in, m_conv_a_w, m_conv_b_w, m_conv_b_b, m_ln_b_g, m_ln_b_b, m_w_out, m_final_g, v_norm_g, v_w_ada, v_b_ada, v_w_in, v_conv_a_w, v_conv_b_w, v_conv_b_b, v_ln_b_g, v_ln_b_b, v_w_out, v_final_g):
    _, s, d = x.shape
    w = conv_b_b.shape[-1]
    cl = conv_a_w.shape[-1]
    na = w_ada.shape[-1]
    assert ln_b_g.shape[-1] == w and w_out.shape[1] * N_DEV == 2 * w and w_in.shape[-1] * N_DEV == 7 * w and cl * N_DEV == w
    ts = min(256, s)
    ts_row = min(128, s)
    cb = min(512, w)
    tm = min(512, s)
    tn = min(1024, d)

    px, py, pc = _position()
    me = 4 * px + 2 * py + pc
    me_arr = jnp.reshape(me, (1,)).astype(jnp.int32)
    core_arr = jnp.reshape(pc, (1,)).astype(jnp.int32)
    chip_arr = jnp.reshape(2 * px + py, (1,)).astype(jnp.int32)

    x2d, tgt = x[0], loss_target[0]
    w_ada2, m_w_ada2, v_w_ada2 = w_ada[0], m_w_ada[0], v_w_ada[0]
    w_in2, m_w_in2, v_w_in2 = w_in[0], m_w_in[0], v_w_in[0]
    w_out2, m_w_out2, v_w_out2 = w_out[0], m_w_out[0], v_w_out[0]
    fg = final_g.reshape(1, d)

    c_act = _silu_rows(c)
    c_all, wa_all, wb_all = _all_gather_vmem([c_act, conv_a_w[0], conv_b_w[0]], "gather_small_in")
    c_all = c_all.reshape(N_DEV, d)
    wa = jnp.transpose(wa_all, (1, 0, 2)).reshape(CONV_A, w)
    wb = jnp.transpose(wb_all, (1, 0, 2)).reshape(CONV_B, w)
    b_loc = lax.dynamic_slice_in_dim(b_ada, me * na, na, axis=1)
    mod_loc = _mod_matmul(c_all, w_ada2, b_loc, min(512, d))
    (mod_all,) = _all_gather_vmem([mod_loc], "gather_mod")
    mod_mine = lax.dynamic_index_in_dim(mod_all, me, axis=1, keepdims=False).reshape(1, 3 * d)
    shift, scale, gate = mod_mine[:, :d], mod_mine[:, d : 2 * d], mod_mine[:, 2 * d :]

    h = _prep(x2d, norm_g, scale, shift, ts_row)
    proj, w_in_full, w_out_full = _proj_gather(h, w_in2.astype(BF16), w_out2.astype(BF16), _arrival_order(), tm)
    w_out_full = w_out_full.reshape(2 * w, d)
    y, cv = _mix_a_fwd(proj, wa, w, ts, cb)
    u2 = _mix_b_conv_fwd(proj, wb, conv_b_b, w, ts, cb)
    y = _mix_b_gate_fwd(u2, proj, ln_b_g, ln_b_b, y, ts)
    out = _mm_out(y, w_out_full, tm, tn)
    dx2, dout, loss_row, dgate_row, dfg_row = _post(x2d, out, tgt, gate, fg, ts_row)

    dy = _mm_dy(dout, w_out_full, tm, tn)
    part_out = _mm_dw_out(y, dout, tn)
    (swapped_out,) = _swap_with_sibling([part_out], "swap_halves_out")
    comb_out = _combine_halves(part_out, swapped_out, core_arr, min(256, part_out.shape[1]))
    du2, dproj, dlg_row, dlb_row, dbb_row = _mix_b_gate_bwd(u2, proj, dy, ln_b_g, ln_b_b, 7 * w, ts)
    dproj, dwb_part, recv_out = _mix_b_conv_bwd(du2, proj, wb, dproj, comb_out, w, ts, cb)
    dproj, dwa_part = _mix_a_bwd(proj, cv, dy, wa, dproj, w, ts, cb)
    comb_in = _mm_dw_in(h, dproj, _dw_in_targets(), tm)
    dh, recv_in = _mm_dh(dproj, w_in_full, comb_in, min(1024, s), tn)
    grad_x, dshift_row, dscale_row, dg1_row = _pre_bwd(x2d, dh, dx2, norm_g, scale, ts_row)
    g_w_in, d_w_in, nm_w_in, nv_w_in = _adam_sharded(w_in2, m_w_in2, v_w_in2, comb_in, recv_in, chip_arr, min(128, d), "adam_w_in")
    g_w_out, d_w_out, nm_w_out, nv_w_out = _adam_sharded(w_out2, m_w_out2, v_w_out2, comb_out, recv_out, chip_arr, min(128, w_out2.shape[0]), "adam_w_out")

    gathered = _all_gather_vmem(
        [loss_row, dfg_row, dg1_row, dshift_row, dscale_row, dgate_row, dbb_row, dlg_row, dlb_row, dwa_part, dwb_part], "gather_small_grads")
    dmod_all = jnp.concatenate([gathered[3].reshape(N_DEV, d), gathered[4].reshape(N_DEV, d), gathered[5].reshape(N_DEV, d)], axis=1)
    dm_loc = lax.dynamic_slice_in_dim(dmod_all, me * na, na, axis=1)
    g_w_ada, d_w_ada, nm_w_ada, nv_w_ada = _adam_w_ada(jnp.transpose(c_all), dm_loc, w_ada2, m_w_ada2, v_w_ada2, min(256, d))

    params = [
        (norm_g, m_norm_g, v_norm_g), (b_ada, m_b_ada, v_b_ada), (conv_a_w[0], m_conv_a_w[0], v_conv_a_w[0]),
        (conv_b_w[0], m_conv_b_w[0], v_conv_b_w[0]), (conv_b_b, m_conv_b_b, v_conv_b_b), (ln_b_g, m_ln_b_g, v_ln_b_g),
        (ln_b_b, m_ln_b_b, v_ln_b_b), (fg, m_final_g.reshape(1, d), v_final_g.reshape(1, d)),
    ]
    small = _small_final(me_arr, gathered, params, d, w, cl)
    loss = small[0].reshape(())
    sm = [small[1 + 4 * i : 5 + 4 * i] for i in range(8)]

    def pick(k):
        return [
            sm[0][k], (g_w_ada, d_w_ada, nm_w_ada, nv_w_ada)[k][None], sm[1][k], (g_w_in, d_w_in, nm_w_in, nv_w_in)[k][None],
            sm[2][k][None], sm[3][k][None], sm[4][k], sm[5][k], sm[6][k], (g_w_out, d_w_out, nm_w_out, nv_w_out)[k][None],
            sm[7][k].reshape(d),
        ]

    return (loss, grad_x[None], *pick(0), *pick(1), *pick(2), *pick(3))
```

```python
import functools

import jax
import jax.numpy as jnp
from jax import lax
from jax.experimental import pallas as pl
from jax.experimental.pallas import tpu as pltpu

F32 = jnp.float32
BF16 = jnp.bfloat16
N_DEV = 8
N_CHIP = 4
EPS = 1e-6
ADAM_LR = 0.001
ADAM_B1 = 0.9
ADAM_B2 = 0.999
ADAM_EPS = 1e-08
ADAM_WD = 0.01
ADAM_STEP = 10
CONV_A = 3
CONV_B = 31
HALO_A = 8
HALO_B = 32
LANES = 128
VMEM_LIMIT = 56 * 1024 * 1024
MESH = pl.DeviceIdType.MESH


def _cparams(n_grid_axes=0):
    if n_grid_axes:
        return pltpu.CompilerParams(dimension_semantics=("arbitrary",) * n_grid_axes, vmem_limit_bytes=VMEM_LIMIT)
    return pltpu.CompilerParams(vmem_limit_bytes=VMEM_LIMIT)


def _flip(v, bit):
    return 1 - v if bit else v


def _position():
    return lax.axis_index("x"), lax.axis_index("y"), lax.axis_index("c")


def _sigmoid(v):
    return jax.nn.sigmoid(v)


def _all_gather_vmem(arrs, name):
    n = len(arrs)

    def body(*refs):
        ins, outs = refs[:n], refs[n : 2 * n]
        send_sems, recv_sems = refs[2 * n :]
        x, y, c = _position()
        me = 4 * x + 2 * y + c
        for a in range(n):
            outs[a][me] = ins[a][...]
        sends = []
        for k in range(1, N_DEV):
            peer = (_flip(x, k & 4), _flip(y, k & 2), _flip(c, k & 1))
            for a in range(n):
                cp = pltpu.make_async_remote_copy(
                    src_ref=ins[a], dst_ref=outs[a].at[me], send_sem=send_sems.at[a, k - 1], recv_sem=recv_sems.at[a, k - 1],
                    device_id=peer, device_id_type=MESH)
                cp.start()
                sends.append(cp)
        for k in range(1, N_DEV):
            peer = (_flip(x, k & 4), _flip(y, k & 2), _flip(c, k & 1))
            src = 4 * peer[0] + 2 * peer[1] + peer[2]
            for a in range(n):
                pltpu.make_async_remote_copy(
                    src_ref=ins[a], dst_ref=outs[a].at[src], send_sem=send_sems.at[a, k - 1], recv_sem=recv_sems.at[a, k - 1],
                    device_id=peer, device_id_type=MESH).wait_recv()
        for cp in sends:
            cp.wait_send()

    vm = pl.BlockSpec(memory_space=pltpu.VMEM)
    return pl.pallas_call(
        body, name=name,
        out_shape=[jax.ShapeDtypeStruct((N_DEV,) + a.shape, a.dtype) for a in arrs],
        in_specs=[vm] * n, out_specs=[vm] * n,
        scratch_shapes=[pltpu.SemaphoreType.DMA((n, N_DEV - 1)), pltpu.SemaphoreType.DMA((n, N_DEV - 1))],
        compiler_params=_cparams(),
    )(*arrs)


NN = (((1,), (0,)), ((), ()))
NT = (((1,), (1,)), ((), ()))
TN = (((0,), (0,)), ((), ()))


def _n_chunk(n):
    return 256 if n % 256 == 0 else LANES


def _mm_body(dims, n_out):
    nc = _n_chunk(n_out)

    def body(a_ref, b_ref, o_ref):
        for n0 in range(0, n_out, nc):
            b = b_ref[n0 : n0 + nc, :] if dims is NT else b_ref[:, n0 : n0 + nc]
            o_ref[:, n0 : n0 + nc] = lax.dot_general(a_ref[...], b, dims, preferred_element_type=F32).astype(o_ref.dtype)

    return body


def _ring_chips():
    x, y, c = _position()
    first = (x + (1 - c) - 2 * x * (1 - c), y + c - 2 * y * c)
    second = (x + c - 2 * x * c, y + (1 - c) - 2 * y * (1 - c))
    return first, second, (1 - x, 1 - y)


def _arrival_order():
    x, y, c = _position()
    first, second, diag = _ring_chips()
    blocks = [(x, y, c), (x, y, 1 - c), (*first, c), (*second, 1 - c), (*second, c), (*first, 1 - c), (*diag, c), (*diag, 1 - c)]
    return jnp.stack([4 * b[0] + 2 * b[1] + b[2] for b in blocks]).astype(jnp.int32)


def _proj_gather(h, w_in_loc, w_out_loc, order, tm):
    s, d = h.shape
    ne = w_in_loc.shape[1]
    ni = s // tm
    nc = _n_chunk(ne)
    n_streams = 7

    def body(order_ref, h_ref, win_ref, wout_ref, proj_ref, fin_ref, fout_ref, bbuf, send_sems, recv_sems, local_sems, load_sems):
        ins, fulls = [win_ref, wout_ref], [fin_ref, fout_ref]
        jj, i = pl.program_id(0), pl.program_id(1)
        x, y, c = _position()
        me = (x, y, c)
        sibling = (x, y, 1 - c)
        first, second, diag = _ring_chips()
        sent = [(me, sibling), (me, (*first, c)), (me, (*second, c)), ((*first, c), (*second, c)),
                ((*first, c), sibling), ((*second, c), sibling), ((*diag, c), sibling)]
        received = [sibling, (*first, c), (*second, c), (*diag, c), (*second, 1 - c), (*first, 1 - c), (*diag, 1 - c)]

        def slot(p):
            return 4 * p[0] + 2 * p[1] + p[2]

        def copy(a, k, block, to):
            dst = fulls[a].at[slot(block)]
            return pltpu.make_async_remote_copy(
                src_ref=ins[a] if k < 3 else dst, dst_ref=dst, send_sem=send_sems.at[a, k], recv_sem=recv_sems.at[a, k],
                device_id=to, device_id_type=MESH)

        def send(a, k):
            return copy(a, k, *sent[k])

        def recv(a, k):
            return copy(a, k, received[k], me)

        def own(a):
            return pltpu.make_async_copy(ins[a], fulls[a].at[slot(me)], local_sems.at[a])

        def load(src, buf):
            return pltpu.make_async_copy(src, bbuf.at[buf], load_sems.at[buf])

        @pl.when((jj == 0) & (i == 0))
        def _():
            load(win_ref, 0).start()
            for a in range(2):
                own(a).start()
            send(0, 0).start()
            send(0, 1).start()
            send(1, 0).start()
            load(win_ref, 0).wait()

        stream_of = [None, 0, 1, 4, 2, 5, 3, 6]
        passes = {1: [(0, 4), (0, 2), (0, 3), (1, 1), (1, 2)], 2: [(0, 5)], 3: [(0, 6)]}
        for nxt in range(1, N_DEV):

            @pl.when((jj == nxt - 1) & (i == ni - 1))
            def _(nxt=nxt):
                k = stream_of[nxt]
                recv(0, k).wait_recv()
                for a, k2 in passes.get(k, []):
                    send(a, k2).start()
                if nxt == N_DEV - 2:
                    recv(1, 1).wait_recv()
                    send(1, 4).start()
                    send(1, 3).start()
                load(fin_ref.at[order_ref[nxt]], nxt % 2).start()

        @pl.when((jj > 0) & (i == 0))
        def _():
            load(fin_ref.at[0], jj % 2).wait()

        for n0 in range(0, ne, nc):
            proj_ref[:, n0 : n0 + nc] = lax.dot_general(
                h_ref[...], bbuf[jj % 2, :, n0 : n0 + nc], NN, preferred_element_type=F32).astype(BF16)

        @pl.when((jj == N_DEV - 1) & (i == ni - 1))
        def _():
            recv(1, 2).wait_recv()
            send(1, 5).start()
            recv(1, 3).wait_recv()
            send(1, 6).start()
            for k in (0, 4, 5, 6):
                recv(1, k).wait_recv()
            for a in range(2):
                for k in range(n_streams):
                    send(a, k).wait_send()
                own(a).wait()

    hbm = pl.BlockSpec(memory_space=pl.ANY)
    return pl.pallas_call(
        body, name="proj_gather",
        grid_spec=pltpu.PrefetchScalarGridSpec(
            num_scalar_prefetch=1, grid=(N_DEV, ni),
            in_specs=[pl.BlockSpec((tm, d), lambda jj, i, od: (i, 0)), hbm, hbm],
            out_specs=[pl.BlockSpec((tm, ne), lambda jj, i, od: (i, od[jj])), hbm, hbm],
            scratch_shapes=[pltpu.VMEM((2, d, ne), BF16), pltpu.SemaphoreType.DMA((2, 7)), pltpu.SemaphoreType.DMA((2, 7)),
                            pltpu.SemaphoreType.DMA((2,)), pltpu.SemaphoreType.DMA((2,))]),
        out_shape=[jax.ShapeDtypeStruct((s, N_DEV * ne), BF16), jax.ShapeDtypeStruct((N_DEV,) + w_in_loc.shape, BF16),
                   jax.ShapeDtypeStruct((N_DEV,) + w_out_loc.shape, BF16)],
        compiler_params=_cparams(2),
    )(order, h, w_in_loc, w_out_loc)


def _mm_out(y, w_out, tm, tn):
    s, m = y.shape
    _, d = w_out.shape
    return pl.pallas_call(
        _mm_body(NN, tn), name="mm_out", grid=(d // tn, s // tm),
        in_specs=[pl.BlockSpec((tm, m), lambda j, i: (i, 0)), pl.BlockSpec((m, tn), lambda j, i: (0, j))],
        out_specs=pl.BlockSpec((tm, tn), lambda j, i: (i, j)),
        out_shape=jax.ShapeDtypeStruct((s, d), F32), compiler_params=_cparams(2),
    )(y, w_out)


def _mm_dy(dout, w_out, tm, tn):
    s, d = dout.shape
    m, _ = w_out.shape
    return pl.pallas_call(
        _mm_body(NT, tn), name="mm_dy", grid=(m // tn, s // tm),
        in_specs=[pl.BlockSpec((tm, d), lambda j, i: (i, 0)), pl.BlockSpec((tn, d), lambda j, i: (j, 0))],
        out_specs=pl.BlockSpec((tm, tn), lambda j, i: (i, j)),
        out_shape=jax.ShapeDtypeStruct((s, m), BF16), compiler_params=_cparams(2),
    )(dout, w_out)


def _dw_targets():
    _, _, c = _position()
    return jnp.stack([2 * q + (1 - c) for q in range(N_CHIP)] + [2 * q + c for q in range(N_CHIP)]).astype(jnp.int32)


def _mm_dw_exchange(lhs, rhs, targets, bm, bn, target_on_lhs, name):
    s = lhs.shape[0]
    ni = (rhs.shape[1] // bn) if target_on_lhs else (lhs.shape[1] // bm)
    rows_b, cols_b = (bm, ni * bn) if target_on_lhs else (ni * bm, bn)
    nc = _n_chunk(bn)

    def body(tg_ref, a_ref, b_ref, comb_ref, psib_ref, rsib_ref, obuf, rbuf, osems, rsems, send_sems, recv_sems):
        del tg_ref
        jj, i = pl.program_id(0), pl.program_id(1)
        x, y, c = _position()
        slot = i % 2

        def tile(ref, q, t):
            if target_on_lhs:
                return ref.at[q, :, pl.ds(pl.multiple_of(t * bn, bn), bn)]
            return ref.at[q, pl.ds(pl.multiple_of(t * bm, bm), bm)]

        def put(sl, q):
            return pltpu.make_async_copy(obuf.at[sl], tile(psib_ref, q, i), osems.at[sl])

        def get(sl, q, t):
            return pltpu.make_async_copy(tile(rsib_ref, q, t), rbuf.at[sl], rsems.at[sl])

        def swap(q):
            return pltpu.make_async_remote_copy(
                src_ref=psib_ref.at[q], dst_ref=rsib_ref.at[q], send_sem=send_sems.at[q], recv_sem=recv_sems.at[q],
                device_id=(x, y, 1 - c), device_id_type=MESH)

        def product(n0):
            return lax.dot_general(a_ref[...], b_ref[:, n0 : n0 + nc], TN, preferred_element_type=F32)

        @pl.when(jj < N_CHIP)
        def _():
            @pl.when(i >= 2)
            def _():
                put(slot, 0).wait()

            for n0 in range(0, bn, nc):
                obuf[slot, :, n0 : n0 + nc] = product(n0).astype(BF16)
            put(slot, jj).start()

            @pl.when(i == ni - 1)
            def _():
                put(slot, 0).wait()
                if ni >= 2:
                    put(1 - slot, 0).wait()
                swap(jj).start()

        @pl.when(jj >= N_CHIP)
        def _():
            q = jj - N_CHIP

            @pl.when(i == 0)
            def _():
                swap(q).wait_recv()
                get(0, q, i).start()

            get(slot, 0, i).wait()

            @pl.when(i + 1 < ni)
            def _():
                get(1 - slot, q, i + 1).start()

            for n0 in range(0, bn, nc):
                comb_ref[:, n0 : n0 + nc] = (product(n0) + rbuf[slot, :, n0 : n0 + nc].astype(F32)).astype(BF16)

            @pl.when((jj == N_DEV - 1) & (i == ni - 1))
            def _():
                for qq in range(N_CHIP):
                    swap(qq).wait_send()

    if target_on_lhs:
        in_specs = [pl.BlockSpec((s, bm), lambda jj, i, tg: (0, tg[jj])), pl.BlockSpec((s, bn), lambda jj, i, tg: (0, i))]
        tile_index = lambda jj, i, tg: (jnp.maximum(jj - N_CHIP, 0), 0, jnp.where(jj < N_CHIP, 0, i))
    else:
        in_specs = [pl.BlockSpec((s, bm), lambda jj, i, tg: (0, i)), pl.BlockSpec((s, bn), lambda jj, i, tg: (0, tg[jj]))]
        tile_index = lambda jj, i, tg: (jnp.maximum(jj - N_CHIP, 0), jnp.where(jj < N_CHIP, 0, i), 0)
    hbm = pl.BlockSpec(memory_space=pl.ANY)
    comb, _, _ = pl.pallas_call(
        body, name=name,
        grid_spec=pltpu.PrefetchScalarGridSpec(
            num_scalar_prefetch=1, grid=(N_DEV, ni), in_specs=in_specs,
            out_specs=[pl.BlockSpec((None, bm, bn), tile_index), hbm, hbm],
            scratch_shapes=[pltpu.VMEM((2, bm, bn), BF16), pltpu.VMEM((2, bm, bn), BF16), pltpu.SemaphoreType.DMA((2,)),
                            pltpu.SemaphoreType.DMA((2,)), pltpu.SemaphoreType.DMA((N_CHIP,)), pltpu.SemaphoreType.DMA((N_CHIP,))]),
        out_shape=[jax.ShapeDtypeStruct((N_CHIP, rows_b, cols_b), BF16)] * 3,
        compiler_params=_cparams(2),
    )(targets, lhs, rhs)
    return comb


def _chip_copies(c_refs, r_refs, send_sems, recv_sems):
    x, y, c = _position()
    chips = [(1 - x, y), (x, 1 - y), (1 - x, 1 - y)]
    return [
        pltpu.make_async_remote_copy(
            src_ref=c_refs[a].at[2 * chip[0] + chip[1]], dst_ref=r_refs[a].at[j], send_sem=send_sems.at[a, j],
            recv_sem=recv_sems.at[a, j], device_id=(*chip, c), device_id_type=MESH)
        for a in range(len(c_refs)) for j, chip in enumerate(chips)]


def _mm_dh(dproj, w_full, comb, tm, tn):
    s = dproj.shape[0]
    _, d, ne = w_full.shape
    _, r_rows, r_cols = comb.shape
    kb = 2
    nk = N_DEV // kb
    nc = _n_chunk(tn)
    mh = min(512, tm)
    grid = (s // tm, d // tn, nk)
    n_steps = grid[0] * grid[1] * grid[2]
    n_relay = (3 * n_steps) // 8
    assert 0 < n_relay < n_steps - 1
    rc = min(512, r_rows)

    def body(a_ref, b_ref, c_ref, o_ref, recv_ref, relay_ref, sum_ref, va, vb, add_sems, send_sems, recv_sems):
        i, n, k = pl.program_id(0), pl.program_id(1), pl.program_id(2)
        step = (i * grid[1] + n) * nk + k
        _, _, c = _position()
        first, second, diag = _ring_chips()

        def chip_slot(p):
            return 2 * p[0] + p[1]

        def stream(j):
            src, dst, to = [(c_ref.at[chip_slot(diag)], relay_ref, first), (c_ref.at[chip_slot(first)], recv_ref.at[0], first),
                            (sum_ref, recv_ref.at[1], second)][j]
            return pltpu.make_async_remote_copy(
                src_ref=src, dst_ref=dst, send_sem=send_sems.at[j], recv_sem=recv_sems.at[j], device_id=(*to, c), device_id_type=MESH)

        @pl.when(step == 0)
        def _():
            stream(0).start()
            stream(1).start()

        @pl.when(step == n_relay)
        def _():
            stream(0).wait_recv()

            n_pieces = r_rows // rc

            def pieces(t, sl):
                rows = pl.ds(t * rc, rc)
                return (pltpu.make_async_copy(c_ref.at[chip_slot(second), rows], va.at[sl], add_sems.at[sl, 0]),
                        pltpu.make_async_copy(relay_ref.at[rows], vb.at[sl], add_sems.at[sl, 1]),
                        pltpu.make_async_copy(va.at[sl], sum_ref.at[rows], add_sems.at[sl, 2]))

            for cp in pieces(0, 0)[:2]:
                cp.start()
            for t in range(n_pieces):
                sl = t % 2
                for cp in pieces(t, sl)[:2]:
                    cp.wait()
                if t + 1 < n_pieces:
                    if t >= 1:
                        pieces(t - 1, 1 - sl)[2].wait()
                    for cp in pieces(t + 1, 1 - sl)[:2]:
                        cp.start()
                va[sl] = (va[sl].astype(F32) + vb[sl].astype(F32)).astype(BF16)
                pieces(t, sl)[2].start()
            for t in range(max(n_pieces - 2, 0), n_pieces):
                pieces(t, t % 2)[2].wait()
            stream(2).start()

        @pl.when(k == 0)
        def _():
            o_ref[...] = jnp.zeros_like(o_ref)

        for m0 in range(0, tm, mh):
            for n0 in range(0, tn, nc):
                r = None
                for jj in range(kb):
                    t = lax.dot_general(a_ref[m0 : m0 + mh, jj * ne : (jj + 1) * ne], b_ref[jj, n0 : n0 + nc, :], NT, preferred_element_type=F32)
                    r = t if r is None else r + t
                o_ref[m0 : m0 + mh, n0 : n0 + nc] += r

        @pl.when(step == n_steps - 1)
        def _():
            stream(1).wait_recv()
            stream(2).wait_recv()
            for j in range(3):
                stream(j).wait_send()

    hbm = pl.BlockSpec(memory_space=pl.ANY)
    block = jax.ShapeDtypeStruct((r_rows, r_cols), comb.dtype)
    dh, recv, _, _ = pl.pallas_call(
        body, name="mm_dh", grid=grid,
        in_specs=[pl.BlockSpec((tm, kb * ne), lambda i, n, k: (i, k)), pl.BlockSpec((kb, tn, ne), lambda i, n, k: (k, n, 0)), hbm],
        out_specs=[pl.BlockSpec((tm, tn), lambda i, n, k: (i, n)), hbm, hbm, hbm],
        out_shape=[jax.ShapeDtypeStruct((s, d), F32), jax.ShapeDtypeStruct((2, r_rows, r_cols), comb.dtype), block, block],
        scratch_shapes=[pltpu.VMEM((2, rc, r_cols), comb.dtype), pltpu.VMEM((2, rc, r_cols), comb.dtype), pltpu.SemaphoreType.DMA((2, 3)),
                        pltpu.SemaphoreType.DMA((3,)), pltpu.SemaphoreType.DMA((3,))],
        compiler_params=_cparams(3),
    )(dproj, w_full, comb)
    return dh, recv


def _mod_matmul(c_all, w_ada, b_loc, tk):
    nb, d = c_all.shape
    na = w_ada.shape[1]

    def body(c_ref, w_ref, b_ref, o_ref):
        k = pl.program_id(0)
        r = jnp.dot(c_ref[...], w_ref[...], preferred_element_type=F32, precision=lax.Precision.HIGHEST)

        @pl.when(k == 0)
        def _():
            o_ref[...] = r + b_ref[...]

        @pl.when(k > 0)
        def _():
            o_ref[...] += r

    return pl.pallas_call(
        body, name="mod_matmul", grid=(d // tk,),
        in_specs=[pl.BlockSpec((nb, tk), lambda k: (0, k)), pl.BlockSpec((tk, na), lambda k: (k, 0)), pl.BlockSpec((1, na), lambda k: (0, 0))],
        out_specs=pl.BlockSpec((nb, na), lambda k: (0, 0)),
        out_shape=jax.ShapeDtypeStruct((nb, na), F32), compiler_params=_cparams(1),
    )(c_all, w_ada, b_loc)


def _col_chunks(width):
    return [slice(c0, c0 + LANES) for c0 in range(0, width, LANES)]


def _row_sum(acc):
    return jnp.sum(acc, axis=1, keepdims=True)


SUBLANES = 8


ROW_BLOCK = 64
DW_BLOCK = 96


def _tap_sum(ext, cs, ts, taps, emit):
    for t0 in range(0, ts, ROW_BLOCK):
        rb = min(ROW_BLOCK, ts - t0)
        acc = None
        for r in range(SUBLANES):
            group = [(o, wv) for o, wv in taps if o % SUBLANES == r]
            if not group:
                continue
            n = rb if r == 0 else rb + SUBLANES
            v = None
            for o, wv in group:
                term = wv * ext[t0 + o - r : t0 + o - r + n, cs]
                v = term if v is None else v + term
            part = v if r == 0 else v[r : r + rb]
            acc = part if acc is None else acc + part
        emit(slice(t0, t0 + rb), acc)


class _SegmentStores:
    def __init__(self, stash, dst, sems, n_seg, step, n_steps, row0, cols):
        self.stash, self.dst, self.sems, self.n_seg = stash, dst, sems, n_seg
        self.step, self.n_steps, self.row0, self.cols = step, n_steps, row0, cols
        self.slot = step % 2

    def _copy(self, slot, g, row0, col0):
        ts, cb = self.stash.shape[2:]
        return pltpu.make_async_copy(self.stash.at[slot, g], self.dst.at[pl.ds(row0, ts), pl.ds(col0, cb)], self.sems.at[slot, g])

    def _wait(self, slot):
        for g in range(self.n_seg):
            self._copy(slot, g, 0, 0).wait()

    def begin(self):
        @pl.when(self.step >= 2)
        def _():
            self._wait(self.slot)

        return self.slot

    def finish(self):
        ts = self.stash.shape[2]
        for g in range(self.n_seg):
            self._copy(self.slot, g, pl.multiple_of(self.row0, ts), pl.multiple_of(self.cols[g], LANES)).start()

        @pl.when(self.step == self.n_steps - 1)
        def _():
            self._wait(self.slot)
            if self.n_steps >= 2:
                self._wait(1 - self.slot)


def _prep(x, g1, scale, shift, ts):
    s, d = x.shape
    chunks = _col_chunks(d)

    def body(x_ref, g_ref, sc_ref, sh_ref, h_ref):
        acc = jnp.zeros((ts, LANES), F32)
        for cs in chunks:
            v = x_ref[:, cs]
            acc = acc + v * v
        r = lax.rsqrt(_row_sum(acc) / d + EPS)
        for cs in chunks:
            n1 = (x_ref[:, cs] * r) * g_ref[:, cs]
            h_ref[:, cs] = (n1 * (1.0 + sc_ref[:, cs]) + sh_ref[:, cs]).astype(BF16)

    row = pl.BlockSpec((1, d), lambda i: (0, 0))
    return pl.pallas_call(
        body, name="prep", grid=(s // ts,),
        in_specs=[pl.BlockSpec((ts, d), lambda i: (i, 0)), row, row, row],
        out_specs=pl.BlockSpec((ts, d), lambda i: (i, 0)),
        out_shape=jax.ShapeDtypeStruct((s, d), BF16), compiler_params=_cparams(1),
    )(x, g1, scale, shift)


def _mix_a_fwd(proj, wa, w, ts, cb):
    s = proj.shape[0]
    nb = w // cb
    chunks = _col_chunks(cb)

    def body(ab, ac, ax, az, wa_ref, y_ref, cv_ref, ext):
        @pl.when(pl.program_id(1) == 0)
        def _():
            ext[0:HALO_A, :] = jnp.zeros((HALO_A, cb), F32)

        for cs in chunks:
            ext[HALO_A : HALO_A + ts, cs] = ac[:, cs].astype(F32) * ax[:, cs].astype(F32)
        for cs in chunks:

            def emit(rows, cv, cs=cs):
                z = az[rows, cs].astype(F32)
                y_ref[rows, cs] = (ab[rows, cs].astype(F32) * cv * (z * _sigmoid(z))).astype(BF16)
                cv_ref[rows, cs] = cv.astype(BF16)

            _tap_sum(ext, cs, ts, [(HALO_A - (CONV_A - 1) + k, wa_ref[k : k + 1, cs]) for k in range(CONV_A)], emit)
        ext[0:HALO_A, :] = ext[ts : ts + HALO_A, :]

    def seg(q):
        return pl.BlockSpec((ts, cb), lambda c, i: (i, q * nb + c))

    return pl.pallas_call(
        body, name="mix_a_fwd", grid=(nb, s // ts),
        in_specs=[seg(0), seg(1), seg(2), seg(3), pl.BlockSpec((CONV_A, cb), lambda c, i: (0, c))],
        out_specs=[pl.BlockSpec((ts, cb), lambda c, i: (i, c)), pl.BlockSpec((ts, cb), lambda c, i: (i, c))],
        out_shape=[jax.ShapeDtypeStruct((s, 2 * w), BF16), jax.ShapeDtypeStruct((s, w), BF16)],
        scratch_shapes=[pltpu.VMEM((HALO_A + ts, cb), F32)], compiler_params=_cparams(2),
    )(proj, proj, proj, proj, wa)


def _mix_b_conv_fwd(proj, wb, bb, w, ts, cb):
    s = proj.shape[0]
    nb = w // cb
    chunks = _col_chunks(cb)

    def body(bv, bg, wb_ref, bb_ref, u2_ref, ext):
        @pl.when(pl.program_id(1) == 0)
        def _():
            ext[0:HALO_B, :] = jnp.zeros((HALO_B, cb), F32)

        for cs in chunks:
            ext[HALO_B : HALO_B + ts, cs] = bv[:, cs].astype(F32) * _sigmoid(bg[:, cs].astype(F32))
        for cs in chunks:

            def emit(rows, acc, cs=cs):
                u2_ref[rows, cs] = acc + bb_ref[:, cs]

            _tap_sum(ext, cs, ts, [(HALO_B - (CONV_B - 1) + k, wb_ref[k : k + 1, cs]) for k in range(CONV_B)], emit)
        ext[0:HALO_B, :] = ext[ts : ts + HALO_B, :]

    return pl.pallas_call(
        body, name="mix_b_conv_fwd", grid=(nb, s // ts),
        in_specs=[pl.BlockSpec((ts, cb), lambda c, i: (i, 4 * nb + c)), pl.BlockSpec((ts, cb), lambda c, i: (i, 5 * nb + c)),
                  pl.BlockSpec((CONV_B, cb), lambda c, i: (0, c)), pl.BlockSpec((1, cb), lambda c, i: (0, c))],
        out_specs=pl.BlockSpec((ts, cb), lambda c, i: (i, c)),
        out_shape=jax.ShapeDtypeStruct((s, w), F32),
        scratch_shapes=[pltpu.VMEM((HALO_B + ts, cb), F32)], compiler_params=_cparams(2),
    )(proj, proj, wb, bb)


def _ln_stats(u2_ref, ts, w, chunks):
    acc = jnp.zeros((ts, LANES), F32)
    for cs in chunks:
        acc = acc + u2_ref[:, cs]
    mu = _row_sum(acc) / w
    acc = jnp.zeros((ts, LANES), F32)
    for cs in chunks:
        xc = u2_ref[:, cs] - mu
        acc = acc + xc * xc
    return mu, lax.rsqrt(_row_sum(acc) / w + EPS)


def _mix_b_gate_fwd(u2, proj, lg, lb, y, ts):
    s, w = u2.shape
    chunks = _col_chunks(w)

    def body(u2_ref, bz, lg_ref, lb_ref, y_in, y_ref):
        del y_in
        mu, rstd = _ln_stats(u2_ref, ts, w, chunks)
        for cs in chunks:
            ln = (u2_ref[:, cs] - mu) * rstd * lg_ref[:, cs] + lb_ref[:, cs]
            z = bz[:, cs].astype(F32)
            y_ref[:, cs] = ((ln * _sigmoid(ln)) * (z * _sigmoid(z))).astype(BF16)

    row = pl.BlockSpec((1, w), lambda i: (0, 0))
    return pl.pallas_call(
        body, name="mix_b_gate_fwd", grid=(s // ts,),
        in_specs=[pl.BlockSpec((ts, w), lambda i: (i, 0)), pl.BlockSpec((ts, w), lambda i: (i, 6)), row, row,
                  pl.BlockSpec(memory_space=pl.ANY)],
        out_specs=pl.BlockSpec((ts, w), lambda i: (i, 1)),
        out_shape=jax.ShapeDtypeStruct(y.shape, y.dtype), input_output_aliases={4: 0}, compiler_params=_cparams(1),
    )(u2, proj, lg, lb, y)


def _post(x, out, tgt, gate, fg, ts):
    s, d = x.shape
    chunks = _col_chunks(d)

    def body(x_ref, o_ref, t_ref, gt_ref, fg_ref, dx2_ref, dout_ref, loss_ref, dgate_ref, dfg_ref):
        @pl.when(pl.program_id(0) == 0)
        def _():
            loss_ref[...] = jnp.zeros_like(loss_ref)
            dgate_ref[...] = jnp.zeros_like(dgate_ref)
            dfg_ref[...] = jnp.zeros_like(dfg_ref)

        acc = jnp.zeros((ts, LANES), F32)
        for cs in chunks:
            x2 = x_ref[:, cs] + gt_ref[:, cs] * o_ref[:, cs]
            acc = acc + x2 * x2
        r2 = lax.rsqrt(_row_sum(acc) / d + EPS)
        acc = jnp.zeros((ts, LANES), F32)
        for cs in chunks:
            n2 = (x_ref[:, cs] + gt_ref[:, cs] * o_ref[:, cs]) * r2
            diff = n2 * fg_ref[:, cs] - t_ref[:, cs]
            loss_ref[:, cs] += jnp.sum(diff * diff, axis=0, keepdims=True)
            dyf = diff / d
            dfg_ref[:, cs] += jnp.sum(dyf * n2, axis=0, keepdims=True)
            acc = acc + (dyf * fg_ref[:, cs]) * n2
        mdot = _row_sum(acc) / d
        for cs in chunks:
            o = o_ref[:, cs]
            n2 = (x_ref[:, cs] + gt_ref[:, cs] * o) * r2
            dn = ((n2 * fg_ref[:, cs] - t_ref[:, cs]) / d) * fg_ref[:, cs]
            dx2 = r2 * (dn - n2 * mdot)
            dx2_ref[:, cs] = dx2
            dgate_ref[:, cs] += jnp.sum(dx2 * o, axis=0, keepdims=True)
            dout_ref[:, cs] = (dx2 * gt_ref[:, cs]).astype(BF16)

    tile = pl.BlockSpec((ts, d), lambda i: (i, 0))
    row = pl.BlockSpec((1, d), lambda i: (0, 0))
    return pl.pallas_call(
        body, name="post", grid=(s // ts,),
        in_specs=[tile, tile, tile, row, row], out_specs=[tile, tile, row, row, row],
        out_shape=[jax.ShapeDtypeStruct((s, d), F32), jax.ShapeDtypeStruct((s, d), BF16)] + [jax.ShapeDtypeStruct((1, d), F32)] * 3,
        compiler_params=_cparams(1),
    )(x, out, tgt, gate, fg)


def _mix_b_gate_bwd(u2, proj, dy, lg, lb, n_cols, ts):
    s, w = u2.shape
    chunks = _col_chunks(w)

    def body(u2_ref, bz, dyb, lg_ref, lb_ref, du2_ref, dbz_ref, dlg_ref, dlb_ref, dbb_ref):
        @pl.when(pl.program_id(0) == 0)
        def _():
            dlg_ref[...] = jnp.zeros_like(dlg_ref)
            dlb_ref[...] = jnp.zeros_like(dlb_ref)
            dbb_ref[...] = jnp.zeros_like(dbb_ref)

        mu, rstd = _ln_stats(u2_ref, ts, w, chunks)
        acc1 = jnp.zeros((ts, LANES), F32)
        acc2 = jnp.zeros((ts, LANES), F32)
        for cs in chunks:
            xh = (u2_ref[:, cs] - mu) * rstd
            ln = xh * lg_ref[:, cs] + lb_ref[:, cs]
            sl = _sigmoid(ln)
            z = bz[:, cs].astype(F32)
            sz = _sigmoid(z)
            g = dyb[:, cs].astype(F32)
            dbz_ref[:, cs] = (g * (ln * sl) * (sz * (1.0 + z * (1.0 - sz)))).astype(BF16)
            dln = g * (z * sz) * (sl * (1.0 + ln * (1.0 - sl)))
            dlg_ref[:, cs] += jnp.sum(dln * xh, axis=0, keepdims=True)
            dlb_ref[:, cs] += jnp.sum(dln, axis=0, keepdims=True)
            dxh = dln * lg_ref[:, cs]
            du2_ref[:, cs] = dxh
            acc1 = acc1 + dxh
            acc2 = acc2 + dxh * xh
        m1 = _row_sum(acc1) / w
        m2 = _row_sum(acc2) / w
        for cs in chunks:
            xh = (u2_ref[:, cs] - mu) * rstd
            du2 = rstd * (du2_ref[:, cs] - m1 - xh * m2)
            du2_ref[:, cs] = du2
            dbb_ref[:, cs] += jnp.sum(du2, axis=0, keepdims=True)

    tile = pl.BlockSpec((ts, w), lambda i: (i, 0))
    row = pl.BlockSpec((1, w), lambda i: (0, 0))
    return pl.pallas_call(
        body, name="mix_b_gate_bwd", grid=(s // ts,),
        in_specs=[tile, pl.BlockSpec((ts, w), lambda i: (i, 6)), pl.BlockSpec((ts, w), lambda i: (i, 1)), row, row],
        out_specs=[tile, pl.BlockSpec((ts, w), lambda i: (i, 6)), row, row, row],
        out_shape=[jax.ShapeDtypeStruct((s, w), F32), jax.ShapeDtypeStruct((s, n_cols), BF16)] + [jax.ShapeDtypeStruct((1, w), F32)] * 3,
        compiler_params=_cparams(1),
    )(u2, proj, dy, lg, lb)


def _mix_b_conv_bwd(du2, proj, wb, dproj, comb, w, ts, cb):
    s = proj.shape[0]
    nb = w // cb
    ns = s // ts
    chunks = _col_chunks(cb)

    u_rows = ts + 2 * HALO_B + SUBLANES
    win = ts + HALO_B
    assert win % DW_BLOCK == 0

    n_seg = 2

    def body(du2_ref, bv, bg, wb_ref, dp_in, comb_ref, dp_ref, dwb_ref, recv_ref, ext, stash, uext, sems, send_sems, recv_sems):
        del dp_in
        c, i = pl.program_id(0), pl.program_id(1)
        stores = _SegmentStores(stash, dp_ref, sems, n_seg, c * ns + i, nb * ns, (ns - 1 - i) * ts, [((4 + g) * nb + c) * cb for g in range(n_seg)])
        slot = stores.begin()

        def copies():
            return _chip_copies([comb_ref], [recv_ref], send_sems, recv_sems)

        @pl.when((c == 0) & (i == 0))
        def _():
            for cp in copies():
                cp.start()

        @pl.when(i == 0)
        def _():
            ext[ts : ts + HALO_B, :] = jnp.zeros((HALO_B, cb), F32)
            uext[0:HALO_B, :] = jnp.zeros((HALO_B, cb), F32)
            uext[HALO_B + ts : u_rows, :] = jnp.zeros((u_rows - HALO_B - ts, cb), F32)
            dwb_ref[...] = jnp.zeros_like(dwb_ref)

        ext[0:ts, :] = du2_ref[...]
        for cs in chunks:
            uext[HALO_B : HALO_B + ts, cs] = bv[:, cs].astype(F32) * _sigmoid(bg[:, cs].astype(F32))

            def emit(rows, du, cs=cs):
                v = bv[rows, cs].astype(F32)
                sg = _sigmoid(bg[rows, cs].astype(F32))
                stash[slot, 0, rows, cs] = (du * sg).astype(BF16)
                stash[slot, 1, rows, cs] = (du * v * (sg * (1.0 - sg))).astype(BF16)

            _tap_sum(ext, cs, ts, [(CONV_B - 1 - k, wb_ref[k : k + 1, cs]) for k in range(CONV_B)], emit)
            for r in range(SUBLANES):
                ks = [k for k in range(CONV_B) if (CONV_B - 1 - k) % SUBLANES == r]
                accs = [jnp.zeros((SUBLANES, LANES), F32) for _ in ks]
                for t0 in range(0, win, DW_BLOCK):
                    e = ext[t0 : t0 + DW_BLOCK, cs]
                    ush = uext[SUBLANES - r + t0 : SUBLANES - r + t0 + DW_BLOCK + HALO_B - SUBLANES, cs]
                    for n, k in enumerate(ks):
                        b0 = HALO_B - SUBLANES - (CONV_B - 1 - k - r)
                        prod = e * ush[b0 : b0 + DW_BLOCK]
                        for v0 in range(0, DW_BLOCK, SUBLANES):
                            accs[n] = accs[n] + prod[v0 : v0 + SUBLANES]
                for n, k in enumerate(ks):
                    dwb_ref[k : k + 1, cs] += jnp.sum(accs[n], axis=0, keepdims=True)
        ext[ts : ts + HALO_B, :] = ext[0:HALO_B, :]
        stores.finish()

        @pl.when((c == nb - 1) & (i == ns - 1))
        def _():
            for cp in copies():
                cp.wait_recv()
            for cp in copies():
                cp.wait_send()

    def rev(col):
        return lambda c, i: (ns - 1 - i, col(c))

    hbm = pl.BlockSpec(memory_space=pl.ANY)
    return pl.pallas_call(
        body, name="mix_b_conv_bwd", grid=(nb, ns),
        in_specs=[pl.BlockSpec((ts, cb), rev(lambda c: c)), pl.BlockSpec((ts, cb), rev(lambda c: 4 * nb + c)),
                  pl.BlockSpec((ts, cb), rev(lambda c: 5 * nb + c)), pl.BlockSpec((CONV_B, cb), lambda c, i: (0, c)), hbm, hbm],
        out_specs=[hbm, pl.BlockSpec((CONV_B, cb), lambda c, i: (0, c)), hbm],
        out_shape=[jax.ShapeDtypeStruct(dproj.shape, dproj.dtype), jax.ShapeDtypeStruct((CONV_B, w), F32),
                   jax.ShapeDtypeStruct((3,) + comb.shape[1:], comb.dtype)],
        input_output_aliases={4: 0},
        scratch_shapes=[pltpu.VMEM((ts + HALO_B, cb), F32), pltpu.VMEM((2, n_seg, ts, cb), BF16), pltpu.VMEM((u_rows, cb), F32),
                        pltpu.SemaphoreType.DMA((2, n_seg)), pltpu.SemaphoreType.DMA((1, 3)), pltpu.SemaphoreType.DMA((1, 3))],
        compiler_params=_cparams(2),
    )(du2, proj, proj, wb, dproj, comb)


def _mix_a_bwd(proj, cv, dy, wa, dproj, w, ts, cb):
    s = proj.shape[0]
    nb = w // cb
    ns = s // ts
    chunks = _col_chunks(cb)

    n_seg = 4

    def body(ab, ac, ax, az, cv_ref, dya, wa_ref, dp_in, dp_ref, dwa_ref, ext, stash, sems):
        del dp_in
        c, i = pl.program_id(0), pl.program_id(1)
        stores = _SegmentStores(stash, dp_ref, sems, n_seg, c * ns + i, nb * ns, (ns - 1 - i) * ts, [(g * nb + c) * cb for g in range(n_seg)])
        slot = stores.begin()

        @pl.when(i == 0)
        def _():
            ext[ts : ts + HALO_A, :] = jnp.zeros((HALO_A, cb), F32)
            dwa_ref[...] = jnp.zeros_like(dwa_ref)

        for cs in chunks:
            b = ab[:, cs].astype(F32)
            z = az[:, cs].astype(F32)
            sz = _sigmoid(z)
            g = dya[:, cs].astype(F32)
            conv = cv_ref[:, cs].astype(F32)
            stash[slot, 0, :, cs] = (g * conv * (z * sz)).astype(BF16)
            stash[slot, 3, :, cs] = (g * b * conv * (sz * (1.0 + z * (1.0 - sz)))).astype(BF16)
            ext[0:ts, cs] = g * b * (z * sz)
        for cs in chunks:

            def emit(rows, dca, cs=cs):
                stash[slot, 1, rows, cs] = (dca * ax[rows, cs].astype(F32)).astype(BF16)
                stash[slot, 2, rows, cs] = (dca * ac[rows, cs].astype(F32)).astype(BF16)

            _tap_sum(ext, cs, ts, [(CONV_A - 1 - k, wa_ref[k : k + 1, cs]) for k in range(CONV_A)], emit)
            ca = ac[:, cs].astype(F32) * ax[:, cs].astype(F32)
            for k in range(CONV_A):
                o = CONV_A - 1 - k
                dwa_ref[k : k + 1, cs] += jnp.sum(ca * ext[o : o + ts, cs], axis=0, keepdims=True)
        ext[ts : ts + HALO_A, :] = ext[0:HALO_A, :]
        stores.finish()

    def rev(col):
        return lambda c, i: (ns - 1 - i, col(c))

    def seg(sq):
        return pl.BlockSpec((ts, cb), rev(lambda c: sq * nb + c))

    hbm = pl.BlockSpec(memory_space=pl.ANY)
    return pl.pallas_call(
        body, name="mix_a_bwd", grid=(nb, ns),
        in_specs=[seg(0), seg(1), seg(2), seg(3), pl.BlockSpec((ts, cb), rev(lambda c: c)), pl.BlockSpec((ts, cb), rev(lambda c: c)),
                  pl.BlockSpec((CONV_A, cb), lambda c, i: (0, c)), hbm],
        out_specs=[hbm, pl.BlockSpec((CONV_A, cb), lambda c, i: (0, c))],
        out_shape=[jax.ShapeDtypeStruct(dproj.shape, dproj.dtype), jax.ShapeDtypeStruct((CONV_A, w), F32)],
        input_output_aliases={7: 0},
        scratch_shapes=[pltpu.VMEM((ts + HALO_A, cb), F32), pltpu.VMEM((2, n_seg, ts, cb), BF16), pltpu.SemaphoreType.DMA((2, n_seg))],
        compiler_params=_cparams(2),
    )(proj, proj, proj, proj, cv, dy, wa, dproj)


def _pre_bwd(x, dh, dx2, g1, scale, ts):
    s, d = x.shape
    chunks = _col_chunks(d)

    def body(x_ref, dh_ref, dx2_ref, g_ref, sc_ref, gx_ref, dsh_ref, dsc_ref, dg_ref):
        @pl.when(pl.program_id(0) == 0)
        def _():
            dsh_ref[...] = jnp.zeros_like(dsh_ref)
            dsc_ref[...] = jnp.zeros_like(dsc_ref)
            dg_ref[...] = jnp.zeros_like(dg_ref)

        acc = jnp.zeros((ts, LANES), F32)
        for cs in chunks:
            v = x_ref[:, cs]
            acc = acc + v * v
        r1 = lax.rsqrt(_row_sum(acc) / d + EPS)
        acc = jnp.zeros((ts, LANES), F32)
        for cs in chunks:
            xn = x_ref[:, cs] * r1
            g = dh_ref[:, cs]
            dsh_ref[:, cs] += jnp.sum(g, axis=0, keepdims=True)
            dsc_ref[:, cs] += jnp.sum(g * (xn * g_ref[:, cs]), axis=0, keepdims=True)
            dn1 = g * (1.0 + sc_ref[:, cs])
            dg_ref[:, cs] += jnp.sum(dn1 * xn, axis=0, keepdims=True)
            acc = acc + (dn1 * g_ref[:, cs]) * xn
        mdot = _row_sum(acc) / d
        for cs in chunks:
            xn = x_ref[:, cs] * r1
            dxn = (dh_ref[:, cs] * (1.0 + sc_ref[:, cs])) * g_ref[:, cs]
            gx_ref[:, cs] = dx2_ref[:, cs] + r1 * (dxn - xn * mdot)

    tile = pl.BlockSpec((ts, d), lambda i: (i, 0))
    row = pl.BlockSpec((1, d), lambda i: (0, 0))
    return pl.pallas_call(
        body, name="pre_bwd", grid=(s // ts,),
        in_specs=[tile, tile, tile, row, row], out_specs=[tile, row, row, row],
        out_shape=[jax.ShapeDtypeStruct((s, d), F32)] + [jax.ShapeDtypeStruct((1, d), F32)] * 3,
        compiler_params=_cparams(1),
    )(x, dh, dx2, g1, scale)


def _adamw(w, g, m, v):
    m = ADAM_B1 * m + (1.0 - ADAM_B1) * g
    v = ADAM_B2 * v + (1.0 - ADAM_B2) * (g * g)
    m_hat = m / (1.0 - ADAM_B1**ADAM_STEP)
    v_hat = v / (1.0 - ADAM_B2**ADAM_STEP)
    delta = -ADAM_LR * (m_hat / (jnp.sqrt(v_hat) + ADAM_EPS) + ADAM_WD * w)
    return delta, m, v


def _adam_sharded(w, m, v, comb, recv, chip, tr, name):
    r, n = w.shape
    nr = recv.shape[0]

    def body(chip_ref, w_ref, m_ref, v_ref, c_ref, r_ref, g_ref, d_ref, nm_ref, nv_ref):
        del chip_ref
        g = c_ref[...].astype(F32)
        for j in range(nr):
            g = g + r_ref[j].astype(F32)
        delta, nm, nv = _adamw(w_ref[...], g, m_ref[...], v_ref[...])
        g_ref[...] = g
        d_ref[...] = delta
        nm_ref[...] = nm
        nv_ref[...] = nv

    tile = pl.BlockSpec((tr, n), lambda i, ch: (i, 0))
    return pl.pallas_call(
        body, name=name,
        grid_spec=pltpu.PrefetchScalarGridSpec(
            num_scalar_prefetch=1, grid=(r // tr,),
            in_specs=[tile, tile, tile, pl.BlockSpec((None, tr, n), lambda i, ch: (ch[0], i, 0)), pl.BlockSpec((nr, tr, n), lambda i, ch: (0, i, 0))],
            out_specs=[tile] * 4),
        out_shape=[jax.ShapeDtypeStruct((r, n), F32)] * 4, compiler_params=_cparams(1),
    )(chip, w, m, v, comb, recv)


def _adam_w_ada(c_t, dm_loc, w, m, v, tr):
    d, nb = c_t.shape
    na = w.shape[1]

    def body(c_ref, dm_ref, w_ref, m_ref, v_ref, g_ref, d_ref, nm_ref, nv_ref):
        g = jnp.dot(c_ref[...], dm_ref[...], preferred_element_type=F32, precision=lax.Precision.HIGHEST)
        delta, nm, nv = _adamw(w_ref[...], g, m_ref[...], v_ref[...])
        g_ref[...] = g
        d_ref[...] = delta
        nm_ref[...] = nm
        nv_ref[...] = nv

    tile = pl.BlockSpec((tr, na), lambda i: (i, 0))
    return pl.pallas_call(
        body, name="adam_w_ada", grid=(d // tr,),
        in_specs=[pl.BlockSpec((tr, nb), lambda i: (i, 0)), pl.BlockSpec((nb, na), lambda i: (0, 0)), tile, tile, tile],
        out_specs=[tile] * 4, out_shape=[jax.ShapeDtypeStruct((d, na), F32)] * 4, compiler_params=_cparams(1),
    )(c_t, dm_loc, w, m, v)


def _small_final(me, gathered, params, d, w, cl):
    flat_params = [t for p in params for t in p]

    def total(ref):
        acc = ref[0]
        for b in range(1, N_DEV):
            acc = acc + ref[b]
        return acc

    def body(me_ref, loss_g, dfg_g, dg1_g, dsh_g, dsc_g, dgt_g, dbb_g, dlg_g, dlb_g, dwa_g, dwb_g, *rest):
        del me_ref
        prm = rest[:24]
        loss_ref = rest[24]
        outs = rest[25:]

        loss_ref[...] = jnp.sum(total(loss_g), axis=1, keepdims=True) * (0.5 / d)

        def update(idx, g, cols=None):
            w_ref, m_ref, v_ref = prm[3 * idx : 3 * idx + 3]
            g_ref, d_ref, nm_ref, nv_ref = outs[4 * idx : 4 * idx + 4]
            sl = (slice(None), slice(None)) if cols is None else (slice(None), cols)
            delta, nm, nv = _adamw(w_ref[sl], g, m_ref[sl], v_ref[sl])
            g_ref[sl] = g
            d_ref[sl] = delta
            nm_ref[sl] = nm
            nv_ref[sl] = nv

        update(0, total(dg1_g))
        update(1, total(dsh_g), slice(0, d))
        update(1, total(dsc_g), slice(d, 2 * d))
        update(1, total(dgt_g), slice(2 * d, 3 * d))
        update(2, total(dwa_g))
        update(3, total(dwb_g))
        update(4, total(dbb_g))
        update(5, total(dlg_g))
        update(6, total(dlb_g))
        update(7, total(dfg_g))

    def full(shape):
        nd = len(shape)
        return pl.BlockSpec(shape, lambda i, mr: (0,) * nd)

    in_specs = [full(g.shape) for g in gathered[:9]]
    in_specs += [pl.BlockSpec((N_DEV, CONV_A, cl), lambda i, mr: (0, 0, mr[0])), pl.BlockSpec((N_DEV, CONV_B, cl), lambda i, mr: (0, 0, mr[0]))]
    in_specs += [full(t.shape) for t in flat_params]
    out_shapes = [jax.ShapeDtypeStruct((1, 1), F32)]
    for p in params:
        out_shapes += [jax.ShapeDtypeStruct(p[0].shape, F32)] * 4
    return pl.pallas_call(
        body, name="small_final",
        grid_spec=pltpu.PrefetchScalarGridSpec(
            num_scalar_prefetch=1, grid=(1,), in_specs=in_specs, out_specs=[full(o.shape) for o in out_shapes]),
        out_shape=out_shapes, compiler_params=_cparams(1),
    )(me, *gathered, *flat_params)


def _silu_rows(c):
    def body(c_ref, o_ref):
        v = c_ref[...]
        o_ref[...] = v * _sigmoid(v)

    return pl.pallas_call(body, name="silu_c", out_shape=jax.ShapeDtypeStruct(c.shape, F32), compiler_params=_cparams())(c)


def kernel(x, c, norm_g, w_ada, b_ada, w_in, conv_a_w, conv_b_w, conv_b_b, ln_b_g, ln_b_b, w_out, final_g, loss_target, m_norm_g, m_w_ada, m_b_ada, m_w_in, m_conv_a_w, m_conv_b_w, m_conv_b_b, m_ln_b_g, m_ln_b_b, m_w_out, m_final_g, v_norm_g, v_w_ada, v_b_ada, v_w_in, v_conv_a_w, v_conv_b_w, v_conv_b_b, v_ln_b_g, v_ln_b_b, v_w_out, v_final_g):
    _, s, d = x.shape
    w = conv_b_b.shape[-1]
    cl = conv_a_w.shape[-1]
    na = w_ada.shape[-1]
    assert ln_b_g.shape[-1] == w and w_out.shape[1] * N_DEV == 2 * w and w_in.shape[-1] * N_DEV == 7 * w and cl * N_DEV == w
    ts = min(256, s)
    ts_row = min(128, s)
    cb = min(512, w)
    tm = min(512, s)
    tn = min(1024, d)

    px, py, pc = _position()
    me = 4 * px + 2 * py + pc
    me_arr = jnp.reshape(me, (1,)).astype(jnp.int32)
    chip_arr = jnp.reshape(2 * px + py, (1,)).astype(jnp.int32)

    x2d, tgt = x[0], loss_target[0]
    w_ada2, m_w_ada2, v_w_ada2 = w_ada[0], m_w_ada[0], v_w_ada[0]
    w_in2, m_w_in2, v_w_in2 = w_in[0], m_w_in[0], v_w_in[0]
    w_out2, m_w_out2, v_w_out2 = w_out[0], m_w_out[0], v_w_out[0]
    fg = final_g.reshape(1, d)

    c_act = _silu_rows(c)
    c_all, wa_all, wb_all = _all_gather_vmem([c_act, conv_a_w[0], conv_b_w[0]], "gather_small_in")
    c_all = c_all.reshape(N_DEV, d)
    wa = jnp.transpose(wa_all, (1, 0, 2)).reshape(CONV_A, w)
    wb = jnp.transpose(wb_all, (1, 0, 2)).reshape(CONV_B, w)
    b_loc = lax.dynamic_slice_in_dim(b_ada, me * na, na, axis=1)
    mod_loc = _mod_matmul(c_all, w_ada2, b_loc, min(512, d))
    (mod_all,) = _all_gather_vmem([mod_loc], "gather_mod")
    mod_mine = lax.dynamic_index_in_dim(mod_all, me, axis=1, keepdims=False).reshape(1, 3 * d)
    shift, scale, gate = mod_mine[:, :d], mod_mine[:, d : 2 * d], mod_mine[:, 2 * d :]

    h = _prep(x2d, norm_g, scale, shift, ts_row)
    proj, w_in_full, w_out_full = _proj_gather(h, w_in2.astype(BF16), w_out2.astype(BF16), _arrival_order(), tm)
    w_out_full = w_out_full.reshape(2 * w, d)
    y, cv = _mix_a_fwd(proj, wa, w, ts, cb)
    u2 = _mix_b_conv_fwd(proj, wb, conv_b_b, w, ts, cb)
    y = _mix_b_gate_fwd(u2, proj, ln_b_g, ln_b_b, y, ts)
    out = _mm_out(y, w_out_full, tm, tn)
    dx2, dout, loss_row, dgate_row, dfg_row = _post(x2d, out, tgt, gate, fg, ts_row)

    dy = _mm_dy(dout, w_out_full, tm, tn)
    comb_out = _mm_dw_exchange(y, dout, _dw_targets(), w_out2.shape[0], tn, True, "mm_dw_out")
    du2, dproj, dlg_row, dlb_row, dbb_row = _mix_b_gate_bwd(u2, proj, dy, ln_b_g, ln_b_b, 7 * w, ts)
    dproj, dwb_part, recv_out = _mix_b_conv_bwd(du2, proj, wb, dproj, comb_out, w, ts, cb)
    dproj, dwa_part = _mix_a_bwd(proj, cv, dy, wa, dproj, w, ts, cb)
    comb_in = _mm_dw_exchange(h, dproj, _dw_targets(), tm, w_in2.shape[1], False, "mm_dw_in")
    dh, recv_in = _mm_dh(dproj, w_in_full, comb_in, min(1024, s), tn)
    grad_x, dshift_row, dscale_row, dg1_row = _pre_bwd(x2d, dh, dx2, norm_g, scale, ts_row)
    g_w_in, d_w_in, nm_w_in, nv_w_in = _adam_sharded(w_in2, m_w_in2, v_w_in2, comb_in, recv_in, chip_arr, min(128, d), "adam_w_in")
    g_w_out, d_w_out, nm_w_out, nv_w_out = _adam_sharded(w_out2, m_w_out2, v_w_out2, comb_out, recv_out, chip_arr, min(128, w_out2.shape[0]), "adam_w_out")

    gathered = _all_gather_vmem(
        [loss_row, dfg_row, dg1_row, dshift_row, dscale_row, dgate_row, dbb_row, dlg_row, dlb_row, dwa_part, dwb_part], "gather_small_grads")
    dmod_all = jnp.concatenate([gathered[3].reshape(N_DEV, d), gathered[4].reshape(N_DEV, d), gathered[5].reshape(N_DEV, d)], axis=1)
    dm_loc = lax.dynamic_slice_in_dim(dmod_all, me * na, na, axis=1)
    g_w_ada, d_w_ada, nm_w_ada, nv_w_ada = _adam_w_ada(jnp.transpose(c_all), dm_loc, w_ada2, m_w_ada2, v_w_ada2, min(256, d))

    params = [
        (norm_g, m_norm_g, v_norm_g), (b_ada, m_b_ada, v_b_ada), (conv_a_w[0], m_conv_a_w[0], v_conv_a_w[0]),
        (conv_b_w[0], m_conv_b_w[0], v_conv_b_w[0]), (conv_b_b, m_conv_b_b, v_conv_b_b), (ln_b_g, m_ln_b_g, v_ln_b_g),
        (ln_b_b, m_ln_b_b, v_ln_b_b), (fg, m_final_g.reshape(1, d), v_final_g.reshape(1, d)),
    ]
    small = _small_final(me_arr, gathered, params, d, w, cl)
    loss = small[0].reshape(())
    sm = [small[1 + 4 * i : 5 + 4 * i] for i in range(8)]

    def pick(k):
        return [
            sm[0][k], (g_w_ada, d_w_ada, nm_w_ada, nv_w_ada)[k][None], sm[1][k], (g_w_in, d_w_in, nm_w_in, nv_w_in)[k][None],
            sm[2][k][None], sm[3][k][None], sm[4][k], sm[5][k], sm[6][k], (g_w_out, d_w_out, nm_w_out, nv_w_out)[k][None],
            sm[7][k].reshape(d),
        ]

    return (loss, grad_x[None], *pick(0), *pick(1), *pick(2), *pick(3))
```

```python
import functools

import jax
import jax.numpy as jnp
from jax import lax
from jax.experimental import pallas as pl
from jax.experimental.pallas import tpu as pltpu

F32 = jnp.float32
BF16 = jnp.bfloat16
N_DEV = 8
N_CHIP = 4
EPS = 1e-6
ADAM_LR = 0.001
ADAM_B1 = 0.9
ADAM_B2 = 0.999
ADAM_EPS = 1e-08
ADAM_WD = 0.01
ADAM_STEP = 10
CONV_A = 3
CONV_B = 31
HALO_A = 8
HALO_B = 32
LANES = 128
VMEM_LIMIT = 56 * 1024 * 1024
MESH = pl.DeviceIdType.MESH


def _cparams(n_grid_axes=0):
    if n_grid_axes:
        return pltpu.CompilerParams(dimension_semantics=("arbitrary",) * n_grid_axes, vmem_limit_bytes=VMEM_LIMIT)
    return pltpu.CompilerParams(vmem_limit_bytes=VMEM_LIMIT)


def _flip(v, bit):
    return 1 - v if bit else v


def _position():
    return lax.axis_index("x"), lax.axis_index("y"), lax.axis_index("c")


def _sigmoid(v):
    return jax.nn.sigmoid(v)


def _all_gather_vmem(arrs, name):
    n = len(arrs)

    def body(*refs):
        ins, outs = refs[:n], refs[n : 2 * n]
        send_sems, recv_sems = refs[2 * n :]
        x, y, c = _position()
        me = 4 * x + 2 * y + c
        for a in range(n):
            outs[a][me] = ins[a][...]
        sends = []
        for k in range(1, N_DEV):
            peer = (_flip(x, k & 4), _flip(y, k & 2), _flip(c, k & 1))
            for a in range(n):
                cp = pltpu.make_async_remote_copy(
                    src_ref=ins[a], dst_ref=outs[a].at[me], send_sem=send_sems.at[a, k - 1], recv_sem=recv_sems.at[a, k - 1],
                    device_id=peer, device_id_type=MESH)
                cp.start()
                sends.append(cp)
        for k in range(1, N_DEV):
            peer = (_flip(x, k & 4), _flip(y, k & 2), _flip(c, k & 1))
            src = 4 * peer[0] + 2 * peer[1] + peer[2]
            for a in range(n):
                pltpu.make_async_remote_copy(
                    src_ref=ins[a], dst_ref=outs[a].at[src], send_sem=send_sems.at[a, k - 1], recv_sem=recv_sems.at[a, k - 1],
                    device_id=peer, device_id_type=MESH).wait_recv()
        for cp in sends:
            cp.wait_send()

    vm = pl.BlockSpec(memory_space=pltpu.VMEM)
    return pl.pallas_call(
        body, name=name,
        out_shape=[jax.ShapeDtypeStruct((N_DEV,) + a.shape, a.dtype) for a in arrs],
        in_specs=[vm] * n, out_specs=[vm] * n,
        scratch_shapes=[pltpu.SemaphoreType.DMA((n, N_DEV - 1)), pltpu.SemaphoreType.DMA((n, N_DEV - 1))],
        compiler_params=_cparams(),
    )(*arrs)


NN = (((1,), (0,)), ((), ()))
NT = (((1,), (1,)), ((), ()))
TN = (((0,), (0,)), ((), ()))


def _n_chunk(n):
    return 256 if n % 256 == 0 else LANES


def _mm_body(dims, m_out, n_out):
    nc = _n_chunk(n_out)
    mh = min(512, m_out)

    def body(a_ref, b_ref, o_ref):
        for m0 in range(0, m_out, mh):
            for n0 in range(0, n_out, nc):
                b = b_ref[n0 : n0 + nc, :] if dims is NT else b_ref[:, n0 : n0 + nc]
                o_ref[m0 : m0 + mh, n0 : n0 + nc] = lax.dot_general(
                    a_ref[m0 : m0 + mh, :], b, dims, preferred_element_type=F32).astype(o_ref.dtype)

    return body


def _ring_chips():
    x, y, c = _position()
    first = (x + (1 - c) - 2 * x * (1 - c), y + c - 2 * y * c)
    second = (x + c - 2 * x * c, y + (1 - c) - 2 * y * (1 - c))
    return first, second, (1 - x, 1 - y)


def _arrival_order():
    x, y, c = _position()
    first, second, diag = _ring_chips()
    blocks = [(x, y, c), (x, y, 1 - c), (*first, c), (*second, 1 - c), (*second, c), (*first, 1 - c), (*diag, c), (*diag, 1 - c)]
    return jnp.stack([4 * b[0] + 2 * b[1] + b[2] for b in blocks]).astype(jnp.int32)


def _proj_gather(h, w_in_loc, w_out_loc, order, tm):
    s, d = h.shape
    ne = w_in_loc.shape[1]
    ni = s // tm
    nc = _n_chunk(ne)
    n_streams = 7

    def body(order_ref, h_ref, win_ref, wout_ref, proj_ref, fin_ref, fout_ref, bbuf, send_sems, recv_sems, local_sems, load_sems):
        ins, fulls = [win_ref, wout_ref], [fin_ref, fout_ref]
        jj, i = pl.program_id(0), pl.program_id(1)
        x, y, c = _position()
        me = (x, y, c)
        sibling = (x, y, 1 - c)
        first, second, diag = _ring_chips()
        sent = [(me, sibling), (me, (*first, c)), (me, (*second, c)), ((*first, c), (*second, c)),
                ((*first, c), sibling), ((*second, c), sibling), ((*diag, c), sibling)]
        received = [sibling, (*first, c), (*second, c), (*diag, c), (*second, 1 - c), (*first, 1 - c), (*diag, 1 - c)]

        def slot(p):
            return 4 * p[0] + 2 * p[1] + p[2]

        def copy(a, k, block, to):
            dst = fulls[a].at[slot(block)]
            return pltpu.make_async_remote_copy(
                src_ref=ins[a] if k < 3 else dst, dst_ref=dst, send_sem=send_sems.at[a, k], recv_sem=recv_sems.at[a, k],
                device_id=to, device_id_type=MESH)

        def send(a, k):
            return copy(a, k, *sent[k])

        def recv(a, k):
            return copy(a, k, received[k], me)

        def own(a):
            return pltpu.make_async_copy(ins[a], fulls[a].at[slot(me)], local_sems.at[a])

        def load(src, buf):
            return pltpu.make_async_copy(src, bbuf.at[buf], load_sems.at[buf])

        @pl.when((jj == 0) & (i == 0))
        def _():
            load(win_ref, 0).start()
            for a in range(2):
                own(a).start()
            send(0, 0).start()
            send(0, 1).start()
            send(1, 0).start()
            load(win_ref, 0).wait()

        stream_of = [None, 0, 1, 4, 2, 5, 3, 6]
        passes = {1: [(0, 4), (0, 2), (0, 3), (1, 1), (1, 2)], 2: [(0, 5)], 3: [(0, 6)]}
        for nxt in range(1, N_DEV):

            @pl.when((jj == nxt - 1) & (i == ni - 1))
            def _(nxt=nxt):
                k = stream_of[nxt]
                recv(0, k).wait_recv()
                for a, k2 in passes.get(k, []):
                    send(a, k2).start()
                if nxt == N_DEV - 2:
                    recv(1, 1).wait_recv()
                    send(1, 4).start()
                    send(1, 3).start()
                load(fin_ref.at[order_ref[nxt]], nxt % 2).start()

        @pl.when((jj > 0) & (i == 0))
        def _():
            load(fin_ref.at[0], jj % 2).wait()

        for n0 in range(0, ne, nc):
            proj_ref[:, n0 : n0 + nc] = lax.dot_general(
                h_ref[...], bbuf[jj % 2, :, n0 : n0 + nc], NN, preferred_element_type=F32).astype(BF16)

        @pl.when((jj == N_DEV - 1) & (i == ni - 1))
        def _():
            recv(1, 2).wait_recv()
            send(1, 5).start()
            recv(1, 3).wait_recv()
            send(1, 6).start()
            for k in (0, 4, 5, 6):
                recv(1, k).wait_recv()
            for a in range(2):
                for k in range(n_streams):
                    send(a, k).wait_send()
                own(a).wait()

    hbm = pl.BlockSpec(memory_space=pl.ANY)
    return pl.pallas_call(
        body, name="proj_gather",
        grid_spec=pltpu.PrefetchScalarGridSpec(
            num_scalar_prefetch=1, grid=(N_DEV, ni),
            in_specs=[pl.BlockSpec((tm, d), lambda jj, i, od: (i, 0)), hbm, hbm],
            out_specs=[pl.BlockSpec((tm, ne), lambda jj, i, od: (i, od[jj])), hbm, hbm],
            scratch_shapes=[pltpu.VMEM((2, d, ne), BF16), pltpu.SemaphoreType.DMA((2, 7)), pltpu.SemaphoreType.DMA((2, 7)),
                            pltpu.SemaphoreType.DMA((2,)), pltpu.SemaphoreType.DMA((2,))]),
        out_shape=[jax.ShapeDtypeStruct((s, N_DEV * ne), BF16), jax.ShapeDtypeStruct((N_DEV,) + w_in_loc.shape, BF16),
                   jax.ShapeDtypeStruct((N_DEV,) + w_out_loc.shape, BF16)],
        compiler_params=_cparams(2),
    )(order, h, w_in_loc, w_out_loc)


def _mm_out(y, w_out, tm, tn):
    s, m = y.shape
    _, d = w_out.shape
    return pl.pallas_call(
        _mm_body(NN, tm, tn), name="mm_out", grid=(d // tn, s // tm),
        in_specs=[pl.BlockSpec((tm, m), lambda j, i: (i, 0)), pl.BlockSpec((m, tn), lambda j, i: (0, j))],
        out_specs=pl.BlockSpec((tm, tn), lambda j, i: (i, j)),
        out_shape=jax.ShapeDtypeStruct((s, d), F32), compiler_params=_cparams(2),
    )(y, w_out)


def _mm_dy(dout, w_out, tm, tn):
    s, d = dout.shape
    m, _ = w_out.shape
    return pl.pallas_call(
        _mm_body(NT, tm, tn), name="mm_dy", grid=(m // tn, s // tm),
        in_specs=[pl.BlockSpec((tm, d), lambda j, i: (i, 0)), pl.BlockSpec((tn, d), lambda j, i: (j, 0))],
        out_specs=pl.BlockSpec((tm, tn), lambda j, i: (i, j)),
        out_shape=jax.ShapeDtypeStruct((s, m), BF16), compiler_params=_cparams(2),
    )(dout, w_out)


def _dw_targets():
    _, _, c = _position()
    return jnp.stack([2 * q + (1 - c) for q in range(N_CHIP)] + [2 * q + c for q in range(N_CHIP)]).astype(jnp.int32)


def _mm_dw_exchange(lhs, rhs, targets, bm, bn, target_on_lhs, name):
    s = lhs.shape[0]
    ni = (rhs.shape[1] // bn) if target_on_lhs else (lhs.shape[1] // bm)
    rows_b, cols_b = (bm, ni * bn) if target_on_lhs else (ni * bm, bn)
    nc = _n_chunk(bn)

    def body(tg_ref, a_ref, b_ref, comb_ref, psib_ref, rsib_ref, obuf, rbuf, osems, rsems, send_sems, recv_sems):
        del tg_ref
        jj, i = pl.program_id(0), pl.program_id(1)
        x, y, c = _position()
        slot = i % 2

        def tile(ref, q, t):
            if target_on_lhs:
                return ref.at[q, :, pl.ds(pl.multiple_of(t * bn, bn), bn)]
            return ref.at[q, pl.ds(pl.multiple_of(t * bm, bm), bm)]

        def put(sl, q):
            return pltpu.make_async_copy(obuf.at[sl], tile(psib_ref, q, i), osems.at[sl])

        def get(sl, q, t):
            return pltpu.make_async_copy(tile(rsib_ref, q, t), rbuf.at[sl], rsems.at[sl])

        def swap(q):
            return pltpu.make_async_remote_copy(
                src_ref=psib_ref.at[q], dst_ref=rsib_ref.at[q], send_sem=send_sems.at[q], recv_sem=recv_sems.at[q],
                device_id=(x, y, 1 - c), device_id_type=MESH)

        def product(n0):
            return lax.dot_general(a_ref[...], b_ref[:, n0 : n0 + nc], TN, preferred_element_type=F32)

        @pl.when(jj < N_CHIP)
        def _():
            @pl.when(i >= 2)
            def _():
                put(slot, 0).wait()

            for n0 in range(0, bn, nc):
                obuf[slot, :, n0 : n0 + nc] = product(n0).astype(BF16)
            put(slot, jj).start()

            @pl.when(i == ni - 1)
            def _():
                put(slot, 0).wait()
                if ni >= 2:
                    put(1 - slot, 0).wait()
                swap(jj).start()

        @pl.when(jj >= N_CHIP)
        def _():
            q = jj - N_CHIP

            @pl.when(i == 0)
            def _():
                swap(q).wait_recv()
                get(0, q, i).start()

            get(slot, 0, i).wait()

            @pl.when(i + 1 < ni)
            def _():
                get(1 - slot, q, i + 1).start()

            for n0 in range(0, bn, nc):
                comb_ref[:, n0 : n0 + nc] = (product(n0) + rbuf[slot, :, n0 : n0 + nc].astype(F32)).astype(BF16)

            @pl.when((jj == N_DEV - 1) & (i == ni - 1))
            def _():
                for qq in range(N_CHIP):
                    swap(qq).wait_send()

    if target_on_lhs:
        in_specs = [pl.BlockSpec((s, bm), lambda jj, i, tg: (0, tg[jj])), pl.BlockSpec((s, bn), lambda jj, i, tg: (0, i))]
        tile_index = lambda jj, i, tg: (jnp.maximum(jj - N_CHIP, 0), 0, jnp.where(jj < N_CHIP, 0, i))
    else:
        in_specs = [pl.BlockSpec((s, bm), lambda jj, i, tg: (0, i)), pl.BlockSpec((s, bn), lambda jj, i, tg: (0, tg[jj]))]
        tile_index = lambda jj, i, tg: (jnp.maximum(jj - N_CHIP, 0), jnp.where(jj < N_CHIP, 0, i), 0)
    hbm = pl.BlockSpec(memory_space=pl.ANY)
    comb, _, _ = pl.pallas_call(
        body, name=name,
        grid_spec=pltpu.PrefetchScalarGridSpec(
            num_scalar_prefetch=1, grid=(N_DEV, ni), in_specs=in_specs,
            out_specs=[pl.BlockSpec((None, bm, bn), tile_index), hbm, hbm],
            scratch_shapes=[pltpu.VMEM((2, bm, bn), BF16), pltpu.VMEM((2, bm, bn), BF16), pltpu.SemaphoreType.DMA((2,)),
                            pltpu.SemaphoreType.DMA((2,)), pltpu.SemaphoreType.DMA((N_CHIP,)), pltpu.SemaphoreType.DMA((N_CHIP,))]),
        out_shape=[jax.ShapeDtypeStruct((N_CHIP, rows_b, cols_b), BF16)] * 3,
        compiler_params=_cparams(2),
    )(targets, lhs, rhs)
    return comb


def _chip_copies(c_refs, r_refs, send_sems, recv_sems):
    x, y, c = _position()
    chips = [(1 - x, y), (x, 1 - y), (1 - x, 1 - y)]
    return [
        pltpu.make_async_remote_copy(
            src_ref=c_refs[a].at[2 * chip[0] + chip[1]], dst_ref=r_refs[a].at[j], send_sem=send_sems.at[a, j],
            recv_sem=recv_sems.at[a, j], device_id=(*chip, c), device_id_type=MESH)
        for a in range(len(c_refs)) for j, chip in enumerate(chips)]


def _mm_dh(dproj, w_full, comb, tm, tn):
    s = dproj.shape[0]
    _, d, ne = w_full.shape
    _, r_rows, r_cols = comb.shape
    kb = 2
    nk = N_DEV // kb
    nc = _n_chunk(tn)
    mh = min(512, tm)
    grid = (s // tm, d // tn, nk)
    n_steps = grid[0] * grid[1] * grid[2]
    n_relay = (3 * n_steps) // 8
    assert 0 < n_relay < n_steps - 1
    rc = min(512, r_rows)

    def body(a_ref, b_ref, c_ref, o_ref, recv_ref, relay_ref, sum_ref, va, vb, add_sems, send_sems, recv_sems):
        i, n, k = pl.program_id(0), pl.program_id(1), pl.program_id(2)
        step = (i * grid[1] + n) * nk + k
        _, _, c = _position()
        first, second, diag = _ring_chips()

        def chip_slot(p):
            return 2 * p[0] + p[1]

        def stream(j):
            src, dst, to = [(c_ref.at[chip_slot(diag)], relay_ref, first), (c_ref.at[chip_slot(first)], recv_ref.at[0], first),
                            (sum_ref, recv_ref.at[1], second)][j]
            return pltpu.make_async_remote_copy(
                src_ref=src, dst_ref=dst, send_sem=send_sems.at[j], recv_sem=recv_sems.at[j], device_id=(*to, c), device_id_type=MESH)

        @pl.when(step == 0)
        def _():
            stream(0).start()
            stream(1).start()

        @pl.when(step == n_relay)
        def _():
            stream(0).wait_recv()

            n_pieces = r_rows // rc

            def pieces(t, sl):
                rows = pl.ds(t * rc, rc)
                return (pltpu.make_async_copy(c_ref.at[chip_slot(second), rows], va.at[sl], add_sems.at[sl, 0]),
                        pltpu.make_async_copy(relay_ref.at[rows], vb.at[sl], add_sems.at[sl, 1]),
                        pltpu.make_async_copy(va.at[sl], sum_ref.at[rows], add_sems.at[sl, 2]))

            for cp in pieces(0, 0)[:2]:
                cp.start()
            for t in range(n_pieces):
                sl = t % 2
                for cp in pieces(t, sl)[:2]:
                    cp.wait()
                if t + 1 < n_pieces:
                    if t >= 1:
                        pieces(t - 1, 1 - sl)[2].wait()
                    for cp in pieces(t + 1, 1 - sl)[:2]:
                        cp.start()
                va[sl] = (va[sl].astype(F32) + vb[sl].astype(F32)).astype(BF16)
                pieces(t, sl)[2].start()
            for t in range(max(n_pieces - 2, 0), n_pieces):
                pieces(t, t % 2)[2].wait()
            stream(2).start()

        @pl.when(k == 0)
        def _():
            o_ref[...] = jnp.zeros_like(o_ref)

        for m0 in range(0, tm, mh):
            for n0 in range(0, tn, nc):
                r = None
                for jj in range(kb):
                    t = lax.dot_general(a_ref[m0 : m0 + mh, jj * ne : (jj + 1) * ne], b_ref[jj, n0 : n0 + nc, :], NT, preferred_element_type=F32)
                    r = t if r is None else r + t
                o_ref[m0 : m0 + mh, n0 : n0 + nc] += r

        @pl.when(step == n_steps - 1)
        def _():
            stream(1).wait_recv()
            stream(2).wait_recv()
            for j in range(3):
                stream(j).wait_send()

    hbm = pl.BlockSpec(memory_space=pl.ANY)
    block = jax.ShapeDtypeStruct((r_rows, r_cols), comb.dtype)
    dh, recv, _, _ = pl.pallas_call(
        body, name="mm_dh", grid=grid,
        in_specs=[pl.BlockSpec((tm, kb * ne), lambda i, n, k: (i, k)), pl.BlockSpec((kb, tn, ne), lambda i, n, k: (k, n, 0)), hbm],
        out_specs=[pl.BlockSpec((tm, tn), lambda i, n, k: (i, n)), hbm, hbm, hbm],
        out_shape=[jax.ShapeDtypeStruct((s, d), F32), jax.ShapeDtypeStruct((2, r_rows, r_cols), comb.dtype), block, block],
        scratch_shapes=[pltpu.VMEM((2, rc, r_cols), comb.dtype), pltpu.VMEM((2, rc, r_cols), comb.dtype), pltpu.SemaphoreType.DMA((2, 3)),
                        pltpu.SemaphoreType.DMA((3,)), pltpu.SemaphoreType.DMA((3,))],
        compiler_params=_cparams(3),
    )(dproj, w_full, comb)
    return dh, recv


def _mod_matmul(c_all, w_ada, b_loc, tk):
    nb, d = c_all.shape
    na = w_ada.shape[1]

    def body(c_ref, w_ref, b_ref, o_ref):
        k = pl.program_id(0)
        r = jnp.dot(c_ref[...], w_ref[...], preferred_element_type=F32, precision=lax.Precision.HIGHEST)

        @pl.when(k == 0)
        def _():
            o_ref[...] = r + b_ref[...]

        @pl.when(k > 0)
        def _():
            o_ref[...] += r

    return pl.pallas_call(
        body, name="mod_matmul", grid=(d // tk,),
        in_specs=[pl.BlockSpec((nb, tk), lambda k: (0, k)), pl.BlockSpec((tk, na), lambda k: (k, 0)), pl.BlockSpec((1, na), lambda k: (0, 0))],
        out_specs=pl.BlockSpec((nb, na), lambda k: (0, 0)),
        out_shape=jax.ShapeDtypeStruct((nb, na), F32), compiler_params=_cparams(1),
    )(c_all, w_ada, b_loc)


def _col_chunks(width):
    return [slice(c0, c0 + LANES) for c0 in range(0, width, LANES)]


def _row_sum(acc):
    return jnp.sum(acc, axis=1, keepdims=True)


SUBLANES = 8


ROW_BLOCK = 64
DW_BLOCK = 96


def _tap_sum(ext, cs, ts, taps, emit):
    for t0 in range(0, ts, ROW_BLOCK):
        rb = min(ROW_BLOCK, ts - t0)
        acc = None
        for r in range(SUBLANES):
            group = [(o, wv) for o, wv in taps if o % SUBLANES == r]
            if not group:
                continue
            n = rb if r == 0 else rb + SUBLANES
            v = None
            for o, wv in group:
                term = wv * ext[t0 + o - r : t0 + o - r + n, cs]
                v = term if v is None else v + term
            part = v if r == 0 else v[r : r + rb]
            acc = part if acc is None else acc + part
        emit(slice(t0, t0 + rb), acc)


class _SegmentStores:
    def __init__(self, stash, dst, sems, n_seg, step, n_steps, row0, cols):
        self.stash, self.dst, self.sems, self.n_seg = stash, dst, sems, n_seg
        self.step, self.n_steps, self.row0, self.cols = step, n_steps, row0, cols
        self.slot = step % 2

    def _copy(self, slot, g, row0, col0):
        ts, cb = self.stash.shape[2:]
        return pltpu.make_async_copy(self.stash.at[slot, g], self.dst.at[pl.ds(row0, ts), pl.ds(col0, cb)], self.sems.at[slot, g])

    def _wait(self, slot):
        for g in range(self.n_seg):
            self._copy(slot, g, 0, 0).wait()

    def begin(self):
        @pl.when(self.step >= 2)
        def _():
            self._wait(self.slot)

        return self.slot

    def finish(self):
        ts = self.stash.shape[2]
        for g in range(self.n_seg):
            self._copy(self.slot, g, pl.multiple_of(self.row0, ts), pl.multiple_of(self.cols[g], LANES)).start()

        @pl.when(self.step == self.n_steps - 1)
        def _():
            self._wait(self.slot)
            if self.n_steps >= 2:
                self._wait(1 - self.slot)


def _prep(x, g1, scale, shift, ts):
    s, d = x.shape
    chunks = _col_chunks(d)

    def body(x_ref, g_ref, sc_ref, sh_ref, h_ref):
        acc = jnp.zeros((ts, LANES), F32)
        for cs in chunks:
            v = x_ref[:, cs]
            acc = acc + v * v
        r = lax.rsqrt(_row_sum(acc) / d + EPS)
        for cs in chunks:
            n1 = (x_ref[:, cs] * r) * g_ref[:, cs]
            h_ref[:, cs] = (n1 * (1.0 + sc_ref[:, cs]) + sh_ref[:, cs]).astype(BF16)

    row = pl.BlockSpec((1, d), lambda i: (0, 0))
    return pl.pallas_call(
        body, name="prep", grid=(s // ts,),
        in_specs=[pl.BlockSpec((ts, d), lambda i: (i, 0)), row, row, row],
        out_specs=pl.BlockSpec((ts, d), lambda i: (i, 0)),
        out_shape=jax.ShapeDtypeStruct((s, d), BF16), compiler_params=_cparams(1),
    )(x, g1, scale, shift)


def _mix_a_fwd(proj, wa, w, ts, cb):
    s = proj.shape[0]
    nb = w // cb
    chunks = _col_chunks(cb)

    def body(ab, ac, ax, az, wa_ref, y_ref, cv_ref, ext):
        @pl.when(pl.program_id(1) == 0)
        def _():
            ext[0:HALO_A, :] = jnp.zeros((HALO_A, cb), F32)

        for cs in chunks:
            ext[HALO_A : HALO_A + ts, cs] = ac[:, cs].astype(F32) * ax[:, cs].astype(F32)
        for cs in chunks:

            def emit(rows, cv, cs=cs):
                z = az[rows, cs].astype(F32)
                y_ref[rows, cs] = (ab[rows, cs].astype(F32) * cv * (z * _sigmoid(z))).astype(BF16)
                cv_ref[rows, cs] = cv.astype(BF16)

            _tap_sum(ext, cs, ts, [(HALO_A - (CONV_A - 1) + k, wa_ref[k : k + 1, cs]) for k in range(CONV_A)], emit)
        ext[0:HALO_A, :] = ext[ts : ts + HALO_A, :]

    def seg(q):
        return pl.BlockSpec((ts, cb), lambda c, i: (i, q * nb + c))

    return pl.pallas_call(
        body, name="mix_a_fwd", grid=(nb, s // ts),
        in_specs=[seg(0), seg(1), seg(2), seg(3), pl.BlockSpec((CONV_A, cb), lambda c, i: (0, c))],
        out_specs=[pl.BlockSpec((ts, cb), lambda c, i: (i, c)), pl.BlockSpec((ts, cb), lambda c, i: (i, c))],
        out_shape=[jax.ShapeDtypeStruct((s, 2 * w), BF16), jax.ShapeDtypeStruct((s, w), BF16)],
        scratch_shapes=[pltpu.VMEM((HALO_A + ts, cb), F32)], compiler_params=_cparams(2),
    )(proj, proj, proj, proj, wa)


def _mix_b_conv_fwd(proj, wb, bb, w, ts, cb):
    s = proj.shape[0]
    nb = w // cb
    chunks = _col_chunks(cb)

    def body(bv, bg, wb_ref, bb_ref, u2_ref, ext):
        @pl.when(pl.program_id(1) == 0)
        def _():
            ext[0:HALO_B, :] = jnp.zeros((HALO_B, cb), F32)

        for cs in chunks:
            ext[HALO_B : HALO_B + ts, cs] = bv[:, cs].astype(F32) * _sigmoid(bg[:, cs].astype(F32))
        for cs in chunks:

            def emit(rows, acc, cs=cs):
                u2_ref[rows, cs] = acc + bb_ref[:, cs]

            _tap_sum(ext, cs, ts, [(HALO_B - (CONV_B - 1) + k, wb_ref[k : k + 1, cs]) for k in range(CONV_B)], emit)
        ext[0:HALO_B, :] = ext[ts : ts + HALO_B, :]

    return pl.pallas_call(
        body, name="mix_b_conv_fwd", grid=(nb, s // ts),
        in_specs=[pl.BlockSpec((ts, cb), lambda c, i: (i, 4 * nb + c)), pl.BlockSpec((ts, cb), lambda c, i: (i, 5 * nb + c)),
                  pl.BlockSpec((CONV_B, cb), lambda c, i: (0, c)), pl.BlockSpec((1, cb), lambda c, i: (0, c))],
        out_specs=pl.BlockSpec((ts, cb), lambda c, i: (i, c)),
        out_shape=jax.ShapeDtypeStruct((s, w), F32),
        scratch_shapes=[pltpu.VMEM((HALO_B + ts, cb), F32)], compiler_params=_cparams(2),
    )(proj, proj, wb, bb)


def _ln_stats(u2_ref, ts, w, chunks):
    acc = jnp.zeros((ts, LANES), F32)
    for cs in chunks:
        acc = acc + u2_ref[:, cs]
    mu = _row_sum(acc) / w
    acc = jnp.zeros((ts, LANES), F32)
    for cs in chunks:
        xc = u2_ref[:, cs] - mu
        acc = acc + xc * xc
    return mu, lax.rsqrt(_row_sum(acc) / w + EPS)


def _mix_b_gate_fwd(u2, proj, lg, lb, y, ts):
    s, w = u2.shape
    chunks = _col_chunks(w)

    def body(u2_ref, bz, lg_ref, lb_ref, y_in, y_ref):
        del y_in
        mu, rstd = _ln_stats(u2_ref, ts, w, chunks)
        for cs in chunks:
            ln = (u2_ref[:, cs] - mu) * rstd * lg_ref[:, cs] + lb_ref[:, cs]
            z = bz[:, cs].astype(F32)
            y_ref[:, cs] = ((ln * _sigmoid(ln)) * (z * _sigmoid(z))).astype(BF16)

    row = pl.BlockSpec((1, w), lambda i: (0, 0))
    return pl.pallas_call(
        body, name="mix_b_gate_fwd", grid=(s // ts,),
        in_specs=[pl.BlockSpec((ts, w), lambda i: (i, 0)), pl.BlockSpec((ts, w), lambda i: (i, 6)), row, row,
                  pl.BlockSpec(memory_space=pl.ANY)],
        out_specs=pl.BlockSpec((ts, w), lambda i: (i, 1)),
        out_shape=jax.ShapeDtypeStruct(y.shape, y.dtype), input_output_aliases={4: 0}, compiler_params=_cparams(1),
    )(u2, proj, lg, lb, y)


def _post(x, out, tgt, gate, fg, ts):
    s, d = x.shape
    chunks = _col_chunks(d)

    def body(x_ref, o_ref, t_ref, gt_ref, fg_ref, dx2_ref, dout_ref, loss_ref, dgate_ref, dfg_ref, keep):
        @pl.when(pl.program_id(0) == 0)
        def _():
            loss_ref[...] = jnp.zeros_like(loss_ref)
            dgate_ref[...] = jnp.zeros_like(dgate_ref)
            dfg_ref[...] = jnp.zeros_like(dfg_ref)

        acc = jnp.zeros((ts, LANES), F32)
        for cs in chunks:
            x2 = x_ref[:, cs] + gt_ref[:, cs] * o_ref[:, cs]
            keep[:, cs] = x2
            acc = acc + x2 * x2
        r2 = lax.rsqrt(_row_sum(acc) / d + EPS)
        acc = jnp.zeros((ts, LANES), F32)
        for cs in chunks:
            n2 = keep[:, cs] * r2
            keep[:, cs] = n2
            diff = n2 * fg_ref[:, cs] - t_ref[:, cs]
            loss_ref[:, cs] += jnp.sum(diff * diff, axis=0, keepdims=True)
            dyf = diff / d
            dfg_ref[:, cs] += jnp.sum(dyf * n2, axis=0, keepdims=True)
            dn = dyf * fg_ref[:, cs]
            dx2_ref[:, cs] = dn
            acc = acc + dn * n2
        mdot = _row_sum(acc) / d
        for cs in chunks:
            dx2 = r2 * (dx2_ref[:, cs] - keep[:, cs] * mdot)
            dx2_ref[:, cs] = dx2
            dgate_ref[:, cs] += jnp.sum(dx2 * o_ref[:, cs], axis=0, keepdims=True)
            dout_ref[:, cs] = (dx2 * gt_ref[:, cs]).astype(BF16)

    tile = pl.BlockSpec((ts, d), lambda i: (i, 0))
    row = pl.BlockSpec((1, d), lambda i: (0, 0))
    return pl.pallas_call(
        body, name="post", grid=(s // ts,),
        in_specs=[tile, tile, tile, row, row], out_specs=[tile, tile, row, row, row],
        out_shape=[jax.ShapeDtypeStruct((s, d), F32), jax.ShapeDtypeStruct((s, d), BF16)] + [jax.ShapeDtypeStruct((1, d), F32)] * 3,
        scratch_shapes=[pltpu.VMEM((ts, d), F32)], compiler_params=_cparams(1),
    )(x, out, tgt, gate, fg)


def _mix_b_gate_bwd(u2, proj, dy, lg, lb, n_cols, ts):
    s, w = u2.shape
    chunks = _col_chunks(w)

    def body(u2_ref, bz, dyb, lg_ref, lb_ref, du2_ref, dbz_ref, dlg_ref, dlb_ref, dbb_ref):
        @pl.when(pl.program_id(0) == 0)
        def _():
            dlg_ref[...] = jnp.zeros_like(dlg_ref)
            dlb_ref[...] = jnp.zeros_like(dlb_ref)
            dbb_ref[...] = jnp.zeros_like(dbb_ref)

        mu, rstd = _ln_stats(u2_ref, ts, w, chunks)
        acc1 = jnp.zeros((ts, LANES), F32)
        acc2 = jnp.zeros((ts, LANES), F32)
        for cs in chunks:
            xh = (u2_ref[:, cs] - mu) * rstd
            ln = xh * lg_ref[:, cs] + lb_ref[:, cs]
            sl = _sigmoid(ln)
            z = bz[:, cs].astype(F32)
            sz = _sigmoid(z)
            g = dyb[:, cs].astype(F32)
            dbz_ref[:, cs] = (g * (ln * sl) * (sz * (1.0 + z * (1.0 - sz)))).astype(BF16)
            dln = g * (z * sz) * (sl * (1.0 + ln * (1.0 - sl)))
            dlg_ref[:, cs] += jnp.sum(dln * xh, axis=0, keepdims=True)
            dlb_ref[:, cs] += jnp.sum(dln, axis=0, keepdims=True)
            dxh = dln * lg_ref[:, cs]
            du2_ref[:, cs] = dxh
            acc1 = acc1 + dxh
            acc2 = acc2 + dxh * xh
        m1 = _row_sum(acc1) / w
        m2 = _row_sum(acc2) / w
        for cs in chunks:
            xh = (u2_ref[:, cs] - mu) * rstd
            du2 = rstd * (du2_ref[:, cs] - m1 - xh * m2)
            du2_ref[:, cs] = du2
            dbb_ref[:, cs] += jnp.sum(du2, axis=0, keepdims=True)

    tile = pl.BlockSpec((ts, w), lambda i: (i, 0))
    row = pl.BlockSpec((1, w), lambda i: (0, 0))
    return pl.pallas_call(
        body, name="mix_b_gate_bwd", grid=(s // ts,),
        in_specs=[tile, pl.BlockSpec((ts, w), lambda i: (i, 6)), pl.BlockSpec((ts, w), lambda i: (i, 1)), row, row],
        out_specs=[tile, pl.BlockSpec((ts, w), lambda i: (i, 6)), row, row, row],
        out_shape=[jax.ShapeDtypeStruct((s, w), F32), jax.ShapeDtypeStruct((s, n_cols), BF16)] + [jax.ShapeDtypeStruct((1, w), F32)] * 3,
        compiler_params=_cparams(1),
    )(u2, proj, dy, lg, lb)


def _mix_b_conv_bwd(du2, proj, wb, dproj, comb, w, ts, cb):
    s = proj.shape[0]
    nb = w // cb
    ns = s // ts
    chunks = _col_chunks(cb)

    u_rows = ts + 2 * HALO_B + SUBLANES
    win = ts + HALO_B
    assert win % DW_BLOCK == 0

    n_seg = 2

    def body(du2_ref, bv, bg, wb_ref, dp_in, comb_ref, dp_ref, dwb_ref, recv_ref, ext, stash, uext, sems, send_sems, recv_sems):
        del dp_in
        c, i = pl.program_id(0), pl.program_id(1)
        stores = _SegmentStores(stash, dp_ref, sems, n_seg, c * ns + i, nb * ns, (ns - 1 - i) * ts, [((4 + g) * nb + c) * cb for g in range(n_seg)])
        slot = stores.begin()

        def copies():
            return _chip_copies([comb_ref], [recv_ref], send_sems, recv_sems)

        @pl.when((c == 0) & (i == 0))
        def _():
            for cp in copies():
                cp.start()

        @pl.when(i == 0)
        def _():
            ext[ts : ts + HALO_B, :] = jnp.zeros((HALO_B, cb), F32)
            uext[0:HALO_B, :] = jnp.zeros((HALO_B, cb), F32)
            uext[HALO_B + ts : u_rows, :] = jnp.zeros((u_rows - HALO_B - ts, cb), F32)
            dwb_ref[...] = jnp.zeros_like(dwb_ref)

        ext[0:ts, :] = du2_ref[...]
        for cs in chunks:
            uext[HALO_B : HALO_B + ts, cs] = bv[:, cs].astype(F32) * _sigmoid(bg[:, cs].astype(F32))

            def emit(rows, du, cs=cs):
                v = bv[rows, cs].astype(F32)
                sg = _sigmoid(bg[rows, cs].astype(F32))
                stash[slot, 0, rows, cs] = (du * sg).astype(BF16)
                stash[slot, 1, rows, cs] = (du * v * (sg * (1.0 - sg))).astype(BF16)

            _tap_sum(ext, cs, ts, [(CONV_B - 1 - k, wb_ref[k : k + 1, cs]) for k in range(CONV_B)], emit)
            for r in range(SUBLANES):
                ks = [k for k in range(CONV_B) if (CONV_B - 1 - k) % SUBLANES == r]
                accs = [jnp.zeros((SUBLANES, LANES), F32) for _ in ks]
                for t0 in range(0, win, DW_BLOCK):
                    e = ext[t0 : t0 + DW_BLOCK, cs]
                    ush = uext[SUBLANES - r + t0 : SUBLANES - r + t0 + DW_BLOCK + HALO_B - SUBLANES, cs]
                    for n, k in enumerate(ks):
                        b0 = HALO_B - SUBLANES - (CONV_B - 1 - k - r)
                        prod = e * ush[b0 : b0 + DW_BLOCK]
                        for v0 in range(0, DW_BLOCK, SUBLANES):
                            accs[n] = accs[n] + prod[v0 : v0 + SUBLANES]
                for n, k in enumerate(ks):
                    dwb_ref[k : k + 1, cs] += jnp.sum(accs[n], axis=0, keepdims=True)
        ext[ts : ts + HALO_B, :] = ext[0:HALO_B, :]
        stores.finish()

        @pl.when((c == nb - 1) & (i == ns - 1))
        def _():
            for cp in copies():
                cp.wait_recv()
            for cp in copies():
                cp.wait_send()

    def rev(col):
        return lambda c, i: (ns - 1 - i, col(c))

    hbm = pl.BlockSpec(memory_space=pl.ANY)
    return pl.pallas_call(
        body, name="mix_b_conv_bwd", grid=(nb, ns),
        in_specs=[pl.BlockSpec((ts, cb), rev(lambda c: c)), pl.BlockSpec((ts, cb), rev(lambda c: 4 * nb + c)),
                  pl.BlockSpec((ts, cb), rev(lambda c: 5 * nb + c)), pl.BlockSpec((CONV_B, cb), lambda c, i: (0, c)), hbm, hbm],
        out_specs=[hbm, pl.BlockSpec((CONV_B, cb), lambda c, i: (0, c)), hbm],
        out_shape=[jax.ShapeDtypeStruct(dproj.shape, dproj.dtype), jax.ShapeDtypeStruct((CONV_B, w), F32),
                   jax.ShapeDtypeStruct((3,) + comb.shape[1:], comb.dtype)],
        input_output_aliases={4: 0},
        scratch_shapes=[pltpu.VMEM((ts + HALO_B, cb), F32), pltpu.VMEM((2, n_seg, ts, cb), BF16), pltpu.VMEM((u_rows, cb), F32),
                        pltpu.SemaphoreType.DMA((2, n_seg)), pltpu.SemaphoreType.DMA((1, 3)), pltpu.SemaphoreType.DMA((1, 3))],
        compiler_params=_cparams(2),
    )(du2, proj, proj, wb, dproj, comb)


def _mix_a_bwd(proj, cv, dy, wa, dproj, w, ts, cb):
    s = proj.shape[0]
    nb = w // cb
    ns = s // ts
    chunks = _col_chunks(cb)

    n_seg = 4

    def body(ab, ac, ax, az, cv_ref, dya, wa_ref, dp_in, dp_ref, dwa_ref, ext, stash, sems):
        del dp_in
        c, i = pl.program_id(0), pl.program_id(1)
        stores = _SegmentStores(stash, dp_ref, sems, n_seg, c * ns + i, nb * ns, (ns - 1 - i) * ts, [(g * nb + c) * cb for g in range(n_seg)])
        slot = stores.begin()

        @pl.when(i == 0)
        def _():
            ext[ts : ts + HALO_A, :] = jnp.zeros((HALO_A, cb), F32)
            dwa_ref[...] = jnp.zeros_like(dwa_ref)

        for cs in chunks:
            b = ab[:, cs].astype(F32)
            z = az[:, cs].astype(F32)
            sz = _sigmoid(z)
            g = dya[:, cs].astype(F32)
            conv = cv_ref[:, cs].astype(F32)
            stash[slot, 0, :, cs] = (g * conv * (z * sz)).astype(BF16)
            stash[slot, 3, :, cs] = (g * b * conv * (sz * (1.0 + z * (1.0 - sz)))).astype(BF16)
            ext[0:ts, cs] = g * b * (z * sz)
        for cs in chunks:

            def emit(rows, dca, cs=cs):
                stash[slot, 1, rows, cs] = (dca * ax[rows, cs].astype(F32)).astype(BF16)
                stash[slot, 2, rows, cs] = (dca * ac[rows, cs].astype(F32)).astype(BF16)

            _tap_sum(ext, cs, ts, [(CONV_A - 1 - k, wa_ref[k : k + 1, cs]) for k in range(CONV_A)], emit)
            ca = ac[:, cs].astype(F32) * ax[:, cs].astype(F32)
            for k in range(CONV_A):
                o = CONV_A - 1 - k
                dwa_ref[k : k + 1, cs] += jnp.sum(ca * ext[o : o + ts, cs], axis=0, keepdims=True)
        ext[ts : ts + HALO_A, :] = ext[0:HALO_A, :]
        stores.finish()

    def rev(col):
        return lambda c, i: (ns - 1 - i, col(c))

    def seg(sq):
        return pl.BlockSpec((ts, cb), rev(lambda c: sq * nb + c))

    hbm = pl.BlockSpec(memory_space=pl.ANY)
    return pl.pallas_call(
        body, name="mix_a_bwd", grid=(nb, ns),
        in_specs=[seg(0), seg(1), seg(2), seg(3), pl.BlockSpec((ts, cb), rev(lambda c: c)), pl.BlockSpec((ts, cb), rev(lambda c: c)),
                  pl.BlockSpec((CONV_A, cb), lambda c, i: (0, c)), hbm],
        out_specs=[hbm, pl.BlockSpec((CONV_A, cb), lambda c, i: (0, c))],
        out_shape=[jax.ShapeDtypeStruct(dproj.shape, dproj.dtype), jax.ShapeDtypeStruct((CONV_A, w), F32)],
        input_output_aliases={7: 0},
        scratch_shapes=[pltpu.VMEM((ts + HALO_A, cb), F32), pltpu.VMEM((2, n_seg, ts, cb), BF16), pltpu.SemaphoreType.DMA((2, n_seg))],
        compiler_params=_cparams(2),
    )(proj, proj, proj, proj, cv, dy, wa, dproj)


def _pre_bwd(x, dh, dx2, g1, scale, ts):
    s, d = x.shape
    chunks = _col_chunks(d)

    def body(x_ref, dh_ref, dx2_ref, g_ref, sc_ref, gx_ref, dsh_ref, dsc_ref, dg_ref, keep):
        @pl.when(pl.program_id(0) == 0)
        def _():
            dsh_ref[...] = jnp.zeros_like(dsh_ref)
            dsc_ref[...] = jnp.zeros_like(dsc_ref)
            dg_ref[...] = jnp.zeros_like(dg_ref)

        acc = jnp.zeros((ts, LANES), F32)
        for cs in chunks:
            v = x_ref[:, cs]
            acc = acc + v * v
        r1 = lax.rsqrt(_row_sum(acc) / d + EPS)
        acc = jnp.zeros((ts, LANES), F32)
        for cs in chunks:
            xn = x_ref[:, cs] * r1
            keep[:, cs] = xn
            g = dh_ref[:, cs]
            dsh_ref[:, cs] += jnp.sum(g, axis=0, keepdims=True)
            dsc_ref[:, cs] += jnp.sum(g * (xn * g_ref[:, cs]), axis=0, keepdims=True)
            dn1 = g * (1.0 + sc_ref[:, cs])
            dg_ref[:, cs] += jnp.sum(dn1 * xn, axis=0, keepdims=True)
            dxn = dn1 * g_ref[:, cs]
            gx_ref[:, cs] = dxn
            acc = acc + dxn * xn
        mdot = _row_sum(acc) / d
        for cs in chunks:
            gx_ref[:, cs] = dx2_ref[:, cs] + r1 * (gx_ref[:, cs] - keep[:, cs] * mdot)

    tile = pl.BlockSpec((ts, d), lambda i: (i, 0))
    row = pl.BlockSpec((1, d), lambda i: (0, 0))
    return pl.pallas_call(
        body, name="pre_bwd", grid=(s // ts,),
        in_specs=[tile, tile, tile, row, row], out_specs=[tile, row, row, row],
        out_shape=[jax.ShapeDtypeStruct((s, d), F32)] + [jax.ShapeDtypeStruct((1, d), F32)] * 3,
        scratch_shapes=[pltpu.VMEM((ts, d), F32)], compiler_params=_cparams(1),
    )(x, dh, dx2, g1, scale)


def _adamw(w, g, m, v):
    m = ADAM_B1 * m + (1.0 - ADAM_B1) * g
    v = ADAM_B2 * v + (1.0 - ADAM_B2) * (g * g)
    m_hat = m / (1.0 - ADAM_B1**ADAM_STEP)
    v_hat = v / (1.0 - ADAM_B2**ADAM_STEP)
    delta = -ADAM_LR * (m_hat / (jnp.sqrt(v_hat) + ADAM_EPS) + ADAM_WD * w)
    return delta, m, v


def _adam_sharded(w, m, v, comb, recv, chip, tr, name):
    r, n = w.shape
    nr = recv.shape[0]

    def body(chip_ref, w_ref, m_ref, v_ref, c_ref, r_ref, g_ref, d_ref, nm_ref, nv_ref):
        del chip_ref
        g = c_ref[...].astype(F32)
        for j in range(nr):
            g = g + r_ref[j].astype(F32)
        delta, nm, nv = _adamw(w_ref[...], g, m_ref[...], v_ref[...])
        g_ref[...] = g
        d_ref[...] = delta
        nm_ref[...] = nm
        nv_ref[...] = nv

    tile = pl.BlockSpec((tr, n), lambda i, ch: (i, 0))
    return pl.pallas_call(
        body, name=name,
        grid_spec=pltpu.PrefetchScalarGridSpec(
            num_scalar_prefetch=1, grid=(r // tr,),
            in_specs=[tile, tile, tile, pl.BlockSpec((None, tr, n), lambda i, ch: (ch[0], i, 0)), pl.BlockSpec((nr, tr, n), lambda i, ch: (0, i, 0))],
            out_specs=[tile] * 4),
        out_shape=[jax.ShapeDtypeStruct((r, n), F32)] * 4, compiler_params=_cparams(1),
    )(chip, w, m, v, comb, recv)


def _adam_w_ada(c_t, dm_loc, w, m, v, tr):
    d, nb = c_t.shape
    na = w.shape[1]

    def body(c_ref, dm_ref, w_ref, m_ref, v_ref, g_ref, d_ref, nm_ref, nv_ref):
        g = jnp.dot(c_ref[...], dm_ref[...], preferred_element_type=F32, precision=lax.Precision.HIGHEST)
        delta, nm, nv = _adamw(w_ref[...], g, m_ref[...], v_ref[...])
        g_ref[...] = g
        d_ref[...] = delta
        nm_ref[...] = nm
        nv_ref[...] = nv

    tile = pl.BlockSpec((tr, na), lambda i: (i, 0))
    return pl.pallas_call(
        body, name="adam_w_ada", grid=(d // tr,),
        in_specs=[pl.BlockSpec((tr, nb), lambda i: (i, 0)), pl.BlockSpec((nb, na), lambda i: (0, 0)), tile, tile, tile],
        out_specs=[tile] * 4, out_shape=[jax.ShapeDtypeStruct((d, na), F32)] * 4, compiler_params=_cparams(1),
    )(c_t, dm_loc, w, m, v)


def _small_final(me, gathered, params, d, w, cl):
    flat_params = [t for p in params for t in p]

    def total(ref):
        acc = ref[0]
        for b in range(1, N_DEV):
            acc = acc + ref[b]
        return acc

    def body(me_ref, loss_g, dfg_g, dg1_g, dsh_g, dsc_g, dgt_g, dbb_g, dlg_g, dlb_g, dwa_g, dwb_g, *rest):
        del me_ref
        prm = rest[:24]
        loss_ref = rest[24]
        outs = rest[25:]

        loss_ref[...] = jnp.sum(total(loss_g), axis=1, keepdims=True) * (0.5 / d)

        def update(idx, g, cols=None):
            w_ref, m_ref, v_ref = prm[3 * idx : 3 * idx + 3]
            g_ref, d_ref, nm_ref, nv_ref = outs[4 * idx : 4 * idx + 4]
            sl = (slice(None), slice(None)) if cols is None else (slice(None), cols)
            delta, nm, nv = _adamw(w_ref[sl], g, m_ref[sl], v_ref[sl])
            g_ref[sl] = g
            d_ref[sl] = delta
            nm_ref[sl] = nm
            nv_ref[sl] = nv

        update(0, total(dg1_g))
        update(1, total(dsh_g), slice(0, d))
        update(1, total(dsc_g), slice(d, 2 * d))
        update(1, total(dgt_g), slice(2 * d, 3 * d))
        update(2, total(dwa_g))
        update(3, total(dwb_g))
        update(4, total(dbb_g))
        update(5, total(dlg_g))
        update(6, total(dlb_g))
        update(7, total(dfg_g))

    def full(shape):
        nd = len(shape)
        return pl.BlockSpec(shape, lambda i, mr: (0,) * nd)

    in_specs = [full(g.shape) for g in gathered[:9]]
    in_specs += [pl.BlockSpec((N_DEV, CONV_A, cl), lambda i, mr: (0, 0, mr[0])), pl.BlockSpec((N_DEV, CONV_B, cl), lambda i, mr: (0, 0, mr[0]))]
    in_specs += [full(t.shape) for t in flat_params]
    out_shapes = [jax.ShapeDtypeStruct((1, 1), F32)]
    for p in params:
        out_shapes += [jax.ShapeDtypeStruct(p[0].shape, F32)] * 4
    return pl.pallas_call(
        body, name="small_final",
        grid_spec=pltpu.PrefetchScalarGridSpec(
            num_scalar_prefetch=1, grid=(1,), in_specs=in_specs, out_specs=[full(o.shape) for o in out_shapes]),
        out_shape=out_shapes, compiler_params=_cparams(1),
    )(me, *gathered, *flat_params)


def _silu_rows(c):
    def body(c_ref, o_ref):
        v = c_ref[...]
        o_ref[...] = v * _sigmoid(v)

    return pl.pallas_call(body, name="silu_c", out_shape=jax.ShapeDtypeStruct(c.shape, F32), compiler_params=_cparams())(c)


def kernel(x, c, norm_g, w_ada, b_ada, w_in, conv_a_w, conv_b_w, conv_b_b, ln_b_g, ln_b_b, w_out, final_g, loss_target, m_norm_g, m_w_ada, m_b_ada, m_w_in, m_conv_a_w, m_conv_b_w, m_conv_b_b, m_ln_b_g, m_ln_b_b, m_w_out, m_final_g, v_norm_g, v_w_ada, v_b_ada, v_w_in, v_conv_a_w, v_conv_b_w, v_conv_b_b, v_ln_b_g, v_ln_b_b, v_w_out, v_final_g):
    _, s, d = x.shape
    w = conv_b_b.shape[-1]
    cl = conv_a_w.shape[-1]
    na = w_ada.shape[-1]
    assert ln_b_g.shape[-1] == w and w_out.shape[1] * N_DEV == 2 * w and w_in.shape[-1] * N_DEV == 7 * w and cl * N_DEV == w
    ts = min(256, s)
    ts_a = min(512, s)
    ts_row = min(128, s)
    cb = min(512, w)
    tm = min(512, s)
    tm_big = min(1024, s)
    tn = min(1024, d)

    px, py, pc = _position()
    me = 4 * px + 2 * py + pc
    me_arr = jnp.reshape(me, (1,)).astype(jnp.int32)
    chip_arr = jnp.reshape(2 * px + py, (1,)).astype(jnp.int32)

    x2d, tgt = x[0], loss_target[0]
    w_ada2, m_w_ada2, v_w_ada2 = w_ada[0], m_w_ada[0], v_w_ada[0]
    w_in2, m_w_in2, v_w_in2 = w_in[0], m_w_in[0], v_w_in[0]
    w_out2, m_w_out2, v_w_out2 = w_out[0], m_w_out[0], v_w_out[0]
    fg = final_g.reshape(1, d)

    c_act = _silu_rows(c)
    c_all, wa_all, wb_all = _all_gather_vmem([c_act, conv_a_w[0], conv_b_w[0]], "gather_small_in")
    c_all = c_all.reshape(N_DEV, d)
    wa = jnp.transpose(wa_all, (1, 0, 2)).reshape(CONV_A, w)
    wb = jnp.transpose(wb_all, (1, 0, 2)).reshape(CONV_B, w)
    b_loc = lax.dynamic_slice_in_dim(b_ada, me * na, na, axis=1)
    mod_loc = _mod_matmul(c_all, w_ada2, b_loc, min(512, d))
    (mod_all,) = _all_gather_vmem([mod_loc], "gather_mod")
    mod_mine = lax.dynamic_index_in_dim(mod_all, me, axis=1, keepdims=False).reshape(1, 3 * d)
    shift, scale, gate = mod_mine[:, :d], mod_mine[:, d : 2 * d], mod_mine[:, 2 * d :]

    h = _prep(x2d, norm_g, scale, shift, ts_row)
    proj, w_in_full, w_out_full = _proj_gather(h, w_in2.astype(BF16), w_out2.astype(BF16), _arrival_order(), tm)
    w_out_full = w_out_full.reshape(2 * w, d)
    y, cv = _mix_a_fwd(proj, wa, w, ts_a, cb)
    u2 = _mix_b_conv_fwd(proj, wb, conv_b_b, w, ts, cb)
    y = _mix_b_gate_fwd(u2, proj, ln_b_g, ln_b_b, y, ts)
    out = _mm_out(y, w_out_full, tm_big, tn)
    dx2, dout, loss_row, dgate_row, dfg_row = _post(x2d, out, tgt, gate, fg, ts_row)

    dy = _mm_dy(dout, w_out_full, tm_big, tn)
    comb_out = _mm_dw_exchange(y, dout, _dw_targets(), w_out2.shape[0], tn, True, "mm_dw_out")
    du2, dproj, dlg_row, dlb_row, dbb_row = _mix_b_gate_bwd(u2, proj, dy, ln_b_g, ln_b_b, 7 * w, ts)
    dproj, dwb_part, recv_out = _mix_b_conv_bwd(du2, proj, wb, dproj, comb_out, w, ts, cb)
    dproj, dwa_part = _mix_a_bwd(proj, cv, dy, wa, dproj, w, ts_a, cb)
    comb_in = _mm_dw_exchange(h, dproj, _dw_targets(), tm, w_in2.shape[1], False, "mm_dw_in")
    dh, recv_in = _mm_dh(dproj, w_in_full, comb_in, min(1024, s), tn)
    grad_x, dshift_row, dscale_row, dg1_row = _pre_bwd(x2d, dh, dx2, norm_g, scale, ts_row)
    g_w_in, d_w_in, nm_w_in, nv_w_in = _adam_sharded(w_in2, m_w_in2, v_w_in2, comb_in, recv_in, chip_arr, min(128, d), "adam_w_in")
    g_w_out, d_w_out, nm_w_out, nv_w_out = _adam_sharded(w_out2, m_w_out2, v_w_out2, comb_out, recv_out, chip_arr, min(128, w_out2.shape[0]), "adam_w_out")

    gathered = _all_gather_vmem(
        [loss_row, dfg_row, dg1_row, dshift_row, dscale_row, dgate_row, dbb_row, dlg_row, dlb_row, dwa_part, dwb_part], "gather_small_grads")
    dmod_all = jnp.concatenate([gathered[3].reshape(N_DEV, d), gathered[4].reshape(N_DEV, d), gathered[5].reshape(N_DEV, d)], axis=1)
    dm_loc = lax.dynamic_slice_in_dim(dmod_all, me * na, na, axis=1)
    g_w_ada, d_w_ada, nm_w_ada, nv_w_ada = _adam_w_ada(jnp.transpose(c_all), dm_loc, w_ada2, m_w_ada2, v_w_ada2, min(256, d))

    params = [
        (norm_g, m_norm_g, v_norm_g), (b_ada, m_b_ada, v_b_ada), (conv_a_w[0], m_conv_a_w[0], v_conv_a_w[0]),
        (conv_b_w[0], m_conv_b_w[0], v_conv_b_w[0]), (conv_b_b, m_conv_b_b, v_conv_b_b), (ln_b_g, m_ln_b_g, v_ln_b_g),
        (ln_b_b, m_ln_b_b, v_ln_b_b), (fg, m_final_g.reshape(1, d), v_final_g.reshape(1, d)),
    ]
    small = _small_final(me_arr, gathered, params, d, w, cl)
    loss = small[0].reshape(())
    sm = [small[1 + 4 * i : 5 + 4 * i] for i in range(8)]

    def pick(k):
        return [
            sm[0][k], (g_w_ada, d_w_ada, nm_w_ada, nv_w_ada)[k][None], sm[1][k], (g_w_in, d_w_in, nm_w_in, nv_w_in)[k][None],
            sm[2][k][None], sm[3][k][None], sm[4][k], sm[5][k], sm[6][k], (g_w_out, d_w_out, nm_w_out, nv_w_out)[k][None],
            sm[7][k].reshape(d),
        ]

    return (loss, grad_x[None], *pick(0), *pick(1), *pick(2), *pick(3))
```

```python
import functools

import jax
import jax.numpy as jnp
from jax import lax
from jax.experimental import pallas as pl
from jax.experimental.pallas import tpu as pltpu

F32 = jnp.float32
BF16 = jnp.bfloat16
N_DEV = 8
N_CHIP = 4
EPS = 1e-6
ADAM_LR = 0.001
ADAM_B1 = 0.9
ADAM_B2 = 0.999
ADAM_EPS = 1e-08
ADAM_WD = 0.01
ADAM_STEP = 10
CONV_A = 3
CONV_B = 31
HALO_A = 8
HALO_B = 32
LANES = 128
VMEM_LIMIT = 56 * 1024 * 1024
MESH = pl.DeviceIdType.MESH


def _cparams(n_grid_axes=0):
    if n_grid_axes:
        return pltpu.CompilerParams(dimension_semantics=("arbitrary",) * n_grid_axes, vmem_limit_bytes=VMEM_LIMIT)
    return pltpu.CompilerParams(vmem_limit_bytes=VMEM_LIMIT)


def _flip(v, bit):
    return 1 - v if bit else v


def _position():
    return lax.axis_index("x"), lax.axis_index("y"), lax.axis_index("c")


def _sigmoid(v):
    return jax.nn.sigmoid(v)


def _all_gather_vmem(arrs, name):
    n = len(arrs)

    def body(*refs):
        ins, outs = refs[:n], refs[n : 2 * n]
        send_sems, recv_sems = refs[2 * n :]
        x, y, c = _position()
        me = 4 * x + 2 * y + c
        for a in range(n):
            outs[a][me] = ins[a][...]
        sends = []
        for k in range(1, N_DEV):
            peer = (_flip(x, k & 4), _flip(y, k & 2), _flip(c, k & 1))
            for a in range(n):
                cp = pltpu.make_async_remote_copy(
                    src_ref=ins[a], dst_ref=outs[a].at[me], send_sem=send_sems.at[a, k - 1], recv_sem=recv_sems.at[a, k - 1],
                    device_id=peer, device_id_type=MESH)
                cp.start()
                sends.append(cp)
        for k in range(1, N_DEV):
            peer = (_flip(x, k & 4), _flip(y, k & 2), _flip(c, k & 1))
            src = 4 * peer[0] + 2 * peer[1] + peer[2]
            for a in range(n):
                pltpu.make_async_remote_copy(
                    src_ref=ins[a], dst_ref=outs[a].at[src], send_sem=send_sems.at[a, k - 1], recv_sem=recv_sems.at[a, k - 1],
                    device_id=peer, device_id_type=MESH).wait_recv()
        for cp in sends:
            cp.wait_send()

    vm = pl.BlockSpec(memory_space=pltpu.VMEM)
    return pl.pallas_call(
        body, name=name,
        out_shape=[jax.ShapeDtypeStruct((N_DEV,) + a.shape, a.dtype) for a in arrs],
        in_specs=[vm] * n, out_specs=[vm] * n,
        scratch_shapes=[pltpu.SemaphoreType.DMA((n, N_DEV - 1)), pltpu.SemaphoreType.DMA((n, N_DEV - 1))],
        compiler_params=_cparams(),
    )(*arrs)


def _packed_rows(arrs, width):
    starts, r = [], 0
    for a in arrs:
        starts.append(r)
        r += a.shape[0] * (a.shape[1] // width)
    return starts, -(-r // SUBLANES) * SUBLANES


def _all_gather_packed(arrs, width, name):
    n = len(arrs)
    _, n_rows = _packed_rows(arrs, width)

    def body(*refs):
        ins, out = refs[:n], refs[n]
        send_sems, recv_sems = refs[n + 1 :]
        x, y, c = _position()
        me = 4 * x + 2 * y + c
        r = 0
        for a in range(n):
            rows_a, cols_a = arrs[a].shape
            for j in range(cols_a // width):
                out[me, r : r + rows_a, :] = ins[a][:, j * width : (j + 1) * width]
                r += rows_a
        if r < n_rows:
            out[me, r:n_rows, :] = jnp.zeros((n_rows - r, width), F32)
        sends = []
        for k in range(1, N_DEV):
            peer = (_flip(x, k & 4), _flip(y, k & 2), _flip(c, k & 1))
            cp = pltpu.make_async_remote_copy(
                src_ref=out.at[me], dst_ref=out.at[me], send_sem=send_sems.at[k - 1], recv_sem=recv_sems.at[k - 1],
                device_id=peer, device_id_type=MESH)
            cp.start()
            sends.append(cp)
        for k in range(1, N_DEV):
            peer = (_flip(x, k & 4), _flip(y, k & 2), _flip(c, k & 1))
            src = 4 * peer[0] + 2 * peer[1] + peer[2]
            pltpu.make_async_remote_copy(
                src_ref=out.at[me], dst_ref=out.at[src], send_sem=send_sems.at[k - 1], recv_sem=recv_sems.at[k - 1],
                device_id=peer, device_id_type=MESH).wait_recv()
        for cp in sends:
            cp.wait_send()

    vm = pl.BlockSpec(memory_space=pltpu.VMEM)
    return pl.pallas_call(
        body, name=name, out_shape=jax.ShapeDtypeStruct((N_DEV, n_rows, width), F32),
        in_specs=[vm] * n, out_specs=vm,
        scratch_shapes=[pltpu.SemaphoreType.DMA((N_DEV - 1,)), pltpu.SemaphoreType.DMA((N_DEV - 1,))],
        compiler_params=_cparams(),
    )(*arrs)


NN = (((1,), (0,)), ((), ()))
NT = (((1,), (1,)), ((), ()))
TN = (((0,), (0,)), ((), ()))


def _n_chunk(n):
    return 256 if n % 256 == 0 else LANES


def _mm_body(dims, m_out, n_out):
    nc = _n_chunk(n_out)
    mh = min(512, m_out)

    def body(a_ref, b_ref, o_ref):
        for m0 in range(0, m_out, mh):
            for n0 in range(0, n_out, nc):
                b = b_ref[n0 : n0 + nc, :] if dims is NT else b_ref[:, n0 : n0 + nc]
                o_ref[m0 : m0 + mh, n0 : n0 + nc] = lax.dot_general(
                    a_ref[m0 : m0 + mh, :], b, dims, preferred_element_type=F32).astype(o_ref.dtype)

    return body


def _ring_chips():
    x, y, c = _position()
    first = (x + (1 - c) - 2 * x * (1 - c), y + c - 2 * y * c)
    second = (x + c - 2 * x * c, y + (1 - c) - 2 * y * (1 - c))
    return first, second, (1 - x, 1 - y)


def _arrival_order():
    x, y, c = _position()
    first, second, diag = _ring_chips()
    blocks = [(x, y, c), (x, y, 1 - c), (*first, c), (*second, 1 - c), (*second, c), (*first, 1 - c), (*diag, c), (*diag, 1 - c)]
    return jnp.stack([4 * b[0] + 2 * b[1] + b[2] for b in blocks]).astype(jnp.int32)


def _proj_gather(h, w_in_loc, w_out_loc, order, tm):
    s, d = h.shape
    ne = w_in_loc.shape[1]
    ni = s // tm
    nc = _n_chunk(ne)
    n_streams = 7

    def body(order_ref, h_ref, win_ref, wout_ref, proj_ref, fin_ref, fout_ref, bbuf, send_sems, recv_sems, local_sems, load_sems):
        ins, fulls = [win_ref, wout_ref], [fin_ref, fout_ref]
        jj, i = pl.program_id(0), pl.program_id(1)
        x, y, c = _position()
        me = (x, y, c)
        sibling = (x, y, 1 - c)
        first, second, diag = _ring_chips()
        sent = [(me, sibling), (me, (*first, c)), (me, (*second, c)), ((*first, c), (*second, c)),
                ((*first, c), sibling), ((*second, c), sibling), ((*diag, c), sibling)]
        received = [sibling, (*first, c), (*second, c), (*diag, c), (*second, 1 - c), (*first, 1 - c), (*diag, 1 - c)]

        def slot(p):
            return 4 * p[0] + 2 * p[1] + p[2]

        def copy(a, k, block, to):
            dst = fulls[a].at[slot(block)]
            return pltpu.make_async_remote_copy(
                src_ref=ins[a] if k < 3 else dst, dst_ref=dst, send_sem=send_sems.at[a, k], recv_sem=recv_sems.at[a, k],
                device_id=to, device_id_type=MESH)

        def send(a, k):
            return copy(a, k, *sent[k])

        def recv(a, k):
            return copy(a, k, received[k], me)

        def own(a):
            return pltpu.make_async_copy(ins[a], fulls[a].at[slot(me)], local_sems.at[a])

        def load(src, buf):
            return pltpu.make_async_copy(src, bbuf.at[buf], load_sems.at[buf])

        @pl.when((jj == 0) & (i == 0))
        def _():
            load(win_ref, 0).start()
            for a in range(2):
                own(a).start()
            send(0, 0).start()
            send(0, 1).start()
            send(1, 0).start()
            load(win_ref, 0).wait()

        stream_of = [None, 0, 1, 4, 2, 5, 3, 6]
        passes = {1: [(0, 4), (0, 2), (0, 3), (1, 1), (1, 2)], 2: [(0, 5)], 3: [(0, 6)]}
        for nxt in range(1, N_DEV):

            @pl.when((jj == nxt - 1) & (i == ni - 1))
            def _(nxt=nxt):
                k = stream_of[nxt]
                recv(0, k).wait_recv()
                for a, k2 in passes.get(k, []):
                    send(a, k2).start()
                if nxt == N_DEV - 2:
                    recv(1, 1).wait_recv()
                    send(1, 4).start()
                    send(1, 3).start()
                load(fin_ref.at[order_ref[nxt]], nxt % 2).start()

        @pl.when((jj > 0) & (i == 0))
        def _():
            load(fin_ref.at[0], jj % 2).wait()

        for n0 in range(0, ne, nc):
            proj_ref[:, n0 : n0 + nc] = lax.dot_general(
                h_ref[...], bbuf[jj % 2, :, n0 : n0 + nc], NN, preferred_element_type=F32).astype(BF16)

        @pl.when((jj == N_DEV - 1) & (i == ni - 1))
        def _():
            recv(1, 2).wait_recv()
            send(1, 5).start()
            recv(1, 3).wait_recv()
            send(1, 6).start()
            for k in (0, 4, 5, 6):
                recv(1, k).wait_recv()
            for a in range(2):
                for k in range(n_streams):
                    send(a, k).wait_send()
                own(a).wait()

    hbm = pl.BlockSpec(memory_space=pl.ANY)
    return pl.pallas_call(
        body, name="proj_gather",
        grid_spec=pltpu.PrefetchScalarGridSpec(
            num_scalar_prefetch=1, grid=(N_DEV, ni),
            in_specs=[pl.BlockSpec((tm, d), lambda jj, i, od: (i, 0)), hbm, hbm],
            out_specs=[pl.BlockSpec((tm, ne), lambda jj, i, od: (i, od[jj])), hbm, hbm],
            scratch_shapes=[pltpu.VMEM((2, d, ne), BF16), pltpu.SemaphoreType.DMA((2, 7)), pltpu.SemaphoreType.DMA((2, 7)),
                            pltpu.SemaphoreType.DMA((2,)), pltpu.SemaphoreType.DMA((2,))]),
        out_shape=[jax.ShapeDtypeStruct((s, N_DEV * ne), BF16), jax.ShapeDtypeStruct((N_DEV,) + w_in_loc.shape, BF16),
                   jax.ShapeDtypeStruct((N_DEV,) + w_out_loc.shape, BF16)],
        compiler_params=_cparams(2),
    )(order, h, w_in_loc, w_out_loc)


def _mm_out(y, w_out, tm, tn):
    s, m = y.shape
    _, d = w_out.shape
    return pl.pallas_call(
        _mm_body(NN, tm, tn), name="mm_out", grid=(d // tn, s // tm),
        in_specs=[pl.BlockSpec((tm, m), lambda j, i: (i, 0)), pl.BlockSpec((m, tn), lambda j, i: (0, j))],
        out_specs=pl.BlockSpec((tm, tn), lambda j, i: (i, j)),
        out_shape=jax.ShapeDtypeStruct((s, d), F32), compiler_params=_cparams(2),
    )(y, w_out)


def _mm_dy(dout, w_out, tm, tn):
    s, d = dout.shape
    m, _ = w_out.shape
    return pl.pallas_call(
        _mm_body(NT, tm, tn), name="mm_dy", grid=(m // tn, s // tm),
        in_specs=[pl.BlockSpec((tm, d), lambda j, i: (i, 0)), pl.BlockSpec((tn, d), lambda j, i: (j, 0))],
        out_specs=pl.BlockSpec((tm, tn), lambda j, i: (i, j)),
        out_shape=jax.ShapeDtypeStruct((s, m), BF16), compiler_params=_cparams(2),
    )(dout, w_out)


def _dw_targets():
    _, _, c = _position()
    return jnp.stack([2 * q + (1 - c) for q in range(N_CHIP)] + [2 * q + c for q in range(N_CHIP)]).astype(jnp.int32)


def _mm_dw_exchange(lhs, rhs, targets, bm, bn, target_on_lhs, name):
    s = lhs.shape[0]
    ni = (rhs.shape[1] // bn) if target_on_lhs else (lhs.shape[1] // bm)
    rows_b, cols_b = (bm, ni * bn) if target_on_lhs else (ni * bm, bn)
    nc = _n_chunk(bn)

    def body(tg_ref, a_ref, b_ref, comb_ref, psib_ref, rsib_ref, obuf, rbuf, osems, rsems, send_sems, recv_sems):
        del tg_ref
        jj, i = pl.program_id(0), pl.program_id(1)
        x, y, c = _position()
        slot = i % 2

        def tile(ref, q, t):
            if target_on_lhs:
                return ref.at[q, :, pl.ds(pl.multiple_of(t * bn, bn), bn)]
            return ref.at[q, pl.ds(pl.multiple_of(t * bm, bm), bm)]

        def put(sl, q):
            return pltpu.make_async_copy(obuf.at[sl], tile(psib_ref, q, i), osems.at[sl])

        def get(sl, q, t):
            return pltpu.make_async_copy(tile(rsib_ref, q, t), rbuf.at[sl], rsems.at[sl])

        def swap(q):
            return pltpu.make_async_remote_copy(
                src_ref=psib_ref.at[q], dst_ref=rsib_ref.at[q], send_sem=send_sems.at[q], recv_sem=recv_sems.at[q],
                device_id=(x, y, 1 - c), device_id_type=MESH)

        def product(n0):
            return lax.dot_general(a_ref[...], b_ref[:, n0 : n0 + nc], TN, preferred_element_type=F32)

        @pl.when(jj < N_CHIP)
        def _():
            @pl.when(i >= 2)
            def _():
                put(slot, 0).wait()

            for n0 in range(0, bn, nc):
                obuf[slot, :, n0 : n0 + nc] = product(n0).astype(BF16)
            put(slot, jj).start()

            @pl.when(i == ni - 1)
            def _():
                put(slot, 0).wait()
                if ni >= 2:
                    put(1 - slot, 0).wait()
                swap(jj).start()

        @pl.when(jj >= N_CHIP)
        def _():
            q = jj - N_CHIP

            @pl.when(i == 0)
            def _():
                swap(q).wait_recv()
                get(0, q, i).start()

            get(slot, 0, i).wait()

            @pl.when(i + 1 < ni)
            def _():
                get(1 - slot, q, i + 1).start()

            for n0 in range(0, bn, nc):
                comb_ref[:, n0 : n0 + nc] = (product(n0) + rbuf[slot, :, n0 : n0 + nc].astype(F32)).astype(BF16)

            @pl.when((jj == N_DEV - 1) & (i == ni - 1))
            def _():
                for qq in range(N_CHIP):
                    swap(qq).wait_send()

    if target_on_lhs:
        in_specs = [pl.BlockSpec((s, bm), lambda jj, i, tg: (0, tg[jj])), pl.BlockSpec((s, bn), lambda jj, i, tg: (0, i))]
        tile_index = lambda jj, i, tg: (jnp.maximum(jj - N_CHIP, 0), 0, jnp.where(jj < N_CHIP, 0, i))
    else:
        in_specs = [pl.BlockSpec((s, bm), lambda jj, i, tg: (0, i)), pl.BlockSpec((s, bn), lambda jj, i, tg: (0, tg[jj]))]
        tile_index = lambda jj, i, tg: (jnp.maximum(jj - N_CHIP, 0), jnp.where(jj < N_CHIP, 0, i), 0)
    hbm = pl.BlockSpec(memory_space=pl.ANY)
    comb, _, _ = pl.pallas_call(
        body, name=name,
        grid_spec=pltpu.PrefetchScalarGridSpec(
            num_scalar_prefetch=1, grid=(N_DEV, ni), in_specs=in_specs,
            out_specs=[pl.BlockSpec((None, bm, bn), tile_index), hbm, hbm],
            scratch_shapes=[pltpu.VMEM((2, bm, bn), BF16), pltpu.VMEM((2, bm, bn), BF16), pltpu.SemaphoreType.DMA((2,)),
                            pltpu.SemaphoreType.DMA((2,)), pltpu.SemaphoreType.DMA((N_CHIP,)), pltpu.SemaphoreType.DMA((N_CHIP,))]),
        out_shape=[jax.ShapeDtypeStruct((N_CHIP, rows_b, cols_b), BF16)] * 3,
        compiler_params=_cparams(2),
    )(targets, lhs, rhs)
    return comb


def _chip_copies(c_refs, r_refs, send_sems, recv_sems):
    x, y, c = _position()
    chips = [(1 - x, y), (x, 1 - y), (1 - x, 1 - y)]
    return [
        pltpu.make_async_remote_copy(
            src_ref=c_refs[a].at[2 * chip[0] + chip[1]], dst_ref=r_refs[a].at[j], send_sem=send_sems.at[a, j],
            recv_sem=recv_sems.at[a, j], device_id=(*chip, c), device_id_type=MESH)
        for a in range(len(c_refs)) for j, chip in enumerate(chips)]


def _mm_dh(dproj, w_full, comb, tm, tn):
    s = dproj.shape[0]
    _, d, ne = w_full.shape
    _, r_rows, r_cols = comb.shape
    kb = 2
    nk = N_DEV // kb
    nc = _n_chunk(tn)
    mh = min(512, tm)
    grid = (s // tm, d // tn, nk)
    n_steps = grid[0] * grid[1] * grid[2]
    n_relay = (3 * n_steps) // 8
    assert 0 < n_relay < n_steps - 1
    rc = min(512, r_rows)

    def body(a_ref, b_ref, c_ref, o_ref, recv_ref, relay_ref, sum_ref, va, vb, add_sems, send_sems, recv_sems):
        i, n, k = pl.program_id(0), pl.program_id(1), pl.program_id(2)
        step = (i * grid[1] + n) * nk + k
        _, _, c = _position()
        first, second, diag = _ring_chips()

        def chip_slot(p):
            return 2 * p[0] + p[1]

        def stream(j):
            src, dst, to = [(c_ref.at[chip_slot(diag)], relay_ref, first), (c_ref.at[chip_slot(first)], recv_ref.at[0], first),
                            (sum_ref, recv_ref.at[1], second)][j]
            return pltpu.make_async_remote_copy(
                src_ref=src, dst_ref=dst, send_sem=send_sems.at[j], recv_sem=recv_sems.at[j], device_id=(*to, c), device_id_type=MESH)

        @pl.when(step == 0)
        def _():
            stream(0).start()
            stream(1).start()

        @pl.when(step == n_relay)
        def _():
            stream(0).wait_recv()

            n_pieces = r_rows // rc

            def pieces(t, sl):
                rows = pl.ds(t * rc, rc)
                return (pltpu.make_async_copy(c_ref.at[chip_slot(second), rows], va.at[sl], add_sems.at[sl, 0]),
                        pltpu.make_async_copy(relay_ref.at[rows], vb.at[sl], add_sems.at[sl, 1]),
                        pltpu.make_async_copy(va.at[sl], sum_ref.at[rows], add_sems.at[sl, 2]))

            for cp in pieces(0, 0)[:2]:
                cp.start()
            for t in range(n_pieces):
                sl = t % 2
                for cp in pieces(t, sl)[:2]:
                    cp.wait()
                if t + 1 < n_pieces:
                    if t >= 1:
                        pieces(t - 1, 1 - sl)[2].wait()
                    for cp in pieces(t + 1, 1 - sl)[:2]:
                        cp.start()
                va[sl] = (va[sl].astype(F32) + vb[sl].astype(F32)).astype(BF16)
                pieces(t, sl)[2].start()
            for t in range(max(n_pieces - 2, 0), n_pieces):
                pieces(t, t % 2)[2].wait()
            stream(2).start()

        @pl.when(k == 0)
        def _():
            o_ref[...] = jnp.zeros_like(o_ref)

        for m0 in range(0, tm, mh):
            for n0 in range(0, tn, nc):
                r = None
                for jj in range(kb):
                    t = lax.dot_general(a_ref[m0 : m0 + mh, jj * ne : (jj + 1) * ne], b_ref[jj, n0 : n0 + nc, :], NT, preferred_element_type=F32)
                    r = t if r is None else r + t
                o_ref[m0 : m0 + mh, n0 : n0 + nc] += r

        @pl.when(step == n_steps - 1)
        def _():
            stream(1).wait_recv()
            stream(2).wait_recv()
            for j in range(3):
                stream(j).wait_send()

    hbm = pl.BlockSpec(memory_space=pl.ANY)
    block = jax.ShapeDtypeStruct((r_rows, r_cols), comb.dtype)
    dh, recv, _, _ = pl.pallas_call(
        body, name="mm_dh", grid=grid,
        in_specs=[pl.BlockSpec((tm, kb * ne), lambda i, n, k: (i, k)), pl.BlockSpec((kb, tn, ne), lambda i, n, k: (k, n, 0)), hbm],
        out_specs=[pl.BlockSpec((tm, tn), lambda i, n, k: (i, n)), hbm, hbm, hbm],
        out_shape=[jax.ShapeDtypeStruct((s, d), F32), jax.ShapeDtypeStruct((2, r_rows, r_cols), comb.dtype), block, block],
        scratch_shapes=[pltpu.VMEM((2, rc, r_cols), comb.dtype), pltpu.VMEM((2, rc, r_cols), comb.dtype), pltpu.SemaphoreType.DMA((2, 3)),
                        pltpu.SemaphoreType.DMA((3,)), pltpu.SemaphoreType.DMA((3,))],
        compiler_params=_cparams(3),
    )(dproj, w_full, comb)
    return dh, recv


def _mod_matmul(c_all, w_ada, b_loc, tk):
    nb, d = c_all.shape
    na = w_ada.shape[1]

    def body(c_ref, w_ref, b_ref, o_ref):
        k = pl.program_id(0)
        r = jnp.dot(c_ref[...], w_ref[...], preferred_element_type=F32, precision=lax.Precision.HIGHEST)

        @pl.when(k == 0)
        def _():
            o_ref[...] = r + b_ref[...]

        @pl.when(k > 0)
        def _():
            o_ref[...] += r

    return pl.pallas_call(
        body, name="mod_matmul", grid=(d // tk,),
        in_specs=[pl.BlockSpec((nb, tk), lambda k: (0, k)), pl.BlockSpec((tk, na), lambda k: (k, 0)), pl.BlockSpec((1, na), lambda k: (0, 0))],
        out_specs=pl.BlockSpec((nb, na), lambda k: (0, 0)),
        out_shape=jax.ShapeDtypeStruct((nb, na), F32), compiler_params=_cparams(1),
    )(c_all, w_ada, b_loc)


def _col_chunks(width):
    return [slice(c0, c0 + LANES) for c0 in range(0, width, LANES)]


def _row_sum(acc):
    return jnp.sum(acc, axis=1, keepdims=True)


SUBLANES = 8


ROW_BLOCK = 64
DW_BLOCK = 96


def _tap_sum(ext, cs, ts, taps, emit):
    for t0 in range(0, ts, ROW_BLOCK):
        rb = min(ROW_BLOCK, ts - t0)
        acc = None
        for r in range(SUBLANES):
            group = [(o, wv) for o, wv in taps if o % SUBLANES == r]
            if not group:
                continue
            n = rb if r == 0 else rb + SUBLANES
            v = None
            for o, wv in group:
                term = wv * ext[t0 + o - r : t0 + o - r + n, cs]
                v = term if v is None else v + term
            part = v if r == 0 else v[r : r + rb]
            acc = part if acc is None else acc + part
        emit(slice(t0, t0 + rb), acc)


class _SegmentStores:
    def __init__(self, stash, dst, sems, n_seg, step, n_steps, row0, cols):
        self.stash, self.dst, self.sems, self.n_seg = stash, dst, sems, n_seg
        self.step, self.n_steps, self.row0, self.cols = step, n_steps, row0, cols
        self.slot = step % 2

    def _copy(self, slot, g, row0, col0):
        ts, cb = self.stash.shape[2:]
        return pltpu.make_async_copy(self.stash.at[slot, g], self.dst.at[pl.ds(row0, ts), pl.ds(col0, cb)], self.sems.at[slot, g])

    def _wait(self, slot):
        for g in range(self.n_seg):
            self._copy(slot, g, 0, 0).wait()

    def begin(self):
        @pl.when(self.step >= 2)
        def _():
            self._wait(self.slot)

        return self.slot

    def finish(self):
        ts = self.stash.shape[2]
        for g in range(self.n_seg):
            self._copy(self.slot, g, pl.multiple_of(self.row0, ts), pl.multiple_of(self.cols[g], LANES)).start()

        @pl.when(self.step == self.n_steps - 1)
        def _():
            self._wait(self.slot)
            if self.n_steps >= 2:
                self._wait(1 - self.slot)


def _prep(x, g1, scale, shift, ts):
    s, d = x.shape
    chunks = _col_chunks(d)

    def body(x_ref, g_ref, sc_ref, sh_ref, h_ref):
        acc = jnp.zeros((ts, LANES), F32)
        for cs in chunks:
            v = x_ref[:, cs]
            acc = acc + v * v
        r = lax.rsqrt(_row_sum(acc) / d + EPS)
        for cs in chunks:
            n1 = (x_ref[:, cs] * r) * g_ref[:, cs]
            h_ref[:, cs] = (n1 * (1.0 + sc_ref[:, cs]) + sh_ref[:, cs]).astype(BF16)

    row = pl.BlockSpec((1, d), lambda i: (0, 0))
    return pl.pallas_call(
        body, name="prep", grid=(s // ts,),
        in_specs=[pl.BlockSpec((ts, d), lambda i: (i, 0)), row, row, row],
        out_specs=pl.BlockSpec((ts, d), lambda i: (i, 0)),
        out_shape=jax.ShapeDtypeStruct((s, d), BF16), compiler_params=_cparams(1),
    )(x, g1, scale, shift)


def _mix_a_fwd(proj, wa, w, ts, cb):
    s = proj.shape[0]
    nb = w // cb
    chunks = _col_chunks(cb)

    def body(ab, ac, ax, az, wa_ref, y_ref, cv_ref, ext):
        @pl.when(pl.program_id(1) == 0)
        def _():
            ext[0:HALO_A, :] = jnp.zeros((HALO_A, cb), F32)

        for cs in chunks:
            ext[HALO_A : HALO_A + ts, cs] = ac[:, cs].astype(F32) * ax[:, cs].astype(F32)
        for cs in chunks:

            def emit(rows, cv, cs=cs):
                z = az[rows, cs].astype(F32)
                y_ref[rows, cs] = (ab[rows, cs].astype(F32) * cv * (z * _sigmoid(z))).astype(BF16)
                cv_ref[rows, cs] = cv.astype(BF16)

            _tap_sum(ext, cs, ts, [(HALO_A - (CONV_A - 1) + k, wa_ref[k : k + 1, cs]) for k in range(CONV_A)], emit)
        ext[0:HALO_A, :] = ext[ts : ts + HALO_A, :]

    def seg(q):
        return pl.BlockSpec((ts, cb), lambda c, i: (i, q * nb + c))

    return pl.pallas_call(
        body, name="mix_a_fwd", grid=(nb, s // ts),
        in_specs=[seg(0), seg(1), seg(2), seg(3), pl.BlockSpec((CONV_A, cb), lambda c, i: (0, c))],
        out_specs=[pl.BlockSpec((ts, cb), lambda c, i: (i, c)), pl.BlockSpec((ts, cb), lambda c, i: (i, c))],
        out_shape=[jax.ShapeDtypeStruct((s, 2 * w), BF16), jax.ShapeDtypeStruct((s, w), BF16)],
        scratch_shapes=[pltpu.VMEM((HALO_A + ts, cb), F32)], compiler_params=_cparams(2),
    )(proj, proj, proj, proj, wa)


def _mix_b_conv_fwd(proj, wb, bb, w, ts, cb):
    s = proj.shape[0]
    nb = w // cb
    chunks = _col_chunks(cb)

    def body(bv, bg, wb_ref, bb_ref, u2_ref, ext):
        @pl.when(pl.program_id(1) == 0)
        def _():
            ext[0:HALO_B, :] = jnp.zeros((HALO_B, cb), F32)

        for cs in chunks:
            ext[HALO_B : HALO_B + ts, cs] = bv[:, cs].astype(F32) * _sigmoid(bg[:, cs].astype(F32))
        for cs in chunks:

            def emit(rows, acc, cs=cs):
                u2_ref[rows, cs] = acc + bb_ref[:, cs]

            _tap_sum(ext, cs, ts, [(HALO_B - (CONV_B - 1) + k, wb_ref[k : k + 1, cs]) for k in range(CONV_B)], emit)
        ext[0:HALO_B, :] = ext[ts : ts + HALO_B, :]

    return pl.pallas_call(
        body, name="mix_b_conv_fwd", grid=(nb, s // ts),
        in_specs=[pl.BlockSpec((ts, cb), lambda c, i: (i, 4 * nb + c)), pl.BlockSpec((ts, cb), lambda c, i: (i, 5 * nb + c)),
                  pl.BlockSpec((CONV_B, cb), lambda c, i: (0, c)), pl.BlockSpec((1, cb), lambda c, i: (0, c))],
        out_specs=pl.BlockSpec((ts, cb), lambda c, i: (i, c)),
        out_shape=jax.ShapeDtypeStruct((s, w), F32),
        scratch_shapes=[pltpu.VMEM((HALO_B + ts, cb), F32)], compiler_params=_cparams(2),
    )(proj, proj, wb, bb)


def _ln_stats(u2_ref, ts, w, chunks):
    acc = jnp.zeros((ts, LANES), F32)
    for cs in chunks:
        acc = acc + u2_ref[:, cs]
    mu = _row_sum(acc) / w
    acc = jnp.zeros((ts, LANES), F32)
    for cs in chunks:
        xc = u2_ref[:, cs] - mu
        acc = acc + xc * xc
    return mu, lax.rsqrt(_row_sum(acc) / w + EPS)


def _mix_b_gate_fwd(u2, proj, lg, lb, y, ts):
    s, w = u2.shape
    chunks = _col_chunks(w)

    def body(u2_ref, bz, lg_ref, lb_ref, y_in, y_ref):
        del y_in
        mu, rstd = _ln_stats(u2_ref, ts, w, chunks)
        for cs in chunks:
            ln = (u2_ref[:, cs] - mu) * rstd * lg_ref[:, cs] + lb_ref[:, cs]
            z = bz[:, cs].astype(F32)
            y_ref[:, cs] = ((ln * _sigmoid(ln)) * (z * _sigmoid(z))).astype(BF16)

    row = pl.BlockSpec((1, w), lambda i: (0, 0))
    return pl.pallas_call(
        body, name="mix_b_gate_fwd", grid=(s // ts,),
        in_specs=[pl.BlockSpec((ts, w), lambda i: (i, 0)), pl.BlockSpec((ts, w), lambda i: (i, 6)), row, row,
                  pl.BlockSpec(memory_space=pl.ANY)],
        out_specs=pl.BlockSpec((ts, w), lambda i: (i, 1)),
        out_shape=jax.ShapeDtypeStruct(y.shape, y.dtype), input_output_aliases={4: 0}, compiler_params=_cparams(1),
    )(u2, proj, lg, lb, y)


def _post(x, out, tgt, gate, fg, ts):
    s, d = x.shape
    chunks = _col_chunks(d)

    def body(x_ref, o_ref, t_ref, gt_ref, fg_ref, dx2_ref, dout_ref, loss_ref, dgate_ref, dfg_ref, keep):
        @pl.when(pl.program_id(0) == 0)
        def _():
            loss_ref[...] = jnp.zeros_like(loss_ref)
            dgate_ref[...] = jnp.zeros_like(dgate_ref)
            dfg_ref[...] = jnp.zeros_like(dfg_ref)

        acc = jnp.zeros((ts, LANES), F32)
        for cs in chunks:
            x2 = x_ref[:, cs] + gt_ref[:, cs] * o_ref[:, cs]
            keep[:, cs] = x2
            acc = acc + x2 * x2
        r2 = lax.rsqrt(_row_sum(acc) / d + EPS)
        acc = jnp.zeros((ts, LANES), F32)
        for cs in chunks:
            n2 = keep[:, cs] * r2
            keep[:, cs] = n2
            diff = n2 * fg_ref[:, cs] - t_ref[:, cs]
            loss_ref[:, cs] += jnp.sum(diff * diff, axis=0, keepdims=True)
            dyf = diff / d
            dfg_ref[:, cs] += jnp.sum(dyf * n2, axis=0, keepdims=True)
            dn = dyf * fg_ref[:, cs]
            dx2_ref[:, cs] = dn
            acc = acc + dn * n2
        mdot = _row_sum(acc) / d
        for cs in chunks:
            dx2 = r2 * (dx2_ref[:, cs] - keep[:, cs] * mdot)
            dx2_ref[:, cs] = dx2
            dgate_ref[:, cs] += jnp.sum(dx2 * o_ref[:, cs], axis=0, keepdims=True)
            dout_ref[:, cs] = (dx2 * gt_ref[:, cs]).astype(BF16)

    tile = pl.BlockSpec((ts, d), lambda i: (i, 0))
    row = pl.BlockSpec((1, d), lambda i: (0, 0))
    return pl.pallas_call(
        body, name="post", grid=(s // ts,),
        in_specs=[tile, tile, tile, row, row], out_specs=[tile, tile, row, row, row],
        out_shape=[jax.ShapeDtypeStruct((s, d), F32), jax.ShapeDtypeStruct((s, d), BF16)] + [jax.ShapeDtypeStruct((1, d), F32)] * 3,
        scratch_shapes=[pltpu.VMEM((ts, d), F32)], compiler_params=_cparams(1),
    )(x, out, tgt, gate, fg)


def _mix_b_gate_bwd(u2, proj, dy, lg, lb, n_cols, ts):
    s, w = u2.shape
    chunks = _col_chunks(w)

    def body(u2_ref, bz, dyb, lg_ref, lb_ref, du2_ref, dbz_ref, dlg_ref, dlb_ref, dbb_ref):
        @pl.when(pl.program_id(0) == 0)
        def _():
            dlg_ref[...] = jnp.zeros_like(dlg_ref)
            dlb_ref[...] = jnp.zeros_like(dlb_ref)
            dbb_ref[...] = jnp.zeros_like(dbb_ref)

        mu, rstd = _ln_stats(u2_ref, ts, w, chunks)
        acc1 = jnp.zeros((ts, LANES), F32)
        acc2 = jnp.zeros((ts, LANES), F32)
        for cs in chunks:
            xh = (u2_ref[:, cs] - mu) * rstd
            ln = xh * lg_ref[:, cs] + lb_ref[:, cs]
            sl = _sigmoid(ln)
            z = bz[:, cs].astype(F32)
            sz = _sigmoid(z)
            g = dyb[:, cs].astype(F32)
            dbz_ref[:, cs] = (g * (ln * sl) * (sz * (1.0 + z * (1.0 - sz)))).astype(BF16)
            dln = g * (z * sz) * (sl * (1.0 + ln * (1.0 - sl)))
            dlg_ref[:, cs] += jnp.sum(dln * xh, axis=0, keepdims=True)
            dlb_ref[:, cs] += jnp.sum(dln, axis=0, keepdims=True)
            dxh = dln * lg_ref[:, cs]
            du2_ref[:, cs] = dxh
            acc1 = acc1 + dxh
            acc2 = acc2 + dxh * xh
        m1 = _row_sum(acc1) / w
        m2 = _row_sum(acc2) / w
        for cs in chunks:
            xh = (u2_ref[:, cs] - mu) * rstd
            du2 = rstd * (du2_ref[:, cs] - m1 - xh * m2)
            du2_ref[:, cs] = du2
            dbb_ref[:, cs] += jnp.sum(du2, axis=0, keepdims=True)

    tile = pl.BlockSpec((ts, w), lambda i: (i, 0))
    row = pl.BlockSpec((1, w), lambda i: (0, 0))
    return pl.pallas_call(
        body, name="mix_b_gate_bwd", grid=(s // ts,),
        in_specs=[tile, pl.BlockSpec((ts, w), lambda i: (i, 6)), pl.BlockSpec((ts, w), lambda i: (i, 1)), row, row],
        out_specs=[tile, pl.BlockSpec((ts, w), lambda i: (i, 6)), row, row, row],
        out_shape=[jax.ShapeDtypeStruct((s, w), F32), jax.ShapeDtypeStruct((s, n_cols), BF16)] + [jax.ShapeDtypeStruct((1, w), F32)] * 3,
        compiler_params=_cparams(1),
    )(u2, proj, dy, lg, lb)


def _mix_b_conv_bwd(du2, proj, wb, dproj, comb, w, ts, cb):
    s = proj.shape[0]
    nb = w // cb
    ns = s // ts
    chunks = _col_chunks(cb)

    u_rows = ts + 2 * HALO_B + SUBLANES
    win = ts + HALO_B
    assert win % DW_BLOCK == 0

    n_seg = 2

    def body(du2_ref, bv, bg, wb_ref, dp_in, comb_ref, dp_ref, dwb_ref, recv_ref, ext, stash, uext, sems, send_sems, recv_sems):
        del dp_in
        c, i = pl.program_id(0), pl.program_id(1)
        stores = _SegmentStores(stash, dp_ref, sems, n_seg, c * ns + i, nb * ns, (ns - 1 - i) * ts, [((4 + g) * nb + c) * cb for g in range(n_seg)])
        slot = stores.begin()

        def copies():
            return _chip_copies([comb_ref], [recv_ref], send_sems, recv_sems)

        @pl.when((c == 0) & (i == 0))
        def _():
            for cp in copies():
                cp.start()

        @pl.when(i == 0)
        def _():
            ext[ts : ts + HALO_B, :] = jnp.zeros((HALO_B, cb), F32)
            uext[0:HALO_B, :] = jnp.zeros((HALO_B, cb), F32)
            uext[HALO_B + ts : u_rows, :] = jnp.zeros((u_rows - HALO_B - ts, cb), F32)
            dwb_ref[...] = jnp.zeros_like(dwb_ref)

        ext[0:ts, :] = du2_ref[...]
        for cs in chunks:
            uext[HALO_B : HALO_B + ts, cs] = bv[:, cs].astype(F32) * _sigmoid(bg[:, cs].astype(F32))

            def emit(rows, du, cs=cs):
                v = bv[rows, cs].astype(F32)
                sg = _sigmoid(bg[rows, cs].astype(F32))
                stash[slot, 0, rows, cs] = (du * sg).astype(BF16)
                stash[slot, 1, rows, cs] = (du * v * (sg * (1.0 - sg))).astype(BF16)

            _tap_sum(ext, cs, ts, [(CONV_B - 1 - k, wb_ref[k : k + 1, cs]) for k in range(CONV_B)], emit)
            for r in range(SUBLANES):
                ks = [k for k in range(CONV_B) if (CONV_B - 1 - k) % SUBLANES == r]
                accs = [jnp.zeros((SUBLANES, LANES), F32) for _ in ks]
                for t0 in range(0, win, DW_BLOCK):
                    e = ext[t0 : t0 + DW_BLOCK, cs]
                    ush = uext[SUBLANES - r + t0 : SUBLANES - r + t0 + DW_BLOCK + HALO_B - SUBLANES, cs]
                    for n, k in enumerate(ks):
                        b0 = HALO_B - SUBLANES - (CONV_B - 1 - k - r)
                        prod = e * ush[b0 : b0 + DW_BLOCK]
                        for v0 in range(0, DW_BLOCK, SUBLANES):
                            accs[n] = accs[n] + prod[v0 : v0 + SUBLANES]
                for n, k in enumerate(ks):
                    dwb_ref[k : k + 1, cs] += jnp.sum(accs[n], axis=0, keepdims=True)
        ext[ts : ts + HALO_B, :] = ext[0:HALO_B, :]
        stores.finish()

        @pl.when((c == nb - 1) & (i == ns - 1))
        def _():
            for cp in copies():
                cp.wait_recv()
            for cp in copies():
                cp.wait_send()

    def rev(col):
        return lambda c, i: (ns - 1 - i, col(c))

    hbm = pl.BlockSpec(memory_space=pl.ANY)
    return pl.pallas_call(
        body, name="mix_b_conv_bwd", grid=(nb, ns),
        in_specs=[pl.BlockSpec((ts, cb), rev(lambda c: c)), pl.BlockSpec((ts, cb), rev(lambda c: 4 * nb + c)),
                  pl.BlockSpec((ts, cb), rev(lambda c: 5 * nb + c)), pl.BlockSpec((CONV_B, cb), lambda c, i: (0, c)), hbm, hbm],
        out_specs=[hbm, pl.BlockSpec((CONV_B, cb), lambda c, i: (0, c)), hbm],
        out_shape=[jax.ShapeDtypeStruct(dproj.shape, dproj.dtype), jax.ShapeDtypeStruct((CONV_B, w), F32),
                   jax.ShapeDtypeStruct((3,) + comb.shape[1:], comb.dtype)],
        input_output_aliases={4: 0},
        scratch_shapes=[pltpu.VMEM((ts + HALO_B, cb), F32), pltpu.VMEM((2, n_seg, ts, cb), BF16), pltpu.VMEM((u_rows, cb), F32),
                        pltpu.SemaphoreType.DMA((2, n_seg)), pltpu.SemaphoreType.DMA((1, 3)), pltpu.SemaphoreType.DMA((1, 3))],
        compiler_params=_cparams(2),
    )(du2, proj, proj, wb, dproj, comb)


def _mix_a_bwd(proj, cv, dy, wa, dproj, w, ts, cb):
    s = proj.shape[0]
    nb = w // cb
    ns = s // ts
    chunks = _col_chunks(cb)

    n_seg = 4

    def body(ab, ac, ax, az, cv_ref, dya, wa_ref, dp_in, dp_ref, dwa_ref, ext, stash, sems):
        del dp_in
        c, i = pl.program_id(0), pl.program_id(1)
        stores = _SegmentStores(stash, dp_ref, sems, n_seg, c * ns + i, nb * ns, (ns - 1 - i) * ts, [(g * nb + c) * cb for g in range(n_seg)])
        slot = stores.begin()

        @pl.when(i == 0)
        def _():
            ext[ts : ts + HALO_A, :] = jnp.zeros((HALO_A, cb), F32)
            dwa_ref[...] = jnp.zeros_like(dwa_ref)

        for cs in chunks:
            b = ab[:, cs].astype(F32)
            z = az[:, cs].astype(F32)
            sz = _sigmoid(z)
            g = dya[:, cs].astype(F32)
            conv = cv_ref[:, cs].astype(F32)
            stash[slot, 0, :, cs] = (g * conv * (z * sz)).astype(BF16)
            stash[slot, 3, :, cs] = (g * b * conv * (sz * (1.0 + z * (1.0 - sz)))).astype(BF16)
            ext[0:ts, cs] = g * b * (z * sz)
        for cs in chunks:

            def emit(rows, dca, cs=cs):
                stash[slot, 1, rows, cs] = (dca * ax[rows, cs].astype(F32)).astype(BF16)
                stash[slot, 2, rows, cs] = (dca * ac[rows, cs].astype(F32)).astype(BF16)

            _tap_sum(ext, cs, ts, [(CONV_A - 1 - k, wa_ref[k : k + 1, cs]) for k in range(CONV_A)], emit)
            ca = ac[:, cs].astype(F32) * ax[:, cs].astype(F32)
            for k in range(CONV_A):
                o = CONV_A - 1 - k
                dwa_ref[k : k + 1, cs] += jnp.sum(ca * ext[o : o + ts, cs], axis=0, keepdims=True)
        ext[ts : ts + HALO_A, :] = ext[0:HALO_A, :]
        stores.finish()

    def rev(col):
        return lambda c, i: (ns - 1 - i, col(c))

    def seg(sq):
        return pl.BlockSpec((ts, cb), rev(lambda c: sq * nb + c))

    hbm = pl.BlockSpec(memory_space=pl.ANY)
    return pl.pallas_call(
        body, name="mix_a_bwd", grid=(nb, ns),
        in_specs=[seg(0), seg(1), seg(2), seg(3), pl.BlockSpec((ts, cb), rev(lambda c: c)), pl.BlockSpec((ts, cb), rev(lambda c: c)),
                  pl.BlockSpec((CONV_A, cb), lambda c, i: (0, c)), hbm],
        out_specs=[hbm, pl.BlockSpec((CONV_A, cb), lambda c, i: (0, c))],
        out_shape=[jax.ShapeDtypeStruct(dproj.shape, dproj.dtype), jax.ShapeDtypeStruct((CONV_A, w), F32)],
        input_output_aliases={7: 0},
        scratch_shapes=[pltpu.VMEM((ts + HALO_A, cb), F32), pltpu.VMEM((2, n_seg, ts, cb), BF16), pltpu.SemaphoreType.DMA((2, n_seg))],
        compiler_params=_cparams(2),
    )(proj, proj, proj, proj, cv, dy, wa, dproj)


def _pre_bwd(x, dh, dx2, g1, scale, ts):
    s, d = x.shape
    chunks = _col_chunks(d)

    def body(x_ref, dh_ref, dx2_ref, g_ref, sc_ref, gx_ref, dsh_ref, dsc_ref, dg_ref, keep):
        @pl.when(pl.program_id(0) == 0)
        def _():
            dsh_ref[...] = jnp.zeros_like(dsh_ref)
            dsc_ref[...] = jnp.zeros_like(dsc_ref)
            dg_ref[...] = jnp.zeros_like(dg_ref)

        acc = jnp.zeros((ts, LANES), F32)
        for cs in chunks:
            v = x_ref[:, cs]
            acc = acc + v * v
        r1 = lax.rsqrt(_row_sum(acc) / d + EPS)
        acc = jnp.zeros((ts, LANES), F32)
        for cs in chunks:
            xn = x_ref[:, cs] * r1
            keep[:, cs] = xn
            g = dh_ref[:, cs]
            dsh_ref[:, cs] += jnp.sum(g, axis=0, keepdims=True)
            dsc_ref[:, cs] += jnp.sum(g * (xn * g_ref[:, cs]), axis=0, keepdims=True)
            dn1 = g * (1.0 + sc_ref[:, cs])
            dg_ref[:, cs] += jnp.sum(dn1 * xn, axis=0, keepdims=True)
            dxn = dn1 * g_ref[:, cs]
            gx_ref[:, cs] = dxn
            acc = acc + dxn * xn
        mdot = _row_sum(acc) / d
        for cs in chunks:
            gx_ref[:, cs] = dx2_ref[:, cs] + r1 * (gx_ref[:, cs] - keep[:, cs] * mdot)

    tile = pl.BlockSpec((ts, d), lambda i: (i, 0))
    row = pl.BlockSpec((1, d), lambda i: (0, 0))
    return pl.pallas_call(
        body, name="pre_bwd", grid=(s // ts,),
        in_specs=[tile, tile, tile, row, row], out_specs=[tile, row, row, row],
        out_shape=[jax.ShapeDtypeStruct((s, d), F32)] + [jax.ShapeDtypeStruct((1, d), F32)] * 3,
        scratch_shapes=[pltpu.VMEM((ts, d), F32)], compiler_params=_cparams(1),
    )(x, dh, dx2, g1, scale)


def _adamw(w, g, m, v):
    m = ADAM_B1 * m + (1.0 - ADAM_B1) * g
    v = ADAM_B2 * v + (1.0 - ADAM_B2) * (g * g)
    m_hat = m / (1.0 - ADAM_B1**ADAM_STEP)
    v_hat = v / (1.0 - ADAM_B2**ADAM_STEP)
    delta = -ADAM_LR * (m_hat / (jnp.sqrt(v_hat) + ADAM_EPS) + ADAM_WD * w)
    return delta, m, v


def _adam_sharded(w, m, v, comb, recv, chip, tr, name):
    r, n = w.shape
    nr = recv.shape[0]

    def body(chip_ref, w_ref, m_ref, v_ref, c_ref, r_ref, g_ref, d_ref, nm_ref, nv_ref):
        del chip_ref
        g = c_ref[...].astype(F32)
        for j in range(nr):
            g = g + r_ref[j].astype(F32)
        delta, nm, nv = _adamw(w_ref[...], g, m_ref[...], v_ref[...])
        g_ref[...] = g
        d_ref[...] = delta
        nm_ref[...] = nm
        nv_ref[...] = nv

    tile = pl.BlockSpec((tr, n), lambda i, ch: (i, 0))
    return pl.pallas_call(
        body, name=name,
        grid_spec=pltpu.PrefetchScalarGridSpec(
            num_scalar_prefetch=1, grid=(r // tr,),
            in_specs=[tile, tile, tile, pl.BlockSpec((None, tr, n), lambda i, ch: (ch[0], i, 0)), pl.BlockSpec((nr, tr, n), lambda i, ch: (0, i, 0))],
            out_specs=[tile] * 4),
        out_shape=[jax.ShapeDtypeStruct((r, n), F32)] * 4, compiler_params=_cparams(1),
    )(chip, w, m, v, comb, recv)


def _adam_w_ada(c_t, dm_loc, w, m, v, tr):
    d, nb = c_t.shape
    na = w.shape[1]

    def body(c_ref, dm_ref, w_ref, m_ref, v_ref, g_ref, d_ref, nm_ref, nv_ref):
        g = jnp.dot(c_ref[...], dm_ref[...], preferred_element_type=F32, precision=lax.Precision.HIGHEST)
        delta, nm, nv = _adamw(w_ref[...], g, m_ref[...], v_ref[...])
        g_ref[...] = g
        d_ref[...] = delta
        nm_ref[...] = nm
        nv_ref[...] = nv

    tile = pl.BlockSpec((tr, na), lambda i: (i, 0))
    return pl.pallas_call(
        body, name="adam_w_ada", grid=(d // tr,),
        in_specs=[pl.BlockSpec((tr, nb), lambda i: (i, 0)), pl.BlockSpec((nb, na), lambda i: (0, 0)), tile, tile, tile],
        out_specs=[tile] * 4, out_shape=[jax.ShapeDtypeStruct((d, na), F32)] * 4, compiler_params=_cparams(1),
    )(c_t, dm_loc, w, m, v)


def _small_final(me, packed, starts, params, d, w, cl):
    flat_params = [t for p in params for t in p]
    kd = d // w
    r_loss, r_dfg, r_dg1, r_dsh, r_dsc, r_dgt, r_dbb, r_dlg, r_dlb, r_dwa, r_dwb = starts

    def total(ref, r0, rows):
        acc = ref[0, r0 : r0 + rows, :]
        for b in range(1, N_DEV):
            acc = acc + ref[b, r0 : r0 + rows, :]
        return acc

    def body(me_ref, full_ref, mine_ref, *rest):
        del me_ref
        prm = rest[:24]
        loss_ref = rest[24]
        outs = rest[25:]

        loss_rows = jnp.sum(total(full_ref, r_loss, kd), axis=1, keepdims=True)
        loss_ref[...] = jnp.sum(loss_rows, axis=0, keepdims=True) * (0.5 / d)

        def update(idx, g, col0=0):
            w_ref, m_ref, v_ref = prm[3 * idx : 3 * idx + 3]
            g_ref, d_ref, nm_ref, nv_ref = outs[4 * idx : 4 * idx + 4]
            sl = (slice(None), slice(col0, col0 + g.shape[1]))
            delta, nm, nv = _adamw(w_ref[sl], g, m_ref[sl], v_ref[sl])
            g_ref[sl] = g
            d_ref[sl] = delta
            nm_ref[sl] = nm
            nv_ref[sl] = nv

        def update_wide(idx, r0, col0=0):
            for j in range(kd):
                update(idx, total(full_ref, r0 + j, 1), col0 + j * w)

        update_wide(0, r_dg1)
        update_wide(1, r_dsh, 0)
        update_wide(1, r_dsc, d)
        update_wide(1, r_dgt, 2 * d)
        update(2, total(mine_ref, r_dwa, CONV_A))
        update(3, total(mine_ref, r_dwb, CONV_B))
        update(4, total(full_ref, r_dbb, 1))
        update(5, total(full_ref, r_dlg, 1))
        update(6, total(full_ref, r_dlb, 1))
        update_wide(7, r_dfg)

    def full(shape):
        nd = len(shape)
        return pl.BlockSpec(shape, lambda i, mr: (0,) * nd)

    n_rows = packed.shape[1]
    in_specs = [full(packed.shape), pl.BlockSpec((N_DEV, n_rows, cl), lambda i, mr: (0, 0, mr[0]))]
    in_specs += [full(t.shape) for t in flat_params]
    out_shapes = [jax.ShapeDtypeStruct((1, 1), F32)]
    for p in params:
        out_shapes += [jax.ShapeDtypeStruct(p[0].shape, F32)] * 4
    return pl.pallas_call(
        body, name="small_final",
        grid_spec=pltpu.PrefetchScalarGridSpec(
            num_scalar_prefetch=1, grid=(1,), in_specs=in_specs, out_specs=[full(o.shape) for o in out_shapes]),
        out_shape=out_shapes, compiler_params=_cparams(1),
    )(me, packed, packed, *flat_params)


def _silu_rows(c):
    def body(c_ref, o_ref):
        v = c_ref[...]
        o_ref[...] = v * _sigmoid(v)

    return pl.pallas_call(body, name="silu_c", out_shape=jax.ShapeDtypeStruct(c.shape, F32), compiler_params=_cparams())(c)


def kernel(x, c, norm_g, w_ada, b_ada, w_in, conv_a_w, conv_b_w, conv_b_b, ln_b_g, ln_b_b, w_out, final_g, loss_target, m_norm_g, m_w_ada, m_b_ada, m_w_in, m_conv_a_w, m_conv_b_w, m_conv_b_b, m_ln_b_g, m_ln_b_b, m_w_out, m_final_g, v_norm_g, v_w_ada, v_b_ada, v_w_in, v_conv_a_w, v_conv_b_w, v_conv_b_b, v_ln_b_g, v_ln_b_b, v_w_out, v_final_g):
    _, s, d = x.shape
    w = conv_b_b.shape[-1]
    cl = conv_a_w.shape[-1]
    na = w_ada.shape[-1]
    assert ln_b_g.shape[-1] == w and w_out.shape[1] * N_DEV == 2 * w and w_in.shape[-1] * N_DEV == 7 * w and cl * N_DEV == w
    ts = min(256, s)
    ts_a = min(512, s)
    ts_row = min(128, s)
    cb = min(512, w)
    tm = min(512, s)
    tm_big = min(1024, s)
    tn = min(1024, d)

    px, py, pc = _position()
    me = 4 * px + 2 * py + pc
    me_arr = jnp.reshape(me, (1,)).astype(jnp.int32)
    chip_arr = jnp.reshape(2 * px + py, (1,)).astype(jnp.int32)

    x2d, tgt = x[0], loss_target[0]
    w_ada2, m_w_ada2, v_w_ada2 = w_ada[0], m_w_ada[0], v_w_ada[0]
    w_in2, m_w_in2, v_w_in2 = w_in[0], m_w_in[0], v_w_in[0]
    w_out2, m_w_out2, v_w_out2 = w_out[0], m_w_out[0], v_w_out[0]
    fg = final_g.reshape(1, d)

    c_act = _silu_rows(c)
    small_in = [c_act, conv_a_w[0], conv_b_w[0]]
    (r_c, r_wa, r_wb), _ = _packed_rows(small_in, cl)
    packed_in = _all_gather_packed(small_in, cl, "gather_small_in")
    c_all = packed_in[:, r_c : r_c + d // cl, :].reshape(N_DEV, d)
    wa = jnp.transpose(packed_in[:, r_wa : r_wa + CONV_A, :], (1, 0, 2)).reshape(CONV_A, w)
    wb = jnp.transpose(packed_in[:, r_wb : r_wb + CONV_B, :], (1, 0, 2)).reshape(CONV_B, w)
    b_loc = lax.dynamic_slice_in_dim(b_ada, me * na, na, axis=1)
    mod_loc = _mod_matmul(c_all, w_ada2, b_loc, min(512, d))
    (mod_all,) = _all_gather_vmem([mod_loc], "gather_mod")
    mod_mine = lax.dynamic_index_in_dim(mod_all, me, axis=1, keepdims=False).reshape(1, 3 * d)
    shift, scale, gate = mod_mine[:, :d], mod_mine[:, d : 2 * d], mod_mine[:, 2 * d :]

    h = _prep(x2d, norm_g, scale, shift, ts_row)
    proj, w_in_full, w_out_full = _proj_gather(h, w_in2.astype(BF16), w_out2.astype(BF16), _arrival_order(), tm)
    w_out_full = w_out_full.reshape(2 * w, d)
    y, cv = _mix_a_fwd(proj, wa, w, ts_a, cb)
    u2 = _mix_b_conv_fwd(proj, wb, conv_b_b, w, ts, cb)
    y = _mix_b_gate_fwd(u2, proj, ln_b_g, ln_b_b, y, ts)
    out = _mm_out(y, w_out_full, tm_big, tn)
    dx2, dout, loss_row, dgate_row, dfg_row = _post(x2d, out, tgt, gate, fg, ts_row)

    dy = _mm_dy(dout, w_out_full, tm_big, tn)
    comb_out = _mm_dw_exchange(y, dout, _dw_targets(), w_out2.shape[0], tn, True, "mm_dw_out")
    du2, dproj, dlg_row, dlb_row, dbb_row = _mix_b_gate_bwd(u2, proj, dy, ln_b_g, ln_b_b, 7 * w, ts)
    dproj, dwb_part, recv_out = _mix_b_conv_bwd(du2, proj, wb, dproj, comb_out, w, ts, cb)
    dproj, dwa_part = _mix_a_bwd(proj, cv, dy, wa, dproj, w, ts_a, cb)
    comb_in = _mm_dw_exchange(h, dproj, _dw_targets(), tm, w_in2.shape[1], False, "mm_dw_in")
    dh, recv_in = _mm_dh(dproj, w_in_full, comb_in, min(1024, s), tn)
    grad_x, dshift_row, dscale_row, dg1_row = _pre_bwd(x2d, dh, dx2, norm_g, scale, ts_row)
    g_w_in, d_w_in, nm_w_in, nv_w_in = _adam_sharded(w_in2, m_w_in2, v_w_in2, comb_in, recv_in, chip_arr, min(128, d), "adam_w_in")
    g_w_out, d_w_out, nm_w_out, nv_w_out = _adam_sharded(w_out2, m_w_out2, v_w_out2, comb_out, recv_out, chip_arr, min(128, w_out2.shape[0]), "adam_w_out")

    small_parts = [loss_row, dfg_row, dg1_row, dshift_row, dscale_row, dgate_row, dbb_row, dlg_row, dlb_row, dwa_part, dwb_part]
    starts, _ = _packed_rows(small_parts, w)
    packed = _all_gather_packed(small_parts, w, "gather_small_grads")
    kd = d // w
    dmod_all = jnp.concatenate([packed[:, starts[j] : starts[j] + kd, :].reshape(N_DEV, d) for j in (3, 4, 5)], axis=1)
    dm_loc = lax.dynamic_slice_in_dim(dmod_all, me * na, na, axis=1)
    g_w_ada, d_w_ada, nm_w_ada, nv_w_ada = _adam_w_ada(jnp.transpose(c_all), dm_loc, w_ada2, m_w_ada2, v_w_ada2, min(256, d))

    params = [
        (norm_g, m_norm_g, v_norm_g), (b_ada, m_b_ada, v_b_ada), (conv_a_w[0], m_conv_a_w[0], v_conv_a_w[0]),
        (conv_b_w[0], m_conv_b_w[0], v_conv_b_w[0]), (conv_b_b, m_conv_b_b, v_conv_b_b), (ln_b_g, m_ln_b_g, v_ln_b_g),
        (ln_b_b, m_ln_b_b, v_ln_b_b), (fg, m_final_g.reshape(1, d), v_final_g.reshape(1, d)),
    ]
    small = _small_final(me_arr, packed, starts, params, d, w, cl)
    loss = small[0].reshape(())
    sm = [small[1 + 4 * i : 5 + 4 * i] for i in range(8)]

    def pick(k):
        return [
            sm[0][k], (g_w_ada, d_w_ada, nm_w_ada, nv_w_ada)[k][None], sm[1][k], (g_w_in, d_w_in, nm_w_in, nv_w_in)[k][None],
            sm[2][k][None], sm[3][k][None], sm[4][k], sm[5][k], sm[6][k], (g_w_out, d_w_out, nm_w_out, nv_w_out)[k][None],
            sm[7][k].reshape(d),
        ]

    return (loss, grad_x[None], *pick(0), *pick(1), *pick(2), *pick(3))
```

```python
import functools

import jax
import jax.numpy as jnp
from jax import lax
from jax.experimental import pallas as pl
from jax.experimental.pallas import tpu as pltpu

F32 = jnp.float32
BF16 = jnp.bfloat16
N_DEV = 8
N_CHIP = 4
EPS = 1e-6
ADAM_LR = 0.001
ADAM_B1 = 0.9
ADAM_B2 = 0.999
ADAM_EPS = 1e-08
ADAM_WD = 0.01
ADAM_STEP = 10
CONV_A = 3
CONV_B = 31
HALO_A = 8
HALO_B = 32
LANES = 128
VMEM_LIMIT = 56 * 1024 * 1024
MESH = pl.DeviceIdType.MESH


def _cparams(n_grid_axes=0):
    if n_grid_axes:
        return pltpu.CompilerParams(dimension_semantics=("arbitrary",) * n_grid_axes, vmem_limit_bytes=VMEM_LIMIT)
    return pltpu.CompilerParams(vmem_limit_bytes=VMEM_LIMIT)


def _flip(v, bit):
    return 1 - v if bit else v


def _position():
    return lax.axis_index("x"), lax.axis_index("y"), lax.axis_index("c")


def _sigmoid(v):
    return jax.nn.sigmoid(v)


def _all_gather_vmem(arrs, name):
    n = len(arrs)

    def body(*refs):
        ins, outs = refs[:n], refs[n : 2 * n]
        send_sems, recv_sems = refs[2 * n :]
        x, y, c = _position()
        me = 4 * x + 2 * y + c
        for a in range(n):
            outs[a][me] = ins[a][...]
        sends = []
        for k in range(1, N_DEV):
            peer = (_flip(x, k & 4), _flip(y, k & 2), _flip(c, k & 1))
            for a in range(n):
                cp = pltpu.make_async_remote_copy(
                    src_ref=ins[a], dst_ref=outs[a].at[me], send_sem=send_sems.at[a, k - 1], recv_sem=recv_sems.at[a, k - 1],
                    device_id=peer, device_id_type=MESH)
                cp.start()
                sends.append(cp)
        for k in range(1, N_DEV):
            peer = (_flip(x, k & 4), _flip(y, k & 2), _flip(c, k & 1))
            src = 4 * peer[0] + 2 * peer[1] + peer[2]
            for a in range(n):
                pltpu.make_async_remote_copy(
                    src_ref=ins[a], dst_ref=outs[a].at[src], send_sem=send_sems.at[a, k - 1], recv_sem=recv_sems.at[a, k - 1],
                    device_id=peer, device_id_type=MESH).wait_recv()
        for cp in sends:
            cp.wait_send()

    vm = pl.BlockSpec(memory_space=pltpu.VMEM)
    return pl.pallas_call(
        body, name=name,
        out_shape=[jax.ShapeDtypeStruct((N_DEV,) + a.shape, a.dtype) for a in arrs],
        in_specs=[vm] * n, out_specs=[vm] * n,
        scratch_shapes=[pltpu.SemaphoreType.DMA((n, N_DEV - 1)), pltpu.SemaphoreType.DMA((n, N_DEV - 1))],
        compiler_params=_cparams(),
    )(*arrs)


def _packed_rows(arrs, width):
    starts, r = [], 0
    for a in arrs:
        starts.append(r)
        r += a.shape[0] * (a.shape[1] // width)
    return starts, -(-r // SUBLANES) * SUBLANES


def _packed_copies(out, send_sems, recv_sems, as_receiver):
    x, y, c = _position()
    me = 4 * x + 2 * y + c
    copies = []
    for k in range(1, N_DEV):
        peer = (_flip(x, k & 4), _flip(y, k & 2), _flip(c, k & 1))
        block = 4 * peer[0] + 2 * peer[1] + peer[2] if as_receiver else me
        copies.append(pltpu.make_async_remote_copy(
            src_ref=out.at[me], dst_ref=out.at[block], send_sem=send_sems.at[k - 1], recv_sem=recv_sems.at[k - 1],
            device_id=peer, device_id_type=MESH))
    return copies


def _packed_gather_start(ins, out, send_sems, recv_sems, stage=None, stage_sem=None):
    _, n_rows, width = out.shape
    x, y, c = _position()
    me = 4 * x + 2 * y + c
    r = 0
    for ref in ins:
        rows_a, cols_a = ref.shape
        for j in range(cols_a // width):
            rows = ref[:, j * width : (j + 1) * width]
            if stage is None:
                out[me, r : r + rows_a, :] = rows
            else:
                stage[r : r + rows_a, :] = rows
            r += rows_a
    if r < n_rows:
        pad = jnp.zeros((n_rows - r, width), F32)
        if stage is None:
            out[me, r:n_rows, :] = pad
        else:
            stage[r:n_rows, :] = pad
    if stage is not None:
        mine = pltpu.make_async_copy(stage, out.at[me], stage_sem)
        mine.start()
        mine.wait()
    for cp in _packed_copies(out, send_sems, recv_sems, False):
        cp.start()


def _packed_gather_wait(out, send_sems, recv_sems):
    for cp in _packed_copies(out, send_sems, recv_sems, True):
        cp.wait_recv()
    for cp in _packed_copies(out, send_sems, recv_sems, False):
        cp.wait_send()


def _all_gather_packed(arrs, width, name):
    n = len(arrs)
    _, n_rows = _packed_rows(arrs, width)

    def body(*refs):
        ins, out = refs[:n], refs[n]
        send_sems, recv_sems = refs[n + 1 :]
        _packed_gather_start(ins, out, send_sems, recv_sems)
        _packed_gather_wait(out, send_sems, recv_sems)

    vm = pl.BlockSpec(memory_space=pltpu.VMEM)
    return pl.pallas_call(
        body, name=name, out_shape=jax.ShapeDtypeStruct((N_DEV, n_rows, width), F32),
        in_specs=[vm] * n, out_specs=vm,
        scratch_shapes=[pltpu.SemaphoreType.DMA((N_DEV - 1,)), pltpu.SemaphoreType.DMA((N_DEV - 1,))],
        compiler_params=_cparams(),
    )(*arrs)


NN = (((1,), (0,)), ((), ()))
NT = (((1,), (1,)), ((), ()))
TN = (((0,), (0,)), ((), ()))


def _n_chunk(n):
    return 256 if n % 256 == 0 else LANES


def _mm_body(dims, m_out, n_out):
    nc = _n_chunk(n_out)
    mh = min(512, m_out)

    def body(a_ref, b_ref, o_ref):
        for m0 in range(0, m_out, mh):
            for n0 in range(0, n_out, nc):
                b = b_ref[n0 : n0 + nc, :] if dims is NT else b_ref[:, n0 : n0 + nc]
                o_ref[m0 : m0 + mh, n0 : n0 + nc] = lax.dot_general(
                    a_ref[m0 : m0 + mh, :], b, dims, preferred_element_type=F32).astype(o_ref.dtype)

    return body


def _ring_chips():
    x, y, c = _position()
    first = (x + (1 - c) - 2 * x * (1 - c), y + c - 2 * y * c)
    second = (x + c - 2 * x * c, y + (1 - c) - 2 * y * (1 - c))
    return first, second, (1 - x, 1 - y)


def _arrival_order():
    x, y, c = _position()
    first, second, diag = _ring_chips()
    blocks = [(x, y, c), (x, y, 1 - c), (*first, c), (*second, 1 - c), (*second, c), (*first, 1 - c), (*diag, c), (*diag, 1 - c)]
    return jnp.stack([4 * b[0] + 2 * b[1] + b[2] for b in blocks]).astype(jnp.int32)


def _proj_gather(h, w_in_loc, w_out_loc, order, tm):
    s, d = h.shape
    ne = w_in_loc.shape[1]
    ni = s // tm
    nc = _n_chunk(ne)
    n_streams = 7

    def body(order_ref, h_ref, win_ref, wout_ref, proj_ref, fin_ref, fout_ref, bbuf, send_sems, recv_sems, local_sems, load_sems):
        ins, fulls = [win_ref, wout_ref], [fin_ref, fout_ref]
        jj, i = pl.program_id(0), pl.program_id(1)
        x, y, c = _position()
        me = (x, y, c)
        sibling = (x, y, 1 - c)
        first, second, diag = _ring_chips()
        sent = [(me, sibling), (me, (*first, c)), (me, (*second, c)), ((*first, c), (*second, c)),
                ((*first, c), sibling), ((*second, c), sibling), ((*diag, c), sibling)]
        received = [sibling, (*first, c), (*second, c), (*diag, c), (*second, 1 - c), (*first, 1 - c), (*diag, 1 - c)]

        def slot(p):
            return 4 * p[0] + 2 * p[1] + p[2]

        def copy(a, k, block, to):
            dst = fulls[a].at[slot(block)]
            return pltpu.make_async_remote_copy(
                src_ref=ins[a] if k < 3 else dst, dst_ref=dst, send_sem=send_sems.at[a, k], recv_sem=recv_sems.at[a, k],
                device_id=to, device_id_type=MESH)

        def send(a, k):
            return copy(a, k, *sent[k])

        def recv(a, k):
            return copy(a, k, received[k], me)

        def own(a):
            return pltpu.make_async_copy(ins[a], fulls[a].at[slot(me)], local_sems.at[a])

        def load(src, buf):
            return pltpu.make_async_copy(src, bbuf.at[buf], load_sems.at[buf])

        @pl.when((jj == 0) & (i == 0))
        def _():
            load(win_ref, 0).start()
            for a in range(2):
                own(a).start()
            send(0, 0).start()
            send(0, 1).start()
            send(1, 0).start()
            load(win_ref, 0).wait()

        stream_of = [None, 0, 1, 4, 2, 5, 3, 6]
        passes = {1: [(0, 4), (0, 2), (0, 3), (1, 1), (1, 2)], 2: [(0, 5)], 3: [(0, 6)]}
        for nxt in range(1, N_DEV):

            @pl.when((jj == nxt - 1) & (i == ni - 1))
            def _(nxt=nxt):
                k = stream_of[nxt]
                recv(0, k).wait_recv()
                for a, k2 in passes.get(k, []):
                    send(a, k2).start()
                if nxt == N_DEV - 2:
                    recv(1, 1).wait_recv()
                    send(1, 4).start()
                    send(1, 3).start()
                load(fin_ref.at[order_ref[nxt]], nxt % 2).start()

        @pl.when((jj > 0) & (i == 0))
        def _():
            load(fin_ref.at[0], jj % 2).wait()

        for n0 in range(0, ne, nc):
            proj_ref[:, n0 : n0 + nc] = lax.dot_general(
                h_ref[...], bbuf[jj % 2, :, n0 : n0 + nc], NN, preferred_element_type=F32).astype(BF16)

        @pl.when((jj == N_DEV - 1) & (i == ni - 1))
        def _():
            recv(1, 2).wait_recv()
            send(1, 5).start()
            recv(1, 3).wait_recv()
            send(1, 6).start()
            for k in (0, 4, 5, 6):
                recv(1, k).wait_recv()
            for a in range(2):
                for k in range(n_streams):
                    send(a, k).wait_send()
                own(a).wait()

    hbm = pl.BlockSpec(memory_space=pl.ANY)
    return pl.pallas_call(
        body, name="proj_gather",
        grid_spec=pltpu.PrefetchScalarGridSpec(
            num_scalar_prefetch=1, grid=(N_DEV, ni),
            in_specs=[pl.BlockSpec((tm, d), lambda jj, i, od: (i, 0)), hbm, hbm],
            out_specs=[pl.BlockSpec((tm, ne), lambda jj, i, od: (i, od[jj])), hbm, hbm],
            scratch_shapes=[pltpu.VMEM((2, d, ne), BF16), pltpu.SemaphoreType.DMA((2, 7)), pltpu.SemaphoreType.DMA((2, 7)),
                            pltpu.SemaphoreType.DMA((2,)), pltpu.SemaphoreType.DMA((2,))]),
        out_shape=[jax.ShapeDtypeStruct((s, N_DEV * ne), BF16), jax.ShapeDtypeStruct((N_DEV,) + w_in_loc.shape, BF16),
                   jax.ShapeDtypeStruct((N_DEV,) + w_out_loc.shape, BF16)],
        compiler_params=_cparams(2),
    )(order, h, w_in_loc, w_out_loc)


def _mm_out(y, w_out, tm, tn):
    s, m = y.shape
    _, d = w_out.shape
    return pl.pallas_call(
        _mm_body(NN, tm, tn), name="mm_out", grid=(d // tn, s // tm),
        in_specs=[pl.BlockSpec((tm, m), lambda j, i: (i, 0)), pl.BlockSpec((m, tn), lambda j, i: (0, j))],
        out_specs=pl.BlockSpec((tm, tn), lambda j, i: (i, j)),
        out_shape=jax.ShapeDtypeStruct((s, d), F32), compiler_params=_cparams(2),
    )(y, w_out)


def _mm_dy(dout, w_out, tm, tn):
    s, d = dout.shape
    m, _ = w_out.shape
    return pl.pallas_call(
        _mm_body(NT, tm, tn), name="mm_dy", grid=(m // tn, s // tm),
        in_specs=[pl.BlockSpec((tm, d), lambda j, i: (i, 0)), pl.BlockSpec((tn, d), lambda j, i: (j, 0))],
        out_specs=pl.BlockSpec((tm, tn), lambda j, i: (i, j)),
        out_shape=jax.ShapeDtypeStruct((s, m), BF16), compiler_params=_cparams(2),
    )(dout, w_out)


def _dw_targets():
    _, _, c = _position()
    return jnp.stack([2 * q + (1 - c) for q in range(N_CHIP)] + [2 * q + c for q in range(N_CHIP)]).astype(jnp.int32)


def _mm_dw_exchange(lhs, rhs, targets, bm, bn, target_on_lhs, name):
    s = lhs.shape[0]
    ni = (rhs.shape[1] // bn) if target_on_lhs else (lhs.shape[1] // bm)
    rows_b, cols_b = (bm, ni * bn) if target_on_lhs else (ni * bm, bn)
    nc = _n_chunk(bn)

    def body(tg_ref, a_ref, b_ref, comb_ref, psib_ref, rsib_ref, obuf, rbuf, osems, rsems, send_sems, recv_sems):
        del tg_ref
        jj, i = pl.program_id(0), pl.program_id(1)
        x, y, c = _position()
        slot = i % 2

        def tile(ref, q, t):
            if target_on_lhs:
                return ref.at[q, :, pl.ds(pl.multiple_of(t * bn, bn), bn)]
            return ref.at[q, pl.ds(pl.multiple_of(t * bm, bm), bm)]

        def put(sl, q):
            return pltpu.make_async_copy(obuf.at[sl], tile(psib_ref, q, i), osems.at[sl])

        def get(sl, q, t):
            return pltpu.make_async_copy(tile(rsib_ref, q, t), rbuf.at[sl], rsems.at[sl])

        def swap(q):
            return pltpu.make_async_remote_copy(
                src_ref=psib_ref.at[q], dst_ref=rsib_ref.at[q], send_sem=send_sems.at[q], recv_sem=recv_sems.at[q],
                device_id=(x, y, 1 - c), device_id_type=MESH)

        def product(n0):
            return lax.dot_general(a_ref[...], b_ref[:, n0 : n0 + nc], TN, preferred_element_type=F32)

        @pl.when(jj < N_CHIP)
        def _():
            @pl.when(i >= 2)
            def _():
                put(slot, 0).wait()

            for n0 in range(0, bn, nc):
                obuf[slot, :, n0 : n0 + nc] = product(n0).astype(BF16)
            put(slot, jj).start()

            @pl.when(i == ni - 1)
            def _():
                put(slot, 0).wait()
                if ni >= 2:
                    put(1 - slot, 0).wait()
                swap(jj).start()

        @pl.when(jj >= N_CHIP)
        def _():
            q = jj - N_CHIP

            @pl.when(i == 0)
            def _():
                swap(q).wait_recv()
                get(0, q, i).start()

            get(slot, 0, i).wait()

            @pl.when(i + 1 < ni)
            def _():
                get(1 - slot, q, i + 1).start()

            for n0 in range(0, bn, nc):
                comb_ref[:, n0 : n0 + nc] = (product(n0) + rbuf[slot, :, n0 : n0 + nc].astype(F32)).astype(BF16)

            @pl.when((jj == N_DEV - 1) & (i == ni - 1))
            def _():
                for qq in range(N_CHIP):
                    swap(qq).wait_send()

    if target_on_lhs:
        in_specs = [pl.BlockSpec((s, bm), lambda jj, i, tg: (0, tg[jj])), pl.BlockSpec((s, bn), lambda jj, i, tg: (0, i))]
        tile_index = lambda jj, i, tg: (jnp.maximum(jj - N_CHIP, 0), 0, jnp.where(jj < N_CHIP, 0, i))
    else:
        in_specs = [pl.BlockSpec((s, bm), lambda jj, i, tg: (0, i)), pl.BlockSpec((s, bn), lambda jj, i, tg: (0, tg[jj]))]
        tile_index = lambda jj, i, tg: (jnp.maximum(jj - N_CHIP, 0), jnp.where(jj < N_CHIP, 0, i), 0)
    hbm = pl.BlockSpec(memory_space=pl.ANY)
    comb, _, _ = pl.pallas_call(
        body, name=name,
        grid_spec=pltpu.PrefetchScalarGridSpec(
            num_scalar_prefetch=1, grid=(N_DEV, ni), in_specs=in_specs,
            out_specs=[pl.BlockSpec((None, bm, bn), tile_index), hbm, hbm],
            scratch_shapes=[pltpu.VMEM((2, bm, bn), BF16), pltpu.VMEM((2, bm, bn), BF16), pltpu.SemaphoreType.DMA((2,)),
                            pltpu.SemaphoreType.DMA((2,)), pltpu.SemaphoreType.DMA((N_CHIP,)), pltpu.SemaphoreType.DMA((N_CHIP,))]),
        out_shape=[jax.ShapeDtypeStruct((N_CHIP, rows_b, cols_b), BF16)] * 3,
        compiler_params=_cparams(2),
    )(targets, lhs, rhs)
    return comb


def _chip_copies(c_refs, r_refs, send_sems, recv_sems):
    x, y, c = _position()
    chips = [(1 - x, y), (x, 1 - y), (1 - x, 1 - y)]
    return [
        pltpu.make_async_remote_copy(
            src_ref=c_refs[a].at[2 * chip[0] + chip[1]], dst_ref=r_refs[a].at[j], send_sem=send_sems.at[a, j],
            recv_sem=recv_sems.at[a, j], device_id=(*chip, c), device_id_type=MESH)
        for a in range(len(c_refs)) for j, chip in enumerate(chips)]


def _mm_dh(dproj, w_full, comb, small_parts, small_width, tm, tn):
    s = dproj.shape[0]
    _, d, ne = w_full.shape
    _, r_rows, r_cols = comb.shape
    kb = 2
    nk = N_DEV // kb
    nc = _n_chunk(tn)
    mh = min(512, tm)
    grid = (s // tm, d // tn, nk)
    n_steps = grid[0] * grid[1] * grid[2]
    n_relay = (3 * n_steps) // 8
    assert 0 < n_relay < n_steps - 1
    rc = min(512, r_rows)

    n_small = len(small_parts)
    _, small_rows = _packed_rows(small_parts, small_width)

    def body(*refs):
        a_ref, b_ref, c_ref = refs[:3]
        small_refs = refs[3 : 3 + n_small]
        (o_ref, recv_ref, relay_ref, sum_ref, packed_ref, va, vb, stage, add_sems, send_sems, recv_sems, small_send, small_recv,
         stage_sem) = refs[3 + n_small :]
        i, n, k = pl.program_id(0), pl.program_id(1), pl.program_id(2)
        step = (i * grid[1] + n) * nk + k
        _, _, c = _position()
        first, second, diag = _ring_chips()

        def chip_slot(p):
            return 2 * p[0] + p[1]

        def stream(j):
            src, dst, to = [(c_ref.at[chip_slot(diag)], relay_ref, first), (c_ref.at[chip_slot(first)], recv_ref.at[0], first),
                            (sum_ref, recv_ref.at[1], second)][j]
            return pltpu.make_async_remote_copy(
                src_ref=src, dst_ref=dst, send_sem=send_sems.at[j], recv_sem=recv_sems.at[j], device_id=(*to, c), device_id_type=MESH)

        @pl.when(step == 0)
        def _():
            stream(0).start()
            stream(1).start()
            _packed_gather_start(small_refs, packed_ref, small_send, small_recv, stage, stage_sem.at[0])

        @pl.when(step == n_relay)
        def _():
            stream(0).wait_recv()

            n_pieces = r_rows // rc

            def pieces(t, sl):
                rows = pl.ds(t * rc, rc)
                return (pltpu.make_async_copy(c_ref.at[chip_slot(second), rows], va.at[sl], add_sems.at[sl, 0]),
                        pltpu.make_async_copy(relay_ref.at[rows], vb.at[sl], add_sems.at[sl, 1]),
                        pltpu.make_async_copy(va.at[sl], sum_ref.at[rows], add_sems.at[sl, 2]))

            for cp in pieces(0, 0)[:2]:
                cp.start()
            for t in range(n_pieces):
                sl = t % 2
                for cp in pieces(t, sl)[:2]:
                    cp.wait()
                if t + 1 < n_pieces:
                    if t >= 1:
                        pieces(t - 1, 1 - sl)[2].wait()
                    for cp in pieces(t + 1, 1 - sl)[:2]:
                        cp.start()
                va[sl] = (va[sl].astype(F32) + vb[sl].astype(F32)).astype(BF16)
                pieces(t, sl)[2].start()
            for t in range(max(n_pieces - 2, 0), n_pieces):
                pieces(t, t % 2)[2].wait()
            stream(2).start()

        @pl.when(k == 0)
        def _():
            o_ref[...] = jnp.zeros_like(o_ref)

        for m0 in range(0, tm, mh):
            for n0 in range(0, tn, nc):
                r = None
                for jj in range(kb):
                    t = lax.dot_general(a_ref[m0 : m0 + mh, jj * ne : (jj + 1) * ne], b_ref[jj, n0 : n0 + nc, :], NT, preferred_element_type=F32)
                    r = t if r is None else r + t
                o_ref[m0 : m0 + mh, n0 : n0 + nc] += r

        @pl.when(step == n_steps - 1)
        def _():
            stream(1).wait_recv()
            stream(2).wait_recv()
            for j in range(3):
                stream(j).wait_send()
            _packed_gather_wait(packed_ref, small_send, small_recv)

    hbm = pl.BlockSpec(memory_space=pl.ANY)
    vm = pl.BlockSpec(memory_space=pltpu.VMEM)
    block = jax.ShapeDtypeStruct((r_rows, r_cols), comb.dtype)
    dh, recv, _, _, packed = pl.pallas_call(
        body, name="mm_dh", grid=grid,
        in_specs=[pl.BlockSpec((tm, kb * ne), lambda i, n, k: (i, k)), pl.BlockSpec((kb, tn, ne), lambda i, n, k: (k, n, 0)), hbm] + [vm] * n_small,
        out_specs=[pl.BlockSpec((tm, tn), lambda i, n, k: (i, n)), hbm, hbm, hbm, hbm],
        out_shape=[jax.ShapeDtypeStruct((s, d), F32), jax.ShapeDtypeStruct((2, r_rows, r_cols), comb.dtype), block, block,
                   jax.ShapeDtypeStruct((N_DEV, small_rows, small_width), F32)],
        scratch_shapes=[pltpu.VMEM((2, rc, r_cols), comb.dtype), pltpu.VMEM((2, rc, r_cols), comb.dtype),
                        pltpu.VMEM((small_rows, small_width), F32), pltpu.SemaphoreType.DMA((2, 3)),
                        pltpu.SemaphoreType.DMA((3,)), pltpu.SemaphoreType.DMA((3,)),
                        pltpu.SemaphoreType.DMA((N_DEV - 1,)), pltpu.SemaphoreType.DMA((N_DEV - 1,)), pltpu.SemaphoreType.DMA((1,))],
        compiler_params=_cparams(3),
    )(dproj, w_full, comb, *small_parts)
    return dh, recv, packed


def _mod_matmul(c_all, w_ada, b_loc, tk):
    nb, d = c_all.shape
    na = w_ada.shape[1]

    def body(c_ref, w_ref, b_ref, o_ref):
        k = pl.program_id(0)
        r = jnp.dot(c_ref[...], w_ref[...], preferred_element_type=F32, precision=lax.Precision.HIGHEST)

        @pl.when(k == 0)
        def _():
            o_ref[...] = r + b_ref[...]

        @pl.when(k > 0)
        def _():
            o_ref[...] += r

    return pl.pallas_call(
        body, name="mod_matmul", grid=(d // tk,),
        in_specs=[pl.BlockSpec((nb, tk), lambda k: (0, k)), pl.BlockSpec((tk, na), lambda k: (k, 0)), pl.BlockSpec((1, na), lambda k: (0, 0))],
        out_specs=pl.BlockSpec((nb, na), lambda k: (0, 0)),
        out_shape=jax.ShapeDtypeStruct((nb, na), F32), compiler_params=_cparams(1),
    )(c_all, w_ada, b_loc)


def _col_chunks(width):
    return [slice(c0, c0 + LANES) for c0 in range(0, width, LANES)]


def _row_sum(acc):
    return jnp.sum(acc, axis=1, keepdims=True)


SUBLANES = 8


ROW_BLOCK = 64
DW_BLOCK = 96


def _tap_sum(ext, cs, ts, taps, emit):
    for t0 in range(0, ts, ROW_BLOCK):
        rb = min(ROW_BLOCK, ts - t0)
        acc = None
        for r in range(SUBLANES):
            group = [(o, wv) for o, wv in taps if o % SUBLANES == r]
            if not group:
                continue
            n = rb if r == 0 else rb + SUBLANES
            v = None
            for o, wv in group:
                term = wv * ext[t0 + o - r : t0 + o - r + n, cs]
                v = term if v is None else v + term
            part = v if r == 0 else v[r : r + rb]
            acc = part if acc is None else acc + part
        emit(slice(t0, t0 + rb), acc)


class _SegmentStores:
    def __init__(self, stash, dst, sems, n_seg, step, n_steps, row0, cols):
        self.stash, self.dst, self.sems, self.n_seg = stash, dst, sems, n_seg
        self.step, self.n_steps, self.row0, self.cols = step, n_steps, row0, cols
        self.slot = step % 2

    def _copy(self, slot, g, row0, col0):
        ts, cb = self.stash.shape[2:]
        return pltpu.make_async_copy(self.stash.at[slot, g], self.dst.at[pl.ds(row0, ts), pl.ds(col0, cb)], self.sems.at[slot, g])

    def _wait(self, slot):
        for g in range(self.n_seg):
            self._copy(slot, g, 0, 0).wait()

    def begin(self):
        @pl.when(self.step >= 2)
        def _():
            self._wait(self.slot)

        return self.slot

    def finish(self):
        ts = self.stash.shape[2]
        for g in range(self.n_seg):
            self._copy(self.slot, g, pl.multiple_of(self.row0, ts), pl.multiple_of(self.cols[g], LANES)).start()

        @pl.when(self.step == self.n_steps - 1)
        def _():
            self._wait(self.slot)
            if self.n_steps >= 2:
                self._wait(1 - self.slot)


def _prep(x, g1, scale, shift, ts):
    s, d = x.shape
    chunks = _col_chunks(d)

    def body(x_ref, g_ref, sc_ref, sh_ref, h_ref):
        acc = jnp.zeros((ts, LANES), F32)
        for cs in chunks:
            v = x_ref[:, cs]
            acc = acc + v * v
        r = lax.rsqrt(_row_sum(acc) / d + EPS)
        for cs in chunks:
            n1 = (x_ref[:, cs] * r) * g_ref[:, cs]
            h_ref[:, cs] = (n1 * (1.0 + sc_ref[:, cs]) + sh_ref[:, cs]).astype(BF16)

    row = pl.BlockSpec((1, d), lambda i: (0, 0))
    return pl.pallas_call(
        body, name="prep", grid=(s // ts,),
        in_specs=[pl.BlockSpec((ts, d), lambda i: (i, 0)), row, row, row],
        out_specs=pl.BlockSpec((ts, d), lambda i: (i, 0)),
        out_shape=jax.ShapeDtypeStruct((s, d), BF16), compiler_params=_cparams(1),
    )(x, g1, scale, shift)


def _mix_a_fwd(proj, wa, w, ts, cb):
    s = proj.shape[0]
    nb = w // cb
    chunks = _col_chunks(cb)

    def body(ab, ac, ax, az, wa_ref, y_ref, cv_ref, ext):
        @pl.when(pl.program_id(1) == 0)
        def _():
            ext[0:HALO_A, :] = jnp.zeros((HALO_A, cb), F32)

        for cs in chunks:
            ext[HALO_A : HALO_A + ts, cs] = ac[:, cs].astype(F32) * ax[:, cs].astype(F32)
        for cs in chunks:

            def emit(rows, cv, cs=cs):
                z = az[rows, cs].astype(F32)
                y_ref[rows, cs] = (ab[rows, cs].astype(F32) * cv * (z * _sigmoid(z))).astype(BF16)
                cv_ref[rows, cs] = cv.astype(BF16)

            _tap_sum(ext, cs, ts, [(HALO_A - (CONV_A - 1) + k, wa_ref[k : k + 1, cs]) for k in range(CONV_A)], emit)
        ext[0:HALO_A, :] = ext[ts : ts + HALO_A, :]

    def seg(q):
        return pl.BlockSpec((ts, cb), lambda c, i: (i, q * nb + c))

    return pl.pallas_call(
        body, name="mix_a_fwd", grid=(nb, s // ts),
        in_specs=[seg(0), seg(1), seg(2), seg(3), pl.BlockSpec((CONV_A, cb), lambda c, i: (0, c))],
        out_specs=[pl.BlockSpec((ts, cb), lambda c, i: (i, c)), pl.BlockSpec((ts, cb), lambda c, i: (i, c))],
        out_shape=[jax.ShapeDtypeStruct((s, 2 * w), BF16), jax.ShapeDtypeStruct((s, w), BF16)],
        scratch_shapes=[pltpu.VMEM((HALO_A + ts, cb), F32)], compiler_params=_cparams(2),
    )(proj, proj, proj, proj, wa)


def _mix_b_conv_fwd(proj, wb, bb, w, ts, cb):
    s = proj.shape[0]
    nb = w // cb
    chunks = _col_chunks(cb)

    def body(bv, bg, wb_ref, bb_ref, u2_ref, ext):
        @pl.when(pl.program_id(1) == 0)
        def _():
            ext[0:HALO_B, :] = jnp.zeros((HALO_B, cb), F32)

        for cs in chunks:
            ext[HALO_B : HALO_B + ts, cs] = bv[:, cs].astype(F32) * _sigmoid(bg[:, cs].astype(F32))
        for cs in chunks:

            def emit(rows, acc, cs=cs):
                u2_ref[rows, cs] = acc + bb_ref[:, cs]

            _tap_sum(ext, cs, ts, [(HALO_B - (CONV_B - 1) + k, wb_ref[k : k + 1, cs]) for k in range(CONV_B)], emit)
        ext[0:HALO_B, :] = ext[ts : ts + HALO_B, :]

    return pl.pallas_call(
        body, name="mix_b_conv_fwd", grid=(nb, s // ts),
        in_specs=[pl.BlockSpec((ts, cb), lambda c, i: (i, 4 * nb + c)), pl.BlockSpec((ts, cb), lambda c, i: (i, 5 * nb + c)),
                  pl.BlockSpec((CONV_B, cb), lambda c, i: (0, c)), pl.BlockSpec((1, cb), lambda c, i: (0, c))],
        out_specs=pl.BlockSpec((ts, cb), lambda c, i: (i, c)),
        out_shape=jax.ShapeDtypeStruct((s, w), F32),
        scratch_shapes=[pltpu.VMEM((HALO_B + ts, cb), F32)], compiler_params=_cparams(2),
    )(proj, proj, wb, bb)


def _ln_stats(u2_ref, ts, w, chunks):
    acc = jnp.zeros((ts, LANES), F32)
    for cs in chunks:
        acc = acc + u2_ref[:, cs]
    mu = _row_sum(acc) / w
    acc = jnp.zeros((ts, LANES), F32)
    for cs in chunks:
        xc = u2_ref[:, cs] - mu
        acc = acc + xc * xc
    return mu, lax.rsqrt(_row_sum(acc) / w + EPS)


def _mix_b_gate_fwd(u2, proj, lg, lb, y, ts):
    s, w = u2.shape
    chunks = _col_chunks(w)

    def body(u2_ref, bz, lg_ref, lb_ref, y_in, y_ref):
        del y_in
        mu, rstd = _ln_stats(u2_ref, ts, w, chunks)
        for cs in chunks:
            ln = (u2_ref[:, cs] - mu) * rstd * lg_ref[:, cs] + lb_ref[:, cs]
            z = bz[:, cs].astype(F32)
            y_ref[:, cs] = ((ln * _sigmoid(ln)) * (z * _sigmoid(z))).astype(BF16)

    row = pl.BlockSpec((1, w), lambda i: (0, 0))
    return pl.pallas_call(
        body, name="mix_b_gate_fwd", grid=(s // ts,),
        in_specs=[pl.BlockSpec((ts, w), lambda i: (i, 0)), pl.BlockSpec((ts, w), lambda i: (i, 6)), row, row,
                  pl.BlockSpec(memory_space=pl.ANY)],
        out_specs=pl.BlockSpec((ts, w), lambda i: (i, 1)),
        out_shape=jax.ShapeDtypeStruct(y.shape, y.dtype), input_output_aliases={4: 0}, compiler_params=_cparams(1),
    )(u2, proj, lg, lb, y)


def _post(x, out, tgt, gate, fg, ts):
    s, d = x.shape
    chunks = _col_chunks(d)

    def body(x_ref, o_ref, t_ref, gt_ref, fg_ref, dx2_ref, dout_ref, loss_ref, dgate_ref, dfg_ref, keep):
        @pl.when(pl.program_id(0) == 0)
        def _():
            loss_ref[...] = jnp.zeros_like(loss_ref)
            dgate_ref[...] = jnp.zeros_like(dgate_ref)
            dfg_ref[...] = jnp.zeros_like(dfg_ref)

        acc = jnp.zeros((ts, LANES), F32)
        for cs in chunks:
            x2 = x_ref[:, cs] + gt_ref[:, cs] * o_ref[:, cs]
            keep[:, cs] = x2
            acc = acc + x2 * x2
        r2 = lax.rsqrt(_row_sum(acc) / d + EPS)
        acc = jnp.zeros((ts, LANES), F32)
        for cs in chunks:
            n2 = keep[:, cs] * r2
            keep[:, cs] = n2
            diff = n2 * fg_ref[:, cs] - t_ref[:, cs]
            loss_ref[:, cs] += jnp.sum(diff * diff, axis=0, keepdims=True)
            dyf = diff / d
            dfg_ref[:, cs] += jnp.sum(dyf * n2, axis=0, keepdims=True)
            dn = dyf * fg_ref[:, cs]
            dx2_ref[:, cs] = dn
            acc = acc + dn * n2
        mdot = _row_sum(acc) / d
        for cs in chunks:
            dx2 = r2 * (dx2_ref[:, cs] - keep[:, cs] * mdot)
            dx2_ref[:, cs] = dx2
            dgate_ref[:, cs] += jnp.sum(dx2 * o_ref[:, cs], axis=0, keepdims=True)
            dout_ref[:, cs] = (dx2 * gt_ref[:, cs]).astype(BF16)

    tile = pl.BlockSpec((ts, d), lambda i: (i, 0))
    row = pl.BlockSpec((1, d), lambda i: (0, 0))
    return pl.pallas_call(
        body, name="post", grid=(s // ts,),
        in_specs=[tile, tile, tile, row, row], out_specs=[tile, tile, row, row, row],
        out_shape=[jax.ShapeDtypeStruct((s, d), F32), jax.ShapeDtypeStruct((s, d), BF16)] + [jax.ShapeDtypeStruct((1, d), F32)] * 3,
        scratch_shapes=[pltpu.VMEM((ts, d), F32)], compiler_params=_cparams(1),
    )(x, out, tgt, gate, fg)


def _mix_b_gate_bwd(u2, proj, dy, lg, lb, n_cols, ts):
    s, w = u2.shape
    chunks = _col_chunks(w)

    def body(u2_ref, bz, dyb, lg_ref, lb_ref, du2_ref, dbz_ref, dlg_ref, dlb_ref, dbb_ref):
        @pl.when(pl.program_id(0) == 0)
        def _():
            dlg_ref[...] = jnp.zeros_like(dlg_ref)
            dlb_ref[...] = jnp.zeros_like(dlb_ref)
            dbb_ref[...] = jnp.zeros_like(dbb_ref)

        mu, rstd = _ln_stats(u2_ref, ts, w, chunks)
        acc1 = jnp.zeros((ts, LANES), F32)
        acc2 = jnp.zeros((ts, LANES), F32)
        for cs in chunks:
            xh = (u2_ref[:, cs] - mu) * rstd
            ln = xh * lg_ref[:, cs] + lb_ref[:, cs]
            sl = _sigmoid(ln)
            z = bz[:, cs].astype(F32)
            sz = _sigmoid(z)
            g = dyb[:, cs].astype(F32)
            dbz_ref[:, cs] = (g * (ln * sl) * (sz * (1.0 + z * (1.0 - sz)))).astype(BF16)
            dln = g * (z * sz) * (sl * (1.0 + ln * (1.0 - sl)))
            dlg_ref[:, cs] += jnp.sum(dln * xh, axis=0, keepdims=True)
            dlb_ref[:, cs] += jnp.sum(dln, axis=0, keepdims=True)
            dxh = dln * lg_ref[:, cs]
            du2_ref[:, cs] = dxh
            acc1 = acc1 + dxh
            acc2 = acc2 + dxh * xh
        m1 = _row_sum(acc1) / w
        m2 = _row_sum(acc2) / w
        for cs in chunks:
            xh = (u2_ref[:, cs] - mu) * rstd
            du2 = rstd * (du2_ref[:, cs] - m1 - xh * m2)
            du2_ref[:, cs] = du2
            dbb_ref[:, cs] += jnp.sum(du2, axis=0, keepdims=True)

    tile = pl.BlockSpec((ts, w), lambda i: (i, 0))
    row = pl.BlockSpec((1, w), lambda i: (0, 0))
    return pl.pallas_call(
        body, name="mix_b_gate_bwd", grid=(s // ts,),
        in_specs=[tile, pl.BlockSpec((ts, w), lambda i: (i, 6)), pl.BlockSpec((ts, w), lambda i: (i, 1)), row, row],
        out_specs=[tile, pl.BlockSpec((ts, w), lambda i: (i, 6)), row, row, row],
        out_shape=[jax.ShapeDtypeStruct((s, w), F32), jax.ShapeDtypeStruct((s, n_cols), BF16)] + [jax.ShapeDtypeStruct((1, w), F32)] * 3,
        compiler_params=_cparams(1),
    )(u2, proj, dy, lg, lb)


def _mix_b_conv_bwd(du2, proj, wb, dproj, comb, w, ts, cb):
    s = proj.shape[0]
    nb = w // cb
    ns = s // ts
    chunks = _col_chunks(cb)

    u_rows = ts + 2 * HALO_B + SUBLANES
    win = ts + HALO_B
    assert win % DW_BLOCK == 0

    n_seg = 2

    def body(du2_ref, bv, bg, wb_ref, dp_in, comb_ref, dp_ref, dwb_ref, recv_ref, ext, stash, uext, sems, send_sems, recv_sems):
        del dp_in
        c, i = pl.program_id(0), pl.program_id(1)
        stores = _SegmentStores(stash, dp_ref, sems, n_seg, c * ns + i, nb * ns, (ns - 1 - i) * ts, [((4 + g) * nb + c) * cb for g in range(n_seg)])
        slot = stores.begin()

        def copies():
            return _chip_copies([comb_ref], [recv_ref], send_sems, recv_sems)

        @pl.when((c == 0) & (i == 0))
        def _():
            for cp in copies():
                cp.start()

        @pl.when(i == 0)
        def _():
            ext[ts : ts + HALO_B, :] = jnp.zeros((HALO_B, cb), F32)
            uext[0:HALO_B, :] = jnp.zeros((HALO_B, cb), F32)
            uext[HALO_B + ts : u_rows, :] = jnp.zeros((u_rows - HALO_B - ts, cb), F32)
            dwb_ref[...] = jnp.zeros_like(dwb_ref)

        ext[0:ts, :] = du2_ref[...]
        for cs in chunks:
            uext[HALO_B : HALO_B + ts, cs] = bv[:, cs].astype(F32) * _sigmoid(bg[:, cs].astype(F32))

            def emit(rows, du, cs=cs):
                v = bv[rows, cs].astype(F32)
                sg = _sigmoid(bg[rows, cs].astype(F32))
                stash[slot, 0, rows, cs] = (du * sg).astype(BF16)
                stash[slot, 1, rows, cs] = (du * v * (sg * (1.0 - sg))).astype(BF16)

            _tap_sum(ext, cs, ts, [(CONV_B - 1 - k, wb_ref[k : k + 1, cs]) for k in range(CONV_B)], emit)
            for r in range(SUBLANES):
                ks = [k for k in range(CONV_B) if (CONV_B - 1 - k) % SUBLANES == r]
                accs = [jnp.zeros((SUBLANES, LANES), F32) for _ in ks]
                for t0 in range(0, win, DW_BLOCK):
                    e = ext[t0 : t0 + DW_BLOCK, cs]
                    ush = uext[SUBLANES - r + t0 : SUBLANES - r + t0 + DW_BLOCK + HALO_B - SUBLANES, cs]
                    for n, k in enumerate(ks):
                        b0 = HALO_B - SUBLANES - (CONV_B - 1 - k - r)
                        prod = e * ush[b0 : b0 + DW_BLOCK]
                        for v0 in range(0, DW_BLOCK, SUBLANES):
                            accs[n] = accs[n] + prod[v0 : v0 + SUBLANES]
                for n, k in enumerate(ks):
                    dwb_ref[k : k + 1, cs] += jnp.sum(accs[n], axis=0, keepdims=True)
        ext[ts : ts + HALO_B, :] = ext[0:HALO_B, :]
        stores.finish()

        @pl.when((c == nb - 1) & (i == ns - 1))
        def _():
            for cp in copies():
                cp.wait_recv()
            for cp in copies():
                cp.wait_send()

    def rev(col):
        return lambda c, i: (ns - 1 - i, col(c))

    hbm = pl.BlockSpec(memory_space=pl.ANY)
    return pl.pallas_call(
        body, name="mix_b_conv_bwd", grid=(nb, ns),
        in_specs=[pl.BlockSpec((ts, cb), rev(lambda c: c)), pl.BlockSpec((ts, cb), rev(lambda c: 4 * nb + c)),
                  pl.BlockSpec((ts, cb), rev(lambda c: 5 * nb + c)), pl.BlockSpec((CONV_B, cb), lambda c, i: (0, c)), hbm, hbm],
        out_specs=[hbm, pl.BlockSpec((CONV_B, cb), lambda c, i: (0, c)), hbm],
        out_shape=[jax.ShapeDtypeStruct(dproj.shape, dproj.dtype), jax.ShapeDtypeStruct((CONV_B, w), F32),
                   jax.ShapeDtypeStruct((3,) + comb.shape[1:], comb.dtype)],
        input_output_aliases={4: 0},
        scratch_shapes=[pltpu.VMEM((ts + HALO_B, cb), F32), pltpu.VMEM((2, n_seg, ts, cb), BF16), pltpu.VMEM((u_rows, cb), F32),
                        pltpu.SemaphoreType.DMA((2, n_seg)), pltpu.SemaphoreType.DMA((1, 3)), pltpu.SemaphoreType.DMA((1, 3))],
        compiler_params=_cparams(2),
    )(du2, proj, proj, wb, dproj, comb)


def _mix_a_bwd(proj, cv, dy, wa, dproj, w, ts, cb):
    s = proj.shape[0]
    nb = w // cb
    ns = s // ts
    chunks = _col_chunks(cb)

    n_seg = 4

    def body(ab, ac, ax, az, cv_ref, dya, wa_ref, dp_in, dp_ref, dwa_ref, ext, stash, sems):
        del dp_in
        c, i = pl.program_id(0), pl.program_id(1)
        stores = _SegmentStores(stash, dp_ref, sems, n_seg, c * ns + i, nb * ns, (ns - 1 - i) * ts, [(g * nb + c) * cb for g in range(n_seg)])
        slot = stores.begin()

        @pl.when(i == 0)
        def _():
            ext[ts : ts + HALO_A, :] = jnp.zeros((HALO_A, cb), F32)
            dwa_ref[...] = jnp.zeros_like(dwa_ref)

        for cs in chunks:
            b = ab[:, cs].astype(F32)
            z = az[:, cs].astype(F32)
            sz = _sigmoid(z)
            g = dya[:, cs].astype(F32)
            conv = cv_ref[:, cs].astype(F32)
            stash[slot, 0, :, cs] = (g * conv * (z * sz)).astype(BF16)
            stash[slot, 3, :, cs] = (g * b * conv * (sz * (1.0 + z * (1.0 - sz)))).astype(BF16)
            ext[0:ts, cs] = g * b * (z * sz)
        for cs in chunks:

            def emit(rows, dca, cs=cs):
                stash[slot, 1, rows, cs] = (dca * ax[rows, cs].astype(F32)).astype(BF16)
                stash[slot, 2, rows, cs] = (dca * ac[rows, cs].astype(F32)).astype(BF16)

            _tap_sum(ext, cs, ts, [(CONV_A - 1 - k, wa_ref[k : k + 1, cs]) for k in range(CONV_A)], emit)
            ca = ac[:, cs].astype(F32) * ax[:, cs].astype(F32)
            for k in range(CONV_A):
                o = CONV_A - 1 - k
                dwa_ref[k : k + 1, cs] += jnp.sum(ca * ext[o : o + ts, cs], axis=0, keepdims=True)
        ext[ts : ts + HALO_A, :] = ext[0:HALO_A, :]
        stores.finish()

    def rev(col):
        return lambda c, i: (ns - 1 - i, col(c))

    def seg(sq):
        return pl.BlockSpec((ts, cb), rev(lambda c: sq * nb + c))

    hbm = pl.BlockSpec(memory_space=pl.ANY)
    return pl.pallas_call(
        body, name="mix_a_bwd", grid=(nb, ns),
        in_specs=[seg(0), seg(1), seg(2), seg(3), pl.BlockSpec((ts, cb), rev(lambda c: c)), pl.BlockSpec((ts, cb), rev(lambda c: c)),
                  pl.BlockSpec((CONV_A, cb), lambda c, i: (0, c)), hbm],
        out_specs=[hbm, pl.BlockSpec((CONV_A, cb), lambda c, i: (0, c))],
        out_shape=[jax.ShapeDtypeStruct(dproj.shape, dproj.dtype), jax.ShapeDtypeStruct((CONV_A, w), F32)],
        input_output_aliases={7: 0},
        scratch_shapes=[pltpu.VMEM((ts + HALO_A, cb), F32), pltpu.VMEM((2, n_seg, ts, cb), BF16), pltpu.SemaphoreType.DMA((2, n_seg))],
        compiler_params=_cparams(2),
    )(proj, proj, proj, proj, cv, dy, wa, dproj)


def _pre_bwd(x, dh, dx2, g1, scale, ts):
    s, d = x.shape
    chunks = _col_chunks(d)

    def body(x_ref, dh_ref, dx2_ref, g_ref, sc_ref, gx_ref, dsh_ref, dsc_ref, dg_ref, keep):
        @pl.when(pl.program_id(0) == 0)
        def _():
            dsh_ref[...] = jnp.zeros_like(dsh_ref)
            dsc_ref[...] = jnp.zeros_like(dsc_ref)
            dg_ref[...] = jnp.zeros_like(dg_ref)

        acc = jnp.zeros((ts, LANES), F32)
        for cs in chunks:
            v = x_ref[:, cs]
            acc = acc + v * v
        r1 = lax.rsqrt(_row_sum(acc) / d + EPS)
        acc = jnp.zeros((ts, LANES), F32)
        for cs in chunks:
            xn = x_ref[:, cs] * r1
            keep[:, cs] = xn
            g = dh_ref[:, cs]
            dsh_ref[:, cs] += jnp.sum(g, axis=0, keepdims=True)
            dsc_ref[:, cs] += jnp.sum(g * (xn * g_ref[:, cs]), axis=0, keepdims=True)
            dn1 = g * (1.0 + sc_ref[:, cs])
            dg_ref[:, cs] += jnp.sum(dn1 * xn, axis=0, keepdims=True)
            dxn = dn1 * g_ref[:, cs]
            gx_ref[:, cs] = dxn
            acc = acc + dxn * xn
        mdot = _row_sum(acc) / d
        for cs in chunks:
            gx_ref[:, cs] = dx2_ref[:, cs] + r1 * (gx_ref[:, cs] - keep[:, cs] * mdot)

    tile = pl.BlockSpec((ts, d), lambda i: (i, 0))
    row = pl.BlockSpec((1, d), lambda i: (0, 0))
    return pl.pallas_call(
        body, name="pre_bwd", grid=(s // ts,),
        in_specs=[tile, tile, tile, row, row], out_specs=[tile, row, row, row],
        out_shape=[jax.ShapeDtypeStruct((s, d), F32)] + [jax.ShapeDtypeStruct((1, d), F32)] * 3,
        scratch_shapes=[pltpu.VMEM((ts, d), F32)], compiler_params=_cparams(1),
    )(x, dh, dx2, g1, scale)


def _adamw(w, g, m, v):
    m = ADAM_B1 * m + (1.0 - ADAM_B1) * g
    v = ADAM_B2 * v + (1.0 - ADAM_B2) * (g * g)
    m_hat = m / (1.0 - ADAM_B1**ADAM_STEP)
    v_hat = v / (1.0 - ADAM_B2**ADAM_STEP)
    delta = -ADAM_LR * (m_hat / (jnp.sqrt(v_hat) + ADAM_EPS) + ADAM_WD * w)
    return delta, m, v


def _adam_sharded(w, m, v, comb, recv, chip, tr, name):
    r, n = w.shape
    nr = recv.shape[0]

    def body(chip_ref, w_ref, m_ref, v_ref, c_ref, r_ref, g_ref, d_ref, nm_ref, nv_ref):
        del chip_ref
        g = c_ref[...].astype(F32)
        for j in range(nr):
            g = g + r_ref[j].astype(F32)
        delta, nm, nv = _adamw(w_ref[...], g, m_ref[...], v_ref[...])
        g_ref[...] = g
        d_ref[...] = delta
        nm_ref[...] = nm
        nv_ref[...] = nv

    tile = pl.BlockSpec((tr, n), lambda i, ch: (i, 0))
    return pl.pallas_call(
        body, name=name,
        grid_spec=pltpu.PrefetchScalarGridSpec(
            num_scalar_prefetch=1, grid=(r // tr,),
            in_specs=[tile, tile, tile, pl.BlockSpec((None, tr, n), lambda i, ch: (ch[0], i, 0)), pl.BlockSpec((nr, tr, n), lambda i, ch: (0, i, 0))],
            out_specs=[tile] * 4),
        out_shape=[jax.ShapeDtypeStruct((r, n), F32)] * 4, compiler_params=_cparams(1),
    )(chip, w, m, v, comb, recv)


def _adam_w_ada(c_t, dm_loc, w, m, v, tr):
    d, nb = c_t.shape
    na = w.shape[1]

    def body(c_ref, dm_ref, w_ref, m_ref, v_ref, g_ref, d_ref, nm_ref, nv_ref):
        g = jnp.dot(c_ref[...], dm_ref[...], preferred_element_type=F32, precision=lax.Precision.HIGHEST)
        delta, nm, nv = _adamw(w_ref[...], g, m_ref[...], v_ref[...])
        g_ref[...] = g
        d_ref[...] = delta
        nm_ref[...] = nm
        nv_ref[...] = nv

    tile = pl.BlockSpec((tr, na), lambda i: (i, 0))
    return pl.pallas_call(
        body, name="adam_w_ada", grid=(d // tr,),
        in_specs=[pl.BlockSpec((tr, nb), lambda i: (i, 0)), pl.BlockSpec((nb, na), lambda i: (0, 0)), tile, tile, tile],
        out_specs=[tile] * 4, out_shape=[jax.ShapeDtypeStruct((d, na), F32)] * 4, compiler_params=_cparams(1),
    )(c_t, dm_loc, w, m, v)


def _small_final(me, packed, starts, late, late_starts, params, d, w, cl):
    flat_params = [t for p in params for t in p]
    kd = d // w
    r_loss, r_dfg, r_dgt, r_dbb, r_dlg, r_dlb, r_dwa, r_dwb = starts
    r_dg1, r_dsh, r_dsc = late_starts

    def total(ref, r0, rows):
        acc = ref[0, r0 : r0 + rows, :]
        for b in range(1, N_DEV):
            acc = acc + ref[b, r0 : r0 + rows, :]
        return acc

    def body(me_ref, full_ref, mine_ref, late_ref, *rest):
        del me_ref
        prm = rest[:24]
        loss_ref = rest[24]
        outs = rest[25:]

        loss_rows = jnp.sum(total(full_ref, r_loss, kd), axis=1, keepdims=True)
        loss_ref[...] = jnp.sum(loss_rows, axis=0, keepdims=True) * (0.5 / d)

        def update(idx, g, col0=0):
            w_ref, m_ref, v_ref = prm[3 * idx : 3 * idx + 3]
            g_ref, d_ref, nm_ref, nv_ref = outs[4 * idx : 4 * idx + 4]
            sl = (slice(None), slice(col0, col0 + g.shape[1]))
            delta, nm, nv = _adamw(w_ref[sl], g, m_ref[sl], v_ref[sl])
            g_ref[sl] = g
            d_ref[sl] = delta
            nm_ref[sl] = nm
            nv_ref[sl] = nv

        def update_wide(idx, ref, r0, col0=0):
            for j in range(kd):
                update(idx, total(ref, r0 + j, 1), col0 + j * w)

        update_wide(0, late_ref, r_dg1)
        update_wide(1, late_ref, r_dsh, 0)
        update_wide(1, late_ref, r_dsc, d)
        update_wide(1, full_ref, r_dgt, 2 * d)
        update(2, total(mine_ref, r_dwa, CONV_A))
        update(3, total(mine_ref, r_dwb, CONV_B))
        update(4, total(full_ref, r_dbb, 1))
        update(5, total(full_ref, r_dlg, 1))
        update(6, total(full_ref, r_dlb, 1))
        update_wide(7, full_ref, r_dfg)

    def full(shape):
        nd = len(shape)
        return pl.BlockSpec(shape, lambda i, mr: (0,) * nd)

    n_rows = packed.shape[1]
    in_specs = [full(packed.shape), pl.BlockSpec((N_DEV, n_rows, cl), lambda i, mr: (0, 0, mr[0])), full(late.shape)]
    in_specs += [full(t.shape) for t in flat_params]
    out_shapes = [jax.ShapeDtypeStruct((1, 1), F32)]
    for p in params:
        out_shapes += [jax.ShapeDtypeStruct(p[0].shape, F32)] * 4
    return pl.pallas_call(
        body, name="small_final",
        grid_spec=pltpu.PrefetchScalarGridSpec(
            num_scalar_prefetch=1, grid=(1,), in_specs=in_specs, out_specs=[full(o.shape) for o in out_shapes]),
        out_shape=out_shapes, compiler_params=_cparams(1),
    )(me, packed, packed, late, *flat_params)


def _silu_rows(c):
    def body(c_ref, o_ref):
        v = c_ref[...]
        o_ref[...] = v * _sigmoid(v)

    return pl.pallas_call(body, name="silu_c", out_shape=jax.ShapeDtypeStruct(c.shape, F32), compiler_params=_cparams())(c)


def kernel(x, c, norm_g, w_ada, b_ada, w_in, conv_a_w, conv_b_w, conv_b_b, ln_b_g, ln_b_b, w_out, final_g, loss_target, m_norm_g, m_w_ada, m_b_ada, m_w_in, m_conv_a_w, m_conv_b_w, m_conv_b_b, m_ln_b_g, m_ln_b_b, m_w_out, m_final_g, v_norm_g, v_w_ada, v_b_ada, v_w_in, v_conv_a_w, v_conv_b_w, v_conv_b_b, v_ln_b_g, v_ln_b_b, v_w_out, v_final_g):
    _, s, d = x.shape
    w = conv_b_b.shape[-1]
    cl = conv_a_w.shape[-1]
    na = w_ada.shape[-1]
    assert ln_b_g.shape[-1] == w and w_out.shape[1] * N_DEV == 2 * w and w_in.shape[-1] * N_DEV == 7 * w and cl * N_DEV == w
    ts = min(256, s)
    ts_a = min(512, s)
    ts_row = min(128, s)
    cb = min(512, w)
    tm = min(512, s)
    tm_big = min(1024, s)
    tn = min(1024, d)

    px, py, pc = _position()
    me = 4 * px + 2 * py + pc
    me_arr = jnp.reshape(me, (1,)).astype(jnp.int32)
    chip_arr = jnp.reshape(2 * px + py, (1,)).astype(jnp.int32)

    x2d, tgt = x[0], loss_target[0]
    w_ada2, m_w_ada2, v_w_ada2 = w_ada[0], m_w_ada[0], v_w_ada[0]
    w_in2, m_w_in2, v_w_in2 = w_in[0], m_w_in[0], v_w_in[0]
    w_out2, m_w_out2, v_w_out2 = w_out[0], m_w_out[0], v_w_out[0]
    fg = final_g.reshape(1, d)

    c_act = _silu_rows(c)
    small_in = [c_act, conv_a_w[0], conv_b_w[0]]
    (r_c, r_wa, r_wb), _ = _packed_rows(small_in, cl)
    packed_in = _all_gather_packed(small_in, cl, "gather_small_in")
    c_all = packed_in[:, r_c : r_c + d // cl, :].reshape(N_DEV, d)
    wa = jnp.transpose(packed_in[:, r_wa : r_wa + CONV_A, :], (1, 0, 2)).reshape(CONV_A, w)
    wb = jnp.transpose(packed_in[:, r_wb : r_wb + CONV_B, :], (1, 0, 2)).reshape(CONV_B, w)
    b_loc = lax.dynamic_slice_in_dim(b_ada, me * na, na, axis=1)
    mod_loc = _mod_matmul(c_all, w_ada2, b_loc, min(512, d))
    (mod_all,) = _all_gather_vmem([mod_loc], "gather_mod")
    mod_mine = lax.dynamic_index_in_dim(mod_all, me, axis=1, keepdims=False).reshape(1, 3 * d)
    shift, scale, gate = mod_mine[:, :d], mod_mine[:, d : 2 * d], mod_mine[:, 2 * d :]

    h = _prep(x2d, norm_g, scale, shift, ts_row)
    proj, w_in_full, w_out_full = _proj_gather(h, w_in2.astype(BF16), w_out2.astype(BF16), _arrival_order(), tm)
    w_out_full = w_out_full.reshape(2 * w, d)
    y, cv = _mix_a_fwd(proj, wa, w, ts_a, cb)
    u2 = _mix_b_conv_fwd(proj, wb, conv_b_b, w, ts, cb)
    y = _mix_b_gate_fwd(u2, proj, ln_b_g, ln_b_b, y, ts)
    out = _mm_out(y, w_out_full, tm_big, tn)
    dx2, dout, loss_row, dgate_row, dfg_row = _post(x2d, out, tgt, gate, fg, ts_row)

    dy = _mm_dy(dout, w_out_full, tm_big, tn)
    comb_out = _mm_dw_exchange(y, dout, _dw_targets(), w_out2.shape[0], tn, True, "mm_dw_out")
    du2, dproj, dlg_row, dlb_row, dbb_row = _mix_b_gate_bwd(u2, proj, dy, ln_b_g, ln_b_b, 7 * w, ts)
    dproj, dwb_part, recv_out = _mix_b_conv_bwd(du2, proj, wb, dproj, comb_out, w, ts, cb)
    dproj, dwa_part = _mix_a_bwd(proj, cv, dy, wa, dproj, w, ts_a, cb)
    comb_in = _mm_dw_exchange(h, dproj, _dw_targets(), tm, w_in2.shape[1], False, "mm_dw_in")
    small_parts = [loss_row, dfg_row, dgate_row, dbb_row, dlg_row, dlb_row, dwa_part, dwb_part]
    starts, _ = _packed_rows(small_parts, w)
    dh, recv_in, packed = _mm_dh(dproj, w_in_full, comb_in, small_parts, w, min(1024, s), tn)
    grad_x, dshift_row, dscale_row, dg1_row = _pre_bwd(x2d, dh, dx2, norm_g, scale, ts_row)
    g_w_in, d_w_in, nm_w_in, nv_w_in = _adam_sharded(w_in2, m_w_in2, v_w_in2, comb_in, recv_in, chip_arr, min(128, d), "adam_w_in")
    g_w_out, d_w_out, nm_w_out, nv_w_out = _adam_sharded(w_out2, m_w_out2, v_w_out2, comb_out, recv_out, chip_arr, min(128, w_out2.shape[0]), "adam_w_out")

    late_parts = [dg1_row, dshift_row, dscale_row]
    late_starts, _ = _packed_rows(late_parts, w)
    late = _all_gather_packed(late_parts, w, "gather_small_grads")
    kd = d // w

    def gathered_rows(buf, r0):
        return buf[:, r0 : r0 + kd, :].reshape(N_DEV, d)

    dmod_all = jnp.concatenate([gathered_rows(late, late_starts[1]), gathered_rows(late, late_starts[2]), gathered_rows(packed, starts[2])], axis=1)
    dm_loc = lax.dynamic_slice_in_dim(dmod_all, me * na, na, axis=1)
    g_w_ada, d_w_ada, nm_w_ada, nv_w_ada = _adam_w_ada(jnp.transpose(c_all), dm_loc, w_ada2, m_w_ada2, v_w_ada2, min(256, d))

    params = [
        (norm_g, m_norm_g, v_norm_g), (b_ada, m_b_ada, v_b_ada), (conv_a_w[0], m_conv_a_w[0], v_conv_a_w[0]),
        (conv_b_w[0], m_conv_b_w[0], v_conv_b_w[0]), (conv_b_b, m_conv_b_b, v_conv_b_b), (ln_b_g, m_ln_b_g, v_ln_b_g),
        (ln_b_b, m_ln_b_b, v_ln_b_b), (fg, m_final_g.reshape(1, d), v_final_g.reshape(1, d)),
    ]
    small = _small_final(me_arr, packed, starts, late, late_starts, params, d, w, cl)
    loss = small[0].reshape(())
    sm = [small[1 + 4 * i : 5 + 4 * i] for i in range(8)]

    def pick(k):
        return [
            sm[0][k], (g_w_ada, d_w_ada, nm_w_ada, nv_w_ada)[k][None], sm[1][k], (g_w_in, d_w_in, nm_w_in, nv_w_in)[k][None],
            sm[2][k][None], sm[3][k][None], sm[4][k], sm[5][k], sm[6][k], (g_w_out, d_w_out, nm_w_out, nv_w_out)[k][None],
            sm[7][k].reshape(d),
        ]

    return (loss, grad_x[None], *pick(0), *pick(1), *pick(2), *pick(3))
```

```python
import functools

import jax
import jax.numpy as jnp
from jax import lax
from jax.experimental import pallas as pl
from jax.experimental.pallas import tpu as pltpu

F32 = jnp.float32
BF16 = jnp.bfloat16
N_DEV = 8
N_CHIP = 4
EPS = 1e-6
ADAM_LR = 0.001
ADAM_B1 = 0.9
ADAM_B2 = 0.999
ADAM_EPS = 1e-08
ADAM_WD = 0.01
ADAM_STEP = 10
CONV_A = 3
CONV_B = 31
HALO_A = 8
HALO_B = 32
LANES = 128
VMEM_LIMIT = 56 * 1024 * 1024
MESH = pl.DeviceIdType.MESH


def _cparams(n_grid_axes=0):
    if n_grid_axes:
        return pltpu.CompilerParams(dimension_semantics=("arbitrary",) * n_grid_axes, vmem_limit_bytes=VMEM_LIMIT)
    return pltpu.CompilerParams(vmem_limit_bytes=VMEM_LIMIT)


def _flip(v, bit):
    return 1 - v if bit else v


def _position():
    return lax.axis_index("x"), lax.axis_index("y"), lax.axis_index("c")


def _sigmoid(v):
    return jax.nn.sigmoid(v)


def _all_gather_vmem(arrs, name):
    n = len(arrs)

    def body(*refs):
        ins, outs = refs[:n], refs[n : 2 * n]
        send_sems, recv_sems = refs[2 * n :]
        x, y, c = _position()
        me = 4 * x + 2 * y + c
        for a in range(n):
            outs[a][me] = ins[a][...]
        sends = []
        for k in range(1, N_DEV):
            peer = (_flip(x, k & 4), _flip(y, k & 2), _flip(c, k & 1))
            for a in range(n):
                cp = pltpu.make_async_remote_copy(
                    src_ref=ins[a], dst_ref=outs[a].at[me], send_sem=send_sems.at[a, k - 1], recv_sem=recv_sems.at[a, k - 1],
                    device_id=peer, device_id_type=MESH)
                cp.start()
                sends.append(cp)
        for k in range(1, N_DEV):
            peer = (_flip(x, k & 4), _flip(y, k & 2), _flip(c, k & 1))
            src = 4 * peer[0] + 2 * peer[1] + peer[2]
            for a in range(n):
                pltpu.make_async_remote_copy(
                    src_ref=ins[a], dst_ref=outs[a].at[src], send_sem=send_sems.at[a, k - 1], recv_sem=recv_sems.at[a, k - 1],
                    device_id=peer, device_id_type=MESH).wait_recv()
        for cp in sends:
            cp.wait_send()

    vm = pl.BlockSpec(memory_space=pltpu.VMEM)
    return pl.pallas_call(
        body, name=name,
        out_shape=[jax.ShapeDtypeStruct((N_DEV,) + a.shape, a.dtype) for a in arrs],
        in_specs=[vm] * n, out_specs=[vm] * n,
        scratch_shapes=[pltpu.SemaphoreType.DMA((n, N_DEV - 1)), pltpu.SemaphoreType.DMA((n, N_DEV - 1))],
        compiler_params=_cparams(),
    )(*arrs)


def _packed_rows(arrs, width):
    starts, r = [], 0
    for a in arrs:
        starts.append(r)
        r += a.shape[0] * (a.shape[1] // width)
    return starts, -(-r // SUBLANES) * SUBLANES


def _packed_copies(out, send_sems, recv_sems, as_receiver):
    x, y, c = _position()
    me = 4 * x + 2 * y + c
    copies = []
    for k in range(1, N_DEV):
        peer = (_flip(x, k & 4), _flip(y, k & 2), _flip(c, k & 1))
        block = 4 * peer[0] + 2 * peer[1] + peer[2] if as_receiver else me
        copies.append(pltpu.make_async_remote_copy(
            src_ref=out.at[me], dst_ref=out.at[block], send_sem=send_sems.at[k - 1], recv_sem=recv_sems.at[k - 1],
            device_id=peer, device_id_type=MESH))
    return copies


def _packed_gather_start(ins, out, send_sems, recv_sems, stage=None, stage_sem=None):
    _, n_rows, width = out.shape
    x, y, c = _position()
    me = 4 * x + 2 * y + c
    r = 0
    for ref in ins:
        rows_a, cols_a = ref.shape
        for j in range(cols_a // width):
            rows = ref[:, j * width : (j + 1) * width]
            if stage is None:
                out[me, r : r + rows_a, :] = rows
            else:
                stage[r : r + rows_a, :] = rows
            r += rows_a
    if r < n_rows:
        pad = jnp.zeros((n_rows - r, width), F32)
        if stage is None:
            out[me, r:n_rows, :] = pad
        else:
            stage[r:n_rows, :] = pad
    if stage is not None:
        mine = pltpu.make_async_copy(stage, out.at[me], stage_sem)
        mine.start()
        mine.wait()
    for cp in _packed_copies(out, send_sems, recv_sems, False):
        cp.start()


def _packed_gather_wait(out, send_sems, recv_sems):
    for cp in _packed_copies(out, send_sems, recv_sems, True):
        cp.wait_recv()
    for cp in _packed_copies(out, send_sems, recv_sems, False):
        cp.wait_send()


def _all_gather_packed(arrs, width, name):
    n = len(arrs)
    _, n_rows = _packed_rows(arrs, width)

    def body(*refs):
        ins, out = refs[:n], refs[n]
        send_sems, recv_sems = refs[n + 1 :]
        _packed_gather_start(ins, out, send_sems, recv_sems)
        _packed_gather_wait(out, send_sems, recv_sems)

    vm = pl.BlockSpec(memory_space=pltpu.VMEM)
    return pl.pallas_call(
        body, name=name, out_shape=jax.ShapeDtypeStruct((N_DEV, n_rows, width), F32),
        in_specs=[vm] * n, out_specs=vm,
        scratch_shapes=[pltpu.SemaphoreType.DMA((N_DEV - 1,)), pltpu.SemaphoreType.DMA((N_DEV - 1,))],
        compiler_params=_cparams(),
    )(*arrs)


NN = (((1,), (0,)), ((), ()))
NT = (((1,), (1,)), ((), ()))
TN = (((0,), (0,)), ((), ()))


def _n_chunk(n):
    return 256 if n % 256 == 0 else LANES


def _mm_body(dims, m_out, n_out):
    nc = _n_chunk(n_out)
    mh = min(512, m_out)

    def body(a_ref, b_ref, o_ref):
        for m0 in range(0, m_out, mh):
            for n0 in range(0, n_out, nc):
                b = b_ref[n0 : n0 + nc, :] if dims is NT else b_ref[:, n0 : n0 + nc]
                o_ref[m0 : m0 + mh, n0 : n0 + nc] = lax.dot_general(
                    a_ref[m0 : m0 + mh, :], b, dims, preferred_element_type=F32).astype(o_ref.dtype)

    return body


def _ring_chips():
    x, y, c = _position()
    first = (x + (1 - c) - 2 * x * (1 - c), y + c - 2 * y * c)
    second = (x + c - 2 * x * c, y + (1 - c) - 2 * y * (1 - c))
    return first, second, (1 - x, 1 - y)


def _arrival_order():
    x, y, c = _position()
    first, second, diag = _ring_chips()
    blocks = [(x, y, c), (x, y, 1 - c), (*first, c), (*second, 1 - c), (*second, c), (*first, 1 - c), (*diag, c), (*diag, 1 - c)]
    return jnp.stack([4 * b[0] + 2 * b[1] + b[2] for b in blocks]).astype(jnp.int32)


def _proj_gather(h, w_in_loc, w_out_loc, order, tm):
    s, d = h.shape
    ne = w_in_loc.shape[1]
    ni = s // tm
    nc = _n_chunk(ne)
    n_streams = 7

    def body(order_ref, h_ref, win_ref, wout_ref, proj_ref, fin_ref, fout_ref, bbuf, send_sems, recv_sems, local_sems, load_sems):
        ins, fulls = [win_ref, wout_ref], [fin_ref, fout_ref]
        jj, i = pl.program_id(0), pl.program_id(1)
        x, y, c = _position()
        me = (x, y, c)
        sibling = (x, y, 1 - c)
        first, second, diag = _ring_chips()
        sent = [(me, sibling), (me, (*first, c)), (me, (*second, c)), ((*first, c), (*second, c)),
                ((*first, c), sibling), ((*second, c), sibling), ((*diag, c), sibling)]
        received = [sibling, (*first, c), (*second, c), (*diag, c), (*second, 1 - c), (*first, 1 - c), (*diag, 1 - c)]

        def slot(p):
            return 4 * p[0] + 2 * p[1] + p[2]

        def copy(a, k, block, to):
            dst = fulls[a].at[slot(block)]
            return pltpu.make_async_remote_copy(
                src_ref=ins[a] if k < 3 else dst, dst_ref=dst, send_sem=send_sems.at[a, k], recv_sem=recv_sems.at[a, k],
                device_id=to, device_id_type=MESH)

        def send(a, k):
            return copy(a, k, *sent[k])

        def recv(a, k):
            return copy(a, k, received[k], me)

        def own(a):
            return pltpu.make_async_copy(ins[a], fulls[a].at[slot(me)], local_sems.at[a])

        def load(src, buf):
            return pltpu.make_async_copy(src, bbuf.at[buf], load_sems.at[buf])

        @pl.when((jj == 0) & (i == 0))
        def _():
            load(win_ref, 0).start()
            for a in range(2):
                own(a).start()
            send(0, 0).start()
            send(0, 1).start()
            send(1, 0).start()
            load(win_ref, 0).wait()

        stream_of = [None, 0, 1, 4, 2, 5, 3, 6]
        passes = {1: [(0, 4), (0, 2), (0, 3), (1, 1), (1, 2)], 2: [(0, 5)], 3: [(0, 6)]}
        for nxt in range(1, N_DEV):

            @pl.when((jj == nxt - 1) & (i == ni - 1))
            def _(nxt=nxt):
                k = stream_of[nxt]
                recv(0, k).wait_recv()
                for a, k2 in passes.get(k, []):
                    send(a, k2).start()
                if nxt == N_DEV - 2:
                    recv(1, 1).wait_recv()
                    send(1, 4).start()
                    send(1, 3).start()
                load(fin_ref.at[order_ref[nxt]], nxt % 2).start()

        @pl.when((jj > 0) & (i == 0))
        def _():
            load(fin_ref.at[0], jj % 2).wait()

        for n0 in range(0, ne, nc):
            proj_ref[:, n0 : n0 + nc] = lax.dot_general(
                h_ref[...], bbuf[jj % 2, :, n0 : n0 + nc], NN, preferred_element_type=F32).astype(BF16)

        @pl.when((jj == N_DEV - 1) & (i == ni - 1))
        def _():
            recv(1, 2).wait_recv()
            send(1, 5).start()
            recv(1, 3).wait_recv()
            send(1, 6).start()
            for k in (0, 4, 5, 6):
                recv(1, k).wait_recv()
            for a in range(2):
                for k in range(n_streams):
                    send(a, k).wait_send()
                own(a).wait()

    hbm = pl.BlockSpec(memory_space=pl.ANY)
    return pl.pallas_call(
        body, name="proj_gather",
        grid_spec=pltpu.PrefetchScalarGridSpec(
            num_scalar_prefetch=1, grid=(N_DEV, ni),
            in_specs=[pl.BlockSpec((tm, d), lambda jj, i, od: (i, 0)), hbm, hbm],
            out_specs=[pl.BlockSpec((tm, ne), lambda jj, i, od: (i, od[jj])), hbm, hbm],
            scratch_shapes=[pltpu.VMEM((2, d, ne), BF16), pltpu.SemaphoreType.DMA((2, 7)), pltpu.SemaphoreType.DMA((2, 7)),
                            pltpu.SemaphoreType.DMA((2,)), pltpu.SemaphoreType.DMA((2,))]),
        out_shape=[jax.ShapeDtypeStruct((s, N_DEV * ne), BF16), jax.ShapeDtypeStruct((N_DEV,) + w_in_loc.shape, BF16),
                   jax.ShapeDtypeStruct((N_DEV,) + w_out_loc.shape, BF16)],
        compiler_params=_cparams(2),
    )(order, h, w_in_loc, w_out_loc)


def _mm_out(y, w_out, tm, tn):
    s, m = y.shape
    _, d = w_out.shape
    return pl.pallas_call(
        _mm_body(NN, tm, tn), name="mm_out", grid=(d // tn, s // tm),
        in_specs=[pl.BlockSpec((tm, m), lambda j, i: (i, 0)), pl.BlockSpec((m, tn), lambda j, i: (0, j))],
        out_specs=pl.BlockSpec((tm, tn), lambda j, i: (i, j)),
        out_shape=jax.ShapeDtypeStruct((s, d), F32), compiler_params=_cparams(2),
    )(y, w_out)


def _mm_dy(dout, w_out, tm, tn):
    s, d = dout.shape
    m, _ = w_out.shape
    return pl.pallas_call(
        _mm_body(NT, tm, tn), name="mm_dy", grid=(m // tn, s // tm),
        in_specs=[pl.BlockSpec((tm, d), lambda j, i: (i, 0)), pl.BlockSpec((tn, d), lambda j, i: (j, 0))],
        out_specs=pl.BlockSpec((tm, tn), lambda j, i: (i, j)),
        out_shape=jax.ShapeDtypeStruct((s, m), BF16), compiler_params=_cparams(2),
    )(dout, w_out)


def _dw_targets():
    _, _, c = _position()
    return jnp.stack([2 * q + (1 - c) for q in range(N_CHIP)] + [2 * q + c for q in range(N_CHIP)]).astype(jnp.int32)


def _mm_dw_exchange(lhs, rhs, targets, bm, bn, target_on_lhs, name):
    s = lhs.shape[0]
    ni = (rhs.shape[1] // bn) if target_on_lhs else (lhs.shape[1] // bm)
    rows_b, cols_b = (bm, ni * bn) if target_on_lhs else (ni * bm, bn)
    nc = _n_chunk(bn)

    def body(tg_ref, a_ref, b_ref, comb_ref, psib_ref, rsib_ref, obuf, rbuf, osems, rsems, send_sems, recv_sems):
        del tg_ref
        jj, i = pl.program_id(0), pl.program_id(1)
        x, y, c = _position()
        slot = i % 2

        def tile(ref, q, t):
            if target_on_lhs:
                return ref.at[q, :, pl.ds(pl.multiple_of(t * bn, bn), bn)]
            return ref.at[q, pl.ds(pl.multiple_of(t * bm, bm), bm)]

        def put(sl, q):
            return pltpu.make_async_copy(obuf.at[sl], tile(psib_ref, q, i), osems.at[sl])

        def get(sl, q, t):
            return pltpu.make_async_copy(tile(rsib_ref, q, t), rbuf.at[sl], rsems.at[sl])

        def swap(q):
            return pltpu.make_async_remote_copy(
                src_ref=psib_ref.at[q], dst_ref=rsib_ref.at[q], send_sem=send_sems.at[q], recv_sem=recv_sems.at[q],
                device_id=(x, y, 1 - c), device_id_type=MESH)

        def product(n0):
            return lax.dot_general(a_ref[...], b_ref[:, n0 : n0 + nc], TN, preferred_element_type=F32)

        @pl.when(jj < N_CHIP)
        def _():
            @pl.when(i >= 2)
            def _():
                put(slot, 0).wait()

            for n0 in range(0, bn, nc):
                obuf[slot, :, n0 : n0 + nc] = product(n0).astype(BF16)
            put(slot, jj).start()

            @pl.when(i == ni - 1)
            def _():
                put(slot, 0).wait()
                if ni >= 2:
                    put(1 - slot, 0).wait()
                swap(jj).start()

        @pl.when(jj >= N_CHIP)
        def _():
            q = jj - N_CHIP

            @pl.when(i == 0)
            def _():
                swap(q).wait_recv()
                get(0, q, i).start()

            get(slot, 0, i).wait()

            @pl.when(i + 1 < ni)
            def _():
                get(1 - slot, q, i + 1).start()

            for n0 in range(0, bn, nc):
                comb_ref[:, n0 : n0 + nc] = (product(n0) + rbuf[slot, :, n0 : n0 + nc].astype(F32)).astype(BF16)

            @pl.when((jj == N_DEV - 1) & (i == ni - 1))
            def _():
                for qq in range(N_CHIP):
                    swap(qq).wait_send()

    if target_on_lhs:
        in_specs = [pl.BlockSpec((s, bm), lambda jj, i, tg: (0, tg[jj])), pl.BlockSpec((s, bn), lambda jj, i, tg: (0, i))]
        tile_index = lambda jj, i, tg: (jnp.maximum(jj - N_CHIP, 0), 0, jnp.where(jj < N_CHIP, 0, i))
    else:
        in_specs = [pl.BlockSpec((s, bm), lambda jj, i, tg: (0, i)), pl.BlockSpec((s, bn), lambda jj, i, tg: (0, tg[jj]))]
        tile_index = lambda jj, i, tg: (jnp.maximum(jj - N_CHIP, 0), jnp.where(jj < N_CHIP, 0, i), 0)
    hbm = pl.BlockSpec(memory_space=pl.ANY)
    comb, _, _ = pl.pallas_call(
        body, name=name,
        grid_spec=pltpu.PrefetchScalarGridSpec(
            num_scalar_prefetch=1, grid=(N_DEV, ni), in_specs=in_specs,
            out_specs=[pl.BlockSpec((None, bm, bn), tile_index), hbm, hbm],
            scratch_shapes=[pltpu.VMEM((2, bm, bn), BF16), pltpu.VMEM((2, bm, bn), BF16), pltpu.SemaphoreType.DMA((2,)),
                            pltpu.SemaphoreType.DMA((2,)), pltpu.SemaphoreType.DMA((N_CHIP,)), pltpu.SemaphoreType.DMA((N_CHIP,))]),
        out_shape=[jax.ShapeDtypeStruct((N_CHIP, rows_b, cols_b), BF16)] * 3,
        compiler_params=_cparams(2),
    )(targets, lhs, rhs)
    return comb


def _chip_copies(c_refs, r_refs, send_sems, recv_sems):
    x, y, c = _position()
    chips = [(1 - x, y), (x, 1 - y), (1 - x, 1 - y)]
    return [
        pltpu.make_async_remote_copy(
            src_ref=c_refs[a].at[2 * chip[0] + chip[1]], dst_ref=r_refs[a].at[j], send_sem=send_sems.at[a, j],
            recv_sem=recv_sems.at[a, j], device_id=(*chip, c), device_id_type=MESH)
        for a in range(len(c_refs)) for j, chip in enumerate(chips)]


def _mm_dh(dproj, w_full, comb, small_parts, small_width, tm, tn):
    s = dproj.shape[0]
    _, d, ne = w_full.shape
    _, r_rows, r_cols = comb.shape
    kb = 2
    nk = N_DEV // kb
    nc = _n_chunk(tn)
    mh = min(512, tm)
    grid = (s // tm, d // tn, nk)
    n_steps = grid[0] * grid[1] * grid[2]
    n_relay = (3 * n_steps) // 8
    assert 0 < n_relay < n_steps - 1
    rc = min(512, r_rows)

    n_small = len(small_parts)
    _, small_rows = _packed_rows(small_parts, small_width)

    def body(*refs):
        a_ref, b_ref, c_ref = refs[:3]
        small_refs = refs[3 : 3 + n_small]
        (o_ref, recv_ref, relay_ref, sum_ref, packed_ref, va, vb, stage, add_sems, send_sems, recv_sems, small_send, small_recv,
         stage_sem) = refs[3 + n_small :]
        i, n, k = pl.program_id(0), pl.program_id(1), pl.program_id(2)
        step = (i * grid[1] + n) * nk + k
        _, _, c = _position()
        first, second, diag = _ring_chips()

        def chip_slot(p):
            return 2 * p[0] + p[1]

        def stream(j):
            src, dst, to = [(c_ref.at[chip_slot(diag)], relay_ref, first), (c_ref.at[chip_slot(first)], recv_ref.at[0], first),
                            (sum_ref, recv_ref.at[1], second)][j]
            return pltpu.make_async_remote_copy(
                src_ref=src, dst_ref=dst, send_sem=send_sems.at[j], recv_sem=recv_sems.at[j], device_id=(*to, c), device_id_type=MESH)

        @pl.when(step == 0)
        def _():
            stream(0).start()
            stream(1).start()
            _packed_gather_start(small_refs, packed_ref, small_send, small_recv, stage, stage_sem.at[0])

        @pl.when(step == n_relay)
        def _():
            stream(0).wait_recv()

            n_pieces = r_rows // rc

            def pieces(t, sl):
                rows = pl.ds(t * rc, rc)
                return (pltpu.make_async_copy(c_ref.at[chip_slot(second), rows], va.at[sl], add_sems.at[sl, 0]),
                        pltpu.make_async_copy(relay_ref.at[rows], vb.at[sl], add_sems.at[sl, 1]),
                        pltpu.make_async_copy(va.at[sl], sum_ref.at[rows], add_sems.at[sl, 2]))

            for cp in pieces(0, 0)[:2]:
                cp.start()
            for t in range(n_pieces):
                sl = t % 2
                for cp in pieces(t, sl)[:2]:
                    cp.wait()
                if t + 1 < n_pieces:
                    if t >= 1:
                        pieces(t - 1, 1 - sl)[2].wait()
                    for cp in pieces(t + 1, 1 - sl)[:2]:
                        cp.start()
                va[sl] = (va[sl].astype(F32) + vb[sl].astype(F32)).astype(BF16)
                pieces(t, sl)[2].start()
            for t in range(max(n_pieces - 2, 0), n_pieces):
                pieces(t, t % 2)[2].wait()
            stream(2).start()

        @pl.when(k == 0)
        def _():
            o_ref[...] = jnp.zeros_like(o_ref)

        for m0 in range(0, tm, mh):
            for n0 in range(0, tn, nc):
                r = None
                for jj in range(kb):
                    t = lax.dot_general(a_ref[m0 : m0 + mh, jj * ne : (jj + 1) * ne], b_ref[jj, n0 : n0 + nc, :], NT, preferred_element_type=F32)
                    r = t if r is None else r + t
                o_ref[m0 : m0 + mh, n0 : n0 + nc] += r

        @pl.when(step == n_steps - 1)
        def _():
            stream(1).wait_recv()
            stream(2).wait_recv()
            for j in range(3):
                stream(j).wait_send()
            _packed_gather_wait(packed_ref, small_send, small_recv)

    hbm = pl.BlockSpec(memory_space=pl.ANY)
    vm = pl.BlockSpec(memory_space=pltpu.VMEM)
    block = jax.ShapeDtypeStruct((r_rows, r_cols), comb.dtype)
    dh, recv, _, _, packed = pl.pallas_call(
        body, name="mm_dh", grid=grid,
        in_specs=[pl.BlockSpec((tm, kb * ne), lambda i, n, k: (i, k)), pl.BlockSpec((kb, tn, ne), lambda i, n, k: (k, n, 0)), hbm] + [vm] * n_small,
        out_specs=[pl.BlockSpec((tm, tn), lambda i, n, k: (i, n)), hbm, hbm, hbm, hbm],
        out_shape=[jax.ShapeDtypeStruct((s, d), F32), jax.ShapeDtypeStruct((2, r_rows, r_cols), comb.dtype), block, block,
                   jax.ShapeDtypeStruct((N_DEV, small_rows, small_width), F32)],
        scratch_shapes=[pltpu.VMEM((2, rc, r_cols), comb.dtype), pltpu.VMEM((2, rc, r_cols), comb.dtype),
                        pltpu.VMEM((small_rows, small_width), F32), pltpu.SemaphoreType.DMA((2, 3)),
                        pltpu.SemaphoreType.DMA((3,)), pltpu.SemaphoreType.DMA((3,)),
                        pltpu.SemaphoreType.DMA((N_DEV - 1,)), pltpu.SemaphoreType.DMA((N_DEV - 1,)), pltpu.SemaphoreType.DMA((1,))],
        compiler_params=_cparams(3),
    )(dproj, w_full, comb, *small_parts)
    return dh, recv, packed


def _mod_matmul(c_all, w_ada, b_loc, tk):
    nb, d = c_all.shape
    na = w_ada.shape[1]

    def body(c_ref, w_ref, b_ref, o_ref):
        k = pl.program_id(0)
        r = jnp.dot(c_ref[...], w_ref[...], preferred_element_type=F32, precision=lax.Precision.HIGHEST)

        @pl.when(k == 0)
        def _():
            o_ref[...] = r + b_ref[...]

        @pl.when(k > 0)
        def _():
            o_ref[...] += r

    return pl.pallas_call(
        body, name="mod_matmul", grid=(d // tk,),
        in_specs=[pl.BlockSpec((nb, tk), lambda k: (0, k)), pl.BlockSpec((tk, na), lambda k: (k, 0)), pl.BlockSpec((1, na), lambda k: (0, 0))],
        out_specs=pl.BlockSpec((nb, na), lambda k: (0, 0)),
        out_shape=jax.ShapeDtypeStruct((nb, na), F32), compiler_params=_cparams(1),
    )(c_all, w_ada, b_loc)


def _col_chunks(width):
    return [slice(c0, c0 + LANES) for c0 in range(0, width, LANES)]


def _row_sum(acc):
    return jnp.sum(acc, axis=1, keepdims=True)


SUBLANES = 8


ROW_BLOCK = 64
DW_BLOCK = 96


def _tap_sum(ext, cs, ts, taps, emit):
    for t0 in range(0, ts, ROW_BLOCK):
        rb = min(ROW_BLOCK, ts - t0)
        acc = None
        for r in range(SUBLANES):
            group = [(o, wv) for o, wv in taps if o % SUBLANES == r]
            if not group:
                continue
            n = rb if r == 0 else rb + SUBLANES
            v = None
            for o, wv in group:
                term = wv * ext[t0 + o - r : t0 + o - r + n, cs]
                v = term if v is None else v + term
            part = v if r == 0 else v[r : r + rb]
            acc = part if acc is None else acc + part
        emit(slice(t0, t0 + rb), acc)


class _SegmentStores:
    def __init__(self, stash, dst, sems, n_seg, step, n_steps, row0, cols):
        self.stash, self.dst, self.sems, self.n_seg = stash, dst, sems, n_seg
        self.step, self.n_steps, self.row0, self.cols = step, n_steps, row0, cols
        self.slot = step % 2

    def _copy(self, slot, g, row0, col0):
        ts, cb = self.stash.shape[2:]
        return pltpu.make_async_copy(self.stash.at[slot, g], self.dst.at[pl.ds(row0, ts), pl.ds(col0, cb)], self.sems.at[slot, g])

    def _wait(self, slot):
        for g in range(self.n_seg):
            self._copy(slot, g, 0, 0).wait()

    def begin(self):
        @pl.when(self.step >= 2)
        def _():
            self._wait(self.slot)

        return self.slot

    def finish(self):
        ts = self.stash.shape[2]
        for g in range(self.n_seg):
            self._copy(self.slot, g, pl.multiple_of(self.row0, ts), pl.multiple_of(self.cols[g], LANES)).start()

        @pl.when(self.step == self.n_steps - 1)
        def _():
            self._wait(self.slot)
            if self.n_steps >= 2:
                self._wait(1 - self.slot)


def _prep(x, g1, scale, shift, ts):
    s, d = x.shape
    chunks = _col_chunks(d)

    def body(x_ref, g_ref, sc_ref, sh_ref, h_ref):
        acc = jnp.zeros((ts, LANES), F32)
        for cs in chunks:
            v = x_ref[:, cs]
            acc = acc + v * v
        r = lax.rsqrt(_row_sum(acc) / d + EPS)
        for cs in chunks:
            n1 = (x_ref[:, cs] * r) * g_ref[:, cs]
            h_ref[:, cs] = (n1 * (1.0 + sc_ref[:, cs]) + sh_ref[:, cs]).astype(BF16)

    row = pl.BlockSpec((1, d), lambda i: (0, 0))
    return pl.pallas_call(
        body, name="prep", grid=(s // ts,),
        in_specs=[pl.BlockSpec((ts, d), lambda i: (i, 0)), row, row, row],
        out_specs=pl.BlockSpec((ts, d), lambda i: (i, 0)),
        out_shape=jax.ShapeDtypeStruct((s, d), BF16), compiler_params=_cparams(1),
    )(x, g1, scale, shift)


def _mix_a_fwd(proj, wa, w, ts, cb):
    s = proj.shape[0]
    nb = w // cb
    chunks = _col_chunks(cb)

    def body(ab, ac, ax, az, wa_ref, y_ref, cv_ref, ext):
        @pl.when(pl.program_id(1) == 0)
        def _():
            ext[0:HALO_A, :] = jnp.zeros((HALO_A, cb), F32)

        for cs in chunks:
            ext[HALO_A : HALO_A + ts, cs] = ac[:, cs].astype(F32) * ax[:, cs].astype(F32)
        for cs in chunks:

            def emit(rows, cv, cs=cs):
                z = az[rows, cs].astype(F32)
                y_ref[rows, cs] = (ab[rows, cs].astype(F32) * cv * (z * _sigmoid(z))).astype(BF16)
                cv_ref[rows, cs] = cv.astype(BF16)

            _tap_sum(ext, cs, ts, [(HALO_A - (CONV_A - 1) + k, wa_ref[k : k + 1, cs]) for k in range(CONV_A)], emit)
        ext[0:HALO_A, :] = ext[ts : ts + HALO_A, :]

    def seg(q):
        return pl.BlockSpec((ts, cb), lambda c, i: (i, q * nb + c))

    return pl.pallas_call(
        body, name="mix_a_fwd", grid=(nb, s // ts),
        in_specs=[seg(0), seg(1), seg(2), seg(3), pl.BlockSpec((CONV_A, cb), lambda c, i: (0, c))],
        out_specs=[pl.BlockSpec((ts, cb), lambda c, i: (i, c)), pl.BlockSpec((ts, cb), lambda c, i: (i, c))],
        out_shape=[jax.ShapeDtypeStruct((s, 2 * w), BF16), jax.ShapeDtypeStruct((s, w), BF16)],
        scratch_shapes=[pltpu.VMEM((HALO_A + ts, cb), F32)], compiler_params=_cparams(2),
    )(proj, proj, proj, proj, wa)


def _mix_b_conv_fwd(proj, wb, bb, w, ts, cb):
    s = proj.shape[0]
    nb = w // cb
    chunks = _col_chunks(cb)

    def body(bv, bg, wb_ref, bb_ref, u2_ref, ext):
        @pl.when(pl.program_id(1) == 0)
        def _():
            ext[0:HALO_B, :] = jnp.zeros((HALO_B, cb), F32)

        for cs in chunks:
            ext[HALO_B : HALO_B + ts, cs] = bv[:, cs].astype(F32) * _sigmoid(bg[:, cs].astype(F32))
        for cs in chunks:

            def emit(rows, acc, cs=cs):
                u2_ref[rows, cs] = acc + bb_ref[:, cs]

            _tap_sum(ext, cs, ts, [(HALO_B - (CONV_B - 1) + k, wb_ref[k : k + 1, cs]) for k in range(CONV_B)], emit)
        ext[0:HALO_B, :] = ext[ts : ts + HALO_B, :]

    return pl.pallas_call(
        body, name="mix_b_conv_fwd", grid=(nb, s // ts),
        in_specs=[pl.BlockSpec((ts, cb), lambda c, i: (i, 4 * nb + c)), pl.BlockSpec((ts, cb), lambda c, i: (i, 5 * nb + c)),
                  pl.BlockSpec((CONV_B, cb), lambda c, i: (0, c)), pl.BlockSpec((1, cb), lambda c, i: (0, c))],
        out_specs=pl.BlockSpec((ts, cb), lambda c, i: (i, c)),
        out_shape=jax.ShapeDtypeStruct((s, w), F32),
        scratch_shapes=[pltpu.VMEM((HALO_B + ts, cb), F32)], compiler_params=_cparams(2),
    )(proj, proj, wb, bb)


def _ln_stats(u2_ref, ts, w, chunks):
    acc = jnp.zeros((ts, LANES), F32)
    for cs in chunks:
        acc = acc + u2_ref[:, cs]
    mu = _row_sum(acc) / w
    acc = jnp.zeros((ts, LANES), F32)
    for cs in chunks:
        xc = u2_ref[:, cs] - mu
        acc = acc + xc * xc
    return mu, lax.rsqrt(_row_sum(acc) / w + EPS)


def _mix_b_gate_fwd(u2, proj, lg, lb, y, ts):
    s, w = u2.shape
    chunks = _col_chunks(w)

    def body(u2_ref, bz, lg_ref, lb_ref, y_in, y_ref):
        del y_in
        mu, rstd = _ln_stats(u2_ref, ts, w, chunks)
        for cs in chunks:
            ln = (u2_ref[:, cs] - mu) * rstd * lg_ref[:, cs] + lb_ref[:, cs]
            z = bz[:, cs].astype(F32)
            y_ref[:, cs] = ((ln * _sigmoid(ln)) * (z * _sigmoid(z))).astype(BF16)

    row = pl.BlockSpec((1, w), lambda i: (0, 0))
    return pl.pallas_call(
        body, name="mix_b_gate_fwd", grid=(s // ts,),
        in_specs=[pl.BlockSpec((ts, w), lambda i: (i, 0)), pl.BlockSpec((ts, w), lambda i: (i, 6)), row, row,
                  pl.BlockSpec(memory_space=pl.ANY)],
        out_specs=pl.BlockSpec((ts, w), lambda i: (i, 1)),
        out_shape=jax.ShapeDtypeStruct(y.shape, y.dtype), input_output_aliases={4: 0}, compiler_params=_cparams(1),
    )(u2, proj, lg, lb, y)


def _post(x, out, tgt, gate, fg, ts):
    s, d = x.shape
    chunks = _col_chunks(d)

    def body(x_ref, o_ref, t_ref, gt_ref, fg_ref, dx2_ref, dout_ref, loss_ref, dgate_ref, dfg_ref, keep):
        @pl.when(pl.program_id(0) == 0)
        def _():
            loss_ref[...] = jnp.zeros_like(loss_ref)
            dgate_ref[...] = jnp.zeros_like(dgate_ref)
            dfg_ref[...] = jnp.zeros_like(dfg_ref)

        acc = jnp.zeros((ts, LANES), F32)
        for cs in chunks:
            x2 = x_ref[:, cs] + gt_ref[:, cs] * o_ref[:, cs]
            keep[:, cs] = x2
            acc = acc + x2 * x2
        r2 = lax.rsqrt(_row_sum(acc) / d + EPS)
        acc = jnp.zeros((ts, LANES), F32)
        for cs in chunks:
            n2 = keep[:, cs] * r2
            keep[:, cs] = n2
            diff = n2 * fg_ref[:, cs] - t_ref[:, cs]
            loss_ref[:, cs] += jnp.sum(diff * diff, axis=0, keepdims=True)
            dyf = diff / d
            dfg_ref[:, cs] += jnp.sum(dyf * n2, axis=0, keepdims=True)
            dn = dyf * fg_ref[:, cs]
            dx2_ref[:, cs] = dn
            acc = acc + dn * n2
        mdot = _row_sum(acc) / d
        for cs in chunks:
            dx2 = r2 * (dx2_ref[:, cs] - keep[:, cs] * mdot)
            dx2_ref[:, cs] = dx2
            dgate_ref[:, cs] += jnp.sum(dx2 * o_ref[:, cs], axis=0, keepdims=True)
            dout_ref[:, cs] = (dx2 * gt_ref[:, cs]).astype(BF16)

    tile = pl.BlockSpec((ts, d), lambda i: (i, 0))
    row = pl.BlockSpec((1, d), lambda i: (0, 0))
    return pl.pallas_call(
        body, name="post", grid=(s // ts,),
        in_specs=[tile, tile, tile, row, row], out_specs=[tile, tile, row, row, row],
        out_shape=[jax.ShapeDtypeStruct((s, d), F32), jax.ShapeDtypeStruct((s, d), BF16)] + [jax.ShapeDtypeStruct((1, d), F32)] * 3,
        scratch_shapes=[pltpu.VMEM((ts, d), F32)], compiler_params=_cparams(1),
    )(x, out, tgt, gate, fg)


def _mix_b_gate_bwd(u2, proj, dy, lg, lb, n_cols, ts):
    s, w = u2.shape
    chunks = _col_chunks(w)

    def body(u2_ref, bz, dyb, lg_ref, lb_ref, du2_ref, dbz_ref, dlg_ref, dlb_ref, dbb_ref):
        @pl.when(pl.program_id(0) == 0)
        def _():
            dlg_ref[...] = jnp.zeros_like(dlg_ref)
            dlb_ref[...] = jnp.zeros_like(dlb_ref)
            dbb_ref[...] = jnp.zeros_like(dbb_ref)

        mu, rstd = _ln_stats(u2_ref, ts, w, chunks)
        acc1 = jnp.zeros((ts, LANES), F32)
        acc2 = jnp.zeros((ts, LANES), F32)
        for cs in chunks:
            xh = (u2_ref[:, cs] - mu) * rstd
            ln = xh * lg_ref[:, cs] + lb_ref[:, cs]
            sl = _sigmoid(ln)
            z = bz[:, cs].astype(F32)
            sz = _sigmoid(z)
            g = dyb[:, cs].astype(F32)
            dbz_ref[:, cs] = (g * (ln * sl) * (sz * (1.0 + z * (1.0 - sz)))).astype(BF16)
            dln = g * (z * sz) * (sl * (1.0 + ln * (1.0 - sl)))
            dlg_ref[:, cs] += jnp.sum(dln * xh, axis=0, keepdims=True)
            dlb_ref[:, cs] += jnp.sum(dln, axis=0, keepdims=True)
            dxh = dln * lg_ref[:, cs]
            du2_ref[:, cs] = dxh
            acc1 = acc1 + dxh
            acc2 = acc2 + dxh * xh
        m1 = _row_sum(acc1) / w
        m2 = _row_sum(acc2) / w
        for cs in chunks:
            xh = (u2_ref[:, cs] - mu) * rstd
            du2 = rstd * (du2_ref[:, cs] - m1 - xh * m2)
            du2_ref[:, cs] = du2
            dbb_ref[:, cs] += jnp.sum(du2, axis=0, keepdims=True)

    tile = pl.BlockSpec((ts, w), lambda i: (i, 0))
    row = pl.BlockSpec((1, w), lambda i: (0, 0))
    return pl.pallas_call(
        body, name="mix_b_gate_bwd", grid=(s // ts,),
        in_specs=[tile, pl.BlockSpec((ts, w), lambda i: (i, 6)), pl.BlockSpec((ts, w), lambda i: (i, 1)), row, row],
        out_specs=[tile, pl.BlockSpec((ts, w), lambda i: (i, 6)), row, row, row],
        out_shape=[jax.ShapeDtypeStruct((s, w), F32), jax.ShapeDtypeStruct((s, n_cols), BF16)] + [jax.ShapeDtypeStruct((1, w), F32)] * 3,
        compiler_params=_cparams(1),
    )(u2, proj, dy, lg, lb)


def _mix_b_conv_bwd(du2, proj, wb, dproj, comb, w, ts, cb):
    s = proj.shape[0]
    nb = w // cb
    ns = s // ts
    chunks = _col_chunks(cb)

    u_rows = ts + 2 * HALO_B + SUBLANES
    win = ts + HALO_B
    assert win % DW_BLOCK == 0

    n_seg = 2

    def body(du2_ref, bv, bg, wb_ref, dp_in, comb_ref, dp_ref, dwb_ref, recv_ref, ext, stash, uext, sems, send_sems, recv_sems):
        del dp_in
        c, i = pl.program_id(0), pl.program_id(1)
        stores = _SegmentStores(stash, dp_ref, sems, n_seg, c * ns + i, nb * ns, (ns - 1 - i) * ts, [((4 + g) * nb + c) * cb for g in range(n_seg)])
        slot = stores.begin()

        def copies():
            return _chip_copies([comb_ref], [recv_ref], send_sems, recv_sems)

        @pl.when((c == 0) & (i == 0))
        def _():
            for cp in copies():
                cp.start()

        @pl.when(i == 0)
        def _():
            ext[ts : ts + HALO_B, :] = jnp.zeros((HALO_B, cb), F32)
            uext[0:HALO_B, :] = jnp.zeros((HALO_B, cb), F32)
            uext[HALO_B + ts : u_rows, :] = jnp.zeros((u_rows - HALO_B - ts, cb), F32)
            dwb_ref[...] = jnp.zeros_like(dwb_ref)

        ext[0:ts, :] = du2_ref[...]
        for cs in chunks:
            uext[HALO_B : HALO_B + ts, cs] = bv[:, cs].astype(F32) * _sigmoid(bg[:, cs].astype(F32))

            def emit(rows, du, cs=cs):
                v = bv[rows, cs].astype(F32)
                sg = _sigmoid(bg[rows, cs].astype(F32))
                stash[slot, 0, rows, cs] = (du * sg).astype(BF16)
                stash[slot, 1, rows, cs] = (du * v * (sg * (1.0 - sg))).astype(BF16)

            _tap_sum(ext, cs, ts, [(CONV_B - 1 - k, wb_ref[k : k + 1, cs]) for k in range(CONV_B)], emit)
            for r in range(SUBLANES):
                ks = [k for k in range(CONV_B) if (CONV_B - 1 - k) % SUBLANES == r]
                accs = [jnp.zeros((SUBLANES, LANES), F32) for _ in ks]
                for t0 in range(0, win, DW_BLOCK):
                    e = ext[t0 : t0 + DW_BLOCK, cs]
                    ush = uext[SUBLANES - r + t0 : SUBLANES - r + t0 + DW_BLOCK + HALO_B - SUBLANES, cs]
                    for n, k in enumerate(ks):
                        b0 = HALO_B - SUBLANES - (CONV_B - 1 - k - r)
                        prod = e * ush[b0 : b0 + DW_BLOCK]
                        for v0 in range(0, DW_BLOCK, SUBLANES):
                            accs[n] = accs[n] + prod[v0 : v0 + SUBLANES]
                for n, k in enumerate(ks):
                    dwb_ref[k : k + 1, cs] += jnp.sum(accs[n], axis=0, keepdims=True)
        ext[ts : ts + HALO_B, :] = ext[0:HALO_B, :]
        stores.finish()

        @pl.when((c == nb - 1) & (i == ns - 1))
        def _():
            for cp in copies():
                cp.wait_recv()
            for cp in copies():
                cp.wait_send()

    def rev(col):
        return lambda c, i: (ns - 1 - i, col(c))

    hbm = pl.BlockSpec(memory_space=pl.ANY)
    return pl.pallas_call(
        body, name="mix_b_conv_bwd", grid=(nb, ns),
        in_specs=[pl.BlockSpec((ts, cb), rev(lambda c: c)), pl.BlockSpec((ts, cb), rev(lambda c: 4 * nb + c)),
                  pl.BlockSpec((ts, cb), rev(lambda c: 5 * nb + c)), pl.BlockSpec((CONV_B, cb), lambda c, i: (0, c)), hbm, hbm],
        out_specs=[hbm, pl.BlockSpec((CONV_B, cb), lambda c, i: (0, c)), hbm],
        out_shape=[jax.ShapeDtypeStruct(dproj.shape, dproj.dtype), jax.ShapeDtypeStruct((CONV_B, w), F32),
                   jax.ShapeDtypeStruct((3,) + comb.shape[1:], comb.dtype)],
        input_output_aliases={4: 0},
        scratch_shapes=[pltpu.VMEM((ts + HALO_B, cb), F32), pltpu.VMEM((2, n_seg, ts, cb), BF16), pltpu.VMEM((u_rows, cb), F32),
                        pltpu.SemaphoreType.DMA((2, n_seg)), pltpu.SemaphoreType.DMA((1, 3)), pltpu.SemaphoreType.DMA((1, 3))],
        compiler_params=_cparams(2),
    )(du2, proj, proj, wb, dproj, comb)


def _mix_a_bwd(proj, cv, dy, wa, dproj, w, ts, cb):
    s = proj.shape[0]
    nb = w // cb
    ns = s // ts
    chunks = _col_chunks(cb)

    n_seg = 4

    def body(ab, ac, ax, az, cv_ref, dya, wa_ref, dp_in, dp_ref, dwa_ref, ext, stash, sems):
        del dp_in
        c, i = pl.program_id(0), pl.program_id(1)
        stores = _SegmentStores(stash, dp_ref, sems, n_seg, c * ns + i, nb * ns, (ns - 1 - i) * ts, [(g * nb + c) * cb for g in range(n_seg)])
        slot = stores.begin()

        @pl.when(i == 0)
        def _():
            ext[ts : ts + HALO_A, :] = jnp.zeros((HALO_A, cb), F32)
            dwa_ref[...] = jnp.zeros_like(dwa_ref)

        for cs in chunks:
            b = ab[:, cs].astype(F32)
            z = az[:, cs].astype(F32)
            sz = _sigmoid(z)
            g = dya[:, cs].astype(F32)
            conv = cv_ref[:, cs].astype(F32)
            stash[slot, 0, :, cs] = (g * conv * (z * sz)).astype(BF16)
            stash[slot, 3, :, cs] = (g * b * conv * (sz * (1.0 + z * (1.0 - sz)))).astype(BF16)
            ext[0:ts, cs] = g * b * (z * sz)
        for cs in chunks:

            def emit(rows, dca, cs=cs):
                stash[slot, 1, rows, cs] = (dca * ax[rows, cs].astype(F32)).astype(BF16)
                stash[slot, 2, rows, cs] = (dca * ac[rows, cs].astype(F32)).astype(BF16)

            _tap_sum(ext, cs, ts, [(CONV_A - 1 - k, wa_ref[k : k + 1, cs]) for k in range(CONV_A)], emit)
            ca = ac[:, cs].astype(F32) * ax[:, cs].astype(F32)
            for k in range(CONV_A):
                o = CONV_A - 1 - k
                dwa_ref[k : k + 1, cs] += jnp.sum(ca * ext[o : o + ts, cs], axis=0, keepdims=True)
        ext[ts : ts + HALO_A, :] = ext[0:HALO_A, :]
        stores.finish()

    def rev(col):
        return lambda c, i: (ns - 1 - i, col(c))

    def seg(sq):
        return pl.BlockSpec((ts, cb), rev(lambda c: sq * nb + c))

    hbm = pl.BlockSpec(memory_space=pl.ANY)
    return pl.pallas_call(
        body, name="mix_a_bwd", grid=(nb, ns),
        in_specs=[seg(0), seg(1), seg(2), seg(3), pl.BlockSpec((ts, cb), rev(lambda c: c)), pl.BlockSpec((ts, cb), rev(lambda c: c)),
                  pl.BlockSpec((CONV_A, cb), lambda c, i: (0, c)), hbm],
        out_specs=[hbm, pl.BlockSpec((CONV_A, cb), lambda c, i: (0, c))],
        out_shape=[jax.ShapeDtypeStruct(dproj.shape, dproj.dtype), jax.ShapeDtypeStruct((CONV_A, w), F32)],
        input_output_aliases={7: 0},
        scratch_shapes=[pltpu.VMEM((ts + HALO_A, cb), F32), pltpu.VMEM((2, n_seg, ts, cb), BF16), pltpu.SemaphoreType.DMA((2, n_seg))],
        compiler_params=_cparams(2),
    )(proj, proj, proj, proj, cv, dy, wa, dproj)


def _pre_bwd(x, dh, dx2, g1, scale, ts):
    s, d = x.shape
    chunks = _col_chunks(d)

    def body(x_ref, dh_ref, dx2_ref, g_ref, sc_ref, gx_ref, dsh_ref, dsc_ref, dg_ref, keep):
        @pl.when(pl.program_id(0) == 0)
        def _():
            dsh_ref[...] = jnp.zeros_like(dsh_ref)
            dsc_ref[...] = jnp.zeros_like(dsc_ref)
            dg_ref[...] = jnp.zeros_like(dg_ref)

        acc = jnp.zeros((ts, LANES), F32)
        for cs in chunks:
            v = x_ref[:, cs]
            acc = acc + v * v
        r1 = lax.rsqrt(_row_sum(acc) / d + EPS)
        acc = jnp.zeros((ts, LANES), F32)
        for cs in chunks:
            xn = x_ref[:, cs] * r1
            keep[:, cs] = xn
            g = dh_ref[:, cs]
            dsh_ref[:, cs] += jnp.sum(g, axis=0, keepdims=True)
            dsc_ref[:, cs] += jnp.sum(g * (xn * g_ref[:, cs]), axis=0, keepdims=True)
            dn1 = g * (1.0 + sc_ref[:, cs])
            dg_ref[:, cs] += jnp.sum(dn1 * xn, axis=0, keepdims=True)
            dxn = dn1 * g_ref[:, cs]
            gx_ref[:, cs] = dxn
            acc = acc + dxn * xn
        mdot = _row_sum(acc) / d
        for cs in chunks:
            gx_ref[:, cs] = dx2_ref[:, cs] + r1 * (gx_ref[:, cs] - keep[:, cs] * mdot)

    tile = pl.BlockSpec((ts, d), lambda i: (i, 0))
    row = pl.BlockSpec((1, d), lambda i: (0, 0))
    return pl.pallas_call(
        body, name="pre_bwd", grid=(s // ts,),
        in_specs=[tile, tile, tile, row, row], out_specs=[tile, row, row, row],
        out_shape=[jax.ShapeDtypeStruct((s, d), F32)] + [jax.ShapeDtypeStruct((1, d), F32)] * 3,
        scratch_shapes=[pltpu.VMEM((ts, d), F32)], compiler_params=_cparams(1),
    )(x, dh, dx2, g1, scale)


def _adamw(w, g, m, v):
    m = ADAM_B1 * m + (1.0 - ADAM_B1) * g
    v = ADAM_B2 * v + (1.0 - ADAM_B2) * (g * g)
    m_hat = m / (1.0 - ADAM_B1**ADAM_STEP)
    v_hat = v / (1.0 - ADAM_B2**ADAM_STEP)
    delta = -ADAM_LR * (m_hat / (jnp.sqrt(v_hat) + ADAM_EPS) + ADAM_WD * w)
    return delta, m, v


def _adam_sharded(w, m, v, comb, recv, chip, tr, name):
    r, n = w.shape
    nr = recv.shape[0]

    def body(chip_ref, w_ref, m_ref, v_ref, c_ref, r_ref, g_ref, d_ref, nm_ref, nv_ref):
        del chip_ref
        g = c_ref[...].astype(F32)
        for j in range(nr):
            g = g + r_ref[j].astype(F32)
        delta, nm, nv = _adamw(w_ref[...], g, m_ref[...], v_ref[...])
        g_ref[...] = g
        d_ref[...] = delta
        nm_ref[...] = nm
        nv_ref[...] = nv

    tile = pl.BlockSpec((tr, n), lambda i, ch: (i, 0))
    return pl.pallas_call(
        body, name=name,
        grid_spec=pltpu.PrefetchScalarGridSpec(
            num_scalar_prefetch=1, grid=(r // tr,),
            in_specs=[tile, tile, tile, pl.BlockSpec((None, tr, n), lambda i, ch: (ch[0], i, 0)), pl.BlockSpec((nr, tr, n), lambda i, ch: (0, i, 0))],
            out_specs=[tile] * 4),
        out_shape=[jax.ShapeDtypeStruct((r, n), F32)] * 4, compiler_params=_cparams(1),
    )(chip, w, m, v, comb, recv)


def _adam_w_ada(c_t, dm_loc, w, m, v, tr):
    d, nb = c_t.shape
    na = w.shape[1]

    def body(c_ref, dm_ref, w_ref, m_ref, v_ref, g_ref, d_ref, nm_ref, nv_ref):
        g = jnp.dot(c_ref[...], dm_ref[...], preferred_element_type=F32, precision=lax.Precision.HIGHEST)
        delta, nm, nv = _adamw(w_ref[...], g, m_ref[...], v_ref[...])
        g_ref[...] = g
        d_ref[...] = delta
        nm_ref[...] = nm
        nv_ref[...] = nv

    tile = pl.BlockSpec((tr, na), lambda i: (i, 0))
    return pl.pallas_call(
        body, name="adam_w_ada", grid=(d // tr,),
        in_specs=[pl.BlockSpec((tr, nb), lambda i: (i, 0)), pl.BlockSpec((nb, na), lambda i: (0, 0)), tile, tile, tile],
        out_specs=[tile] * 4, out_shape=[jax.ShapeDtypeStruct((d, na), F32)] * 4, compiler_params=_cparams(1),
    )(c_t, dm_loc, w, m, v)


def _small_final(me, packed, starts, late, late_starts, params, d, w, cl):
    flat_params = [t for p in params for t in p]
    kd = d // w
    r_loss, r_dfg, r_dgt, r_dbb, r_dlg, r_dlb, r_dwa, r_dwb = starts
    r_dg1, r_dsh, r_dsc = late_starts

    def total(ref, r0, rows):
        acc = ref[0, r0 : r0 + rows, :]
        for b in range(1, N_DEV):
            acc = acc + ref[b, r0 : r0 + rows, :]
        return acc

    def body(me_ref, full_ref, mine_ref, late_ref, *rest):
        del me_ref
        prm = rest[:24]
        loss_ref = rest[24]
        outs = rest[25:]

        loss_rows = jnp.sum(total(full_ref, r_loss, kd), axis=1, keepdims=True)
        loss_ref[...] = jnp.sum(loss_rows, axis=0, keepdims=True) * (0.5 / d)

        def update(idx, g, col0=0):
            w_ref, m_ref, v_ref = prm[3 * idx : 3 * idx + 3]
            g_ref, d_ref, nm_ref, nv_ref = outs[4 * idx : 4 * idx + 4]
            sl = (slice(None), slice(col0, col0 + g.shape[1]))
            delta, nm, nv = _adamw(w_ref[sl], g, m_ref[sl], v_ref[sl])
            g_ref[sl] = g
            d_ref[sl] = delta
            nm_ref[sl] = nm
            nv_ref[sl] = nv

        def update_wide(idx, ref, r0, col0=0):
            for j in range(kd):
                update(idx, total(ref, r0 + j, 1), col0 + j * w)

        update_wide(0, late_ref, r_dg1)
        update_wide(1, late_ref, r_dsh, 0)
        update_wide(1, late_ref, r_dsc, d)
        update_wide(1, full_ref, r_dgt, 2 * d)
        update(2, total(mine_ref, r_dwa, CONV_A))
        update(3, total(mine_ref, r_dwb, CONV_B))
        update(4, total(full_ref, r_dbb, 1))
        update(5, total(full_ref, r_dlg, 1))
        update(6, total(full_ref, r_dlb, 1))
        update_wide(7, full_ref, r_dfg)

    def full(shape):
        nd = len(shape)
        return pl.BlockSpec(shape, lambda i, mr: (0,) * nd)

    n_rows = packed.shape[1]
    in_specs = [full(packed.shape), pl.BlockSpec((N_DEV, n_rows, cl), lambda i, mr: (0, 0, mr[0])), full(late.shape)]
    in_specs += [full(t.shape) for t in flat_params]
    out_shapes = [jax.ShapeDtypeStruct((1, 1), F32)]
    for p in params:
        out_shapes += [jax.ShapeDtypeStruct(p[0].shape, F32)] * 4
    return pl.pallas_call(
        body, name="small_final",
        grid_spec=pltpu.PrefetchScalarGridSpec(
            num_scalar_prefetch=1, grid=(1,), in_specs=in_specs, out_specs=[full(o.shape) for o in out_shapes]),
        out_shape=out_shapes, compiler_params=_cparams(1),
    )(me, packed, packed, late, *flat_params)


def _silu_rows(c):
    def body(c_ref, o_ref):
        v = c_ref[...]
        o_ref[...] = v * _sigmoid(v)

    return pl.pallas_call(body, name="silu_c", out_shape=jax.ShapeDtypeStruct(c.shape, F32), compiler_params=_cparams())(c)


def kernel(x, c, norm_g, w_ada, b_ada, w_in, conv_a_w, conv_b_w, conv_b_b, ln_b_g, ln_b_b, w_out, final_g, loss_target, m_norm_g, m_w_ada, m_b_ada, m_w_in, m_conv_a_w, m_conv_b_w, m_conv_b_b, m_ln_b_g, m_ln_b_b, m_w_out, m_final_g, v_norm_g, v_w_ada, v_b_ada, v_w_in, v_conv_a_w, v_conv_b_w, v_conv_b_b, v_ln_b_g, v_ln_b_b, v_w_out, v_final_g):
    _, s, d = x.shape
    w = conv_b_b.shape[-1]
    cl = conv_a_w.shape[-1]
    na = w_ada.shape[-1]
    assert ln_b_g.shape[-1] == w and w_out.shape[1] * N_DEV == 2 * w and w_in.shape[-1] * N_DEV == 7 * w and cl * N_DEV == w
    ts = min(256, s)
    ts_a = min(512, s)
    ts_row = min(256, s)
    cb = min(512, w)
    tm = min(512, s)
    tm_big = min(1024, s)
    tn = min(1024, d)

    px, py, pc = _position()
    me = 4 * px + 2 * py + pc
    me_arr = jnp.reshape(me, (1,)).astype(jnp.int32)
    chip_arr = jnp.reshape(2 * px + py, (1,)).astype(jnp.int32)

    x2d, tgt = x[0], loss_target[0]
    w_ada2, m_w_ada2, v_w_ada2 = w_ada[0], m_w_ada[0], v_w_ada[0]
    w_in2, m_w_in2, v_w_in2 = w_in[0], m_w_in[0], v_w_in[0]
    w_out2, m_w_out2, v_w_out2 = w_out[0], m_w_out[0], v_w_out[0]
    fg = final_g.reshape(1, d)

    c_act = _silu_rows(c)
    small_in = [c_act, conv_a_w[0], conv_b_w[0]]
    (r_c, r_wa, r_wb), _ = _packed_rows(small_in, cl)
    packed_in = _all_gather_packed(small_in, cl, "gather_small_in")
    c_all = packed_in[:, r_c : r_c + d // cl, :].reshape(N_DEV, d)
    wa = jnp.transpose(packed_in[:, r_wa : r_wa + CONV_A, :], (1, 0, 2)).reshape(CONV_A, w)
    wb = jnp.transpose(packed_in[:, r_wb : r_wb + CONV_B, :], (1, 0, 2)).reshape(CONV_B, w)
    b_loc = lax.dynamic_slice_in_dim(b_ada, me * na, na, axis=1)
    mod_loc = _mod_matmul(c_all, w_ada2, b_loc, min(512, d))
    (mod_all,) = _all_gather_vmem([mod_loc], "gather_mod")
    mod_mine = lax.dynamic_index_in_dim(mod_all, me, axis=1, keepdims=False).reshape(1, 3 * d)
    shift, scale, gate = mod_mine[:, :d], mod_mine[:, d : 2 * d], mod_mine[:, 2 * d :]

    h = _prep(x2d, norm_g, scale, shift, ts_row)
    proj, w_in_full, w_out_full = _proj_gather(h, w_in2.astype(BF16), w_out2.astype(BF16), _arrival_order(), tm)
    w_out_full = w_out_full.reshape(2 * w, d)
    y, cv = _mix_a_fwd(proj, wa, w, ts_a, cb)
    u2 = _mix_b_conv_fwd(proj, wb, conv_b_b, w, ts, cb)
    y = _mix_b_gate_fwd(u2, proj, ln_b_g, ln_b_b, y, ts)
    out = _mm_out(y, w_out_full, tm_big, tn)
    dx2, dout, loss_row, dgate_row, dfg_row = _post(x2d, out, tgt, gate, fg, ts_row)

    dy = _mm_dy(dout, w_out_full, tm_big, tn)
    comb_out = _mm_dw_exchange(y, dout, _dw_targets(), w_out2.shape[0], tn, True, "mm_dw_out")
    du2, dproj, dlg_row, dlb_row, dbb_row = _mix_b_gate_bwd(u2, proj, dy, ln_b_g, ln_b_b, 7 * w, ts)
    dproj, dwb_part, recv_out = _mix_b_conv_bwd(du2, proj, wb, dproj, comb_out, w, ts, cb)
    dproj, dwa_part = _mix_a_bwd(proj, cv, dy, wa, dproj, w, ts_a, cb)
    comb_in = _mm_dw_exchange(h, dproj, _dw_targets(), tm, w_in2.shape[1], False, "mm_dw_in")
    small_parts = [loss_row, dfg_row, dgate_row, dbb_row, dlg_row, dlb_row, dwa_part, dwb_part]
    starts, _ = _packed_rows(small_parts, w)
    dh, recv_in, packed = _mm_dh(dproj, w_in_full, comb_in, small_parts, w, min(1024, s), tn)
    grad_x, dshift_row, dscale_row, dg1_row = _pre_bwd(x2d, dh, dx2, norm_g, scale, ts_row)
    g_w_in, d_w_in, nm_w_in, nv_w_in = _adam_sharded(w_in2, m_w_in2, v_w_in2, comb_in, recv_in, chip_arr, min(128, d), "adam_w_in")
    g_w_out, d_w_out, nm_w_out, nv_w_out = _adam_sharded(w_out2, m_w_out2, v_w_out2, comb_out, recv_out, chip_arr, min(128, w_out2.shape[0]), "adam_w_out")

    late_parts = [dg1_row, dshift_row, dscale_row]
    late_starts, _ = _packed_rows(late_parts, w)
    late = _all_gather_packed(late_parts, w, "gather_small_grads")
    kd = d // w

    def gathered_rows(buf, r0):
        return buf[:, r0 : r0 + kd, :].reshape(N_DEV, d)

    dmod_all = jnp.concatenate([gathered_rows(late, late_starts[1]), gathered_rows(late, late_starts[2]), gathered_rows(packed, starts[2])], axis=1)
    dm_loc = lax.dynamic_slice_in_dim(dmod_all, me * na, na, axis=1)
    g_w_ada, d_w_ada, nm_w_ada, nv_w_ada = _adam_w_ada(jnp.transpose(c_all), dm_loc, w_ada2, m_w_ada2, v_w_ada2, min(256, d))

    params = [
        (norm_g, m_norm_g, v_norm_g), (b_ada, m_b_ada, v_b_ada), (conv_a_w[0], m_conv_a_w[0], v_conv_a_w[0]),
        (conv_b_w[0], m_conv_b_w[0], v_conv_b_w[0]), (conv_b_b, m_conv_b_b, v_conv_b_b), (ln_b_g, m_ln_b_g, v_ln_b_g),
        (ln_b_b, m_ln_b_b, v_ln_b_b), (fg, m_final_g.reshape(1, d), v_final_g.reshape(1, d)),
    ]
    small = _small_final(me_arr, packed, starts, late, late_starts, params, d, w, cl)
    loss = small[0].reshape(())
    sm = [small[1 + 4 * i : 5 + 4 * i] for i in range(8)]

    def pick(k):
        return [
            sm[0][k], (g_w_ada, d_w_ada, nm_w_ada, nv_w_ada)[k][None], sm[1][k], (g_w_in, d_w_in, nm_w_in, nv_w_in)[k][None],
            sm[2][k][None], sm[3][k][None], sm[4][k], sm[5][k], sm[6][k], (g_w_out, d_w_out, nm_w_out, nv_w_out)[k][None],
            sm[7][k].reshape(d),
        ]

    return (loss, grad_x[None], *pick(0), *pick(1), *pick(2), *pick(3))
```

```python
import functools

import jax
import jax.numpy as jnp
from jax import lax
from jax.experimental import pallas as pl
from jax.experimental.pallas import tpu as pltpu

F32 = jnp.float32
BF16 = jnp.bfloat16
N_DEV = 8
N_CHIP = 4
EPS = 1e-6
ADAM_LR = 0.001
ADAM_B1 = 0.9
ADAM_B2 = 0.999
ADAM_EPS = 1e-08
ADAM_WD = 0.01
ADAM_STEP = 10
CONV_A = 3
CONV_B = 31
HALO_A = 8
HALO_B = 32
LANES = 128
VMEM_LIMIT = 56 * 1024 * 1024
MESH = pl.DeviceIdType.MESH


def _cparams(n_grid_axes=0):
    if n_grid_axes:
        return pltpu.CompilerParams(dimension_semantics=("arbitrary",) * n_grid_axes, vmem_limit_bytes=VMEM_LIMIT)
    return pltpu.CompilerParams(vmem_limit_bytes=VMEM_LIMIT)


def _flip(v, bit):
    return 1 - v if bit else v


def _position():
    return lax.axis_index("x"), lax.axis_index("y"), lax.axis_index("c")


def _sigmoid(v):
    return jax.nn.sigmoid(v)


def _all_gather_vmem(arrs, name):
    n = len(arrs)

    def body(*refs):
        ins, outs = refs[:n], refs[n : 2 * n]
        send_sems, recv_sems = refs[2 * n :]
        x, y, c = _position()
        me = 4 * x + 2 * y + c
        for a in range(n):
            outs[a][me] = ins[a][...]
        sends = []
        for k in range(1, N_DEV):
            peer = (_flip(x, k & 4), _flip(y, k & 2), _flip(c, k & 1))
            for a in range(n):
                cp = pltpu.make_async_remote_copy(
                    src_ref=ins[a], dst_ref=outs[a].at[me], send_sem=send_sems.at[a, k - 1], recv_sem=recv_sems.at[a, k - 1],
                    device_id=peer, device_id_type=MESH)
                cp.start()
                sends.append(cp)
        for k in range(1, N_DEV):
            peer = (_flip(x, k & 4), _flip(y, k & 2), _flip(c, k & 1))
            src = 4 * peer[0] + 2 * peer[1] + peer[2]
            for a in range(n):
                pltpu.make_async_remote_copy(
                    src_ref=ins[a], dst_ref=outs[a].at[src], send_sem=send_sems.at[a, k - 1], recv_sem=recv_sems.at[a, k - 1],
                    device_id=peer, device_id_type=MESH).wait_recv()
        for cp in sends:
            cp.wait_send()

    vm = pl.BlockSpec(memory_space=pltpu.VMEM)
    return pl.pallas_call(
        body, name=name,
        out_shape=[jax.ShapeDtypeStruct((N_DEV,) + a.shape, a.dtype) for a in arrs],
        in_specs=[vm] * n, out_specs=[vm] * n,
        scratch_shapes=[pltpu.SemaphoreType.DMA((n, N_DEV - 1)), pltpu.SemaphoreType.DMA((n, N_DEV - 1))],
        compiler_params=_cparams(),
    )(*arrs)


def _packed_rows(arrs, width):
    starts, r = [], 0
    for a in arrs:
        starts.append(r)
        r += a.shape[0] * (a.shape[1] // width)
    return starts, -(-r // SUBLANES) * SUBLANES


def _packed_copies(out, send_sems, recv_sems, as_receiver):
    x, y, c = _position()
    me = 4 * x + 2 * y + c
    copies = []
    for k in range(1, N_DEV):
        peer = (_flip(x, k & 4), _flip(y, k & 2), _flip(c, k & 1))
        block = 4 * peer[0] + 2 * peer[1] + peer[2] if as_receiver else me
        copies.append(pltpu.make_async_remote_copy(
            src_ref=out.at[me], dst_ref=out.at[block], send_sem=send_sems.at[k - 1], recv_sem=recv_sems.at[k - 1],
            device_id=peer, device_id_type=MESH))
    return copies


def _packed_gather_start(ins, out, send_sems, recv_sems, stage=None, stage_sem=None):
    _, n_rows, width = out.shape
    x, y, c = _position()
    me = 4 * x + 2 * y + c
    r = 0
    for ref in ins:
        rows_a, cols_a = ref.shape
        for j in range(cols_a // width):
            rows = ref[:, j * width : (j + 1) * width]
            if stage is None:
                out[me, r : r + rows_a, :] = rows
            else:
                stage[r : r + rows_a, :] = rows
            r += rows_a
    if r < n_rows:
        pad = jnp.zeros((n_rows - r, width), F32)
        if stage is None:
            out[me, r:n_rows, :] = pad
        else:
            stage[r:n_rows, :] = pad
    if stage is not None:
        mine = pltpu.make_async_copy(stage, out.at[me], stage_sem)
        mine.start()
        mine.wait()
    for cp in _packed_copies(out, send_sems, recv_sems, False):
        cp.start()


def _packed_gather_wait(out, send_sems, recv_sems):
    for cp in _packed_copies(out, send_sems, recv_sems, True):
        cp.wait_recv()
    for cp in _packed_copies(out, send_sems, recv_sems, False):
        cp.wait_send()


def _all_gather_packed(arrs, width, name):
    n = len(arrs)
    _, n_rows = _packed_rows(arrs, width)

    def body(*refs):
        ins, out = refs[:n], refs[n]
        send_sems, recv_sems = refs[n + 1 :]
        _packed_gather_start(ins, out, send_sems, recv_sems)
        _packed_gather_wait(out, send_sems, recv_sems)

    vm = pl.BlockSpec(memory_space=pltpu.VMEM)
    return pl.pallas_call(
        body, name=name, out_shape=jax.ShapeDtypeStruct((N_DEV, n_rows, width), F32),
        in_specs=[vm] * n, out_specs=vm,
        scratch_shapes=[pltpu.SemaphoreType.DMA((N_DEV - 1,)), pltpu.SemaphoreType.DMA((N_DEV - 1,))],
        compiler_params=_cparams(),
    )(*arrs)


NN = (((1,), (0,)), ((), ()))
NT = (((1,), (1,)), ((), ()))
TN = (((0,), (0,)), ((), ()))


def _n_chunk(n):
    return 256 if n % 256 == 0 else LANES


def _mm_body(dims, m_out, n_out):
    nc = _n_chunk(n_out)
    mh = min(512, m_out)

    def body(a_ref, b_ref, o_ref):
        for m0 in range(0, m_out, mh):
            for n0 in range(0, n_out, nc):
                b = b_ref[n0 : n0 + nc, :] if dims is NT else b_ref[:, n0 : n0 + nc]
                o_ref[m0 : m0 + mh, n0 : n0 + nc] = lax.dot_general(
                    a_ref[m0 : m0 + mh, :], b, dims, preferred_element_type=F32).astype(o_ref.dtype)

    return body


def _ring_chips():
    x, y, c = _position()
    first = (x + (1 - c) - 2 * x * (1 - c), y + c - 2 * y * c)
    second = (x + c - 2 * x * c, y + (1 - c) - 2 * y * (1 - c))
    return first, second, (1 - x, 1 - y)


def _arrival_order():
    x, y, c = _position()
    first, second, diag = _ring_chips()
    blocks = [(x, y, c), (x, y, 1 - c), (*first, c), (*second, 1 - c), (*second, c), (*first, 1 - c), (*diag, c), (*diag, 1 - c)]
    return jnp.stack([4 * b[0] + 2 * b[1] + b[2] for b in blocks]).astype(jnp.int32)


def _proj_gather(h, w_in_loc, w_out_loc, order, tm):
    s, d = h.shape
    ne = w_in_loc.shape[1]
    ni = s // tm
    nc = _n_chunk(ne)
    n_streams = 7

    def body(order_ref, h_ref, win_ref, wout_ref, proj_ref, fin_ref, fout_ref, bbuf, send_sems, recv_sems, local_sems, load_sems):
        ins, fulls = [win_ref, wout_ref], [fin_ref, fout_ref]
        jj, i = pl.program_id(0), pl.program_id(1)
        x, y, c = _position()
        me = (x, y, c)
        sibling = (x, y, 1 - c)
        first, second, diag = _ring_chips()
        sent = [(me, sibling), (me, (*first, c)), (me, (*second, c)), ((*first, c), (*second, c)),
                ((*first, c), sibling), ((*second, c), sibling), ((*diag, c), sibling)]
        received = [sibling, (*first, c), (*second, c), (*diag, c), (*second, 1 - c), (*first, 1 - c), (*diag, 1 - c)]

        def slot(p):
            return 4 * p[0] + 2 * p[1] + p[2]

        def copy(a, k, block, to):
            dst = fulls[a].at[slot(block)]
            return pltpu.make_async_remote_copy(
                src_ref=ins[a] if k < 3 else dst, dst_ref=dst, send_sem=send_sems.at[a, k], recv_sem=recv_sems.at[a, k],
                device_id=to, device_id_type=MESH)

        def send(a, k):
            return copy(a, k, *sent[k])

        def recv(a, k):
            return copy(a, k, received[k], me)

        def own(a):
            return pltpu.make_async_copy(ins[a], fulls[a].at[slot(me)], local_sems.at[a])

        def load(src, buf):
            return pltpu.make_async_copy(src, bbuf.at[buf], load_sems.at[buf])

        @pl.when((jj == 0) & (i == 0))
        def _():
            load(win_ref, 0).start()
            for a in range(2):
                own(a).start()
            send(0, 0).start()
            send(0, 1).start()
            send(1, 0).start()
            load(win_ref, 0).wait()

        stream_of = [None, 0, 1, 4, 2, 5, 3, 6]
        passes = {1: [(0, 4), (0, 2), (0, 3), (1, 1), (1, 2)], 2: [(0, 5)], 3: [(0, 6)]}
        for nxt in range(1, N_DEV):

            @pl.when((jj == nxt - 1) & (i == ni - 1))
            def _(nxt=nxt):
                k = stream_of[nxt]
                recv(0, k).wait_recv()
                for a, k2 in passes.get(k, []):
                    send(a, k2).start()
                if nxt == N_DEV - 2:
                    recv(1, 1).wait_recv()
                    send(1, 4).start()
                    send(1, 3).start()
                load(fin_ref.at[order_ref[nxt]], nxt % 2).start()

        @pl.when((jj > 0) & (i == 0))
        def _():
            load(fin_ref.at[0], jj % 2).wait()

        for n0 in range(0, ne, nc):
            proj_ref[:, n0 : n0 + nc] = lax.dot_general(
                h_ref[...], bbuf[jj % 2, :, n0 : n0 + nc], NN, preferred_element_type=F32).astype(BF16)

        @pl.when((jj == N_DEV - 1) & (i == ni - 1))
        def _():
            recv(1, 2).wait_recv()
            send(1, 5).start()
            recv(1, 3).wait_recv()
            send(1, 6).start()
            for k in (0, 4, 5, 6):
                recv(1, k).wait_recv()
            for a in range(2):
                for k in range(n_streams):
                    send(a, k).wait_send()
                own(a).wait()

    hbm = pl.BlockSpec(memory_space=pl.ANY)
    return pl.pallas_call(
        body, name="proj_gather",
        grid_spec=pltpu.PrefetchScalarGridSpec(
            num_scalar_prefetch=1, grid=(N_DEV, ni),
            in_specs=[pl.BlockSpec((tm, d), lambda jj, i, od: (i, 0)), hbm, hbm],
            out_specs=[pl.BlockSpec((tm, ne), lambda jj, i, od: (i, od[jj])), hbm, hbm],
            scratch_shapes=[pltpu.VMEM((2, d, ne), BF16), pltpu.SemaphoreType.DMA((2, 7)), pltpu.SemaphoreType.DMA((2, 7)),
                            pltpu.SemaphoreType.DMA((2,)), pltpu.SemaphoreType.DMA((2,))]),
        out_shape=[jax.ShapeDtypeStruct((s, N_DEV * ne), BF16), jax.ShapeDtypeStruct((N_DEV,) + w_in_loc.shape, BF16),
                   jax.ShapeDtypeStruct((N_DEV,) + w_out_loc.shape, BF16)],
        compiler_params=_cparams(2),
    )(order, h, w_in_loc, w_out_loc)


def _mm_out(y, w_out, tm, tn):
    s, m = y.shape
    _, d = w_out.shape
    return pl.pallas_call(
        _mm_body(NN, tm, tn), name="mm_out", grid=(d // tn, s // tm),
        in_specs=[pl.BlockSpec((tm, m), lambda j, i: (i, 0)), pl.BlockSpec((m, tn), lambda j, i: (0, j))],
        out_specs=pl.BlockSpec((tm, tn), lambda j, i: (i, j)),
        out_shape=jax.ShapeDtypeStruct((s, d), F32), compiler_params=_cparams(2),
    )(y, w_out)


def _mm_dy(dout, w_out, tm, tn):
    s, d = dout.shape
    m, _ = w_out.shape
    return pl.pallas_call(
        _mm_body(NT, tm, tn), name="mm_dy", grid=(m // tn, s // tm),
        in_specs=[pl.BlockSpec((tm, d), lambda j, i: (i, 0)), pl.BlockSpec((tn, d), lambda j, i: (j, 0))],
        out_specs=pl.BlockSpec((tm, tn), lambda j, i: (i, j)),
        out_shape=jax.ShapeDtypeStruct((s, m), BF16), compiler_params=_cparams(2),
    )(dout, w_out)


def _dw_targets():
    _, _, c = _position()
    return jnp.stack([2 * q + (1 - c) for q in range(N_CHIP)] + [2 * q + c for q in range(N_CHIP)]).astype(jnp.int32)


def _mm_dw_exchange(lhs, rhs, targets, bm, bn, target_on_lhs, name):
    s = lhs.shape[0]
    ni = (rhs.shape[1] // bn) if target_on_lhs else (lhs.shape[1] // bm)
    rows_b, cols_b = (bm, ni * bn) if target_on_lhs else (ni * bm, bn)
    nc = _n_chunk(bn)

    def body(tg_ref, a_ref, b_ref, comb_ref, psib_ref, rsib_ref, obuf, rbuf, osems, rsems, send_sems, recv_sems):
        del tg_ref
        jj, i = pl.program_id(0), pl.program_id(1)
        x, y, c = _position()
        slot = i % 2

        def tile(ref, q, t):
            if target_on_lhs:
                return ref.at[q, :, pl.ds(pl.multiple_of(t * bn, bn), bn)]
            return ref.at[q, pl.ds(pl.multiple_of(t * bm, bm), bm)]

        def put(sl, q):
            return pltpu.make_async_copy(obuf.at[sl], tile(psib_ref, q, i), osems.at[sl])

        def get(sl, q, t):
            return pltpu.make_async_copy(tile(rsib_ref, q, t), rbuf.at[sl], rsems.at[sl])

        def swap(q):
            return pltpu.make_async_remote_copy(
                src_ref=psib_ref.at[q], dst_ref=rsib_ref.at[q], send_sem=send_sems.at[q], recv_sem=recv_sems.at[q],
                device_id=(x, y, 1 - c), device_id_type=MESH)

        def product(n0):
            return lax.dot_general(a_ref[...], b_ref[:, n0 : n0 + nc], TN, preferred_element_type=F32)

        @pl.when(jj < N_CHIP)
        def _():
            @pl.when(i >= 2)
            def _():
                put(slot, 0).wait()

            for n0 in range(0, bn, nc):
                obuf[slot, :, n0 : n0 + nc] = product(n0).astype(BF16)
            put(slot, jj).start()

            @pl.when(i == ni - 1)
            def _():
                put(slot, 0).wait()
                if ni >= 2:
                    put(1 - slot, 0).wait()
                swap(jj).start()

        @pl.when(jj >= N_CHIP)
        def _():
            q = jj - N_CHIP

            @pl.when(i == 0)
            def _():
                swap(q).wait_recv()
                get(0, q, i).start()

            get(slot, 0, i).wait()

            @pl.when(i + 1 < ni)
            def _():
                get(1 - slot, q, i + 1).start()

            for n0 in range(0, bn, nc):
                comb_ref[:, n0 : n0 + nc] = (product(n0) + rbuf[slot, :, n0 : n0 + nc].astype(F32)).astype(BF16)

            @pl.when((jj == N_DEV - 1) & (i == ni - 1))
            def _():
                for qq in range(N_CHIP):
                    swap(qq).wait_send()

    if target_on_lhs:
        in_specs = [pl.BlockSpec((s, bm), lambda jj, i, tg: (0, tg[jj])), pl.BlockSpec((s, bn), lambda jj, i, tg: (0, i))]
        tile_index = lambda jj, i, tg: (jnp.maximum(jj - N_CHIP, 0), 0, jnp.where(jj < N_CHIP, 0, i))
    else:
        in_specs = [pl.BlockSpec((s, bm), lambda jj, i, tg: (0, i)), pl.BlockSpec((s, bn), lambda jj, i, tg: (0, tg[jj]))]
        tile_index = lambda jj, i, tg: (jnp.maximum(jj - N_CHIP, 0), jnp.where(jj < N_CHIP, 0, i), 0)
    hbm = pl.BlockSpec(memory_space=pl.ANY)
    comb, _, _ = pl.pallas_call(
        body, name=name,
        grid_spec=pltpu.PrefetchScalarGridSpec(
            num_scalar_prefetch=1, grid=(N_DEV, ni), in_specs=in_specs,
            out_specs=[pl.BlockSpec((None, bm, bn), tile_index), hbm, hbm],
            scratch_shapes=[pltpu.VMEM((2, bm, bn), BF16), pltpu.VMEM((2, bm, bn), BF16), pltpu.SemaphoreType.DMA((2,)),
                            pltpu.SemaphoreType.DMA((2,)), pltpu.SemaphoreType.DMA((N_CHIP,)), pltpu.SemaphoreType.DMA((N_CHIP,))]),
        out_shape=[jax.ShapeDtypeStruct((N_CHIP, rows_b, cols_b), BF16)] * 3,
        compiler_params=_cparams(2),
    )(targets, lhs, rhs)
    return comb


def _chip_copies(c_refs, r_refs, send_sems, recv_sems):
    x, y, c = _position()
    chips = [(1 - x, y), (x, 1 - y), (1 - x, 1 - y)]
    return [
        pltpu.make_async_remote_copy(
            src_ref=c_refs[a].at[2 * chip[0] + chip[1]], dst_ref=r_refs[a].at[j], send_sem=send_sems.at[a, j],
            recv_sem=recv_sems.at[a, j], device_id=(*chip, c), device_id_type=MESH)
        for a in range(len(c_refs)) for j, chip in enumerate(chips)]


def _mm_dh(dproj, w_full, comb, small_parts, small_width, tm, tn):
    s = dproj.shape[0]
    _, d, ne = w_full.shape
    _, r_rows, r_cols = comb.shape
    kb = 2
    nk = N_DEV // kb
    nc = _n_chunk(tn)
    mh = min(512, tm)
    grid = (s // tm, d // tn, nk)
    n_steps = grid[0] * grid[1] * grid[2]
    n_relay = (3 * n_steps) // 8
    assert 0 < n_relay < n_steps - 1
    rc = min(512, r_rows)

    n_small = len(small_parts)
    _, small_rows = _packed_rows(small_parts, small_width)

    def body(*refs):
        a_ref, b_ref, c_ref = refs[:3]
        small_refs = refs[3 : 3 + n_small]
        (o_ref, recv_ref, relay_ref, sum_ref, packed_ref, va, vb, stage, add_sems, send_sems, recv_sems, small_send, small_recv,
         stage_sem) = refs[3 + n_small :]
        i, n, k = pl.program_id(0), pl.program_id(1), pl.program_id(2)
        step = (i * grid[1] + n) * nk + k
        _, _, c = _position()
        first, second, diag = _ring_chips()

        def chip_slot(p):
            return 2 * p[0] + p[1]

        def stream(j):
            src, dst, to = [(c_ref.at[chip_slot(diag)], relay_ref, first), (c_ref.at[chip_slot(first)], recv_ref.at[0], first),
                            (sum_ref, recv_ref.at[1], second)][j]
            return pltpu.make_async_remote_copy(
                src_ref=src, dst_ref=dst, send_sem=send_sems.at[j], recv_sem=recv_sems.at[j], device_id=(*to, c), device_id_type=MESH)

        @pl.when(step == 0)
        def _():
            stream(0).start()
            stream(1).start()
            _packed_gather_start(small_refs, packed_ref, small_send, small_recv, stage, stage_sem.at[0])

        @pl.when(step == n_relay)
        def _():
            stream(0).wait_recv()

            n_pieces = r_rows // rc

            def pieces(t, sl):
                rows = pl.ds(t * rc, rc)
                return (pltpu.make_async_copy(c_ref.at[chip_slot(second), rows], va.at[sl], add_sems.at[sl, 0]),
                        pltpu.make_async_copy(relay_ref.at[rows], vb.at[sl], add_sems.at[sl, 1]),
                        pltpu.make_async_copy(va.at[sl], sum_ref.at[rows], add_sems.at[sl, 2]))

            for cp in pieces(0, 0)[:2]:
                cp.start()
            for t in range(n_pieces):
                sl = t % 2
                for cp in pieces(t, sl)[:2]:
                    cp.wait()
                if t + 1 < n_pieces:
                    if t >= 1:
                        pieces(t - 1, 1 - sl)[2].wait()
                    for cp in pieces(t + 1, 1 - sl)[:2]:
                        cp.start()
                va[sl] = (va[sl].astype(F32) + vb[sl].astype(F32)).astype(BF16)
                pieces(t, sl)[2].start()
            for t in range(max(n_pieces - 2, 0), n_pieces):
                pieces(t, t % 2)[2].wait()
            stream(2).start()

        @pl.when(k == 0)
        def _():
            o_ref[...] = jnp.zeros_like(o_ref)

        for m0 in range(0, tm, mh):
            for n0 in range(0, tn, nc):
                r = None
                for jj in range(kb):
                    t = lax.dot_general(a_ref[m0 : m0 + mh, jj * ne : (jj + 1) * ne], b_ref[jj, n0 : n0 + nc, :], NT, preferred_element_type=F32)
                    r = t if r is None else r + t
                o_ref[m0 : m0 + mh, n0 : n0 + nc] += r

        @pl.when(step == n_steps - 1)
        def _():
            stream(1).wait_recv()
            stream(2).wait_recv()
            for j in range(3):
                stream(j).wait_send()
            _packed_gather_wait(packed_ref, small_send, small_recv)

    hbm = pl.BlockSpec(memory_space=pl.ANY)
    vm = pl.BlockSpec(memory_space=pltpu.VMEM)
    block = jax.ShapeDtypeStruct((r_rows, r_cols), comb.dtype)
    dh, recv, _, _, packed = pl.pallas_call(
        body, name="mm_dh", grid=grid,
        in_specs=[pl.BlockSpec((tm, kb * ne), lambda i, n, k: (i, k)), pl.BlockSpec((kb, tn, ne), lambda i, n, k: (k, n, 0)), hbm] + [vm] * n_small,
        out_specs=[pl.BlockSpec((tm, tn), lambda i, n, k: (i, n)), hbm, hbm, hbm, hbm],
        out_shape=[jax.ShapeDtypeStruct((s, d), F32), jax.ShapeDtypeStruct((2, r_rows, r_cols), comb.dtype), block, block,
                   jax.ShapeDtypeStruct((N_DEV, small_rows, small_width), F32)],
        scratch_shapes=[pltpu.VMEM((2, rc, r_cols), comb.dtype), pltpu.VMEM((2, rc, r_cols), comb.dtype),
                        pltpu.VMEM((small_rows, small_width), F32), pltpu.SemaphoreType.DMA((2, 3)),
                        pltpu.SemaphoreType.DMA((3,)), pltpu.SemaphoreType.DMA((3,)),
                        pltpu.SemaphoreType.DMA((N_DEV - 1,)), pltpu.SemaphoreType.DMA((N_DEV - 1,)), pltpu.SemaphoreType.DMA((1,))],
        compiler_params=_cparams(3),
    )(dproj, w_full, comb, *small_parts)
    return dh, recv, packed


def _mod_matmul(c_all, w_ada, b_loc, tk):
    nb, d = c_all.shape
    na = w_ada.shape[1]

    def body(c_ref, w_ref, b_ref, o_ref):
        k = pl.program_id(0)
        r = jnp.dot(c_ref[...], w_ref[...], preferred_element_type=F32, precision=lax.Precision.HIGHEST)

        @pl.when(k == 0)
        def _():
            o_ref[...] = r + b_ref[...]

        @pl.when(k > 0)
        def _():
            o_ref[...] += r

    return pl.pallas_call(
        body, name="mod_matmul", grid=(d // tk,),
        in_specs=[pl.BlockSpec((nb, tk), lambda k: (0, k)), pl.BlockSpec((tk, na), lambda k: (k, 0)), pl.BlockSpec((1, na), lambda k: (0, 0))],
        out_specs=pl.BlockSpec((nb, na), lambda k: (0, 0)),
        out_shape=jax.ShapeDtypeStruct((nb, na), F32), compiler_params=_cparams(1),
    )(c_all, w_ada, b_loc)


def _col_chunks(width):
    return [slice(c0, c0 + LANES) for c0 in range(0, width, LANES)]


def _row_sum(acc):
    return jnp.sum(acc, axis=1, keepdims=True)


SUBLANES = 8


ROW_BLOCK = 64
DW_BLOCK = 96


def _tap_sum(ext, cs, ts, taps, emit):
    for t0 in range(0, ts, ROW_BLOCK):
        rb = min(ROW_BLOCK, ts - t0)
        acc = None
        for r in range(SUBLANES):
            group = [(o, wv) for o, wv in taps if o % SUBLANES == r]
            if not group:
                continue
            n = rb if r == 0 else rb + SUBLANES
            v = None
            for o, wv in group:
                term = wv * ext[t0 + o - r : t0 + o - r + n, cs]
                v = term if v is None else v + term
            part = v if r == 0 else v[r : r + rb]
            acc = part if acc is None else acc + part
        emit(slice(t0, t0 + rb), acc)


class _SegmentStores:
    def __init__(self, stash, dst, sems, n_seg, step, n_steps, row0, cols):
        self.stash, self.dst, self.sems, self.n_seg = stash, dst, sems, n_seg
        self.step, self.n_steps, self.row0, self.cols = step, n_steps, row0, cols
        self.slot = step % 2

    def _copy(self, slot, g, row0, col0):
        ts, cb = self.stash.shape[2:]
        return pltpu.make_async_copy(self.stash.at[slot, g], self.dst.at[pl.ds(row0, ts), pl.ds(col0, cb)], self.sems.at[slot, g])

    def _wait(self, slot):
        for g in range(self.n_seg):
            self._copy(slot, g, 0, 0).wait()

    def begin(self):
        @pl.when(self.step >= 2)
        def _():
            self._wait(self.slot)

        return self.slot

    def finish(self):
        ts = self.stash.shape[2]
        for g in range(self.n_seg):
            self._copy(self.slot, g, pl.multiple_of(self.row0, ts), pl.multiple_of(self.cols[g], LANES)).start()

        @pl.when(self.step == self.n_steps - 1)
        def _():
            self._wait(self.slot)
            if self.n_steps >= 2:
                self._wait(1 - self.slot)


def _prep(x, g1, scale, shift, ts):
    s, d = x.shape
    chunks = _col_chunks(d)

    def body(x_ref, g_ref, sc_ref, sh_ref, h_ref):
        acc = jnp.zeros((ts, LANES), F32)
        for cs in chunks:
            v = x_ref[:, cs]
            acc = acc + v * v
        r = lax.rsqrt(_row_sum(acc) / d + EPS)
        for cs in chunks:
            n1 = (x_ref[:, cs] * r) * g_ref[:, cs]
            h_ref[:, cs] = (n1 * (1.0 + sc_ref[:, cs]) + sh_ref[:, cs]).astype(BF16)

    row = pl.BlockSpec((1, d), lambda i: (0, 0))
    return pl.pallas_call(
        body, name="prep", grid=(s // ts,),
        in_specs=[pl.BlockSpec((ts, d), lambda i: (i, 0)), row, row, row],
        out_specs=pl.BlockSpec((ts, d), lambda i: (i, 0)),
        out_shape=jax.ShapeDtypeStruct((s, d), BF16), compiler_params=_cparams(1),
    )(x, g1, scale, shift)


def _mix_a_fwd(proj, wa, w, ts, cb):
    s = proj.shape[0]
    nb = w // cb
    chunks = _col_chunks(cb)

    def body(ab, ac, ax, az, wa_ref, y_ref, cv_ref, ext):
        @pl.when(pl.program_id(1) == 0)
        def _():
            ext[0:HALO_A, :] = jnp.zeros((HALO_A, cb), F32)

        for cs in chunks:
            ext[HALO_A : HALO_A + ts, cs] = ac[:, cs].astype(F32) * ax[:, cs].astype(F32)
        for cs in chunks:

            def emit(rows, cv, cs=cs):
                z = az[rows, cs].astype(F32)
                y_ref[rows, cs] = (ab[rows, cs].astype(F32) * cv * (z * _sigmoid(z))).astype(BF16)
                cv_ref[rows, cs] = cv.astype(BF16)

            _tap_sum(ext, cs, ts, [(HALO_A - (CONV_A - 1) + k, wa_ref[k : k + 1, cs]) for k in range(CONV_A)], emit)
        ext[0:HALO_A, :] = ext[ts : ts + HALO_A, :]

    def seg(q):
        return pl.BlockSpec((ts, cb), lambda c, i: (i, q * nb + c))

    return pl.pallas_call(
        body, name="mix_a_fwd", grid=(nb, s // ts),
        in_specs=[seg(0), seg(1), seg(2), seg(3), pl.BlockSpec((CONV_A, cb), lambda c, i: (0, c))],
        out_specs=[pl.BlockSpec((ts, cb), lambda c, i: (i, c)), pl.BlockSpec((ts, cb), lambda c, i: (i, c))],
        out_shape=[jax.ShapeDtypeStruct((s, 2 * w), BF16), jax.ShapeDtypeStruct((s, w), BF16)],
        scratch_shapes=[pltpu.VMEM((HALO_A + ts, cb), F32)], compiler_params=_cparams(2),
    )(proj, proj, proj, proj, wa)


def _mix_b_conv_fwd(proj, wb, bb, w, ts, cb):
    s = proj.shape[0]
    nb = w // cb
    chunks = _col_chunks(cb)

    def body(bv, bg, wb_ref, bb_ref, u2_ref, ext):
        @pl.when(pl.program_id(1) == 0)
        def _():
            ext[0:HALO_B, :] = jnp.zeros((HALO_B, cb), F32)

        for cs in chunks:
            ext[HALO_B : HALO_B + ts, cs] = bv[:, cs].astype(F32) * _sigmoid(bg[:, cs].astype(F32))
        for cs in chunks:

            def emit(rows, acc, cs=cs):
                u2_ref[rows, cs] = acc + bb_ref[:, cs]

            _tap_sum(ext, cs, ts, [(HALO_B - (CONV_B - 1) + k, wb_ref[k : k + 1, cs]) for k in range(CONV_B)], emit)
        ext[0:HALO_B, :] = ext[ts : ts + HALO_B, :]

    return pl.pallas_call(
        body, name="mix_b_conv_fwd", grid=(nb, s // ts),
        in_specs=[pl.BlockSpec((ts, cb), lambda c, i: (i, 4 * nb + c)), pl.BlockSpec((ts, cb), lambda c, i: (i, 5 * nb + c)),
                  pl.BlockSpec((CONV_B, cb), lambda c, i: (0, c)), pl.BlockSpec((1, cb), lambda c, i: (0, c))],
        out_specs=pl.BlockSpec((ts, cb), lambda c, i: (i, c)),
        out_shape=jax.ShapeDtypeStruct((s, w), F32),
        scratch_shapes=[pltpu.VMEM((HALO_B + ts, cb), F32)], compiler_params=_cparams(2),
    )(proj, proj, wb, bb)


def _ln_stats(u2_ref, ts, w, chunks):
    acc = jnp.zeros((ts, LANES), F32)
    for cs in chunks:
        acc = acc + u2_ref[:, cs]
    mu = _row_sum(acc) / w
    acc = jnp.zeros((ts, LANES), F32)
    for cs in chunks:
        xc = u2_ref[:, cs] - mu
        acc = acc + xc * xc
    return mu, lax.rsqrt(_row_sum(acc) / w + EPS)


def _mix_b_gate_fwd(u2, proj, lg, lb, y, ts):
    s, w = u2.shape
    chunks = _col_chunks(w)

    def body(u2_ref, bz, lg_ref, lb_ref, y_in, y_ref):
        del y_in
        mu, rstd = _ln_stats(u2_ref, ts, w, chunks)
        for cs in chunks:
            ln = (u2_ref[:, cs] - mu) * rstd * lg_ref[:, cs] + lb_ref[:, cs]
            z = bz[:, cs].astype(F32)
            y_ref[:, cs] = ((ln * _sigmoid(ln)) * (z * _sigmoid(z))).astype(BF16)

    row = pl.BlockSpec((1, w), lambda i: (0, 0))
    return pl.pallas_call(
        body, name="mix_b_gate_fwd", grid=(s // ts,),
        in_specs=[pl.BlockSpec((ts, w), lambda i: (i, 0)), pl.BlockSpec((ts, w), lambda i: (i, 6)), row, row,
                  pl.BlockSpec(memory_space=pl.ANY)],
        out_specs=pl.BlockSpec((ts, w), lambda i: (i, 1)),
        out_shape=jax.ShapeDtypeStruct(y.shape, y.dtype), input_output_aliases={4: 0}, compiler_params=_cparams(1),
    )(u2, proj, lg, lb, y)


def _post(x, out, tgt, gate, fg, ts):
    s, d = x.shape
    chunks = _col_chunks(d)

    def body(x_ref, o_ref, t_ref, gt_ref, fg_ref, dx2_ref, dout_ref, loss_ref, dgate_ref, dfg_ref, keep):
        @pl.when(pl.program_id(0) == 0)
        def _():
            loss_ref[...] = jnp.zeros_like(loss_ref)
            dgate_ref[...] = jnp.zeros_like(dgate_ref)
            dfg_ref[...] = jnp.zeros_like(dfg_ref)

        acc = jnp.zeros((ts, LANES), F32)
        for cs in chunks:
            x2 = x_ref[:, cs] + gt_ref[:, cs] * o_ref[:, cs]
            keep[:, cs] = x2
            acc = acc + x2 * x2
        r2 = lax.rsqrt(_row_sum(acc) / d + EPS)
        acc = jnp.zeros((ts, LANES), F32)
        for cs in chunks:
            n2 = keep[:, cs] * r2
            keep[:, cs] = n2
            diff = n2 * fg_ref[:, cs] - t_ref[:, cs]
            loss_ref[:, cs] += jnp.sum(diff * diff, axis=0, keepdims=True)
            dyf = diff / d
            dfg_ref[:, cs] += jnp.sum(dyf * n2, axis=0, keepdims=True)
            dn = dyf * fg_ref[:, cs]
            dx2_ref[:, cs] = dn
            acc = acc + dn * n2
        mdot = _row_sum(acc) / d
        for cs in chunks:
            dx2 = r2 * (dx2_ref[:, cs] - keep[:, cs] * mdot)
            dx2_ref[:, cs] = dx2
            dgate_ref[:, cs] += jnp.sum(dx2 * o_ref[:, cs], axis=0, keepdims=True)
            dout_ref[:, cs] = (dx2 * gt_ref[:, cs]).astype(BF16)

    tile = pl.BlockSpec((ts, d), lambda i: (i, 0))
    row = pl.BlockSpec((1, d), lambda i: (0, 0))
    return pl.pallas_call(
        body, name="post", grid=(s // ts,),
        in_specs=[tile, tile, tile, row, row], out_specs=[tile, tile, row, row, row],
        out_shape=[jax.ShapeDtypeStruct((s, d), F32), jax.ShapeDtypeStruct((s, d), BF16)] + [jax.ShapeDtypeStruct((1, d), F32)] * 3,
        scratch_shapes=[pltpu.VMEM((ts, d), F32)], compiler_params=_cparams(1),
    )(x, out, tgt, gate, fg)


def _mix_b_gate_bwd(u2, proj, dy, lg, lb, n_cols, ts):
    s, w = u2.shape
    chunks = _col_chunks(w)

    def body(u2_ref, bz, dyb, lg_ref, lb_ref, du2_ref, dbz_ref, dlg_ref, dlb_ref, dbb_ref):
        @pl.when(pl.program_id(0) == 0)
        def _():
            dlg_ref[...] = jnp.zeros_like(dlg_ref)
            dlb_ref[...] = jnp.zeros_like(dlb_ref)
            dbb_ref[...] = jnp.zeros_like(dbb_ref)

        mu, rstd = _ln_stats(u2_ref, ts, w, chunks)
        acc1 = jnp.zeros((ts, LANES), F32)
        acc2 = jnp.zeros((ts, LANES), F32)
        for cs in chunks:
            xh = (u2_ref[:, cs] - mu) * rstd
            ln = xh * lg_ref[:, cs] + lb_ref[:, cs]
            sl = _sigmoid(ln)
            z = bz[:, cs].astype(F32)
            sz = _sigmoid(z)
            g = dyb[:, cs].astype(F32)
            dbz_ref[:, cs] = (g * (ln * sl) * (sz * (1.0 + z * (1.0 - sz)))).astype(BF16)
            dln = g * (z * sz) * (sl * (1.0 + ln * (1.0 - sl)))
            dlg_ref[:, cs] += jnp.sum(dln * xh, axis=0, keepdims=True)
            dlb_ref[:, cs] += jnp.sum(dln, axis=0, keepdims=True)
            dxh = dln * lg_ref[:, cs]
            du2_ref[:, cs] = dxh
            acc1 = acc1 + dxh
            acc2 = acc2 + dxh * xh
        m1 = _row_sum(acc1) / w
        m2 = _row_sum(acc2) / w
        for cs in chunks:
            xh = (u2_ref[:, cs] - mu) * rstd
            du2 = rstd * (du2_ref[:, cs] - m1 - xh * m2)
            du2_ref[:, cs] = du2
            dbb_ref[:, cs] += jnp.sum(du2, axis=0, keepdims=True)

    tile = pl.BlockSpec((ts, w), lambda i: (i, 0))
    row = pl.BlockSpec((1, w), lambda i: (0, 0))
    return pl.pallas_call(
        body, name="mix_b_gate_bwd", grid=(s // ts,),
        in_specs=[tile, pl.BlockSpec((ts, w), lambda i: (i, 6)), pl.BlockSpec((ts, w), lambda i: (i, 1)), row, row],
        out_specs=[tile, pl.BlockSpec((ts, w), lambda i: (i, 6)), row, row, row],
        out_shape=[jax.ShapeDtypeStruct((s, w), F32), jax.ShapeDtypeStruct((s, n_cols), BF16)] + [jax.ShapeDtypeStruct((1, w), F32)] * 3,
        compiler_params=_cparams(1),
    )(u2, proj, dy, lg, lb)


def _mix_b_conv_bwd(du2, proj, wb, dproj, comb, w, ts, cb):
    s = proj.shape[0]
    nb = w // cb
    ns = s // ts
    chunks = _col_chunks(cb)

    u_rows = ts + 2 * HALO_B + SUBLANES
    win = ts + HALO_B
    assert win % DW_BLOCK == 0

    n_seg = 2

    def body(du2_ref, bv, bg, wb_ref, dp_in, comb_ref, dp_ref, dwb_ref, recv_ref, ext, stash, uext, sems, send_sems, recv_sems):
        del dp_in
        c, i = pl.program_id(0), pl.program_id(1)
        stores = _SegmentStores(stash, dp_ref, sems, n_seg, c * ns + i, nb * ns, (ns - 1 - i) * ts, [((4 + g) * nb + c) * cb for g in range(n_seg)])
        slot = stores.begin()

        def copies():
            return _chip_copies([comb_ref], [recv_ref], send_sems, recv_sems)

        @pl.when((c == 0) & (i == 0))
        def _():
            for cp in copies():
                cp.start()

        @pl.when(i == 0)
        def _():
            ext[ts : ts + HALO_B, :] = jnp.zeros((HALO_B, cb), F32)
            uext[0:HALO_B, :] = jnp.zeros((HALO_B, cb), F32)
            uext[HALO_B + ts : u_rows, :] = jnp.zeros((u_rows - HALO_B - ts, cb), F32)
            dwb_ref[...] = jnp.zeros_like(dwb_ref)

        ext[0:ts, :] = du2_ref[...]
        for cs in chunks:
            uext[HALO_B : HALO_B + ts, cs] = bv[:, cs].astype(F32) * _sigmoid(bg[:, cs].astype(F32))

            def emit(rows, du, cs=cs):
                v = bv[rows, cs].astype(F32)
                sg = _sigmoid(bg[rows, cs].astype(F32))
                stash[slot, 0, rows, cs] = (du * sg).astype(BF16)
                stash[slot, 1, rows, cs] = (du * v * (sg * (1.0 - sg))).astype(BF16)

            _tap_sum(ext, cs, ts, [(CONV_B - 1 - k, wb_ref[k : k + 1, cs]) for k in range(CONV_B)], emit)
            for r in range(SUBLANES):
                ks = [k for k in range(CONV_B) if (CONV_B - 1 - k) % SUBLANES == r]
                accs = [jnp.zeros((SUBLANES, LANES), F32) for _ in ks]
                for t0 in range(0, win, DW_BLOCK):
                    e = ext[t0 : t0 + DW_BLOCK, cs]
                    ush = uext[SUBLANES - r + t0 : SUBLANES - r + t0 + DW_BLOCK + HALO_B - SUBLANES, cs]
                    for n, k in enumerate(ks):
                        b0 = HALO_B - SUBLANES - (CONV_B - 1 - k - r)
                        prod = e * ush[b0 : b0 + DW_BLOCK]
                        for v0 in range(0, DW_BLOCK, SUBLANES):
                            accs[n] = accs[n] + prod[v0 : v0 + SUBLANES]
                for n, k in enumerate(ks):
                    dwb_ref[k : k + 1, cs] += jnp.sum(accs[n], axis=0, keepdims=True)
        ext[ts : ts + HALO_B, :] = ext[0:HALO_B, :]
        stores.finish()

        @pl.when((c == nb - 1) & (i == ns - 1))
        def _():
            for cp in copies():
                cp.wait_recv()
            for cp in copies():
                cp.wait_send()

    def rev(col):
        return lambda c, i: (ns - 1 - i, col(c))

    hbm = pl.BlockSpec(memory_space=pl.ANY)
    return pl.pallas_call(
        body, name="mix_b_conv_bwd", grid=(nb, ns),
        in_specs=[pl.BlockSpec((ts, cb), rev(lambda c: c)), pl.BlockSpec((ts, cb), rev(lambda c: 4 * nb + c)),
                  pl.BlockSpec((ts, cb), rev(lambda c: 5 * nb + c)), pl.BlockSpec((CONV_B, cb), lambda c, i: (0, c)), hbm, hbm],
        out_specs=[hbm, pl.BlockSpec((CONV_B, cb), lambda c, i: (0, c)), hbm],
        out_shape=[jax.ShapeDtypeStruct(dproj.shape, dproj.dtype), jax.ShapeDtypeStruct((CONV_B, w), F32),
                   jax.ShapeDtypeStruct((3,) + comb.shape[1:], comb.dtype)],
        input_output_aliases={4: 0},
        scratch_shapes=[pltpu.VMEM((ts + HALO_B, cb), F32), pltpu.VMEM((2, n_seg, ts, cb), BF16), pltpu.VMEM((u_rows, cb), F32),
                        pltpu.SemaphoreType.DMA((2, n_seg)), pltpu.SemaphoreType.DMA((1, 3)), pltpu.SemaphoreType.DMA((1, 3))],
        compiler_params=_cparams(2),
    )(du2, proj, proj, wb, dproj, comb)


def _mix_a_bwd(proj, cv, dy, wa, dproj, w, ts, cb):
    s = proj.shape[0]
    nb = w // cb
    ns = s // ts
    chunks = _col_chunks(cb)

    n_seg = 4

    def body(ab, ac, ax, az, cv_ref, dya, wa_ref, dp_in, dp_ref, dwa_ref, ext, stash, sems):
        del dp_in
        c, i = pl.program_id(0), pl.program_id(1)
        stores = _SegmentStores(stash, dp_ref, sems, n_seg, c * ns + i, nb * ns, (ns - 1 - i) * ts, [(g * nb + c) * cb for g in range(n_seg)])
        slot = stores.begin()

        @pl.when(i == 0)
        def _():
            ext[ts : ts + HALO_A, :] = jnp.zeros((HALO_A, cb), F32)
            dwa_ref[...] = jnp.zeros_like(dwa_ref)

        for cs in chunks:
            b = ab[:, cs].astype(F32)
            z = az[:, cs].astype(F32)
            sz = _sigmoid(z)
            g = dya[:, cs].astype(F32)
            conv = cv_ref[:, cs].astype(F32)
            stash[slot, 0, :, cs] = (g * conv * (z * sz)).astype(BF16)
            stash[slot, 3, :, cs] = (g * b * conv * (sz * (1.0 + z * (1.0 - sz)))).astype(BF16)
            ext[0:ts, cs] = g * b * (z * sz)
        for cs in chunks:

            def emit(rows, dca, cs=cs):
                stash[slot, 1, rows, cs] = (dca * ax[rows, cs].astype(F32)).astype(BF16)
                stash[slot, 2, rows, cs] = (dca * ac[rows, cs].astype(F32)).astype(BF16)

            _tap_sum(ext, cs, ts, [(CONV_A - 1 - k, wa_ref[k : k + 1, cs]) for k in range(CONV_A)], emit)
            ca = ac[:, cs].astype(F32) * ax[:, cs].astype(F32)
            for k in range(CONV_A):
                o = CONV_A - 1 - k
                dwa_ref[k : k + 1, cs] += jnp.sum(ca * ext[o : o + ts, cs], axis=0, keepdims=True)
        ext[ts : ts + HALO_A, :] = ext[0:HALO_A, :]
        stores.finish()

    def rev(col):
        return lambda c, i: (ns - 1 - i, col(c))

    def seg(sq):
        return pl.BlockSpec((ts, cb), rev(lambda c: sq * nb + c))

    hbm = pl.BlockSpec(memory_space=pl.ANY)
    return pl.pallas_call(
        body, name="mix_a_bwd", grid=(nb, ns),
        in_specs=[seg(0), seg(1), seg(2), seg(3), pl.BlockSpec((ts, cb), rev(lambda c: c)), pl.BlockSpec((ts, cb), rev(lambda c: c)),
                  pl.BlockSpec((CONV_A, cb), lambda c, i: (0, c)), hbm],
        out_specs=[hbm, pl.BlockSpec((CONV_A, cb), lambda c, i: (0, c))],
        out_shape=[jax.ShapeDtypeStruct(dproj.shape, dproj.dtype), jax.ShapeDtypeStruct((CONV_A, w), F32)],
        input_output_aliases={7: 0},
        scratch_shapes=[pltpu.VMEM((ts + HALO_A, cb), F32), pltpu.VMEM((2, n_seg, ts, cb), BF16), pltpu.SemaphoreType.DMA((2, n_seg))],
        compiler_params=_cparams(2),
    )(proj, proj, proj, proj, cv, dy, wa, dproj)


def _pre_bwd(x, dh, dx2, g1, scale, ts):
    s, d = x.shape
    chunks = _col_chunks(d)

    def body(x_ref, dh_ref, dx2_ref, g_ref, sc_ref, gx_ref, dsh_ref, dsc_ref, dg_ref, keep):
        @pl.when(pl.program_id(0) == 0)
        def _():
            dsh_ref[...] = jnp.zeros_like(dsh_ref)
            dsc_ref[...] = jnp.zeros_like(dsc_ref)
            dg_ref[...] = jnp.zeros_like(dg_ref)

        acc = jnp.zeros((ts, LANES), F32)
        for cs in chunks:
            v = x_ref[:, cs]
            acc = acc + v * v
        r1 = lax.rsqrt(_row_sum(acc) / d + EPS)
        acc = jnp.zeros((ts, LANES), F32)
        for cs in chunks:
            xn = x_ref[:, cs] * r1
            keep[:, cs] = xn
            g = dh_ref[:, cs]
            dsh_ref[:, cs] += jnp.sum(g, axis=0, keepdims=True)
            dsc_ref[:, cs] += jnp.sum(g * (xn * g_ref[:, cs]), axis=0, keepdims=True)
            dn1 = g * (1.0 + sc_ref[:, cs])
            dg_ref[:, cs] += jnp.sum(dn1 * xn, axis=0, keepdims=True)
            dxn = dn1 * g_ref[:, cs]
            gx_ref[:, cs] = dxn
            acc = acc + dxn * xn
        mdot = _row_sum(acc) / d
        for cs in chunks:
            gx_ref[:, cs] = dx2_ref[:, cs] + r1 * (gx_ref[:, cs] - keep[:, cs] * mdot)

    tile = pl.BlockSpec((ts, d), lambda i: (i, 0))
    row = pl.BlockSpec((1, d), lambda i: (0, 0))
    return pl.pallas_call(
        body, name="pre_bwd", grid=(s // ts,),
        in_specs=[tile, tile, tile, row, row], out_specs=[tile, row, row, row],
        out_shape=[jax.ShapeDtypeStruct((s, d), F32)] + [jax.ShapeDtypeStruct((1, d), F32)] * 3,
        scratch_shapes=[pltpu.VMEM((ts, d), F32)], compiler_params=_cparams(1),
    )(x, dh, dx2, g1, scale)


def _adamw(w, g, m, v):
    m = ADAM_B1 * m + (1.0 - ADAM_B1) * g
    v = ADAM_B2 * v + (1.0 - ADAM_B2) * (g * g)
    m_hat = m / (1.0 - ADAM_B1**ADAM_STEP)
    v_hat = v / (1.0 - ADAM_B2**ADAM_STEP)
    delta = -ADAM_LR * (m_hat / (jnp.sqrt(v_hat) + ADAM_EPS) + ADAM_WD * w)
    return delta, m, v


def _adam_sharded(w, m, v, comb, recv, chip, tr, name):
    r, n = w.shape
    nr = recv.shape[0]

    def body(chip_ref, w_ref, m_ref, v_ref, c_ref, r_ref, g_ref, d_ref, nm_ref, nv_ref):
        del chip_ref
        g = c_ref[...].astype(F32)
        for j in range(nr):
            g = g + r_ref[j].astype(F32)
        delta, nm, nv = _adamw(w_ref[...], g, m_ref[...], v_ref[...])
        g_ref[...] = g
        d_ref[...] = delta
        nm_ref[...] = nm
        nv_ref[...] = nv

    tile = pl.BlockSpec((tr, n), lambda i, ch: (i, 0))
    return pl.pallas_call(
        body, name=name,
        grid_spec=pltpu.PrefetchScalarGridSpec(
            num_scalar_prefetch=1, grid=(r // tr,),
            in_specs=[tile, tile, tile, pl.BlockSpec((None, tr, n), lambda i, ch: (ch[0], i, 0)), pl.BlockSpec((nr, tr, n), lambda i, ch: (0, i, 0))],
            out_specs=[tile] * 4),
        out_shape=[jax.ShapeDtypeStruct((r, n), F32)] * 4, compiler_params=_cparams(1),
    )(chip, w, m, v, comb, recv)


def _adam_w_ada(c_t, dm_loc, w, m, v, tr):
    d, nb = c_t.shape
    na = w.shape[1]

    def body(c_ref, dm_ref, w_ref, m_ref, v_ref, g_ref, d_ref, nm_ref, nv_ref):
        g = jnp.dot(c_ref[...], dm_ref[...], preferred_element_type=F32, precision=lax.Precision.HIGHEST)
        delta, nm, nv = _adamw(w_ref[...], g, m_ref[...], v_ref[...])
        g_ref[...] = g
        d_ref[...] = delta
        nm_ref[...] = nm
        nv_ref[...] = nv

    tile = pl.BlockSpec((tr, na), lambda i: (i, 0))
    return pl.pallas_call(
        body, name="adam_w_ada", grid=(d // tr,),
        in_specs=[pl.BlockSpec((tr, nb), lambda i: (i, 0)), pl.BlockSpec((nb, na), lambda i: (0, 0)), tile, tile, tile],
        out_specs=[tile] * 4, out_shape=[jax.ShapeDtypeStruct((d, na), F32)] * 4, compiler_params=_cparams(1),
    )(c_t, dm_loc, w, m, v)


def _small_final(me, packed, starts, late, late_starts, params, d, w, cl):
    flat_params = [t for p in params for t in p]
    kd = d // w
    r_loss, r_dfg, r_dgt, r_dbb, r_dlg, r_dlb, r_dwa, r_dwb = starts
    r_dg1, r_dsh, r_dsc = late_starts

    def total(ref, r0, rows):
        acc = ref[0, r0 : r0 + rows, :]
        for b in range(1, N_DEV):
            acc = acc + ref[b, r0 : r0 + rows, :]
        return acc

    def body(me_ref, full_ref, mine_ref, late_ref, *rest):
        del me_ref
        prm = rest[:24]
        loss_ref = rest[24]
        outs = rest[25:]

        loss_rows = jnp.sum(total(full_ref, r_loss, kd), axis=1, keepdims=True)
        loss_ref[...] = jnp.sum(loss_rows, axis=0, keepdims=True) * (0.5 / d)

        def update(idx, g, col0=0):
            w_ref, m_ref, v_ref = prm[3 * idx : 3 * idx + 3]
            g_ref, d_ref, nm_ref, nv_ref = outs[4 * idx : 4 * idx + 4]
            sl = (slice(None), slice(col0, col0 + g.shape[1]))
            delta, nm, nv = _adamw(w_ref[sl], g, m_ref[sl], v_ref[sl])
            g_ref[sl] = g
            d_ref[sl] = delta
            nm_ref[sl] = nm
            nv_ref[sl] = nv

        def update_wide(idx, ref, r0, col0=0):
            for j in range(kd):
                update(idx, total(ref, r0 + j, 1), col0 + j * w)

        update_wide(0, late_ref, r_dg1)
        update_wide(1, late_ref, r_dsh, 0)
        update_wide(1, late_ref, r_dsc, d)
        update_wide(1, full_ref, r_dgt, 2 * d)
        update(2, total(mine_ref, r_dwa, CONV_A))
        update(3, total(mine_ref, r_dwb, CONV_B))
        update(4, total(full_ref, r_dbb, 1))
        update(5, total(full_ref, r_dlg, 1))
        update(6, total(full_ref, r_dlb, 1))
        update_wide(7, full_ref, r_dfg)

    def full(shape):
        nd = len(shape)
        return pl.BlockSpec(shape, lambda i, mr: (0,) * nd)

    n_rows = packed.shape[1]
    in_specs = [full(packed.shape), pl.BlockSpec((N_DEV, n_rows, cl), lambda i, mr: (0, 0, mr[0])), full(late.shape)]
    in_specs += [full(t.shape) for t in flat_params]
    out_shapes = [jax.ShapeDtypeStruct((1, 1), F32)]
    for p in params:
        out_shapes += [jax.ShapeDtypeStruct(p[0].shape, F32)] * 4
    return pl.pallas_call(
        body, name="small_final",
        grid_spec=pltpu.PrefetchScalarGridSpec(
            num_scalar_prefetch=1, grid=(1,), in_specs=in_specs, out_specs=[full(o.shape) for o in out_shapes]),
        out_shape=out_shapes, compiler_params=_cparams(1),
    )(me, packed, packed, late, *flat_params)


def _silu_rows(c):
    def body(c_ref, o_ref):
        v = c_ref[...]
        o_ref[...] = v * _sigmoid(v)

    return pl.pallas_call(body, name="silu_c", out_shape=jax.ShapeDtypeStruct(c.shape, F32), compiler_params=_cparams())(c)


def kernel(x, c, norm_g, w_ada, b_ada, w_in, conv_a_w, conv_b_w, conv_b_b, ln_b_g, ln_b_b, w_out, final_g, loss_target, m_norm_g, m_w_ada, m_b_ada, m_w_in, m_conv_a_w, m_conv_b_w, m_conv_b_b, m_ln_b_g, m_ln_b_b, m_w_out, m_final_g, v_norm_g, v_w_ada, v_b_ada, v_w_in, v_conv_a_w, v_conv_b_w, v_conv_b_b, v_ln_b_g, v_ln_b_b, v_w_out, v_final_g):
    _, s, d = x.shape
    w = conv_b_b.shape[-1]
    cl = conv_a_w.shape[-1]
    na = w_ada.shape[-1]
    assert ln_b_g.shape[-1] == w and w_out.shape[1] * N_DEV == 2 * w and w_in.shape[-1] * N_DEV == 7 * w and cl * N_DEV == w
    ts = min(256, s)
    ts_a = min(512, s)
    ts_row = min(256, s)
    cb = min(512, w)
    tm = min(512, s)
    tm_big = min(1024, s)
    tn = min(1024, d)

    px, py, pc = _position()
    me = 4 * px + 2 * py + pc
    me_arr = jnp.reshape(me, (1,)).astype(jnp.int32)
    chip_arr = jnp.reshape(2 * px + py, (1,)).astype(jnp.int32)

    x2d, tgt = x[0], loss_target[0]
    w_ada2, m_w_ada2, v_w_ada2 = w_ada[0], m_w_ada[0], v_w_ada[0]
    w_in2, m_w_in2, v_w_in2 = w_in[0], m_w_in[0], v_w_in[0]
    w_out2, m_w_out2, v_w_out2 = w_out[0], m_w_out[0], v_w_out[0]
    fg = final_g.reshape(1, d)

    c_act = _silu_rows(c)
    small_in = [c_act, conv_a_w[0], conv_b_w[0]]
    (r_c, r_wa, r_wb), _ = _packed_rows(small_in, cl)
    packed_in = _all_gather_packed(small_in, cl, "gather_small_in")
    c_all = packed_in[:, r_c : r_c + d // cl, :].reshape(N_DEV, d)
    wa = jnp.transpose(packed_in[:, r_wa : r_wa + CONV_A, :], (1, 0, 2)).reshape(CONV_A, w)
    wb = jnp.transpose(packed_in[:, r_wb : r_wb + CONV_B, :], (1, 0, 2)).reshape(CONV_B, w)
    b_loc = lax.dynamic_slice_in_dim(b_ada, me * na, na, axis=1)
    mod_loc = _mod_matmul(c_all, w_ada2, b_loc, min(512, d))
    (mod_all,) = _all_gather_vmem([mod_loc], "gather_mod")
    mod_mine = lax.dynamic_index_in_dim(mod_all, me, axis=1, keepdims=False).reshape(1, 3 * d)
    shift, scale, gate = mod_mine[:, :d], mod_mine[:, d : 2 * d], mod_mine[:, 2 * d :]

    h = _prep(x2d, norm_g, scale, shift, ts_row)
    proj, w_in_full, w_out_full = _proj_gather(h, w_in2.astype(BF16), w_out2.astype(BF16), _arrival_order(), tm)
    w_out_full = w_out_full.reshape(2 * w, d)
    y, cv = _mix_a_fwd(proj, wa, w, ts_a, cb)
    u2 = _mix_b_conv_fwd(proj, wb, conv_b_b, w, ts, cb)
    y = _mix_b_gate_fwd(u2, proj, ln_b_g, ln_b_b, y, ts)
    out = _mm_out(y, w_out_full, tm_big, tn)
    dx2, dout, loss_row, dgate_row, dfg_row = _post(x2d, out, tgt, gate, fg, ts_row)

    dy = _mm_dy(dout, w_out_full, tm_big, tn)
    comb_out = _mm_dw_exchange(y, dout, _dw_targets(), w_out2.shape[0], tn, True, "mm_dw_out")
    du2, dproj, dlg_row, dlb_row, dbb_row = _mix_b_gate_bwd(u2, proj, dy, ln_b_g, ln_b_b, 7 * w, ts)
    dproj, dwb_part, recv_out = _mix_b_conv_bwd(du2, proj, wb, dproj, comb_out, w, ts, cb)
    dproj, dwa_part = _mix_a_bwd(proj, cv, dy, wa, dproj, w, ts_a, cb)
    comb_in = _mm_dw_exchange(h, dproj, _dw_targets(), tm, w_in2.shape[1], False, "mm_dw_in")
    small_parts = [loss_row, dfg_row, dgate_row, dbb_row, dlg_row, dlb_row, dwa_part, dwb_part]
    starts, _ = _packed_rows(small_parts, w)
    dh, recv_in, packed = _mm_dh(dproj, w_in_full, comb_in, small_parts, w, min(1024, s), tn)
    grad_x, dshift_row, dscale_row, dg1_row = _pre_bwd(x2d, dh, dx2, norm_g, scale, ts_row)
    g_w_in, d_w_in, nm_w_in, nv_w_in = _adam_sharded(w_in2, m_w_in2, v_w_in2, comb_in, recv_in, chip_arr, min(256, d), "adam_w_in")
    g_w_out, d_w_out, nm_w_out, nv_w_out = _adam_sharded(w_out2, m_w_out2, v_w_out2, comb_out, recv_out, chip_arr, min(128, w_out2.shape[0]), "adam_w_out")

    late_parts = [dg1_row, dshift_row, dscale_row]
    late_starts, _ = _packed_rows(late_parts, w)
    late = _all_gather_packed(late_parts, w, "gather_small_grads")
    kd = d // w

    def gathered_rows(buf, r0):
        return buf[:, r0 : r0 + kd, :].reshape(N_DEV, d)

    dmod_all = jnp.concatenate([gathered_rows(late, late_starts[1]), gathered_rows(late, late_starts[2]), gathered_rows(packed, starts[2])], axis=1)
    dm_loc = lax.dynamic_slice_in_dim(dmod_all, me * na, na, axis=1)
    g_w_ada, d_w_ada, nm_w_ada, nv_w_ada = _adam_w_ada(jnp.transpose(c_all), dm_loc, w_ada2, m_w_ada2, v_w_ada2, min(256, d))

    params = [
        (norm_g, m_norm_g, v_norm_g), (b_ada, m_b_ada, v_b_ada), (conv_a_w[0], m_conv_a_w[0], v_conv_a_w[0]),
        (conv_b_w[0], m_conv_b_w[0], v_conv_b_w[0]), (conv_b_b, m_conv_b_b, v_conv_b_b), (ln_b_g, m_ln_b_g, v_ln_b_g),
        (ln_b_b, m_ln_b_b, v_ln_b_b), (fg, m_final_g.reshape(1, d), v_final_g.reshape(1, d)),
    ]
    small = _small_final(me_arr, packed, starts, late, late_starts, params, d, w, cl)
    loss = small[0].reshape(())
    sm = [small[1 + 4 * i : 5 + 4 * i] for i in range(8)]

    def pick(k):
        return [
            sm[0][k], (g_w_ada, d_w_ada, nm_w_ada, nv_w_ada)[k][None], sm[1][k], (g_w_in, d_w_in, nm_w_in, nv_w_in)[k][None],
            sm[2][k][None], sm[3][k][None], sm[4][k], sm[5][k], sm[6][k], (g_w_out, d_w_out, nm_w_out, nv_w_out)[k][None],
            sm[7][k].reshape(d),
        ]

    return (loss, grad_x[None], *pick(0), *pick(1), *pick(2), *pick(3))
```
